```python
import math
import jax, jax.numpy as jnp
from jax import lax
import numpy as np

D_MODEL = 1024
BATCH = 4
SEQ = 4096
DEPTH = 2

GRID_W = 64
CTX_LEN = 256
N_MIXERS = 4
GROUP_W = D_MODEL // N_MIXERS
MIX_W = N_MIXERS * GROUP_W
NORM_EPS = 1e-6
GLA_HEADS = 4
GLA_DV = GROUP_W // GLA_HEADS
GLA_DK = GLA_DV // 2
GLA_GATE_RANK = 16
GLA_TAU = 16.0
GLA_CHUNK = 64
ROPE_BASE = 10000.0
NA_HEADS = 4
NA_DH = GROUP_W // NA_HEADS
NA_KR = 8
NA_KC = 16
S5_GROUP = 16
S5_GROUPS = GROUP_W // S5_GROUP
S5_STATE = 64
POOL_WINDOWS = (2, 4, 8, 16)
POOL_GW = GROUP_W // len(POOL_WINDOWS)
CTX_SPLIT = (GLA_HEADS * GLA_DK, GROUP_W, GLA_GATE_RANK, GLA_GATE_RANK, GROUP_W, GROUP_W, GROUP_W)
LAT_SPLIT = (GLA_HEADS * GLA_DK, GROUP_W, GROUP_W, MIX_W)
N_CTX_COLS = sum(CTX_SPLIT)
N_IN_COLS = N_CTX_COLS + sum(LAT_SPLIT)

kernel_name = "hybrid_parallel_head_groups_diffusion_block"


def rmsnorm(x, g):
    xf = x.astype(jnp.float32)
    y = xf * lax.rsqrt(jnp.mean(xf * xf, axis=-1, keepdims=True) + NORM_EPS)
    return (y * g.astype(jnp.float32)).astype(x.dtype)


def split_cols(p, sizes):
    idx = [int(i) for i in np.cumsum(sizes)[:-1]]
    return jnp.split(p, idx, axis=-1)


def flip_seq(a):
    return jnp.flip(a, axis=1)


def axial_rope_tables(n_tokens):
    t = jnp.arange(n_tokens)
    half = GLA_DK // 2
    freqs = ROPE_BASE ** (-jnp.arange(0, half, 2, dtype=jnp.float32) / half)
    def table(pos):
        ang = pos.astype(jnp.float32)[:, None] * freqs[None, :]
        ang = jnp.concatenate([ang, ang], axis=-1)
        return jnp.cos(ang)[:, None, :], jnp.sin(ang)[:, None, :]
    return table(t // GRID_W) + table(t % GRID_W)


def rotate_half(x):
    x1, x2 = jnp.split(x, 2, axis=-1)
    return jnp.concatenate([-x2, x1], axis=-1)


def apply_axial_rope(x, tables):
    cos_r, sin_r, cos_c, sin_c = tables
    xr, xc = jnp.split(x, 2, axis=-1)
    return jnp.concatenate([xr * cos_r + rotate_half(xr) * sin_r,
                            xc * cos_c + rotate_half(xc) * sin_c], axis=-1)


def gla_chunked(q, k, v, logg, s0):
    bsz, n_tok, n_h, _ = q.shape
    dv = v.shape[-1]
    n_chunks = n_tok // GLA_CHUNK
    def to_chunks(a):
        return a.reshape(bsz, n_chunks, GLA_CHUNK, n_h, a.shape[-1]).transpose(1, 0, 3, 2, 4)
    lower = jnp.tril(jnp.ones((GLA_CHUNK, GLA_CHUNK), dtype=bool))
    def step(state, inp):
        qc, kc, vc, gc = inp
        b = jnp.cumsum(gc, axis=-2)
        diff = b[:, :, :, None, :] - b[:, :, None, :, :]
        decay = jnp.exp(jnp.where(lower[:, :, None], diff, -jnp.inf))
        att = jnp.einsum('bhid,bhjd,bhijd->bhij', qc, kc, decay)
        o = (jnp.einsum('bhij,bhje->bhie', att, vc)
             + jnp.einsum('bhid,bhde->bhie', qc * jnp.exp(b), state))
        b_last = b[:, :, -1:, :]
        state = (jnp.exp(b_last)[:, :, 0, :, None] * state
                 + jnp.einsum('bhjd,bhje->bhde', kc * jnp.exp(b_last - b), vc))
        return state, o
    _, o = lax.scan(step, s0, (to_chunks(q), to_chunks(k), to_chunks(v), to_chunks(logg)))
    return o.transpose(1, 0, 3, 2, 4).reshape(bsz, n_tok, n_h, dv)


def gla_final_state(k, v, logg):
    b = jnp.cumsum(logg, axis=1)
    return jnp.einsum('bnhd,bnhe->bhde', k * jnp.exp(b[:, -1:] - b), v)


def gla_head_norm(o, g):
    y = o * lax.rsqrt(jnp.mean(o * o, axis=-1, keepdims=True) + NORM_EPS) * g.astype(jnp.float32)
    return y.reshape(o.shape[0], o.shape[1], GLA_HEADS * GLA_DV)


def gla_mixer(q, k, v, gf, gb, kc, vc, gfc, gbc, qc, w_gate, b_gate, g_norm, rope):
    f32 = jnp.float32
    def heads(a, d):
        return a.astype(f32).reshape(a.shape[0], a.shape[1], GLA_HEADS, d)
    def log_gate(g_lr, direction):
        z = jnp.einsum('bnr,re->bne', g_lr.astype(f32), w_gate[direction].astype(f32)) + b_gate[direction].astype(f32)
        return heads(jax.nn.log_sigmoid(z) / GLA_TAU, GLA_DK)
    kch, vch = heads(kc, GLA_DK), heads(vc, GLA_DV)
    lgc_f, lgc_b = log_gate(gfc, 0), log_gate(gbc, 1)
    s_f = gla_final_state(kch, vch, lgc_f)
    s_b = gla_final_state(flip_seq(kch), flip_seq(vch), flip_seq(lgc_b))
    qh = apply_axial_rope(heads(q, GLA_DK), rope) * GLA_DK ** -0.5
    kh = apply_axial_rope(heads(k, GLA_DK), rope)
    vh = heads(v, GLA_DV)
    o = (gla_chunked(qh, kh, vh, log_gate(gf, 0), s_f)
         + flip_seq(gla_chunked(flip_seq(qh), flip_seq(kh), flip_seq(vh), flip_seq(log_gate(gb, 1)), s_b)))
    y = gla_head_norm(o, g_norm)
    y_ctx = None
    if qc is not None:
        qch = heads(qc, GLA_DK) * GLA_DK ** -0.5
        zero = jnp.zeros_like(s_f)
        oc = (gla_chunked(qch, kch, vch, lgc_f, zero)
              + flip_seq(gla_chunked(flip_seq(qch), flip_seq(kch), flip_seq(vch), flip_seq(lgc_b), zero)))
        y_ctx = gla_head_norm(oc, g_norm)
    return y, y_ctx


def na_mixer(q, k, v, kc, vc, qc, rpb, rows):
    f32 = jnp.float32
    bsz, n_tok, _ = q.shape
    scale = NA_DH ** -0.5
    kr = min(NA_KR, rows)
    def grid(a):
        return a.astype(f32).reshape(bsz, rows, GRID_W, NA_HEADS, NA_DH).transpose(0, 3, 1, 2, 4)
    def seq_heads(a):
        return a.astype(f32).reshape(a.shape[0], a.shape[1], NA_HEADS, NA_DH).transpose(0, 2, 1, 3)
    qg, kg, vg = grid(q) * scale, grid(k), grid(v)
    kch, vch = seq_heads(kc), seq_heads(vc)
    r = jnp.arange(rows)
    row_idx = jnp.clip(r - kr // 2, 0, rows - kr)[:, None] + jnp.arange(kr)[None, :]
    col = jnp.arange(GRID_W)
    cs = jnp.clip(col - NA_KC // 2, 0, GRID_W - NA_KC)
    col_mask = (col[None, :] >= cs[:, None]) & (col[None, :] < cs[:, None] + NA_KC)
    ro = row_idx - r[:, None] + (NA_KR - 1)
    co = jnp.clip(col[None, :] - col[:, None], -(NA_KC - 1), NA_KC - 1) + (NA_KC - 1)
    bias = rpb.astype(f32)[:, ro[:, None, :, None], co[None, :, None, :]]
    k_rows, v_rows = kg[:, :, row_idx], vg[:, :, row_idx]
    s_win = jnp.einsum('bhrqd,bhrikd->bhrqik', qg, k_rows) + bias[None]
    s_win = jnp.where(col_mask[:, None, :], s_win, -jnp.inf)
    s_ctx = jnp.einsum('bhrqd,bhnd->bhrqn', qg, kch)
    n_win = kr * GRID_W
    p = jax.nn.softmax(jnp.concatenate([s_win.reshape(bsz, NA_HEADS, rows, GRID_W, n_win), s_ctx], axis=-1), axis=-1)
    o = (jnp.einsum('bhrqik,bhrikd->bhrqd', p[..., :n_win].reshape(s_win.shape), v_rows)
         + jnp.einsum('bhrqn,bhnd->bhrqd', p[..., n_win:], vch))
    y = o.transpose(0, 2, 3, 1, 4).reshape(bsz, n_tok, NA_HEADS * NA_DH)
    y_ctx = None
    if qc is not None:
        pc = jax.nn.softmax(jnp.einsum('bhnd,bhmd->bhnm', seq_heads(qc) * scale, kch), axis=-1)
        oc = jnp.einsum('bhnm,bhmd->bhnd', pc, vch)
        y_ctx = oc.transpose(0, 2, 1, 3).reshape(qc.shape[0], qc.shape[1], NA_HEADS * NA_DH)
    return y, y_ctx


def s5_discretise(lam_re, lam_im, log_dt, b_re, b_im):
    dt = jnp.exp(log_dt)[:, None]
    mag = jnp.exp(lam_re * dt)
    ang = lam_im * dt
    lb_re, lb_im = mag * jnp.cos(ang), mag * jnp.sin(ang)
    num_re, num_im = lb_re - 1.0, lb_im
    den = lam_re * lam_re + lam_im * lam_im
    coef_re = ((num_re * lam_re + num_im * lam_im) / den)[..., None]
    coef_im = ((num_im * lam_re - num_re * lam_im) / den)[..., None]
    bb_re = coef_re * b_re - coef_im * b_im
    bb_im = coef_re * b_im + coef_im * b_re
    return lb_re, lb_im, bb_re, bb_im


def complex_affine_combine(e1, e2):
    a1r, a1i, b1r, b1i = e1
    a2r, a2i, b2r, b2i = e2
    return (a1r * a2r - a1i * a2i, a1r * a2i + a1i * a2r,
            a2r * b1r - a2i * b1i + b2r, a2r * b1i + a2i * b1r + b2i)


def s5_scan(u, disc, x0, reverse):
    lb_re, lb_im, bb_re, bb_im = disc
    bu_re = jnp.einsum('bngh,gph->bngp', u, bb_re)
    bu_im = jnp.einsum('bngh,gph->bngp', u, bb_im)
    a_re = jnp.broadcast_to(lb_re, bu_re.shape)
    a_im = jnp.broadcast_to(lb_im, bu_im.shape)
    ar, ai, xr, xi = lax.associative_scan(complex_affine_combine, (a_re, a_im, bu_re, bu_im), reverse=reverse, axis=1)
    if x0 is not None:
        x0r, x0i = x0[0][:, None], x0[1][:, None]
        xr, xi = xr + ar * x0r - ai * x0i, xi + ar * x0i + ai * x0r
    return xr, xi


def s5_readout(xr, xi, c_re, c_im):
    y = (jnp.einsum('bngp,ghp->bngh', xr, c_re.astype(jnp.float32))
         - jnp.einsum('bngp,ghp->bngh', xi, c_im.astype(jnp.float32)))
    return y.reshape(y.shape[0], y.shape[1], S5_GROUPS * S5_GROUP)


def s5_glu(y, w, b):
    g = jax.nn.gelu(y)
    return g * jax.nn.sigmoid(g @ w.astype(jnp.float32) + b.astype(jnp.float32))


def s5_mixer(u, uc, ctx_out, lam_re, lam_im, log_dt, b_re, b_im, c_re, c_im, d_skip, w_glu, b_glu):
    f32 = jnp.float32
    def groups(a):
        return a.astype(f32).reshape(a.shape[0], a.shape[1], S5_GROUPS, S5_GROUP)
    ul, ucg = groups(u), groups(uc)
    d_skip = d_skip.astype(f32)
    y = u.astype(f32) * d_skip
    yc = uc.astype(f32) * d_skip if ctx_out else None
    for direction, rev in enumerate((False, True)):
        disc = s5_discretise(lam_re[direction].astype(f32), lam_im[direction].astype(f32),
                             log_dt[direction].astype(f32), b_re[direction].astype(f32), b_im[direction].astype(f32))
        xcr, xci = s5_scan(ucg, disc, None, rev)
        fin = 0 if rev else -1
        xr, xi = s5_scan(ul, disc, (xcr[:, fin], xci[:, fin]), rev)
        y = y + s5_readout(xr, xi, c_re[direction], c_im[direction])
        if ctx_out:
            yc = yc + s5_readout(xcr, xci, c_re[direction], c_im[direction])
    return s5_glu(y, w_glu, b_glu), (s5_glu(yc, w_glu, b_glu) if ctx_out else None)


def centred_mean(u, w):
    n = u.shape[1]
    csum = jnp.concatenate([jnp.zeros_like(u[:, :1]), jnp.cumsum(u, axis=1)], axis=1)
    t = jnp.arange(n)
    lo = jnp.clip(t - w // 2, 0, n)
    hi = jnp.clip(t - w // 2 + w, 0, n)
    return (csum[:, hi] - csum[:, lo]) / (hi - lo).astype(u.dtype)[None, :, None]


def pool_mixer(u, w_pool, scale):
    uf = u.astype(jnp.float32)
    parts = jnp.split(uf, len(POOL_WINDOWS), axis=-1)
    outs = [jnp.einsum('bnc,ce->bne', centred_mean(g, w) - g, w_pool[i].astype(jnp.float32))
            for i, (g, w) in enumerate(zip(parts, POOL_WINDOWS))]
    return jnp.concatenate(outs, axis=-1) * scale.astype(jnp.float32)


def setup_inputs(seed: int = 0) -> dict:
    key = jax.random.key(seed)
    ks = jax.random.split(key, 32)
    f32 = jnp.float32
    def nrm(k, shape, std):
        return std * jax.random.normal(k, shape, f32)
    L = DEPTH
    lam_im0 = jnp.pi * jnp.arange(S5_STATE, dtype=f32)
    return {
        "x": nrm(ks[0], (BATCH, SEQ, D_MODEL), 1.0),
        "c": nrm(ks[1], (BATCH, D_MODEL), 1.0),
        "ctx": nrm(ks[2], (BATCH, CTX_LEN, D_MODEL), 1.0),
        "c_ctx": nrm(ks[3], (D_MODEL,), 1.0),
        "w_mod": nrm(ks[4], (L, D_MODEL, 3 * D_MODEL), 0.5 * D_MODEL ** -0.5),
        "b_mod": nrm(ks[5], (L, 3 * D_MODEL), 0.02),
        "g_pre": 1.0 + nrm(ks[6], (L, D_MODEL), 0.02),
        "g_post": 1.0 + nrm(ks[7], (L, D_MODEL), 0.02),
        "w_in": nrm(ks[8], (L, D_MODEL, N_IN_COLS), D_MODEL ** -0.5),
        "w_out": nrm(ks[9], (L, MIX_W, D_MODEL), MIX_W ** -0.5),
        "gla_w_gate": nrm(ks[10], (L, 2, GLA_GATE_RANK, GLA_HEADS * GLA_DK), GLA_GATE_RANK ** -0.5),
        "gla_b_gate": nrm(ks[11], (L, 2, GLA_HEADS * GLA_DK), 0.1),
        "gla_g_norm": 1.0 + nrm(ks[12], (L, GLA_DV), 0.02),
        "na_rpb": nrm(ks[13], (L, NA_HEADS, 2 * NA_KR - 1, 2 * NA_KC - 1), 0.1),
        "s5_lam_re": -0.5 + nrm(ks[14], (L, 2, S5_GROUPS, S5_STATE), 0.01),
        "s5_lam_im": lam_im0 + nrm(ks[15], (L, 2, S5_GROUPS, S5_STATE), 0.01),
        "s5_log_dt": jax.random.uniform(ks[16], (L, 2, S5_GROUPS), f32, math.log(1e-3), math.log(1e-1)),
        "s5_b_re": nrm(ks[17], (L, 2, S5_GROUPS, S5_STATE, S5_GROUP), (2.0 * S5_GROUP) ** -0.5),
        "s5_b_im": nrm(ks[18], (L, 2, S5_GROUPS, S5_STATE, S5_GROUP), (2.0 * S5_GROUP) ** -0.5),
        "s5_c_re": nrm(ks[19], (L, 2, S5_GROUPS, S5_GROUP, S5_STATE), S5_STATE ** -0.5),
        "s5_c_im": nrm(ks[20], (L, 2, S5_GROUPS, S5_GROUP, S5_STATE), S5_STATE ** -0.5),
        "s5_d": nrm(ks[21], (L, GROUP_W), 1.0),
        "s5_w_glu": nrm(ks[22], (L, GROUP_W, GROUP_W), GROUP_W ** -0.5),
        "s5_b_glu": nrm(ks[23], (L, GROUP_W), 0.02),
        "pool_w": nrm(ks[24], (L, len(POOL_WINDOWS), POOL_GW, POOL_GW), POOL_GW ** -0.5),
        "pool_scale": 1.0 + nrm(ks[25], (L, GROUP_W), 0.02),
    }


def reference(x, c, ctx, c_ctx, w_mod, b_mod, g_pre, g_post, w_in, w_out,
              gla_w_gate, gla_b_gate, gla_g_norm, na_rpb,
              s5_lam_re, s5_lam_im, s5_log_dt, s5_b_re, s5_b_im, s5_c_re, s5_c_im, s5_d, s5_w_glu, s5_b_glu,
              pool_w, pool_scale):
    dt = x.dtype
    n_tok = x.shape[1]
    rows = n_tok // GRID_W
    rope = axial_rope_tables(n_tok)
    xc = ctx
    for l in range(DEPTH):
        last = l == DEPTH - 1
        shift, scale, gate = jnp.split(jax.nn.silu(c) @ w_mod[l] + b_mod[l], 3, axis=-1)
        shift_c, scale_c, gate_c = jnp.split(jax.nn.silu(c_ctx) @ w_mod[l] + b_mod[l], 3, axis=-1)
        h = rmsnorm(x, g_pre[l]) * (1.0 + scale[:, None]) + shift[:, None]
        hc = rmsnorm(xc, g_pre[l]) * (1.0 + scale_c) + shift_c
        p_kv, p_q = jnp.split(h @ w_in[l], [N_CTX_COLS], axis=-1)
        gla_k, gla_v, gla_gf, gla_gb, na_k, na_v, s5_u = split_cols(p_kv, CTX_SPLIT)
        gla_q, na_q, pool_u, gate_cols = split_cols(p_q, LAT_SPLIT)
        if last:
            pc_kv = hc @ w_in[l][:, :N_CTX_COLS]
            gla_qc = na_qc = pool_uc = gate_cols_c = None
        else:
            pc_kv, pc_q = jnp.split(hc @ w_in[l], [N_CTX_COLS], axis=-1)
            gla_qc, na_qc, pool_uc, gate_cols_c = split_cols(pc_q, LAT_SPLIT)
        gla_kc, gla_vc, gla_gfc, gla_gbc, na_kc, na_vc, s5_uc = split_cols(pc_kv, CTX_SPLIT)
        y_gla, yc_gla = gla_mixer(gla_q, gla_k, gla_v, gla_gf, gla_gb, gla_kc, gla_vc, gla_gfc, gla_gbc, gla_qc,
                                  gla_w_gate[l], gla_b_gate[l], gla_g_norm[l], rope)
        y_na, yc_na = na_mixer(na_q, na_k, na_v, na_kc, na_vc, na_qc, na_rpb[l], rows)
        y_s5, yc_s5 = s5_mixer(s5_u, s5_uc, not last, s5_lam_re[l], s5_lam_im[l], s5_log_dt[l], s5_b_re[l],
                               s5_b_im[l], s5_c_re[l], s5_c_im[l], s5_d[l], s5_w_glu[l], s5_b_glu[l])
        y_pool = pool_mixer(pool_u, pool_w[l], pool_scale[l])
        y = jnp.concatenate([y_gla, y_na, y_s5, y_pool], axis=-1).astype(dt) * jax.nn.silu(gate_cols)
        x = x + gate[:, None] * rmsnorm(y @ w_out[l], g_post[l])
        if not last:
            yc_pool = pool_mixer(pool_uc, pool_w[l], pool_scale[l])
            yc = jnp.concatenate([yc_gla, yc_na, yc_s5, yc_pool], axis=-1).astype(dt) * jax.nn.silu(gate_cols_c)
            xc = xc + gate_c * rmsnorm(yc @ w_out[l], g_post[l])
    return x
```

```python
import functools
import math

import numpy as np
import jax
import jax.numpy as jnp
from jax import lax
from jax.experimental import pallas as pl
from jax.experimental.pallas import tpu as pltpu

F32 = jnp.float32
BF16 = jnp.bfloat16

D_MODEL = 1024
GRID_W = 64
CTX_LEN = 256
GROUP_W = 256
MIX_W = 1024
NORM_EPS = 1e-6
GLA_HEADS = 4
GLA_DV = 64
GLA_DK = 32
GLA_RANK = 16
GLA_TAU = 16.0
ROPE_BASE = 10000.0
NA_HEADS = 4
NA_DH = 64
NA_KR = 8
NA_KC = 16
S5_HG = 16
S5_G = 16
S5_P = 64
POOL_WINDOWS = (2, 4, 8, 16)
POOL_GW = 64

TILE = 256
GLA_C = 128
S5_T = 16
NA_ROWS = TILE // GRID_W
NEG_BIG = -1e30

_COL = {}
_off = 0
for _name, _w in (("gq", 128), ("gk", 128), ("gv", 256), ("gg", 128), ("nq", 256), ("nk", 256),
                  ("nv", 256), ("su", 256), ("pu", 256), ("gate", 1024)):
    _COL[_name] = (_off, _off + _w)
    _off += _w
N_PERM_COLS = _off

_VMEM_LIMIT = 56 * 1024 * 1024


def _cp(sem, vmem=_VMEM_LIMIT):
    return pltpu.CompilerParams(dimension_semantics=sem, vmem_limit_bytes=vmem)


def _dot(a, b):
    return jnp.dot(a, b, preferred_element_type=F32)


def _dot_nt(a, b):
    return lax.dot_general(a, b, (((1,), (1,)), ((), ())), preferred_element_type=F32)


def _split_bf16(a):
    hi = a.astype(BF16)
    lo = (a - hi.astype(F32)).astype(BF16)
    return hi, lo


def _mod_kernel(s_ref, w_ref, b_ref, o_ref):
    s = s_ref[...]
    a = (s * jax.nn.sigmoid(s)).astype(BF16)
    o_ref[0] = _dot(a, w_ref[0].astype(BF16)) + b_ref[0]


def _modulation(c_rows, w_mod, b_mod):
    n_l, d, d3 = w_mod.shape
    return pl.pallas_call(
        _mod_kernel,
        out_shape=jax.ShapeDtypeStruct((n_l, 8, d3), F32),
        grid=(n_l, d3 // d),
        in_specs=[pl.BlockSpec((8, d), lambda l, j: (0, 0)),
                  pl.BlockSpec((1, d, d), lambda l, j: (l, 0, j)),
                  pl.BlockSpec((1, 1, d), lambda l, j: (l, 0, j))],
        out_specs=pl.BlockSpec((1, 8, d), lambda l, j: (l, 0, j)),
        compiler_params=_cp(("parallel", "parallel")),
        name="adaln_modulation",
    )(c_rows, w_mod, b_mod.reshape(n_l, 1, d3))


def _inproj_kernel(x_ref, xc_ref, mod_ref, gpre_ref, w_ref, wg_ref, bg_ref, cos_ref, sa_ref, sb_ref,
                   qk_ref, v_ref, lg_ref, nq_ref, nk_ref, nv_ref, su_ref, pu_ref, gate_ref, *, n_batch):
    b = pl.program_id(0)
    j = pl.program_id(1)
    is_ctx = j == 0
    xt = jnp.where(is_ctx, xc_ref[0], x_ref[0])
    m = mod_ref[pl.ds(jnp.where(is_ctx, n_batch, b), 1), :]
    shift = m[:, 0:D_MODEL]
    scale = m[:, D_MODEL:2 * D_MODEL]
    ms = jnp.mean(xt * xt, axis=-1, keepdims=True)
    h = xt * lax.rsqrt(ms + NORM_EPS) * gpre_ref[...] * (1.0 + scale) + shift
    hb = h.astype(BF16)

    def proj(name):
        lo, hi = _COL[name]
        return _dot(hb, w_ref[:, lo:hi])

    cos = jnp.where(is_ctx, 1.0, cos_ref[...])
    sa = jnp.where(is_ctx, 0.0, sa_ref[...])
    sb = jnp.where(is_ctx, 0.0, sb_ref[...])

    def rope(t):
        return t * cos + pltpu.roll(t, 128 - 8, 1) * sa + pltpu.roll(t, 8, 1) * sb

    qk_ref[0, :, 0:128] = rope(proj("gq")) * (GLA_DK ** -0.5)
    qk_ref[0, :, 128:256] = rope(proj("gk"))
    v_ref[0] = proj("gv")
    z = _dot(proj("gg").astype(BF16), wg_ref[...]) + bg_ref[...]
    lg_ref[0] = (jnp.minimum(z, 0.0) - jnp.log(1.0 + jnp.exp(-jnp.abs(z)))) * (1.0 / GLA_TAU)
    nq_ref[0] = proj("nq") * (NA_DH ** -0.5)
    nk_ref[0] = proj("nk")
    nv_ref[0] = proj("nv")
    su_ref[0] = proj("su")
    pu_ref[0] = proj("pu")
    gate_ref[0] = proj("gate")


def _inproj(x, xc, mod_l, g_pre, w_perm, wg, bg, rope_tabs):
    n_b, n_lat, d = x.shape
    n_tiles = (n_lat + CTX_LEN) // TILE
    t_all = n_lat + CTX_LEN
    lat_map = lambda b, j: (b, jnp.maximum(j - 1, 0), 0)
    tab_map = lambda b, j: (jnp.maximum(j - 1, 0), 0)
    const2 = lambda b, j: (0, 0)
    out_map = lambda b, j: (b, j, 0)
    widths = (256, 256, 256, 256, 256, 256, 256, 256, 1024)
    return pl.pallas_call(
        functools.partial(_inproj_kernel, n_batch=n_b),
        out_shape=[jax.ShapeDtypeStruct((n_b, t_all, w), F32) for w in widths],
        grid=(n_b, n_tiles),
        in_specs=[pl.BlockSpec((1, TILE, d), lat_map),
                  pl.BlockSpec((1, CTX_LEN, d), lambda b, j: (b, 0, 0)),
                  pl.BlockSpec((8, 3 * d), const2),
                  pl.BlockSpec((1, d), const2),
                  pl.BlockSpec((d, N_PERM_COLS), const2),
                  pl.BlockSpec((128, 256), const2),
                  pl.BlockSpec((1, 256), const2),
                  pl.BlockSpec((TILE, 128), tab_map),
                  pl.BlockSpec((TILE, 128), tab_map),
                  pl.BlockSpec((TILE, 128), tab_map)],
        out_specs=[pl.BlockSpec((1, TILE, w), out_map) for w in widths],
        compiler_params=_cp(("parallel", "arbitrary")),
        name="modnorm_inproj",
    )(x, xc, mod_l, g_pre.reshape(1, d), w_perm, wg, bg, *rope_tabs)


def _gla_kernel(qk_ref, v_ref, lg_ref, gn_ref, y_ref, st_ref, *, n_chunks, n_ctx_chunks):
    c_len = GLA_C
    row = lax.broadcasted_iota(jnp.int32, (c_len, c_len), 0)
    col = lax.broadcasted_iota(jnp.int32, (c_len, c_len), 1)
    tri_f = (col <= row).astype(BF16)
    tri_b = (col >= row).astype(BF16)
    arow = lax.broadcasted_iota(jnp.int32, (4 * c_len, c_len), 0) % c_len
    acol = lax.broadcasted_iota(jnp.int32, (4 * c_len, c_len), 1)
    amask_f = acol <= arow
    amask_b = acol >= arow
    head_k = lax.broadcasted_iota(jnp.int32, (1, 128), 1) // GLA_DK
    head_v = lax.broadcasted_iota(jnp.int32, (1, 256), 1) // GLA_DV
    bd_mask = (lax.broadcasted_iota(jnp.int32, (256, 128), 0) // GLA_DV
               == lax.broadcasted_iota(jnp.int32, (256, 128), 1) // GLA_DK)
    ones_bd = jnp.where(lax.broadcasted_iota(jnp.int32, (256, 256), 0) // GLA_DV
                        == lax.broadcasted_iota(jnp.int32, (256, 256), 1) // GLA_DV,
                        1.0 / GLA_DV, 0.0).astype(BF16)
    mid = c_len // 2

    def chunk(c, fwd):
        r0 = pl.multiple_of(c * c_len, c_len)
        q = qk_ref[0, pl.ds(r0, c_len), 0:128]
        k = qk_ref[0, pl.ds(r0, c_len), 128:256]
        v = v_ref[0, pl.ds(r0, c_len), :]
        lg = lg_ref[0, pl.ds(r0, c_len), 0:128] if fwd else lg_ref[0, pl.ds(r0, c_len), 128:256]
        lg_hi, lg_lo = _split_bf16(lg)
        tri = tri_f if fwd else tri_b
        cum = _dot(tri, lg_hi) + _dot(tri, lg_lo)
        c_ref = cum[mid:mid + 1, :]
        c_end = cum[c_len - 1:c_len, :] if fwd else cum[0:1, :]
        qt = q * jnp.exp(cum - c_ref)
        kt = k * jnp.exp(c_ref - cum)
        qh = (qt * jnp.exp(c_ref)).astype(BF16)
        kh = (kt * jnp.exp(c_end - c_ref)).astype(BF16)
        vb = v.astype(BF16)
        qs = jnp.concatenate([jnp.where(head_k == hh, qt, 0.0) for hh in range(GLA_HEADS)], axis=0).astype(BF16)
        att = _dot_nt(qs, kt.astype(BF16))
        att = jnp.where(amask_f if fwd else amask_b, att, 0.0).astype(BF16)
        r = _dot(att, vb)
        o = jnp.where(head_v == 0, r[0:c_len], 0.0)
        for hh in range(1, GLA_HEADS):
            o = o + jnp.where(head_v == hh, r[hh * c_len:(hh + 1) * c_len], 0.0)
        st = st_ref[...]
        o = o + _dot_nt(qh, st.astype(BF16))
        upd = _dot(v.T.astype(BF16), kh)
        st_ref[...] = st * jnp.exp(c_end) + jnp.where(bd_mask, upd, 0.0)
        return r0, o

    def fwd_body(c, carry):
        r0, o = chunk(c, True)
        y_ref[0, pl.ds(r0, c_len), :] = o
        return carry

    def bwd_body(i, carry):
        c = jnp.where(i < n_ctx_chunks, n_ctx_chunks - 1 - i, n_chunks - 1 - (i - n_ctx_chunks))
        r0, o = chunk(c, False)
        o = o + y_ref[0, pl.ds(r0, c_len), :]
        sq_hi, sq_lo = _split_bf16(o * o)
        ms = _dot(sq_hi, ones_bd) + _dot(sq_lo, ones_bd)
        y_ref[0, pl.ds(r0, c_len), :] = o * lax.rsqrt(ms + NORM_EPS) * gn_ref[...]
        return carry

    st_ref[...] = jnp.zeros_like(st_ref)
    lax.fori_loop(0, n_chunks, fwd_body, 0)
    st_ref[...] = jnp.zeros_like(st_ref)
    lax.fori_loop(0, n_chunks, bwd_body, 0)


def _gla(qk, v, lg, g_norm):
    n_b, t_all, _ = qk.shape
    blk = lambda w: pl.BlockSpec((1, t_all, w), lambda b: (b, 0, 0))
    return pl.pallas_call(
        functools.partial(_gla_kernel, n_chunks=t_all // GLA_C, n_ctx_chunks=CTX_LEN // GLA_C),
        out_shape=jax.ShapeDtypeStruct((n_b, t_all, 256), F32),
        grid=(n_b,),
        in_specs=[blk(256), blk(256), blk(256), pl.BlockSpec((1, 256), lambda b: (0, 0))],
        out_specs=blk(256),
        scratch_shapes=[pltpu.VMEM((256, 128), F32)],
        compiler_params=_cp(("parallel",)),
        name="gla_mixer",
    )(qk, v, lg, jnp.tile(g_norm, GLA_HEADS).reshape(1, 256))


def _na_kernel(q_ref, k0_ref, k1_ref, k2_ref, v0_ref, v1_ref, v2_ref, kc_ref, vc_ref, bias_ref, o_ref):
    j = pl.program_id(1)

    def attend(with_window):
        for hh in range(NA_HEADS):
            lo, hi = hh * NA_DH, (hh + 1) * NA_DH
            qh = q_ref[0, :, lo:hi].astype(BF16)
            scores = [_dot_nt(qh, kc_ref[0, :, lo:hi].astype(BF16))]
            values = [vc_ref]
            if with_window:
                for i, (kr, vr) in enumerate(((k0_ref, v0_ref), (k1_ref, v1_ref), (k2_ref, v2_ref))):
                    s = _dot_nt(qh, kr[0, :, lo:hi].astype(BF16)) + bias_ref[0, hh, :, i * TILE:(i + 1) * TILE]
                    scores.append(s)
                    values.append(vr)
            mx = scores[0]
            for s in scores[1:]:
                mx = jnp.maximum(mx, s)
            mx = jnp.max(mx, axis=-1, keepdims=True)
            den = None
            acc = None
            for s, vr in zip(scores, values):
                p = jnp.exp(s - mx)
                d = jnp.sum(p, axis=-1, keepdims=True)
                a = _dot(p.astype(BF16), vr[0, :, lo:hi].astype(BF16))
                den = d if den is None else den + d
                acc = a if acc is None else acc + a
            o_ref[0, :, lo:hi] = acc / den

    @pl.when(j == 0)
    def _():
        attend(False)

    @pl.when(j > 0)
    def _():
        attend(True)


def _na(nq, nk, nv, bias_tab):
    n_b, t_all, _ = nq.shape
    n_tiles = t_all // TILE
    n_groups = n_tiles - 1

    def nbr(delta):
        return lambda b, j: (b, jnp.clip(j + delta, 1, n_tiles - 1), 0)

    def variant(b, j):
        return (jnp.where(j <= 1, 0, jnp.where(j == n_groups, 2, 1)), 0, 0, 0)

    tile = lambda f: pl.BlockSpec((1, TILE, 256), f)
    return pl.pallas_call(
        _na_kernel,
        out_shape=jax.ShapeDtypeStruct((n_b, t_all, 256), F32),
        grid=(n_b, n_tiles),
        in_specs=[tile(lambda b, j: (b, j, 0)),
                  tile(nbr(-1)), tile(nbr(0)), tile(nbr(1)),
                  tile(nbr(-1)), tile(nbr(0)), tile(nbr(1)),
                  tile(lambda b, j: (b, 0, 0)), tile(lambda b, j: (b, 0, 0)),
                  pl.BlockSpec((1, NA_HEADS, TILE, 3 * TILE), variant)],
        out_specs=tile(lambda b, j: (b, j, 0)),
        compiler_params=_cp(("parallel", "arbitrary")),
        name="na_mixer",
    )(nq, nk, nk, nk, nv, nv, nv, nk, nv, bias_tab)


def _na_bias_table(rpb, rows):
    n_qr = NA_ROWS
    n_kr = 3 * NA_ROWS
    col = np.arange(GRID_W)
    cs = np.clip(col - NA_KC // 2, 0, GRID_W - NA_KC)
    col_mask = (col[None, :] >= cs[:, None]) & (col[None, :] < cs[:, None] + NA_KC)
    co = np.clip(col[None, :] - col[:, None], -(NA_KC - 1), NA_KC - 1) + (NA_KC - 1)
    onehot = (co[None] == np.arange(2 * NA_KC - 1)[:, None, None]).astype(np.float32)
    toe = jnp.einsum("hrc,cqk->hrqk", rpb.astype(F32), jnp.asarray(onehot),
                     precision=lax.Precision.HIGHEST)
    ro = np.arange(n_kr)[None, :] - np.arange(n_qr)[:, None] + (NA_KR - 1 - NA_ROWS)
    blocks = toe[:, ro]
    n_groups = rows // NA_ROWS
    kr_w = min(NA_KR, rows)
    valid = np.zeros((3, n_qr, n_kr), dtype=bool)
    for var, g in enumerate((0, 1, n_groups - 1)):
        for i in range(n_qr):
            r = g * NA_ROWS + i
            start = int(np.clip(r - kr_w // 2, 0, rows - kr_w))
            for jj in range(n_kr):
                kr_abs = g * NA_ROWS - NA_ROWS + jj
                valid[var, i, jj] = start <= kr_abs < start + kr_w
    full_valid = valid[:, None, :, :, None, None] & col_mask[None, None, None, None]
    tab = jnp.where(jnp.asarray(full_valid), blocks[None], NEG_BIG)
    tab = tab.transpose(0, 1, 2, 4, 3, 5)
    return tab.reshape(3, NA_HEADS, n_qr * GRID_W, n_kr * GRID_W)


def _s5_local_kernel(a_ref, bexp_ref, erf_ref, eif_ref, erb_ref, eib_ref):
    outs = (erf_ref, eif_ref, erb_ref, eib_ref)
    for g in range(S5_G):
        e = _dot(a_ref[g, 0], bexp_ref[g])
        for kind in range(4):
            outs[kind][0, :, g * S5_P:(g + 1) * S5_P] = e[:, kind * S5_P:(kind + 1) * S5_P]


def _s5_scan_kernel(erf_ref, eif_ref, erb_ref, eib_ref, pw_ref, xrf_ref, xif_ref, xrb_ref, xib_ref,
                    *, n_chunks, n_ctx_chunks):
    afr, afi, abr, abi = pw_ref[0:1, :], pw_ref[1:2, :], pw_ref[2:3, :], pw_ref[3:4, :]

    def body(i, carry):
        fr, fi, br, bi = carry
        cf = i
        cb = jnp.where(i < n_ctx_chunks, n_ctx_chunks - 1 - i, n_chunks - 1 - (i - n_ctx_chunks))
        xrf_ref[0, pl.ds(cf, 1), :] = fr
        xif_ref[0, pl.ds(cf, 1), :] = fi
        xrb_ref[0, pl.ds(cb, 1), :] = br
        xib_ref[0, pl.ds(cb, 1), :] = bi
        er = erf_ref[0, pl.ds(cf, 1), :]
        ei = eif_ref[0, pl.ds(cf, 1), :]
        gr = erb_ref[0, pl.ds(cb, 1), :]
        gi = eib_ref[0, pl.ds(cb, 1), :]
        return (afr * fr - afi * fi + er, afr * fi + afi * fr + ei,
                abr * br - abi * bi + gr, abr * bi + abi * br + gi)

    z = jnp.zeros((1, S5_G * S5_P), F32)
    lax.fori_loop(0, n_chunks, body, (z, z, z, z))


def _s5_out_kernel(a_ref, ktoe_ref, cexp_ref, xrf_ref, xif_ref, xrb_ref, xib_ref, y_ref):
    xs = (xrf_ref, xif_ref, xrb_ref, xib_ref)
    for g in range(S5_G):
        y = _dot(a_ref[g, 0], ktoe_ref[g])
        for kind in range(4):
            y = y + _dot(xs[kind][0, :, g * S5_P:(g + 1) * S5_P].astype(BF16), cexp_ref[g, kind])
        y_ref[g, 0] = y


def _s5(su, tabs):
    ktoe, bexp, cexp, pw16 = tabs
    n_b, t_all, _ = su.shape
    n_chunks = t_all // S5_T
    a = su.astype(BF16).reshape(n_b, n_chunks, S5_T, S5_G, S5_HG).transpose(3, 0, 1, 2, 4)
    a = a.reshape(S5_G, n_b, n_chunks, S5_T * S5_HG)
    a_spec = pl.BlockSpec((S5_G, 1, n_chunks, 256), lambda b: (0, b, 0, 0))
    e_spec = pl.BlockSpec((1, n_chunks, S5_G * S5_P), lambda b: (b, 0, 0))
    e_shape = jax.ShapeDtypeStruct((n_b, n_chunks, S5_G * S5_P), F32)
    full = lambda shape: pl.BlockSpec(shape, lambda b: (0,) * len(shape))
    es = pl.pallas_call(
        _s5_local_kernel, out_shape=[e_shape] * 4, grid=(n_b,),
        in_specs=[a_spec, full((S5_G, 256, 256))], out_specs=[e_spec] * 4,
        compiler_params=_cp(("parallel",)), name="s5_local_states",
    )(a, bexp)
    xs = pl.pallas_call(
        functools.partial(_s5_scan_kernel, n_chunks=n_chunks, n_ctx_chunks=CTX_LEN // S5_T),
        out_shape=[e_shape] * 4, grid=(n_b,),
        in_specs=[e_spec] * 4 + [full((4, S5_G * S5_P))], out_specs=[e_spec] * 4,
        compiler_params=_cp(("parallel",)), name="s5_chunk_scan",
    )(*es, pw16)
    y = pl.pallas_call(
        _s5_out_kernel, out_shape=jax.ShapeDtypeStruct((S5_G, n_b, n_chunks, 256), F32), grid=(n_b,),
        in_specs=[a_spec, full((S5_G, 256, 256)), full((S5_G, 4, S5_P, 256))] + [e_spec] * 4,
        out_specs=a_spec,
        compiler_params=_cp(("parallel",)), name="s5_outputs",
    )(a, ktoe, cexp, *xs)
    y = y.reshape(S5_G, n_b, n_chunks, S5_T, S5_HG).transpose(1, 2, 3, 0, 4)
    return y.reshape(n_b, t_all, GROUP_W)


def _s5_tables(lam_re, lam_im, log_dt, b_re, b_im, c_re, c_im, d_skip):
    hp = lax.Precision.HIGHEST
    lam_re, lam_im, log_dt = lam_re.astype(F32), lam_im.astype(F32), log_dt.astype(F32)
    b_re, b_im, c_re, c_im = b_re.astype(F32), b_im.astype(F32), c_re.astype(F32), c_im.astype(F32)
    dt = jnp.exp(log_dt)[..., None]
    n = jnp.arange(S5_T + 1, dtype=F32)[:, None, None, None]
    pw_re = jnp.exp(n * lam_re * dt) * jnp.cos(n * lam_im * dt)
    pw_im = jnp.exp(n * lam_re * dt) * jnp.sin(n * lam_im * dt)
    lb_re, lb_im = pw_re[1], pw_im[1]
    num_re, num_im = lb_re - 1.0, lb_im
    den = lam_re * lam_re + lam_im * lam_im
    coef_re = ((num_re * lam_re + num_im * lam_im) / den)[..., None]
    coef_im = ((num_im * lam_re - num_re * lam_im) / den)[..., None]
    bb_re = coef_re * b_re - coef_im * b_im
    bb_im = coef_re * b_im + coef_im * b_re
    cp_re = c_re[None] * pw_re[:, :, :, None, :] - c_im[None] * pw_im[:, :, :, None, :]
    cp_im = c_re[None] * pw_im[:, :, :, None, :] + c_im[None] * pw_re[:, :, :, None, :]
    lag = (jnp.einsum("ndgop,dgpi->ndgoi", cp_re, bb_re, precision=hp)
           - jnp.einsum("ndgop,dgpi->ndgoi", cp_im, bb_im, precision=hp))
    s_idx = np.arange(S5_T)[:, None]
    t_idx = np.arange(S5_T)[None, :]
    kf = jnp.where(jnp.asarray(t_idx >= s_idx)[:, :, None, None, None],
                   lag[:, 0][np.clip(t_idx - s_idx, 0, S5_T)], 0.0)
    kb = jnp.where(jnp.asarray(s_idx >= t_idx)[:, :, None, None, None],
                   lag[:, 1][np.clip(s_idx - t_idx, 0, S5_T)], 0.0)
    skip = (jnp.asarray(np.eye(S5_T, dtype=np.float32))[:, :, None, None, None]
            * jnp.asarray(np.eye(S5_HG, dtype=np.float32))[None, None, None]
            * d_skip.astype(F32).reshape(S5_G, S5_HG)[None, None, :, None, :])
    ktoe = (kf + kb + skip).transpose(2, 0, 4, 1, 3).reshape(S5_G, S5_T * S5_HG, S5_T * S5_HG)
    pf_re, pf_im = pw_re[S5_T - 1 - np.arange(S5_T), 0], pw_im[S5_T - 1 - np.arange(S5_T), 0]
    pb_re, pb_im = pw_re[np.arange(S5_T), 1], pw_im[np.arange(S5_T), 1]

    def cmul_b(p_re, p_im, d):
        re = p_re[..., None] * bb_re[d][None] - p_im[..., None] * bb_im[d][None]
        im = p_re[..., None] * bb_im[d][None] + p_im[..., None] * bb_re[d][None]
        to = lambda z: z.transpose(1, 0, 3, 2).reshape(S5_G, S5_T * S5_HG, S5_P)
        return to(re), to(im)

    bexp = jnp.concatenate(cmul_b(pf_re, pf_im, 0) + cmul_b(pb_re, pb_im, 1), axis=-1)
    qf_re, qf_im = pw_re[np.arange(S5_T) + 1, 0], pw_im[np.arange(S5_T) + 1, 0]
    qb_re, qb_im = pw_re[S5_T - np.arange(S5_T), 1], pw_im[S5_T - np.arange(S5_T), 1]

    def cmul_c(p_re, p_im, d):
        re = c_re[d][None] * p_re[:, :, None, :] - c_im[d][None] * p_im[:, :, None, :]
        im = c_re[d][None] * p_im[:, :, None, :] + c_im[d][None] * p_re[:, :, None, :]
        to = lambda z: z.transpose(1, 3, 0, 2).reshape(S5_G, S5_P, S5_T * S5_HG)
        return to(re), to(-im)

    cexp = jnp.stack(cmul_c(qf_re, qf_im, 0) + cmul_c(qb_re, qb_im, 1), axis=1)
    pw16 = jnp.stack([pw_re[S5_T, 0].reshape(-1), pw_im[S5_T, 0].reshape(-1),
                      pw_re[S5_T, 1].reshape(-1), pw_im[S5_T, 1].reshape(-1)], axis=0)
    return ktoe.astype(BF16), bexp.astype(BF16), cexp.astype(BF16), pw16


def _pool_kernel(u_ref, w_ref, scale_ref, o_ref, pad_ref, *, n_lat):
    pad = 8
    seg = ((pad, 0, CTX_LEN), (pad + CTX_LEN + 2 * pad, CTX_LEN, n_lat))
    pad_ref[...] = jnp.zeros_like(pad_ref)
    for p0, u0, n in seg:
        pad_ref[p0:p0 + n, :] = u_ref[0, u0:u0 + n, :]
    for p0, u0, n in seg:
        for t0 in range(0, n, TILE):
            t = t0 + lax.broadcasted_iota(jnp.int32, (TILE, 1), 0)
            outs = []
            for i, w in enumerate(POOL_WINDOWS):
                lo, hi = i * POOL_GW, (i + 1) * POOL_GW
                acc = None
                for jj in range(w):
                    s = p0 + t0 + jj - w // 2
                    piece = pad_ref[s:s + TILE, lo:hi]
                    acc = piece if acc is None else acc + piece
                cnt = (jnp.clip(t - w // 2 + w, 0, n) - jnp.clip(t - w // 2, 0, n)).astype(F32)
                g = u_ref[0, u0 + t0:u0 + t0 + TILE, lo:hi]
                outs.append(_dot((acc / cnt - g).astype(BF16), w_ref[i].astype(BF16)))
            o_ref[0, u0 + t0:u0 + t0 + TILE, :] = jnp.concatenate(outs, axis=-1) * scale_ref[...]


def _pool(pu, w_pool, scale):
    n_b, t_all, _ = pu.shape
    blk = pl.BlockSpec((1, t_all, 256), lambda b: (b, 0, 0))
    return pl.pallas_call(
        functools.partial(_pool_kernel, n_lat=t_all - CTX_LEN),
        out_shape=jax.ShapeDtypeStruct((n_b, t_all, 256), F32),
        grid=(n_b,),
        in_specs=[blk, pl.BlockSpec((len(POOL_WINDOWS), POOL_GW, POOL_GW), lambda b: (0, 0, 0)),
                  pl.BlockSpec((1, 256), lambda b: (0, 0))],
        out_specs=blk,
        scratch_shapes=[pltpu.VMEM((t_all + 32, 256), F32)],
        compiler_params=_cp(("parallel",)),
        name="pool_mixer",
    )(pu, w_pool, scale.reshape(1, 256))


def _gelu_tanh(x):
    return 0.5 * x * (1.0 + jnp.tanh(math.sqrt(2.0 / math.pi) * (x + 0.044715 * (x * x * x))))


def _outproj_kernel(ygla_ref, yna_ref, ys5_ref, ypool_ref, gate_ref, x_ref, xc_ref, mod_ref, wout_ref, gpost_ref,
                    wglu_ref, bglu_ref, *out_refs, n_batch, with_ctx):
    b = pl.program_id(0)
    j = pl.program_id(1)
    is_ctx = jnp.logical_and(j == 0, with_ctx)
    gate_m = mod_ref[pl.ds(jnp.where(is_ctx, n_batch, b), 1), :][:, 2 * D_MODEL:3 * D_MODEL]
    xt = jnp.where(is_ctx, xc_ref[0], x_ref[0]) if with_ctx else x_ref[0]
    gc = gate_ref[0]
    sg = gc * jax.nn.sigmoid(gc)
    g5 = _gelu_tanh(ys5_ref[0])
    s5o = g5 * jax.nn.sigmoid(_dot(g5.astype(BF16), wglu_ref[...]) + bglu_ref[...])
    parts = (ygla_ref[0], yna_ref[0], s5o, ypool_ref[0])
    acc = None
    for i, part in enumerate(parts):
        lo, hi = i * GROUP_W, (i + 1) * GROUP_W
        a = _dot((part * sg[:, lo:hi]).astype(BF16), wout_ref[lo:hi, :])
        acc = a if acc is None else acc + a
    ms = jnp.mean(acc * acc, axis=-1, keepdims=True)
    res = xt + gate_m * (acc * lax.rsqrt(ms + NORM_EPS) * gpost_ref[...])
    if with_ctx:
        xo_ref, xco_ref = out_refs

        @pl.when(j == 0)
        def _():
            xco_ref[0] = res

        @pl.when(j > 0)
        def _():
            xo_ref[0] = res
    else:
        out_refs[0][0] = res


def _outproj(ys, gate_cols, x, xc, mod_l, w_out, g_post, w_glu, b_glu, with_ctx):
    n_b, n_lat, d = x.shape
    n_lat_tiles = n_lat // TILE
    off = 0 if with_ctx else 1
    n_steps = n_lat_tiles + (1 if with_ctx else 0)
    mix_map = lambda b, j: (b, j + off, 0)
    lat_map = (lambda b, j: (b, jnp.maximum(j - 1, 0), 0)) if with_ctx else (lambda b, j: (b, j, 0))
    ctx_map = lambda b, j: (b, 0, 0)
    const2 = lambda b, j: (0, 0)
    out_shape = [jax.ShapeDtypeStruct(x.shape, F32)]
    out_specs = [pl.BlockSpec((1, TILE, d), lat_map)]
    if with_ctx:
        out_shape.append(jax.ShapeDtypeStruct(xc.shape, F32))
        out_specs.append(pl.BlockSpec((1, CTX_LEN, d), ctx_map))
    outs = pl.pallas_call(
        functools.partial(_outproj_kernel, n_batch=n_b, with_ctx=with_ctx),
        out_shape=out_shape,
        grid=(n_b, n_steps),
        in_specs=[pl.BlockSpec((1, TILE, 256), mix_map)] * 4
                 + [pl.BlockSpec((1, TILE, MIX_W), mix_map),
                    pl.BlockSpec((1, TILE, d), lat_map),
                    pl.BlockSpec((1, CTX_LEN, d), ctx_map),
                    pl.BlockSpec((8, 3 * d), const2),
                    pl.BlockSpec((MIX_W, d), const2),
                    pl.BlockSpec((1, d), const2),
                    pl.BlockSpec((GROUP_W, GROUP_W), const2),
                    pl.BlockSpec((1, GROUP_W), const2)],
        out_specs=out_specs,
        compiler_params=_cp(("parallel", "arbitrary")),
        name="gate_outproj_residual",
    )(*ys, gate_cols, x, xc, mod_l, w_out.astype(BF16), g_post.reshape(1, d), w_glu.astype(BF16),
      b_glu.reshape(1, GROUP_W))
    return (outs[0], outs[1]) if with_ctx else (outs[0], None)


def _rope_tables(n_tok):
    t = np.arange(n_tok)
    half = GLA_DK // 2
    freqs = ROPE_BASE ** (-np.arange(0, half, 2, dtype=np.float32) / half)
    d = np.arange(GLA_DK)
    pos = np.where((d // half)[None, :] == 0, (t // GRID_W)[:, None], (t % GRID_W)[:, None]).astype(np.float32)
    ang = jnp.asarray(pos) * jnp.asarray(freqs[(d % half) % (half // 2)])[None, :]
    first = jnp.asarray(((d % half) < half // 2)[None, :])
    cos, sin = jnp.cos(ang), jnp.sin(ang)
    tabs = (cos, jnp.where(first, -sin, 0.0), jnp.where(first, 0.0, sin))
    return tuple(jnp.tile(z, (1, GLA_HEADS)).astype(F32) for z in tabs)


def _permute_w_in(w):
    ck, cv, cgf, cgb, cnk, cnv, csu = np.cumsum([0, 128, 256, 16, 16, 256, 256])[:7]
    base = 128 + 256 + 16 + 16 + 256 + 256 + 256
    gq, nq, pu, gate = base, base + 128, base + 384, base + 640
    sl = lambda a, n: w[:, a:a + n]
    zeros = jnp.zeros((w.shape[0], 128 - 2 * GLA_RANK), w.dtype)
    cols = [sl(gq, 128), sl(ck, 128), sl(cv, 256), sl(cgf, 16), sl(cgb, 16), zeros, sl(nq, 256), sl(cnk, 256),
            sl(cnv, 256), sl(csu, 256), sl(pu, 256), sl(gate, 1024)]
    return jnp.concatenate(cols, axis=1).astype(BF16)


def _gate_weights(w_gate, b_gate):
    wg = jnp.zeros((128, 256), F32)
    wg = wg.at[0:GLA_RANK, 0:128].set(w_gate[0].astype(F32))
    wg = wg.at[GLA_RANK:2 * GLA_RANK, 128:256].set(w_gate[1].astype(F32))
    return wg.astype(BF16), b_gate.astype(F32).reshape(1, 256)


def kernel(x, c, ctx, c_ctx, w_mod, b_mod, g_pre, g_post, w_in, w_out, gla_w_gate, gla_b_gate, gla_g_norm, na_rpb,
           s5_lam_re, s5_lam_im, s5_log_dt, s5_b_re, s5_b_im, s5_c_re, s5_c_im, s5_d, s5_w_glu, s5_b_glu,
           pool_w, pool_scale):
    n_b, n_lat, d = x.shape
    depth = w_mod.shape[0]
    assert d == D_MODEL and ctx.shape[1] == CTX_LEN and n_lat % TILE == 0 and n_b < 8
    rows = n_lat // GRID_W
    c_rows = jnp.zeros((8, d), F32).at[0:n_b].set(c).at[n_b].set(c_ctx)
    mod_all = _modulation(c_rows, w_mod, b_mod)
    rope_tabs = _rope_tables(n_lat)
    xc = ctx
    for l in range(depth):
        last = l == depth - 1
        wg, bg = _gate_weights(gla_w_gate[l], gla_b_gate[l])
        qk, gv, lg, nq, nk, nv, su, pu, gate_cols = _inproj(x, xc, mod_all[l], g_pre[l], _permute_w_in(w_in[l]),
                                                             wg, bg, rope_tabs)
        y_gla = _gla(qk, gv, lg, gla_g_norm[l])
        y_na = _na(nq, nk, nv, _na_bias_table(na_rpb[l], rows))
        y_s5 = _s5(su, _s5_tables(s5_lam_re[l], s5_lam_im[l], s5_log_dt[l], s5_b_re[l], s5_b_im[l],
                                  s5_c_re[l], s5_c_im[l], s5_d[l]))
        y_pool = _pool(pu, pool_w[l], pool_scale[l])
        x, xc = _outproj((y_gla, y_na, y_s5, y_pool), gate_cols, x, xc, mod_all[l], w_out[l], g_post[l],
                         s5_w_glu[l], s5_b_glu[l], with_ctx=not last)
    return x
```

```python
import functools
import math

import numpy as np
import jax
import jax.numpy as jnp
from jax import lax
from jax.experimental import pallas as pl
from jax.experimental.pallas import tpu as pltpu

F32 = jnp.float32
BF16 = jnp.bfloat16

D_MODEL = 1024
GRID_W = 64
CTX_LEN = 256
GROUP_W = 256
MIX_W = 1024
NORM_EPS = 1e-6
GLA_HEADS = 4
GLA_DV = 64
GLA_DK = 32
GLA_RANK = 16
GLA_TAU = 16.0
ROPE_BASE = 10000.0
NA_HEADS = 4
NA_DH = 64
NA_KR = 8
NA_KC = 16
S5_HG = 16
S5_G = 16
S5_P = 64
POOL_WINDOWS = (2, 4, 8, 16)
POOL_GW = 64

TILE = 256
GLA_C = 128
S5_T = 16
NA_ROWS = TILE // GRID_W
NA_REL_W = 1024
NEG_BIG = -1e30

_COL = {}
_off = 0
for _name, _w in (("gk", 128), ("gv", 256), ("gg", 128), ("nk", 256), ("nv", 256), ("su", 256),
                  ("gq", 128), ("nq", 256), ("pu", 256), ("gate", 1024)):
    _COL[_name] = (_off, _off + _w)
    _off += _w
N_PAD_COLS = _off
N_GATE_END = 128 + 256 + 2 * GLA_RANK

_VMEM_LIMIT = 56 * 1024 * 1024


def _cp(sem, vmem=_VMEM_LIMIT):
    return pltpu.CompilerParams(dimension_semantics=sem, vmem_limit_bytes=vmem)


def _dot(a, b):
    return jnp.dot(a, b, preferred_element_type=F32)


def _dot_nt(a, b):
    return lax.dot_general(a, b, (((1,), (1,)), ((), ())), preferred_element_type=F32)


def _split_bf16(a):
    hi = a.astype(BF16)
    lo = (a - hi.astype(F32)).astype(BF16)
    return hi, lo


def _mod_kernel(s_ref, w_ref, b_ref, o_ref):
    s = s_ref[...]
    a = (s * jax.nn.sigmoid(s)).astype(BF16)
    o_ref[0] = _dot(a, w_ref[0].astype(BF16)) + b_ref[0]


def _modulation(c_rows, w_mod, b_mod):
    n_l, d, d3 = w_mod.shape
    return pl.pallas_call(
        _mod_kernel,
        out_shape=jax.ShapeDtypeStruct((n_l, 8, d3), F32),
        grid=(n_l, d3 // d),
        in_specs=[pl.BlockSpec((8, d), lambda l, j: (0, 0)),
                  pl.BlockSpec((1, d, d), lambda l, j: (l, 0, j)),
                  pl.BlockSpec((1, 1, d), lambda l, j: (l, 0, j))],
        out_specs=pl.BlockSpec((1, 8, d), lambda l, j: (l, 0, j)),
        compiler_params=_cp(("parallel", "parallel")),
        name="adaln_modulation",
    )(c_rows, w_mod, b_mod.reshape(n_l, 1, d3))


def _inproj_kernel(x_ref, xc_ref, mod_ref, gpre_ref, w_ref, wg_ref, bg_ref, cos_ref, sa_ref, sb_ref,
                   qk_ref, v_ref, lg_ref, nq_ref, nk_ref, nv_ref, su0_ref, su1_ref, pu_ref, gate_ref, *, n_batch):
    b = pl.program_id(0)
    j = pl.program_id(1)
    is_ctx = j == 0
    xt = jnp.where(is_ctx, xc_ref[0], x_ref[0])
    m = mod_ref[pl.ds(jnp.where(is_ctx, n_batch, b), 1), :]
    shift = m[:, 0:D_MODEL]
    scale = m[:, D_MODEL:2 * D_MODEL]
    ms = jnp.mean(xt * xt, axis=-1, keepdims=True)
    h = xt * lax.rsqrt(ms + NORM_EPS) * gpre_ref[...] * (1.0 + scale) + shift
    hb = h.astype(BF16)

    def proj(name):
        lo, hi = _COL[name]
        return _dot(hb, w_ref[:, lo:hi])

    cos = jnp.where(is_ctx, 1.0, cos_ref[...])
    sa = jnp.where(is_ctx, 0.0, sa_ref[...])
    sb = jnp.where(is_ctx, 0.0, sb_ref[...])

    def rope(t):
        return t * cos + pltpu.roll(t, 128 - 8, 1) * sa + pltpu.roll(t, 8, 1) * sb

    qk_ref[0, :, 0:128] = rope(proj("gq")) * (GLA_DK ** -0.5)
    qk_ref[0, :, 128:256] = rope(proj("gk"))
    v_ref[0] = proj("gv")
    z = _dot(proj("gg").astype(BF16), wg_ref[...]) + bg_ref[...]
    lg_ref[0] = (jnp.minimum(z, 0.0) - jnp.log(1.0 + jnp.exp(-jnp.abs(z)))) * (1.0 / GLA_TAU)
    nq_ref[0] = proj("nq") * (NA_DH ** -0.5)
    nk_ref[0] = proj("nk")
    nv_ref[0] = proj("nv")
    su = proj("su")
    su0_ref[0] = su[:, 0:128]
    su1_ref[0] = su[:, 128:256]
    pu_ref[0] = proj("pu")
    gate_ref[0] = proj("gate")


def _layer_spec(shape, layer):
    return pl.BlockSpec((None,) + tuple(shape), lambda *_: (layer,) + (0,) * len(shape))


def _inproj(x, xc, layer, mod_all, g_pre, w_pad, wg, bg, rope_tabs):
    n_b, n_lat, d = x.shape
    n_tiles = (n_lat + CTX_LEN) // TILE
    t_all = n_lat + CTX_LEN
    lat_map = lambda b, j: (b, jnp.maximum(j - 1, 0), 0)
    tab_map = lambda b, j: (jnp.maximum(j - 1, 0), 0)
    out_map = lambda b, j: (b, j, 0)
    widths = (256, 256, 256, 256, 256, 256, 128, 128, 256, 1024)
    return pl.pallas_call(
        functools.partial(_inproj_kernel, n_batch=n_b),
        out_shape=[jax.ShapeDtypeStruct((n_b, t_all, w), F32) for w in widths],
        grid=(n_b, n_tiles),
        in_specs=[pl.BlockSpec((1, TILE, d), lat_map),
                  pl.BlockSpec((1, CTX_LEN, d), lambda b, j: (b, 0, 0)),
                  _layer_spec((8, 3 * d), layer),
                  _layer_spec((1, d), layer),
                  _layer_spec((d, N_PAD_COLS), layer),
                  _layer_spec((128, 256), layer),
                  _layer_spec((1, 256), layer),
                  pl.BlockSpec((TILE, 128), tab_map),
                  pl.BlockSpec((TILE, 128), tab_map),
                  pl.BlockSpec((TILE, 128), tab_map)],
        out_specs=[pl.BlockSpec((1, TILE, w), out_map) for w in widths],
        compiler_params=_cp(("parallel", "arbitrary")),
        name="modnorm_inproj",
    )(x, xc, mod_all, g_pre, w_pad, wg, bg, *rope_tabs)


def _gla_kernel(qk_ref, v_ref, lg_ref, gn_ref, y_ref, st_ref, *, n_chunks, n_ctx_chunks):
    c_len = GLA_C
    row = lax.broadcasted_iota(jnp.int32, (c_len, c_len), 0)
    col = lax.broadcasted_iota(jnp.int32, (c_len, c_len), 1)
    tri_f = (col <= row).astype(BF16)
    tri_b = (col >= row).astype(BF16)
    arow = lax.broadcasted_iota(jnp.int32, (4 * c_len, c_len), 0) % c_len
    acol = lax.broadcasted_iota(jnp.int32, (4 * c_len, c_len), 1)
    amask_f = acol <= arow
    amask_b = acol >= arow
    head_k = lax.broadcasted_iota(jnp.int32, (1, 128), 1) // GLA_DK
    head_v = lax.broadcasted_iota(jnp.int32, (1, 256), 1) // GLA_DV
    bd_mask = (lax.broadcasted_iota(jnp.int32, (256, 128), 0) // GLA_DV
               == lax.broadcasted_iota(jnp.int32, (256, 128), 1) // GLA_DK)
    ones_bd = jnp.where(lax.broadcasted_iota(jnp.int32, (256, 256), 0) // GLA_DV
                        == lax.broadcasted_iota(jnp.int32, (256, 256), 1) // GLA_DV,
                        1.0 / GLA_DV, 0.0).astype(BF16)
    mid = c_len // 2

    def chunk(c, fwd):
        r0 = pl.multiple_of(c * c_len, c_len)
        q = qk_ref[0, pl.ds(r0, c_len), 0:128]
        k = qk_ref[0, pl.ds(r0, c_len), 128:256]
        v = v_ref[0, pl.ds(r0, c_len), :]
        lg = lg_ref[0, pl.ds(r0, c_len), 0:128] if fwd else lg_ref[0, pl.ds(r0, c_len), 128:256]
        lg_hi, lg_lo = _split_bf16(lg)
        tri = tri_f if fwd else tri_b
        cum = _dot(tri, lg_hi) + _dot(tri, lg_lo)
        c_ref = cum[mid:mid + 1, :]
        c_end = cum[c_len - 1:c_len, :] if fwd else cum[0:1, :]
        qt = q * jnp.exp(cum - c_ref)
        kt = k * jnp.exp(c_ref - cum)
        qh = (qt * jnp.exp(c_ref)).astype(BF16)
        kh = (kt * jnp.exp(c_end - c_ref)).astype(BF16)
        vb = v.astype(BF16)
        qs = jnp.concatenate([jnp.where(head_k == hh, qt, 0.0) for hh in range(GLA_HEADS)], axis=0).astype(BF16)
        att = _dot_nt(qs, kt.astype(BF16))
        att = jnp.where(amask_f if fwd else amask_b, att, 0.0).astype(BF16)
        r = _dot(att, vb)
        o = jnp.where(head_v == 0, r[0:c_len], 0.0)
        for hh in range(1, GLA_HEADS):
            o = o + jnp.where(head_v == hh, r[hh * c_len:(hh + 1) * c_len], 0.0)
        st = st_ref[...]
        o = o + _dot_nt(qh, st.astype(BF16))
        upd = _dot(v.T.astype(BF16), kh)
        st_ref[...] = st * jnp.exp(c_end) + jnp.where(bd_mask, upd, 0.0)
        return r0, o

    def fwd_body(c, carry):
        r0, o = chunk(c, True)
        y_ref[0, pl.ds(r0, c_len), :] = o
        return carry

    def bwd_body(i, carry):
        c = jnp.where(i < n_ctx_chunks, n_ctx_chunks - 1 - i, n_chunks - 1 - (i - n_ctx_chunks))
        r0, o = chunk(c, False)
        o = o + y_ref[0, pl.ds(r0, c_len), :]
        sq_hi, sq_lo = _split_bf16(o * o)
        ms = _dot(sq_hi, ones_bd) + _dot(sq_lo, ones_bd)
        y_ref[0, pl.ds(r0, c_len), :] = o * lax.rsqrt(ms + NORM_EPS) * gn_ref[...]
        return carry

    st_ref[...] = jnp.zeros_like(st_ref)
    lax.fori_loop(0, n_chunks, fwd_body, 0)
    st_ref[...] = jnp.zeros_like(st_ref)
    lax.fori_loop(0, n_chunks, bwd_body, 0)


def _gla(qk, v, lg, layer, g_norm):
    n_b, t_all, _ = qk.shape
    blk = lambda w: pl.BlockSpec((1, t_all, w), lambda b: (b, 0, 0))
    return pl.pallas_call(
        functools.partial(_gla_kernel, n_chunks=t_all // GLA_C, n_ctx_chunks=CTX_LEN // GLA_C),
        out_shape=jax.ShapeDtypeStruct((n_b, t_all, 256), F32),
        grid=(n_b,),
        in_specs=[blk(256), blk(256), blk(256), _layer_spec((1, 256), layer)],
        out_specs=blk(256),
        scratch_shapes=[pltpu.VMEM((256, 128), F32)],
        compiler_params=_cp(("parallel",)),
        name="gla_mixer",
    )(qk, v, lg, g_norm)


def _na_kernel(q_ref, k0_ref, k1_ref, k2_ref, v0_ref, v1_ref, v2_ref, kc_ref, vc_ref, rel_ref, o_ref, *, rows):
    j = pl.program_id(1)

    def attend(with_window):
        if with_window:
            g = j - 1
            q_row = lax.broadcasted_iota(jnp.int32, (TILE, TILE), 0) // GRID_W
            k_row = lax.broadcasted_iota(jnp.int32, (TILE, TILE), 1) // GRID_W
            first = jnp.clip(NA_ROWS * g + q_row - NA_KR // 2, 0, rows - NA_KR) - NA_ROWS * (g - 1)
            window = [jnp.where((k_row + NA_ROWS * d >= first) & (k_row + NA_ROWS * d < first + NA_KR), 0.0, NEG_BIG)
                      for d in range(3)]
        for hh in range(NA_HEADS):
            lo, hi = hh * NA_DH, (hh + 1) * NA_DH
            qh = q_ref[0, :, lo:hi].astype(BF16)
            scores = [_dot_nt(qh, kc_ref[0, :, lo:hi].astype(BF16))]
            values = [vc_ref]
            if with_window:
                for d, (kr, vr) in enumerate(((k0_ref, v0_ref), (k1_ref, v1_ref), (k2_ref, v2_ref))):
                    ro0 = NA_KR - 1 - NA_ROWS
                    rel = jnp.concatenate(
                        [rel_ref[hh, (ro0 - i) % 2, :,
                                 GRID_W * ((ro0 - i) // 2 * 2) + d * TILE:GRID_W * ((ro0 - i) // 2 * 2) + (d + 1) * TILE]
                         for i in range(NA_ROWS)], axis=0)
                    s = _dot_nt(qh, kr[0, :, lo:hi].astype(BF16)) + rel + window[d]
                    scores.append(s)
                    values.append(vr)
            mx = scores[0]
            for s in scores[1:]:
                mx = jnp.maximum(mx, s)
            mx = jnp.max(mx, axis=-1, keepdims=True)
            den = None
            acc = None
            for s, vr in zip(scores, values):
                p = jnp.exp(s - mx)
                d = jnp.sum(p, axis=-1, keepdims=True)
                a = _dot(p.astype(BF16), vr[0, :, lo:hi].astype(BF16))
                den = d if den is None else den + d
                acc = a if acc is None else acc + a
            o_ref[0, :, lo:hi] = acc / den

    @pl.when(j == 0)
    def _():
        attend(False)

    @pl.when(j > 0)
    def _():
        attend(True)


def _na(nq, nk, nv, layer, rel_tab):
    n_b, t_all, _ = nq.shape
    n_tiles = t_all // TILE
    rows = (t_all - CTX_LEN) // GRID_W

    def nbr(delta):
        return lambda b, j: (b, jnp.clip(j + delta, 1, n_tiles - 1), 0)

    tile = lambda f: pl.BlockSpec((1, TILE, 256), f)
    return pl.pallas_call(
        functools.partial(_na_kernel, rows=rows),
        out_shape=jax.ShapeDtypeStruct((n_b, t_all, 256), F32),
        grid=(n_b, n_tiles),
        in_specs=[tile(lambda b, j: (b, j, 0)),
                  tile(nbr(-1)), tile(nbr(0)), tile(nbr(1)),
                  tile(nbr(-1)), tile(nbr(0)), tile(nbr(1)),
                  tile(lambda b, j: (b, 0, 0)), tile(lambda b, j: (b, 0, 0)),
                  _layer_spec((NA_HEADS, 2, GRID_W, NA_REL_W), layer)],
        out_specs=tile(lambda b, j: (b, j, 0)),
        compiler_params=_cp(("parallel", "arbitrary")),
        name="na_mixer",
    )(nq, nk, nk, nk, nv, nv, nv, nk, nv, rel_tab)


def _na_rel_table(rpb):
    col = np.arange(GRID_W)
    cs = np.clip(col - NA_KC // 2, 0, GRID_W - NA_KC)
    col_mask = (col[None, :] >= cs[:, None]) & (col[None, :] < cs[:, None] + NA_KC)
    co = np.clip(col[None, :] - col[:, None], -(NA_KC - 1), NA_KC - 1) + (NA_KC - 1)
    onehot = (co[None] == np.arange(2 * NA_KC - 1)[:, None, None]).astype(np.float32)
    toe = jnp.einsum("hrc,cqk->hqrk", rpb.astype(F32), jnp.asarray(onehot),
                     precision=lax.Precision.HIGHEST)
    toe = jnp.where(jnp.asarray(col_mask)[None, :, None, :], toe, NEG_BIG)
    flat = toe.reshape(NA_HEADS, GRID_W, (2 * NA_KR - 1) * GRID_W)
    flat = jnp.pad(flat, ((0, 0), (0, 0), (0, NA_REL_W + GRID_W - flat.shape[-1])), constant_values=NEG_BIG)
    return jnp.stack([flat[:, :, :NA_REL_W], flat[:, :, GRID_W:]], axis=1)


def _s5_kernel(su0_ref, su1_ref, ktoe_ref, bexp_ref, cexp_ref, pw_ref, y0_ref, y1_ref, a_ref, e_ref, x_ref,
               *, n_chunks, n_ctx_chunks):
    blk = 8
    lane_blk = lax.broadcasted_iota(jnp.int32, (blk, 128), 1) // S5_HG
    masks = [lane_blk == m for m in range(8)]
    gp = S5_G * S5_P

    def merge(pieces, src_pos):
        acc = None
        for m in range(8):
            sh = (S5_HG * (m - src_pos)) % 128
            r = pltpu.roll(pieces[m], sh, 1) if sh else pieces[m]
            acc = r if acc is None else jnp.where(masks[m], r, acc)
        return acc

    def regroup_in(rb, carry):
        c0 = pl.multiple_of(rb * blk, blk)
        us = [[ref[0, pl.ds(rb * (blk * S5_T) + s, blk, stride=S5_T), :] for s in range(S5_T)]
              for ref in (su0_ref, su1_ref)]
        for g in range(S5_G):
            hg, gm = divmod(g, 8)
            for o in range(2):
                pieces = [us[hg][o * 8 + m] for m in range(8)]
                a_ref[g, pl.ds(c0, blk), o * 128:(o + 1) * 128] = merge(pieces, gm)
        return carry

    lax.fori_loop(0, n_chunks // blk, regroup_in, 0)

    for g in range(S5_G):
        e = _dot(a_ref[g].astype(BF16), bexp_ref[g])
        for kind in range(4):
            e_ref[kind, :, g * S5_P:(g + 1) * S5_P] = e[:, kind * S5_P:(kind + 1) * S5_P]

    afr, afi, abr, abi = pw_ref[0:1, :], pw_ref[1:2, :], pw_ref[2:3, :], pw_ref[3:4, :]

    def scan_body(i, carry):
        fr, fi, br, bi = carry
        cf = i
        cb = jnp.where(i < n_ctx_chunks, n_ctx_chunks - 1 - i, n_chunks - 1 - (i - n_ctx_chunks))
        x_ref[0, pl.ds(cf, 1), :] = fr
        x_ref[1, pl.ds(cf, 1), :] = fi
        x_ref[2, pl.ds(cb, 1), :] = br
        x_ref[3, pl.ds(cb, 1), :] = bi
        er = e_ref[0, pl.ds(cf, 1), :]
        ei = e_ref[1, pl.ds(cf, 1), :]
        gr = e_ref[2, pl.ds(cb, 1), :]
        gi = e_ref[3, pl.ds(cb, 1), :]
        return (afr * fr - afi * fi + er, afr * fi + afi * fr + ei,
                abr * br - abi * bi + gr, abr * bi + abi * br + gi)

    z = jnp.zeros((1, gp), F32)
    lax.fori_loop(0, n_chunks, scan_body, (z, z, z, z))

    for g in range(S5_G):
        y = _dot(a_ref[g].astype(BF16), ktoe_ref[g])
        for kind in range(4):
            y = y + _dot(x_ref[kind, :, g * S5_P:(g + 1) * S5_P].astype(BF16), cexp_ref[g, kind])
        a_ref[g] = y

    def regroup_out(rb, carry):
        c0 = pl.multiple_of(rb * blk, blk)
        ys = [a_ref[g, pl.ds(c0, blk), :] for g in range(S5_G)]
        for t in range(S5_T):
            ht, tm = divmod(t, 8)
            for og in range(2):
                pieces = [ys[og * 8 + m][:, ht * 128:(ht + 1) * 128] for m in range(8)]
                (y0_ref, y1_ref)[og][0, pl.ds(rb * (blk * S5_T) + t, blk, stride=S5_T), :] = merge(pieces, tm)
        return carry

    lax.fori_loop(0, n_chunks // blk, regroup_out, 0)


def _s5(su0, su1, layer, ktoe, bexp, cexp, pw16):
    n_b, t_all, _ = su0.shape
    n_chunks = t_all // S5_T
    gp = S5_G * S5_P
    blk = pl.BlockSpec((1, t_all, 128), lambda b: (b, 0, 0))
    return pl.pallas_call(
        functools.partial(_s5_kernel, n_chunks=n_chunks, n_ctx_chunks=CTX_LEN // S5_T),
        out_shape=[jax.ShapeDtypeStruct((n_b, t_all, 128), F32)] * 2,
        grid=(n_b,),
        in_specs=[blk, blk, _layer_spec((S5_G, 256, 256), layer), _layer_spec((S5_G, 256, 256), layer),
                  _layer_spec((S5_G, 4, S5_P, 256), layer), _layer_spec((4, gp), layer)],
        out_specs=[blk, blk],
        scratch_shapes=[pltpu.VMEM((S5_G, n_chunks, S5_T * S5_HG), F32),
                        pltpu.VMEM((4, n_chunks, gp), F32),
                        pltpu.VMEM((4, n_chunks, gp), F32)],
        compiler_params=_cp(("parallel",)),
        name="s5_mixer",
    )(su0, su1, ktoe, bexp, cexp, pw16)


def _s5_tables(lam_re, lam_im, log_dt, b_re, b_im, c_re, c_im, d_skip):
    hp = lax.Precision.HIGHEST
    lam_re, lam_im, log_dt = lam_re.astype(F32), lam_im.astype(F32), log_dt.astype(F32)
    b_re, b_im, c_re, c_im = b_re.astype(F32), b_im.astype(F32), c_re.astype(F32), c_im.astype(F32)
    dt = jnp.exp(log_dt)[..., None]
    n = jnp.arange(S5_T + 1, dtype=F32)[:, None, None, None]
    pw_re = jnp.exp(n * lam_re * dt) * jnp.cos(n * lam_im * dt)
    pw_im = jnp.exp(n * lam_re * dt) * jnp.sin(n * lam_im * dt)
    lb_re, lb_im = pw_re[1], pw_im[1]
    num_re, num_im = lb_re - 1.0, lb_im
    den = lam_re * lam_re + lam_im * lam_im
    coef_re = ((num_re * lam_re + num_im * lam_im) / den)[..., None]
    coef_im = ((num_im * lam_re - num_re * lam_im) / den)[..., None]
    bb_re = coef_re * b_re - coef_im * b_im
    bb_im = coef_re * b_im + coef_im * b_re
    cp_re = c_re[None] * pw_re[:, :, :, None, :] - c_im[None] * pw_im[:, :, :, None, :]
    cp_im = c_re[None] * pw_im[:, :, :, None, :] + c_im[None] * pw_re[:, :, :, None, :]
    lag = (jnp.einsum("ndgop,dgpi->ndgoi", cp_re, bb_re, precision=hp)
           - jnp.einsum("ndgop,dgpi->ndgoi", cp_im, bb_im, precision=hp))
    s_idx = np.arange(S5_T)[:, None]
    t_idx = np.arange(S5_T)[None, :]
    kf = jnp.where(jnp.asarray(t_idx >= s_idx)[:, :, None, None, None],
                   lag[:, 0][np.clip(t_idx - s_idx, 0, S5_T)], 0.0)
    kb = jnp.where(jnp.asarray(s_idx >= t_idx)[:, :, None, None, None],
                   lag[:, 1][np.clip(s_idx - t_idx, 0, S5_T)], 0.0)
    skip = (jnp.asarray(np.eye(S5_T, dtype=np.float32))[:, :, None, None, None]
            * jnp.asarray(np.eye(S5_HG, dtype=np.float32))[None, None, None]
            * d_skip.astype(F32).reshape(S5_G, S5_HG)[None, None, :, None, :])
    ktoe = (kf + kb + skip).transpose(2, 0, 4, 1, 3).reshape(S5_G, S5_T * S5_HG, S5_T * S5_HG)
    pf_re, pf_im = pw_re[S5_T - 1 - np.arange(S5_T), 0], pw_im[S5_T - 1 - np.arange(S5_T), 0]
    pb_re, pb_im = pw_re[np.arange(S5_T), 1], pw_im[np.arange(S5_T), 1]

    def cmul_b(p_re, p_im, d):
        re = p_re[..., None] * bb_re[d][None] - p_im[..., None] * bb_im[d][None]
        im = p_re[..., None] * bb_im[d][None] + p_im[..., None] * bb_re[d][None]
        to = lambda z: z.transpose(1, 0, 3, 2).reshape(S5_G, S5_T * S5_HG, S5_P)
        return to(re), to(im)

    bexp = jnp.concatenate(cmul_b(pf_re, pf_im, 0) + cmul_b(pb_re, pb_im, 1), axis=-1)
    qf_re, qf_im = pw_re[np.arange(S5_T) + 1, 0], pw_im[np.arange(S5_T) + 1, 0]
    qb_re, qb_im = pw_re[S5_T - np.arange(S5_T), 1], pw_im[S5_T - np.arange(S5_T), 1]

    def cmul_c(p_re, p_im, d):
        re = c_re[d][None] * p_re[:, :, None, :] - c_im[d][None] * p_im[:, :, None, :]
        im = c_re[d][None] * p_im[:, :, None, :] + c_im[d][None] * p_re[:, :, None, :]
        to = lambda z: z.transpose(1, 3, 0, 2).reshape(S5_G, S5_P, S5_T * S5_HG)
        return to(re), to(-im)

    cexp = jnp.stack(cmul_c(qf_re, qf_im, 0) + cmul_c(qb_re, qb_im, 1), axis=1)
    pw16 = jnp.stack([pw_re[S5_T, 0].reshape(-1), pw_im[S5_T, 0].reshape(-1),
                      pw_re[S5_T, 1].reshape(-1), pw_im[S5_T, 1].reshape(-1)], axis=0)
    return ktoe.astype(BF16), bexp.astype(BF16), cexp.astype(BF16), pw16


def _pool_kernel(u_ref, w_ref, scale_ref, o_ref, pad_ref, *, n_lat):
    pad = 8
    seg = ((pad, 0, CTX_LEN), (pad + CTX_LEN + 2 * pad, CTX_LEN, n_lat))
    pad_ref[...] = jnp.zeros_like(pad_ref)
    for p0, u0, n in seg:
        pad_ref[p0:p0 + n, :] = u_ref[0, u0:u0 + n, :]
    for p0, u0, n in seg:
        for t0 in range(0, n, TILE):
            t = t0 + lax.broadcasted_iota(jnp.int32, (TILE, 1), 0)
            outs = []
            for i, w in enumerate(POOL_WINDOWS):
                lo, hi = i * POOL_GW, (i + 1) * POOL_GW
                acc = None
                for jj in range(w):
                    s = p0 + t0 + jj - w // 2
                    piece = pad_ref[s:s + TILE, lo:hi]
                    acc = piece if acc is None else acc + piece
                cnt = (jnp.clip(t - w // 2 + w, 0, n) - jnp.clip(t - w // 2, 0, n)).astype(F32)
                g = u_ref[0, u0 + t0:u0 + t0 + TILE, lo:hi]
                outs.append(_dot((acc / cnt - g).astype(BF16), w_ref[i].astype(BF16)))
            o_ref[0, u0 + t0:u0 + t0 + TILE, :] = jnp.concatenate(outs, axis=-1) * scale_ref[...]


def _pool(pu, layer, w_pool, scale):
    n_b, t_all, _ = pu.shape
    blk = pl.BlockSpec((1, t_all, 256), lambda b: (b, 0, 0))
    return pl.pallas_call(
        functools.partial(_pool_kernel, n_lat=t_all - CTX_LEN),
        out_shape=jax.ShapeDtypeStruct((n_b, t_all, 256), F32),
        grid=(n_b,),
        in_specs=[blk, _layer_spec((len(POOL_WINDOWS), POOL_GW, POOL_GW), layer), _layer_spec((1, 256), layer)],
        out_specs=blk,
        scratch_shapes=[pltpu.VMEM((t_all + 32, 256), F32)],
        compiler_params=_cp(("parallel",)),
        name="pool_mixer",
    )(pu, w_pool, scale)


def _gelu_tanh(x):
    return 0.5 * x * (1.0 + jnp.tanh(math.sqrt(2.0 / math.pi) * (x + 0.044715 * (x * x * x))))


def _outproj_kernel(ygla_ref, yna_ref, ys50_ref, ys51_ref, ypool_ref, gate_ref, x_ref, xc_ref, mod_ref, wout_ref,
                    gpost_ref, wglu_ref, bglu_ref, *out_refs, n_batch, with_ctx):
    b = pl.program_id(0)
    j = pl.program_id(1)
    is_ctx = jnp.logical_and(j == 0, with_ctx)
    gate_m = mod_ref[pl.ds(jnp.where(is_ctx, n_batch, b), 1), :][:, 2 * D_MODEL:3 * D_MODEL]
    xt = jnp.where(is_ctx, xc_ref[0], x_ref[0]) if with_ctx else x_ref[0]
    gc = gate_ref[0]
    sg = gc * jax.nn.sigmoid(gc)
    g5 = _gelu_tanh(jnp.concatenate([ys50_ref[0], ys51_ref[0]], axis=-1))
    s5o = g5 * jax.nn.sigmoid(_dot(g5.astype(BF16), wglu_ref[...]) + bglu_ref[...])
    parts = (ygla_ref[0], yna_ref[0], s5o, ypool_ref[0])
    acc = None
    for i, part in enumerate(parts):
        lo, hi = i * GROUP_W, (i + 1) * GROUP_W
        a = _dot((part * sg[:, lo:hi]).astype(BF16), wout_ref[lo:hi, :])
        acc = a if acc is None else acc + a
    ms = jnp.mean(acc * acc, axis=-1, keepdims=True)
    res = xt + gate_m * (acc * lax.rsqrt(ms + NORM_EPS) * gpost_ref[...])
    if with_ctx:
        xo_ref, xco_ref = out_refs

        @pl.when(j == 0)
        def _():
            xco_ref[0] = res

        @pl.when(j > 0)
        def _():
            xo_ref[0] = res
    else:
        out_refs[0][0] = res


def _outproj(ys, gate_cols, x, xc, layer, mod_all, w_out, g_post, w_glu, b_glu, with_ctx):
    n_b, n_lat, d = x.shape
    n_lat_tiles = n_lat // TILE
    off = 0 if with_ctx else 1
    n_steps = n_lat_tiles + (1 if with_ctx else 0)
    mix_map = lambda b, j: (b, j + off, 0)
    lat_map = (lambda b, j: (b, jnp.maximum(j - 1, 0), 0)) if with_ctx else (lambda b, j: (b, j, 0))
    ctx_map = lambda b, j: (b, 0, 0)
    out_shape = [jax.ShapeDtypeStruct(x.shape, F32)]
    out_specs = [pl.BlockSpec((1, TILE, d), lat_map)]
    if with_ctx:
        out_shape.append(jax.ShapeDtypeStruct(xc.shape, F32))
        out_specs.append(pl.BlockSpec((1, CTX_LEN, d), ctx_map))
    outs = pl.pallas_call(
        functools.partial(_outproj_kernel, n_batch=n_b, with_ctx=with_ctx),
        out_shape=out_shape,
        grid=(n_b, n_steps),
        in_specs=[pl.BlockSpec((1, TILE, w), mix_map) for w in (256, 256, 128, 128, 256)]
                 + [pl.BlockSpec((1, TILE, MIX_W), mix_map),
                    pl.BlockSpec((1, TILE, d), lat_map),
                    pl.BlockSpec((1, CTX_LEN, d), ctx_map),
                    _layer_spec((8, 3 * d), layer),
                    _layer_spec((MIX_W, d), layer),
                    _layer_spec((1, d), layer),
                    _layer_spec((GROUP_W, GROUP_W), layer),
                    _layer_spec((1, GROUP_W), layer)],
        out_specs=out_specs,
        compiler_params=_cp(("parallel", "arbitrary")),
        name="gate_outproj_residual",
    )(*ys, gate_cols, x, xc, mod_all, w_out, g_post, w_glu, b_glu)
    return (outs[0], outs[1]) if with_ctx else (outs[0], None)


def _rope_tables(n_tok):
    t = np.arange(n_tok)
    half = GLA_DK // 2
    freqs = ROPE_BASE ** (-np.arange(0, half, 2, dtype=np.float32) / half)
    d = np.arange(GLA_DK)
    pos = np.where((d // half)[None, :] == 0, (t // GRID_W)[:, None], (t % GRID_W)[:, None]).astype(np.float32)
    ang = jnp.asarray(pos) * jnp.asarray(freqs[(d % half) % (half // 2)])[None, :]
    first = jnp.asarray(((d % half) < half // 2)[None, :])
    cos, sin = jnp.cos(ang), jnp.sin(ang)
    tabs = (cos, jnp.where(first, -sin, 0.0), jnp.where(first, 0.0, sin))
    return tuple(jnp.tile(z, (1, GLA_HEADS)).astype(F32) for z in tabs)


def _pad_w_in(w_in):
    zeros = jnp.zeros(w_in.shape[:2] + (128 - 2 * GLA_RANK,), w_in.dtype)
    return jnp.concatenate([w_in[:, :, :N_GATE_END], zeros, w_in[:, :, N_GATE_END:]], axis=2).astype(BF16)


def _gate_weights(w_gate, b_gate):
    n_l = w_gate.shape[0]
    wg = jnp.zeros((n_l, 128, 256), F32)
    wg = wg.at[:, 0:GLA_RANK, 0:128].set(w_gate[:, 0].astype(F32))
    wg = wg.at[:, GLA_RANK:2 * GLA_RANK, 128:256].set(w_gate[:, 1].astype(F32))
    return wg.astype(BF16), b_gate.astype(F32).reshape(n_l, 1, 256)


def kernel(x, c, ctx, c_ctx, w_mod, b_mod, g_pre, g_post, w_in, w_out, gla_w_gate, gla_b_gate, gla_g_norm, na_rpb,
           s5_lam_re, s5_lam_im, s5_log_dt, s5_b_re, s5_b_im, s5_c_re, s5_c_im, s5_d, s5_w_glu, s5_b_glu,
           pool_w, pool_scale):
    n_b, n_lat, d = x.shape
    depth = w_mod.shape[0]
    rows = n_lat // GRID_W
    assert d == D_MODEL and ctx.shape[1] == CTX_LEN and n_lat % TILE == 0 and n_b < 8 and rows >= NA_KR
    c_rows = jnp.zeros((8, d), F32).at[0:n_b].set(c).at[n_b].set(c_ctx)
    mod_all = _modulation(c_rows, w_mod, b_mod)
    rope_tabs = _rope_tables(n_lat)
    w_pad = _pad_w_in(w_in)
    wg, bg = _gate_weights(gla_w_gate, gla_b_gate)
    g_pre3 = g_pre.astype(F32).reshape(depth, 1, d)
    g_post3 = g_post.astype(F32).reshape(depth, 1, d)
    gn = jnp.tile(gla_g_norm.astype(F32), (1, GLA_HEADS)).reshape(depth, 1, GLA_HEADS * GLA_DV)
    rel_tab = jax.vmap(_na_rel_table)(na_rpb)
    ktoe, bexp, cexp, pw16 = jax.vmap(_s5_tables)(s5_lam_re, s5_lam_im, s5_log_dt, s5_b_re, s5_b_im,
                                                  s5_c_re, s5_c_im, s5_d)
    w_out_b = w_out.astype(BF16)
    w_glu_b = s5_w_glu.astype(BF16)
    b_glu3 = s5_b_glu.astype(F32).reshape(depth, 1, GROUP_W)
    pool_scale3 = pool_scale.astype(F32).reshape(depth, 1, GROUP_W)
    xc = ctx
    for l in range(depth):
        last = l == depth - 1
        qk, gv, lg, nq, nk, nv, su0, su1, pu, gate_cols = _inproj(x, xc, l, mod_all, g_pre3, w_pad, wg, bg,
                                                                  rope_tabs)
        y_gla = _gla(qk, gv, lg, l, gn)
        y_na = _na(nq, nk, nv, l, rel_tab)
        y_s50, y_s51 = _s5(su0, su1, l, ktoe, bexp, cexp, pw16)
        y_pool = _pool(pu, l, pool_w, pool_scale3)
        x, xc = _outproj((y_gla, y_na, y_s50, y_s51, y_pool), gate_cols, x, xc, l, mod_all, w_out_b, g_post3,
                         w_glu_b, b_glu3, with_ctx=not last)
    return x
```

```python
import functools
import math

import numpy as np
import jax
import jax.numpy as jnp
from jax import lax
from jax.experimental import pallas as pl
from jax.experimental.pallas import tpu as pltpu

F32 = jnp.float32
BF16 = jnp.bfloat16

D_MODEL = 1024
GRID_W = 64
CTX_LEN = 256
GROUP_W = 256
MIX_W = 1024
NORM_EPS = 1e-6
GLA_HEADS = 4
GLA_DV = 64
GLA_DK = 32
GLA_RANK = 16
GLA_TAU = 16.0
ROPE_BASE = 10000.0
NA_HEADS = 4
NA_DH = 64
NA_KR = 8
NA_KC = 16
S5_HG = 16
S5_G = 16
S5_P = 64
POOL_WINDOWS = (2, 4, 8, 16)
POOL_GW = 64

TILE = 256
GLA_C = 128
S5_T = 16
NA_ROWS = TILE // GRID_W
NA_REL_W = 1024
NEG_BIG = -1e30

_COL = {}
_off = 0
for _name, _w in (("gk", 128), ("gv", 256), ("gg", 128), ("nk", 256), ("nv", 256), ("su", 256),
                  ("gq", 128), ("nq", 256), ("pu", 256), ("gate", 1024)):
    _COL[_name] = (_off, _off + _w)
    _off += _w
N_PAD_COLS = _off
N_GATE_END = 128 + 256 + 2 * GLA_RANK

_VMEM_LIMIT = 56 * 1024 * 1024


def _cp(sem, vmem=_VMEM_LIMIT):
    return pltpu.CompilerParams(dimension_semantics=sem, vmem_limit_bytes=vmem)


def _dot(a, b):
    return jnp.dot(a, b, preferred_element_type=F32)


def _dot_nt(a, b):
    return lax.dot_general(a, b, (((1,), (1,)), ((), ())), preferred_element_type=F32)


def _split_bf16(a):
    hi = a.astype(BF16)
    lo = (a - hi.astype(F32)).astype(BF16)
    return hi, lo


def _mod_kernel(s_ref, w_ref, b_ref, o_ref):
    s = s_ref[...]
    a = (s * jax.nn.sigmoid(s)).astype(BF16)
    o_ref[0] = _dot(a, w_ref[0].astype(BF16)) + b_ref[0]


def _modulation(c_rows, w_mod, b_mod):
    n_l, d, d3 = w_mod.shape
    return pl.pallas_call(
        _mod_kernel,
        out_shape=jax.ShapeDtypeStruct((n_l, 8, d3), F32),
        grid=(n_l, d3 // d),
        in_specs=[pl.BlockSpec((8, d), lambda l, j: (0, 0)),
                  pl.BlockSpec((1, d, d), lambda l, j: (l, 0, j)),
                  pl.BlockSpec((1, 1, d), lambda l, j: (l, 0, j))],
        out_specs=pl.BlockSpec((1, 8, d), lambda l, j: (l, 0, j)),
        compiler_params=_cp(("parallel", "parallel")),
        name="adaln_modulation",
    )(c_rows, w_mod, b_mod.reshape(n_l, 1, d3))


def _inproj_kernel(x_ref, xc_ref, mod_ref, gpre_ref, w_ref, wg_ref, bg_ref, cos_ref, sa_ref, sb_ref,
                   qk_ref, v_ref, lg_ref, nq_ref, nk_ref, nv_ref, su0_ref, su1_ref, pu_ref, gate_ref, *, n_batch):
    b = pl.program_id(0)
    j = pl.program_id(1)
    is_ctx = j == 0
    xt = jnp.where(is_ctx, xc_ref[0], x_ref[0])
    m = mod_ref[pl.ds(jnp.where(is_ctx, n_batch, b), 1), :]
    shift = m[:, 0:D_MODEL]
    scale = m[:, D_MODEL:2 * D_MODEL]
    ms = jnp.mean(xt * xt, axis=-1, keepdims=True)
    h = xt * lax.rsqrt(ms + NORM_EPS) * gpre_ref[...] * (1.0 + scale) + shift
    hb = h.astype(BF16)

    def proj(name):
        lo, hi = _COL[name]
        return _dot(hb, w_ref[:, lo:hi])

    cos = jnp.where(is_ctx, 1.0, cos_ref[...])
    sa = jnp.where(is_ctx, 0.0, sa_ref[...])
    sb = jnp.where(is_ctx, 0.0, sb_ref[...])

    def rope(t):
        return t * cos + pltpu.roll(t, 128 - 8, 1) * sa + pltpu.roll(t, 8, 1) * sb

    qk_ref[0, :, 0:128] = rope(proj("gq")) * (GLA_DK ** -0.5)
    qk_ref[0, :, 128:256] = rope(proj("gk"))
    v_ref[0] = proj("gv")
    z = _dot(proj("gg").astype(BF16), wg_ref[...]) + bg_ref[...]
    lg_ref[0] = (jnp.minimum(z, 0.0) - jnp.log(1.0 + jnp.exp(-jnp.abs(z)))) * (1.0 / GLA_TAU)
    nq_ref[0] = proj("nq") * (NA_DH ** -0.5)
    nk_ref[0] = proj("nk")
    nv_ref[0] = proj("nv")
    su = proj("su")
    su0_ref[0] = su[:, 0:128]
    su1_ref[0] = su[:, 128:256]
    pu_ref[0] = proj("pu")
    gate_ref[0] = proj("gate")


def _layer_spec(shape, layer):
    return pl.BlockSpec((None,) + tuple(shape), lambda *_: (layer,) + (0,) * len(shape))


def _inproj(x, xc, layer, mod_all, g_pre, w_pad, wg, bg, rope_tabs):
    n_b, n_lat, d = x.shape
    n_tiles = (n_lat + CTX_LEN) // TILE
    t_all = n_lat + CTX_LEN
    lat_map = lambda b, j: (b, jnp.maximum(j - 1, 0), 0)
    tab_map = lambda b, j: (jnp.maximum(j - 1, 0), 0)
    out_map = lambda b, j: (b, j, 0)
    widths = (256, 256, 256, 256, 256, 256, 128, 128, 256, 1024)
    return pl.pallas_call(
        functools.partial(_inproj_kernel, n_batch=n_b),
        out_shape=[jax.ShapeDtypeStruct((n_b, t_all, w), F32) for w in widths],
        grid=(n_b, n_tiles),
        in_specs=[pl.BlockSpec((1, TILE, d), lat_map),
                  pl.BlockSpec((1, CTX_LEN, d), lambda b, j: (b, 0, 0)),
                  _layer_spec((8, 3 * d), layer),
                  _layer_spec((1, d), layer),
                  _layer_spec((d, N_PAD_COLS), layer),
                  _layer_spec((128, 256), layer),
                  _layer_spec((1, 256), layer),
                  pl.BlockSpec((TILE, 128), tab_map),
                  pl.BlockSpec((TILE, 128), tab_map),
                  pl.BlockSpec((TILE, 128), tab_map)],
        out_specs=[pl.BlockSpec((1, TILE, w), out_map) for w in widths],
        compiler_params=_cp(("parallel", "arbitrary")),
        name="modnorm_inproj",
    )(x, xc, mod_all, g_pre, w_pad, wg, bg, *rope_tabs)


def _gla_kernel(qk_ref, v_ref, lg_ref, gn_ref, y_ref, stf_ref, stb_ref, ob_ref, *, n_chunks, n_ctx_chunks):
    c_len = GLA_C
    row = lax.broadcasted_iota(jnp.int32, (c_len, c_len), 0)
    col = lax.broadcasted_iota(jnp.int32, (c_len, c_len), 1)
    tri_f = (col <= row).astype(BF16)
    tri_b = (col >= row).astype(BF16)
    arow = lax.broadcasted_iota(jnp.int32, (4 * c_len, c_len), 0) % c_len
    acol = lax.broadcasted_iota(jnp.int32, (4 * c_len, c_len), 1)
    amask_f = acol <= arow
    amask_b = acol >= arow
    head_k = lax.broadcasted_iota(jnp.int32, (1, 128), 1) // GLA_DK
    head_v = lax.broadcasted_iota(jnp.int32, (1, 256), 1) // GLA_DV
    bd_mask = (lax.broadcasted_iota(jnp.int32, (256, 128), 0) // GLA_DV
               == lax.broadcasted_iota(jnp.int32, (256, 128), 1) // GLA_DK)
    ones_bd = jnp.where(lax.broadcasted_iota(jnp.int32, (256, 256), 0) // GLA_DV
                        == lax.broadcasted_iota(jnp.int32, (256, 256), 1) // GLA_DV,
                        1.0 / GLA_DV, 0.0).astype(BF16)
    mid = c_len // 2

    def scan_body(i, carry):
        cb = jnp.where(i < n_ctx_chunks, n_ctx_chunks - 1 - i, n_chunks - 1 - (i - n_ctx_chunks))
        two = (0, 1)
        r0 = [pl.multiple_of(c * c_len, c_len) for c in (i, cb)]
        st_refs = (stf_ref, stb_ref)
        tri = (tri_f, tri_b)
        amask = (amask_f, amask_b)
        q = [qk_ref[0, pl.ds(r0[d], c_len), 0:128] for d in two]
        k = [qk_ref[0, pl.ds(r0[d], c_len), 128:256] for d in two]
        v = [v_ref[0, pl.ds(r0[d], c_len), :] for d in two]
        lg = [_split_bf16(lg_ref[0, pl.ds(r0[d], c_len), 128 * d:128 * (d + 1)]) for d in two]
        cum = [_dot(tri[d], lg[d][0]) + _dot(tri[d], lg[d][1]) for d in two]
        c_ref = [cum[d][mid:mid + 1, :] for d in two]
        c_end = [cum[0][c_len - 1:c_len, :], cum[1][0:1, :]]
        qt = [q[d] * jnp.exp(cum[d] - c_ref[d]) for d in two]
        kt = [k[d] * jnp.exp(c_ref[d] - cum[d]) for d in two]
        qs = [jnp.concatenate([jnp.where(head_k == hh, qt[d], 0.0) for hh in range(GLA_HEADS)],
                              axis=0).astype(BF16) for d in two]
        att = [_dot_nt(qs[d], kt[d].astype(BF16)) for d in two]
        st = [st_refs[d][...] for d in two]
        qh = [(qt[d] * jnp.exp(c_ref[d])).astype(BF16) for d in two]
        kh = [(kt[d] * jnp.exp(c_end[d] - c_ref[d])).astype(BF16) for d in two]
        o_inter = [_dot_nt(qh[d], st[d].astype(BF16)) for d in two]
        upd = [_dot(v[d].T.astype(BF16), kh[d]) for d in two]
        att = [jnp.where(amask[d], att[d], 0.0).astype(BF16) for d in two]
        r = [_dot(att[d], v[d].astype(BF16)) for d in two]
        outs = (y_ref.at[0], ob_ref)
        for d in two:
            o = o_inter[d]
            for hh in range(GLA_HEADS):
                o = o + jnp.where(head_v == hh, r[d][hh * c_len:(hh + 1) * c_len], 0.0)
            outs[d][pl.ds(r0[d], c_len), :] = o
            st_refs[d][...] = st[d] * jnp.exp(c_end[d]) + jnp.where(bd_mask, upd[d], 0.0)
        return carry

    def norm_body(c, carry):
        r0 = pl.multiple_of(c * TILE, TILE)
        o = y_ref[0, pl.ds(r0, TILE), :] + ob_ref[pl.ds(r0, TILE), :]
        sq_hi, sq_lo = _split_bf16(o * o)
        ms = _dot(sq_hi, ones_bd) + _dot(sq_lo, ones_bd)
        y_ref[0, pl.ds(r0, TILE), :] = o * lax.rsqrt(ms + NORM_EPS) * gn_ref[...]
        return carry

    stf_ref[...] = jnp.zeros_like(stf_ref)
    stb_ref[...] = jnp.zeros_like(stb_ref)
    lax.fori_loop(0, n_chunks, scan_body, 0)
    lax.fori_loop(0, n_chunks * c_len // TILE, norm_body, 0)


def _gla(qk, v, lg, layer, g_norm):
    n_b, t_all, _ = qk.shape
    blk = lambda w: pl.BlockSpec((1, t_all, w), lambda b: (b, 0, 0))
    return pl.pallas_call(
        functools.partial(_gla_kernel, n_chunks=t_all // GLA_C, n_ctx_chunks=CTX_LEN // GLA_C),
        out_shape=jax.ShapeDtypeStruct((n_b, t_all, 256), F32),
        grid=(n_b,),
        in_specs=[blk(256), blk(256), blk(256), _layer_spec((1, 256), layer)],
        out_specs=blk(256),
        scratch_shapes=[pltpu.VMEM((256, 128), F32), pltpu.VMEM((256, 128), F32), pltpu.VMEM((t_all, 256), F32)],
        compiler_params=_cp(("parallel",)),
        name="gla_mixer",
    )(qk, v, lg, g_norm)


def _na_kernel(q_ref, k0_ref, k1_ref, k2_ref, v0_ref, v1_ref, v2_ref, kc_ref, vc_ref, rel_ref, o_ref, *, rows):
    j = pl.program_id(1)

    def attend(with_window):
        if with_window:
            g = j - 1
            q_row = lax.broadcasted_iota(jnp.int32, (TILE, TILE), 0) // GRID_W
            k_row = lax.broadcasted_iota(jnp.int32, (TILE, TILE), 1) // GRID_W
            first = jnp.clip(NA_ROWS * g + q_row - NA_KR // 2, 0, rows - NA_KR) - NA_ROWS * (g - 1)
            window = [jnp.where((k_row + NA_ROWS * d >= first) & (k_row + NA_ROWS * d < first + NA_KR), 0.0, NEG_BIG)
                      for d in range(3)]
        for hh in range(NA_HEADS):
            lo, hi = hh * NA_DH, (hh + 1) * NA_DH
            qh = q_ref[0, :, lo:hi].astype(BF16)
            scores = [_dot_nt(qh, kc_ref[0, :, lo:hi].astype(BF16))]
            values = [vc_ref]
            if with_window:
                for d, (kr, vr) in enumerate(((k0_ref, v0_ref), (k1_ref, v1_ref), (k2_ref, v2_ref))):
                    ro0 = NA_KR - 1 - NA_ROWS
                    rel = jnp.concatenate(
                        [rel_ref[hh, (ro0 - i) % 2, :,
                                 GRID_W * ((ro0 - i) // 2 * 2) + d * TILE:GRID_W * ((ro0 - i) // 2 * 2) + (d + 1) * TILE]
                         for i in range(NA_ROWS)], axis=0)
                    s = _dot_nt(qh, kr[0, :, lo:hi].astype(BF16)) + rel + window[d]
                    scores.append(s)
                    values.append(vr)
            mx = scores[0]
            for s in scores[1:]:
                mx = jnp.maximum(mx, s)
            mx = jnp.max(mx, axis=-1, keepdims=True)
            den = None
            acc = None
            for s, vr in zip(scores, values):
                p = jnp.exp(s - mx)
                d = jnp.sum(p, axis=-1, keepdims=True)
                a = _dot(p.astype(BF16), vr[0, :, lo:hi].astype(BF16))
                den = d if den is None else den + d
                acc = a if acc is None else acc + a
            o_ref[0, :, lo:hi] = acc / den

    @pl.when(j == 0)
    def _():
        attend(False)

    @pl.when(j > 0)
    def _():
        attend(True)


def _na(nq, nk, nv, layer, rel_tab):
    n_b, t_all, _ = nq.shape
    n_tiles = t_all // TILE
    rows = (t_all - CTX_LEN) // GRID_W

    def nbr(delta):
        return lambda b, j: (b, jnp.clip(j + delta, 1, n_tiles - 1), 0)

    tile = lambda f: pl.BlockSpec((1, TILE, 256), f)
    return pl.pallas_call(
        functools.partial(_na_kernel, rows=rows),
        out_shape=jax.ShapeDtypeStruct((n_b, t_all, 256), F32),
        grid=(n_b, n_tiles),
        in_specs=[tile(lambda b, j: (b, j, 0)),
                  tile(nbr(-1)), tile(nbr(0)), tile(nbr(1)),
                  tile(nbr(-1)), tile(nbr(0)), tile(nbr(1)),
                  tile(lambda b, j: (b, 0, 0)), tile(lambda b, j: (b, 0, 0)),
                  _layer_spec((NA_HEADS, 2, GRID_W, NA_REL_W), layer)],
        out_specs=tile(lambda b, j: (b, j, 0)),
        compiler_params=_cp(("parallel", "arbitrary")),
        name="na_mixer",
    )(nq, nk, nk, nk, nv, nv, nv, nk, nv, rel_tab)


def _na_rel_table(rpb):
    col = np.arange(GRID_W)
    cs = np.clip(col - NA_KC // 2, 0, GRID_W - NA_KC)
    col_mask = (col[None, :] >= cs[:, None]) & (col[None, :] < cs[:, None] + NA_KC)
    co = np.clip(col[None, :] - col[:, None], -(NA_KC - 1), NA_KC - 1) + (NA_KC - 1)
    onehot = (co[None] == np.arange(2 * NA_KC - 1)[:, None, None]).astype(np.float32)
    toe = jnp.einsum("hrc,cqk->hqrk", rpb.astype(F32), jnp.asarray(onehot),
                     precision=lax.Precision.HIGHEST)
    toe = jnp.where(jnp.asarray(col_mask)[None, :, None, :], toe, NEG_BIG)
    flat = toe.reshape(NA_HEADS, GRID_W, (2 * NA_KR - 1) * GRID_W)
    flat = jnp.pad(flat, ((0, 0), (0, 0), (0, NA_REL_W + GRID_W - flat.shape[-1])), constant_values=NEG_BIG)
    return jnp.stack([flat[:, :, :NA_REL_W], flat[:, :, GRID_W:]], axis=1)


def _s5_kernel(su0_ref, su1_ref, ktoe_ref, bexp_ref, cexp_ref, pw_ref, y0_ref, y1_ref, a_ref, e_ref, x_ref,
               *, n_chunks, n_ctx_chunks):
    blk = 8
    lane_blk = lax.broadcasted_iota(jnp.int32, (blk, 128), 1) // S5_HG
    masks = [lane_blk == m for m in range(8)]
    gp = S5_G * S5_P

    def pick(vals):
        acc = vals[0]
        for k in range(1, 8):
            acc = jnp.where(masks[k], vals[k], acc)
        return acc

    def block_transpose(src):
        rolled = []
        for d in range(8):
            diag = pick([src[(k + d) % 8] for k in range(8)])
            rolled.append(pltpu.roll(diag, S5_HG * d, 1) if d else diag)
        return [pick([rolled[(m - a) % 8] for m in range(8)]) for a in range(8)]

    def regroup_in(rb, carry):
        c0 = pl.multiple_of(rb * blk, blk)
        for hg, ref in enumerate((su0_ref, su1_ref)):
            for o in range(2):
                src = [ref[0, pl.ds(rb * (blk * S5_T) + o * 8 + m, blk, stride=S5_T), :] for m in range(8)]
                for a, val in enumerate(block_transpose(src)):
                    a_ref[hg * 8 + a, pl.ds(c0, blk), o * 128:(o + 1) * 128] = val
        return carry

    lax.fori_loop(0, n_chunks // blk, regroup_in, 0)

    for g in range(S5_G):
        e = _dot(a_ref[g].astype(BF16), bexp_ref[g])
        for kind in range(4):
            e_ref[kind, :, g * S5_P:(g + 1) * S5_P] = e[:, kind * S5_P:(kind + 1) * S5_P]

    afr, afi, abr, abi = pw_ref[0:1, :], pw_ref[1:2, :], pw_ref[2:3, :], pw_ref[3:4, :]

    def scan_body(i, carry):
        fr, fi, br, bi = carry
        cf = i
        cb = jnp.where(i < n_ctx_chunks, n_ctx_chunks - 1 - i, n_chunks - 1 - (i - n_ctx_chunks))
        x_ref[0, pl.ds(cf, 1), :] = fr
        x_ref[1, pl.ds(cf, 1), :] = fi
        x_ref[2, pl.ds(cb, 1), :] = br
        x_ref[3, pl.ds(cb, 1), :] = bi
        er = e_ref[0, pl.ds(cf, 1), :]
        ei = e_ref[1, pl.ds(cf, 1), :]
        gr = e_ref[2, pl.ds(cb, 1), :]
        gi = e_ref[3, pl.ds(cb, 1), :]
        return (afr * fr - afi * fi + er, afr * fi + afi * fr + ei,
                abr * br - abi * bi + gr, abr * bi + abi * br + gi)

    z = jnp.zeros((1, gp), F32)
    lax.fori_loop(0, n_chunks, scan_body, (z, z, z, z))

    for g in range(S5_G):
        y = _dot(a_ref[g].astype(BF16), ktoe_ref[g])
        for kind in range(4):
            y = y + _dot(x_ref[kind, :, g * S5_P:(g + 1) * S5_P].astype(BF16), cexp_ref[g, kind])
        a_ref[g] = y

    def regroup_out(rb, carry):
        c0 = pl.multiple_of(rb * blk, blk)
        for og, ref in enumerate((y0_ref, y1_ref)):
            for ht in range(2):
                src = [a_ref[og * 8 + m, pl.ds(c0, blk), ht * 128:(ht + 1) * 128] for m in range(8)]
                for a, val in enumerate(block_transpose(src)):
                    ref[0, pl.ds(rb * (blk * S5_T) + ht * 8 + a, blk, stride=S5_T), :] = val
        return carry

    lax.fori_loop(0, n_chunks // blk, regroup_out, 0)


def _s5(su0, su1, layer, ktoe, bexp, cexp, pw16):
    n_b, t_all, _ = su0.shape
    n_chunks = t_all // S5_T
    gp = S5_G * S5_P
    blk = pl.BlockSpec((1, t_all, 128), lambda b: (b, 0, 0))
    return pl.pallas_call(
        functools.partial(_s5_kernel, n_chunks=n_chunks, n_ctx_chunks=CTX_LEN // S5_T),
        out_shape=[jax.ShapeDtypeStruct((n_b, t_all, 128), F32)] * 2,
        grid=(n_b,),
        in_specs=[blk, blk, _layer_spec((S5_G, 256, 256), layer), _layer_spec((S5_G, 256, 256), layer),
                  _layer_spec((S5_G, 4, S5_P, 256), layer), _layer_spec((4, gp), layer)],
        out_specs=[blk, blk],
        scratch_shapes=[pltpu.VMEM((S5_G, n_chunks, S5_T * S5_HG), F32),
                        pltpu.VMEM((4, n_chunks, gp), F32),
                        pltpu.VMEM((4, n_chunks, gp), F32)],
        compiler_params=_cp(("parallel",)),
        name="s5_mixer",
    )(su0, su1, ktoe, bexp, cexp, pw16)


def _s5_tables(lam_re, lam_im, log_dt, b_re, b_im, c_re, c_im, d_skip):
    hp = lax.Precision.HIGHEST
    lam_re, lam_im, log_dt = lam_re.astype(F32), lam_im.astype(F32), log_dt.astype(F32)
    b_re, b_im, c_re, c_im = b_re.astype(F32), b_im.astype(F32), c_re.astype(F32), c_im.astype(F32)
    dt = jnp.exp(log_dt)[..., None]
    n = jnp.arange(-S5_T, S5_T + 1, dtype=F32)[:, None, None, None]
    pw_re = jnp.exp(n * lam_re * dt) * jnp.cos(n * lam_im * dt)
    pw_im = jnp.exp(n * lam_re * dt) * jnp.sin(n * lam_im * dt)
    neg_re, neg_im = pw_re[S5_T::-1], pw_im[S5_T::-1]
    pw_re, pw_im = pw_re[S5_T:], pw_im[S5_T:]
    lb_re, lb_im = pw_re[1], pw_im[1]
    num_re, num_im = lb_re - 1.0, lb_im
    den = lam_re * lam_re + lam_im * lam_im
    coef_re = ((num_re * lam_re + num_im * lam_im) / den)[..., None]
    coef_im = ((num_im * lam_re - num_re * lam_im) / den)[..., None]
    bb_re = coef_re * b_re - coef_im * b_im
    bb_im = coef_re * b_im + coef_im * b_re
    steps = np.arange(S5_T)

    def cmul_b(p_re, p_im, d):
        re = p_re[..., None] * bb_re[d][None] - p_im[..., None] * bb_im[d][None]
        im = p_re[..., None] * bb_im[d][None] + p_im[..., None] * bb_re[d][None]
        to = lambda z: z.transpose(1, 0, 3, 2).reshape(S5_G, S5_T * S5_HG, S5_P)
        return to(re), to(im)

    def cmul_c(p_re, p_im, d):
        re = c_re[d][None] * p_re[:, :, None, :] - c_im[d][None] * p_im[:, :, None, :]
        im = c_re[d][None] * p_im[:, :, None, :] + c_im[d][None] * p_re[:, :, None, :]
        to = lambda z: z.transpose(1, 3, 0, 2).reshape(S5_G, S5_P, S5_T * S5_HG)
        return to(re), to(-im)

    def lag_operator(u, v):
        return jnp.einsum("gap,gpb->gab", jnp.concatenate(u, axis=-1), jnp.concatenate(v, axis=1), precision=hp)

    kf = lag_operator(cmul_b(neg_re[steps, 0], neg_im[steps, 0], 0), cmul_c(pw_re[steps, 0], pw_im[steps, 0], 0))
    kb = lag_operator(cmul_b(pw_re[steps, 1], pw_im[steps, 1], 1), cmul_c(neg_re[steps, 1], neg_im[steps, 1], 1))
    tok = np.arange(S5_T * S5_HG) // S5_HG
    skip = jnp.asarray(np.eye(S5_T * S5_HG, dtype=np.float32)) * jnp.tile(
        d_skip.astype(F32).reshape(S5_G, S5_HG), (1, S5_T))[:, None, :]
    ktoe = (jnp.where(jnp.asarray(tok[None, :] >= tok[:, None]), kf, 0.0)
            + jnp.where(jnp.asarray(tok[:, None] >= tok[None, :]), kb, 0.0) + skip)
    bexp = jnp.concatenate(cmul_b(pw_re[S5_T - 1 - steps, 0], pw_im[S5_T - 1 - steps, 0], 0)
                           + cmul_b(pw_re[steps, 1], pw_im[steps, 1], 1), axis=-1)
    cexp = jnp.stack(cmul_c(pw_re[steps + 1, 0], pw_im[steps + 1, 0], 0)
                     + cmul_c(pw_re[S5_T - steps, 1], pw_im[S5_T - steps, 1], 1), axis=1)
    pw16 = jnp.stack([pw_re[S5_T, 0].reshape(-1), pw_im[S5_T, 0].reshape(-1),
                      pw_re[S5_T, 1].reshape(-1), pw_im[S5_T, 1].reshape(-1)], axis=0)
    return ktoe.astype(BF16), bexp.astype(BF16), cexp.astype(BF16), pw16


def _pool_kernel(u_ref, w_ref, scale_ref, o_ref, pad_ref, *, n_lat):
    pad = 8
    seg = ((pad, 0, CTX_LEN), (pad + CTX_LEN + 2 * pad, CTX_LEN, n_lat))
    pad_ref[...] = jnp.zeros_like(pad_ref)
    for p0, u0, n in seg:
        pad_ref[p0:p0 + n, :] = u_ref[0, u0:u0 + n, :]
    for p0, u0, n in seg:
        for t0 in range(0, n, TILE):
            t = t0 + lax.broadcasted_iota(jnp.int32, (TILE, 1), 0)
            outs = []
            for i, w in enumerate(POOL_WINDOWS):
                lo, hi = i * POOL_GW, (i + 1) * POOL_GW
                acc = None
                for jj in range(w):
                    s = p0 + t0 + jj - w // 2
                    piece = pad_ref[s:s + TILE, lo:hi]
                    acc = piece if acc is None else acc + piece
                cnt = (jnp.clip(t - w // 2 + w, 0, n) - jnp.clip(t - w // 2, 0, n)).astype(F32)
                g = u_ref[0, u0 + t0:u0 + t0 + TILE, lo:hi]
                outs.append(_dot((acc / cnt - g).astype(BF16), w_ref[i].astype(BF16)))
            o_ref[0, u0 + t0:u0 + t0 + TILE, :] = jnp.concatenate(outs, axis=-1) * scale_ref[...]


def _pool(pu, layer, w_pool, scale):
    n_b, t_all, _ = pu.shape
    blk = pl.BlockSpec((1, t_all, 256), lambda b: (b, 0, 0))
    return pl.pallas_call(
        functools.partial(_pool_kernel, n_lat=t_all - CTX_LEN),
        out_shape=jax.ShapeDtypeStruct((n_b, t_all, 256), F32),
        grid=(n_b,),
        in_specs=[blk, _layer_spec((len(POOL_WINDOWS), POOL_GW, POOL_GW), layer), _layer_spec((1, 256), layer)],
        out_specs=blk,
        scratch_shapes=[pltpu.VMEM((t_all + 32, 256), F32)],
        compiler_params=_cp(("parallel",)),
        name="pool_mixer",
    )(pu, w_pool, scale)


def _gelu_tanh(x):
    return 0.5 * x * (1.0 + jnp.tanh(math.sqrt(2.0 / math.pi) * (x + 0.044715 * (x * x * x))))


def _outproj_kernel(ygla_ref, yna_ref, ys50_ref, ys51_ref, ypool_ref, gate_ref, x_ref, xc_ref, mod_ref, wout_ref,
                    gpost_ref, wglu_ref, bglu_ref, *out_refs, n_batch, with_ctx):
    b = pl.program_id(0)
    j = pl.program_id(1)
    is_ctx = jnp.logical_and(j == 0, with_ctx)
    gate_m = mod_ref[pl.ds(jnp.where(is_ctx, n_batch, b), 1), :][:, 2 * D_MODEL:3 * D_MODEL]
    xt = jnp.where(is_ctx, xc_ref[0], x_ref[0]) if with_ctx else x_ref[0]
    gc = gate_ref[0]
    sg = gc * jax.nn.sigmoid(gc)
    g5 = _gelu_tanh(jnp.concatenate([ys50_ref[0], ys51_ref[0]], axis=-1))
    s5o = g5 * jax.nn.sigmoid(_dot(g5.astype(BF16), wglu_ref[...]) + bglu_ref[...])
    parts = (ygla_ref[0], yna_ref[0], s5o, ypool_ref[0])
    acc = None
    for i, part in enumerate(parts):
        lo, hi = i * GROUP_W, (i + 1) * GROUP_W
        a = _dot((part * sg[:, lo:hi]).astype(BF16), wout_ref[lo:hi, :])
        acc = a if acc is None else acc + a
    ms = jnp.mean(acc * acc, axis=-1, keepdims=True)
    res = xt + gate_m * (acc * lax.rsqrt(ms + NORM_EPS) * gpost_ref[...])
    if with_ctx:
        xo_ref, xco_ref = out_refs

        @pl.when(j == 0)
        def _():
            xco_ref[0] = res

        @pl.when(j > 0)
        def _():
            xo_ref[0] = res
    else:
        out_refs[0][0] = res


def _outproj(ys, gate_cols, x, xc, layer, mod_all, w_out, g_post, w_glu, b_glu, with_ctx):
    n_b, n_lat, d = x.shape
    n_lat_tiles = n_lat // TILE
    off = 0 if with_ctx else 1
    n_steps = n_lat_tiles + (1 if with_ctx else 0)
    mix_map = lambda b, j: (b, j + off, 0)
    lat_map = (lambda b, j: (b, jnp.maximum(j - 1, 0), 0)) if with_ctx else (lambda b, j: (b, j, 0))
    ctx_map = lambda b, j: (b, 0, 0)
    out_shape = [jax.ShapeDtypeStruct(x.shape, F32)]
    out_specs = [pl.BlockSpec((1, TILE, d), lat_map)]
    if with_ctx:
        out_shape.append(jax.ShapeDtypeStruct(xc.shape, F32))
        out_specs.append(pl.BlockSpec((1, CTX_LEN, d), ctx_map))
    outs = pl.pallas_call(
        functools.partial(_outproj_kernel, n_batch=n_b, with_ctx=with_ctx),
        out_shape=out_shape,
        grid=(n_b, n_steps),
        in_specs=[pl.BlockSpec((1, TILE, w), mix_map) for w in (256, 256, 128, 128, 256)]
                 + [pl.BlockSpec((1, TILE, MIX_W), mix_map),
                    pl.BlockSpec((1, TILE, d), lat_map),
                    pl.BlockSpec((1, CTX_LEN, d), ctx_map),
                    _layer_spec((8, 3 * d), layer),
                    _layer_spec((MIX_W, d), layer),
                    _layer_spec((1, d), layer),
                    _layer_spec((GROUP_W, GROUP_W), layer),
                    _layer_spec((1, GROUP_W), layer)],
        out_specs=out_specs,
        compiler_params=_cp(("parallel", "arbitrary")),
        name="gate_outproj_residual",
    )(*ys, gate_cols, x, xc, mod_all, w_out, g_post, w_glu, b_glu)
    return (outs[0], outs[1]) if with_ctx else (outs[0], None)


def _rope_tables(n_tok):
    t = np.arange(n_tok)
    half = GLA_DK // 2
    freqs = ROPE_BASE ** (-np.arange(0, half, 2, dtype=np.float32) / half)
    d = np.arange(GLA_DK)
    pos = np.where((d // half)[None, :] == 0, (t // GRID_W)[:, None], (t % GRID_W)[:, None]).astype(np.float32)
    ang = jnp.asarray(pos) * jnp.asarray(freqs[(d % half) % (half // 2)])[None, :]
    first = jnp.asarray(((d % half) < half // 2)[None, :])
    cos, sin = jnp.cos(ang), jnp.sin(ang)
    tabs = (cos, jnp.where(first, -sin, 0.0), jnp.where(first, 0.0, sin))
    return tuple(jnp.tile(z, (1, GLA_HEADS)).astype(F32) for z in tabs)


def _pad_w_in(w_in):
    zeros = jnp.zeros(w_in.shape[:2] + (128 - 2 * GLA_RANK,), w_in.dtype)
    return jnp.concatenate([w_in[:, :, :N_GATE_END], zeros, w_in[:, :, N_GATE_END:]], axis=2).astype(BF16)


def _gate_weights(w_gate, b_gate):
    n_l = w_gate.shape[0]
    wg = jnp.zeros((n_l, 128, 256), F32)
    wg = wg.at[:, 0:GLA_RANK, 0:128].set(w_gate[:, 0].astype(F32))
    wg = wg.at[:, GLA_RANK:2 * GLA_RANK, 128:256].set(w_gate[:, 1].astype(F32))
    return wg.astype(BF16), b_gate.astype(F32).reshape(n_l, 1, 256)


def kernel(x, c, ctx, c_ctx, w_mod, b_mod, g_pre, g_post, w_in, w_out, gla_w_gate, gla_b_gate, gla_g_norm, na_rpb,
           s5_lam_re, s5_lam_im, s5_log_dt, s5_b_re, s5_b_im, s5_c_re, s5_c_im, s5_d, s5_w_glu, s5_b_glu,
           pool_w, pool_scale):
    n_b, n_lat, d = x.shape
    depth = w_mod.shape[0]
    rows = n_lat // GRID_W
    assert d == D_MODEL and ctx.shape[1] == CTX_LEN and n_lat % TILE == 0 and n_b < 8 and rows >= NA_KR
    c_rows = jnp.zeros((8, d), F32).at[0:n_b].set(c).at[n_b].set(c_ctx)
    mod_all = _modulation(c_rows, w_mod, b_mod)
    rope_tabs = _rope_tables(n_lat)
    w_pad = _pad_w_in(w_in)
    wg, bg = _gate_weights(gla_w_gate, gla_b_gate)
    g_pre3 = g_pre.astype(F32).reshape(depth, 1, d)
    g_post3 = g_post.astype(F32).reshape(depth, 1, d)
    gn = jnp.tile(gla_g_norm.astype(F32), (1, GLA_HEADS)).reshape(depth, 1, GLA_HEADS * GLA_DV)
    rel_tab = jax.vmap(_na_rel_table)(na_rpb)
    ktoe, bexp, cexp, pw16 = jax.vmap(_s5_tables)(s5_lam_re, s5_lam_im, s5_log_dt, s5_b_re, s5_b_im,
                                                  s5_c_re, s5_c_im, s5_d)
    w_out_b = w_out.astype(BF16)
    w_glu_b = s5_w_glu.astype(BF16)
    b_glu3 = s5_b_glu.astype(F32).reshape(depth, 1, GROUP_W)
    pool_scale3 = pool_scale.astype(F32).reshape(depth, 1, GROUP_W)
    xc = ctx
    for l in range(depth):
        last = l == depth - 1
        qk, gv, lg, nq, nk, nv, su0, su1, pu, gate_cols = _inproj(x, xc, l, mod_all, g_pre3, w_pad, wg, bg,
                                                                  rope_tabs)
        y_gla = _gla(qk, gv, lg, l, gn)
        y_na = _na(nq, nk, nv, l, rel_tab)
        y_s50, y_s51 = _s5(su0, su1, l, ktoe, bexp, cexp, pw16)
        y_pool = _pool(pu, l, pool_w, pool_scale3)
        x, xc = _outproj((y_gla, y_na, y_s50, y_s51, y_pool), gate_cols, x, xc, l, mod_all, w_out_b, g_post3,
                         w_glu_b, b_glu3, with_ctx=not last)
    return x
```

```python
import functools
import math

import numpy as np
import jax
import jax.numpy as jnp
from jax import lax
from jax.experimental import pallas as pl
from jax.experimental.pallas import tpu as pltpu

F32 = jnp.float32
BF16 = jnp.bfloat16

D_MODEL = 1024
GRID_W = 64
CTX_LEN = 256
GROUP_W = 256
MIX_W = 1024
NORM_EPS = 1e-6
GLA_HEADS = 4
GLA_DV = 64
GLA_DK = 32
GLA_RANK = 16
GLA_TAU = 16.0
ROPE_BASE = 10000.0
NA_HEADS = 4
NA_DH = 64
NA_KR = 8
NA_KC = 16
S5_HG = 16
S5_G = 16
S5_P = 64
POOL_WINDOWS = (2, 4, 8, 16)
POOL_GW = 64

TILE = 256
GLA_C = 128
S5_T = 16
NA_ROWS = TILE // GRID_W
NA_REL_W = 1024
NEG_BIG = -1e30

_COL = {}
_off = 0
for _name, _w in (("gk", 128), ("gv", 256), ("gg", 128), ("nk", 256), ("nv", 256), ("su", 256),
                  ("gq", 128), ("nq", 256), ("pu", 256), ("gate", 1024)):
    _COL[_name] = (_off, _off + _w)
    _off += _w
N_PAD_COLS = _off
N_GATE_END = 128 + 256 + 2 * GLA_RANK

_VMEM_LIMIT = 56 * 1024 * 1024


def _cp(sem, vmem=_VMEM_LIMIT):
    return pltpu.CompilerParams(dimension_semantics=sem, vmem_limit_bytes=vmem)


def _dot(a, b):
    return jnp.dot(a, b, preferred_element_type=F32)


def _dot_nt(a, b):
    return lax.dot_general(a, b, (((1,), (1,)), ((), ())), preferred_element_type=F32)


def _split_bf16(a):
    hi = a.astype(BF16)
    lo = (a - hi.astype(F32)).astype(BF16)
    return hi, lo


def _mod_kernel(s_ref, w_ref, b_ref, o_ref):
    s = s_ref[...]
    a = (s * jax.nn.sigmoid(s)).astype(BF16)
    o_ref[0] = _dot(a, w_ref[0].astype(BF16)) + b_ref[0]


def _modulation(c_rows, w_mod, b_mod):
    n_l, d, d3 = w_mod.shape
    return pl.pallas_call(
        _mod_kernel,
        out_shape=jax.ShapeDtypeStruct((n_l, 8, d3), F32),
        grid=(n_l, d3 // d),
        in_specs=[pl.BlockSpec((8, d), lambda l, j: (0, 0)),
                  pl.BlockSpec((1, d, d), lambda l, j: (l, 0, j)),
                  pl.BlockSpec((1, 1, d), lambda l, j: (l, 0, j))],
        out_specs=pl.BlockSpec((1, 8, d), lambda l, j: (l, 0, j)),
        compiler_params=_cp(("parallel", "parallel")),
        name="adaln_modulation",
    )(c_rows, w_mod, b_mod.reshape(n_l, 1, d3))


def _inproj_kernel(x_ref, xc_ref, mod_ref, gpre_ref, w_ref, wg_ref, bg_ref, cos_ref, sa_ref, sb_ref,
                   qk_ref, v_ref, lg_ref, nq_ref, nk_ref, nv_ref, su0_ref, su1_ref, pu_ref, gate_ref, *, n_batch):
    b = pl.program_id(0)
    j = pl.program_id(1)
    is_ctx = j == 0
    xt = jnp.where(is_ctx, xc_ref[0], x_ref[0])
    m = mod_ref[pl.ds(jnp.where(is_ctx, n_batch, b), 1), :]
    shift = m[:, 0:D_MODEL]
    scale = m[:, D_MODEL:2 * D_MODEL]
    ms = jnp.mean(xt * xt, axis=-1, keepdims=True)
    h = xt * lax.rsqrt(ms + NORM_EPS) * gpre_ref[...] * (1.0 + scale) + shift
    hb = h.astype(BF16)

    def proj(name):
        lo, hi = _COL[name]
        return _dot(hb, w_ref[:, lo:hi])

    cos = jnp.where(is_ctx, 1.0, cos_ref[...])
    sa = jnp.where(is_ctx, 0.0, sa_ref[...])
    sb = jnp.where(is_ctx, 0.0, sb_ref[...])

    def rope(t):
        return t * cos + pltpu.roll(t, 128 - 8, 1) * sa + pltpu.roll(t, 8, 1) * sb

    qk_ref[0, :, 0:128] = rope(proj("gq")) * (GLA_DK ** -0.5)
    qk_ref[0, :, 128:256] = rope(proj("gk"))
    v_ref[0] = proj("gv")
    z = _dot(proj("gg").astype(BF16), wg_ref[...]) + bg_ref[...]
    lg_ref[0] = (jnp.minimum(z, 0.0) - jnp.log(1.0 + jnp.exp(-jnp.abs(z)))) * (1.0 / GLA_TAU)
    nq_ref[0] = (proj("nq") * (NA_DH ** -0.5)).astype(BF16)
    nk_ref[0] = proj("nk").T.astype(BF16)
    nv_ref[0] = proj("nv").astype(BF16)
    su = proj("su")
    su0_ref[0] = su[:, 0:128]
    su1_ref[0] = su[:, 128:256]
    pu_ref[0] = proj("pu")
    gate_ref[0] = proj("gate").astype(BF16)


def _layer_spec(shape, layer):
    return pl.BlockSpec((None,) + tuple(shape), lambda *_: (layer,) + (0,) * len(shape))


def _inproj(x, xc, layer, mod_all, g_pre, w_pad, wg, bg, rope_tabs):
    n_b, n_lat, d = x.shape
    n_tiles = (n_lat + CTX_LEN) // TILE
    t_all = n_lat + CTX_LEN
    lat_map = lambda b, j: (b, jnp.maximum(j - 1, 0), 0)
    tab_map = lambda b, j: (jnp.maximum(j - 1, 0), 0)
    out_map = lambda b, j: (b, j, 0)
    outs = (("qk", 256, F32), ("gv", 256, F32), ("lg", 256, F32), ("nq", 256, BF16), ("nkT", 256, BF16),
            ("nv", 256, BF16), ("su0", 128, F32), ("su1", 128, F32), ("pu", 256, F32), ("gate", 1024, BF16))
    out_shape = [jax.ShapeDtypeStruct((n_b, w, t_all) if n == "nkT" else (n_b, t_all, w), dt) for n, w, dt in outs]
    out_specs = [pl.BlockSpec((1, w, TILE), lambda b, j: (b, 0, j)) if n == "nkT"
                 else pl.BlockSpec((1, TILE, w), out_map) for n, w, dt in outs]
    return pl.pallas_call(
        functools.partial(_inproj_kernel, n_batch=n_b),
        out_shape=out_shape,
        grid=(n_b, n_tiles),
        in_specs=[pl.BlockSpec((1, TILE, d), lat_map),
                  pl.BlockSpec((1, CTX_LEN, d), lambda b, j: (b, 0, 0)),
                  _layer_spec((8, 3 * d), layer),
                  _layer_spec((1, d), layer),
                  _layer_spec((d, N_PAD_COLS), layer),
                  _layer_spec((128, 256), layer),
                  _layer_spec((1, 256), layer),
                  pl.BlockSpec((TILE, 128), tab_map),
                  pl.BlockSpec((TILE, 128), tab_map),
                  pl.BlockSpec((TILE, 128), tab_map)],
        out_specs=out_specs,
        compiler_params=_cp(("parallel", "arbitrary")),
        name="modnorm_inproj",
    )(x, xc, mod_all, g_pre, w_pad, wg, bg, *rope_tabs)


def _gla_kernel(qk_ref, v_ref, lg_ref, gn_ref, y_ref, stf_ref, stb_ref, of_ref, ob_ref, *, n_chunks, n_ctx_chunks):
    c_len = GLA_C
    row = lax.broadcasted_iota(jnp.int32, (c_len, c_len), 0)
    col = lax.broadcasted_iota(jnp.int32, (c_len, c_len), 1)
    tri_f = (col <= row).astype(BF16)
    tri_b = (col >= row).astype(BF16)
    arow = lax.broadcasted_iota(jnp.int32, (4 * c_len, c_len), 0) % c_len
    acol = lax.broadcasted_iota(jnp.int32, (4 * c_len, c_len), 1)
    amask_f = acol <= arow
    amask_b = acol >= arow
    head_k = lax.broadcasted_iota(jnp.int32, (1, 128), 1) // GLA_DK
    head_v = lax.broadcasted_iota(jnp.int32, (1, 256), 1) // GLA_DV
    bd_mask = (lax.broadcasted_iota(jnp.int32, (256, 128), 0) // GLA_DV
               == lax.broadcasted_iota(jnp.int32, (256, 128), 1) // GLA_DK)
    ones_bd = jnp.where(lax.broadcasted_iota(jnp.int32, (256, 256), 0) // GLA_DV
                        == lax.broadcasted_iota(jnp.int32, (256, 256), 1) // GLA_DV,
                        1.0 / GLA_DV, 0.0).astype(BF16)
    mid = c_len // 2

    def scan_body(i, carry):
        cb = jnp.where(i < n_ctx_chunks, n_ctx_chunks - 1 - i, n_chunks - 1 - (i - n_ctx_chunks))
        two = (0, 1)
        r0 = [pl.multiple_of(c * c_len, c_len) for c in (i, cb)]
        st_refs = (stf_ref, stb_ref)
        tri = (tri_f, tri_b)
        amask = (amask_f, amask_b)
        q = [qk_ref[0, pl.ds(r0[d], c_len), 0:128] for d in two]
        k = [qk_ref[0, pl.ds(r0[d], c_len), 128:256] for d in two]
        v = [v_ref[0, pl.ds(r0[d], c_len), :] for d in two]
        lg = [_split_bf16(lg_ref[0, pl.ds(r0[d], c_len), 128 * d:128 * (d + 1)]) for d in two]
        cum = [_dot(tri[d], lg[d][0]) + _dot(tri[d], lg[d][1]) for d in two]
        c_ref = [cum[d][mid:mid + 1, :] for d in two]
        c_end = [cum[0][c_len - 1:c_len, :], cum[1][0:1, :]]
        qt = [q[d] * jnp.exp(cum[d] - c_ref[d]) for d in two]
        kt = [k[d] * jnp.exp(c_ref[d] - cum[d]) for d in two]
        qs = [jnp.concatenate([jnp.where(head_k == hh, qt[d], 0.0) for hh in range(GLA_HEADS)],
                              axis=0).astype(BF16) for d in two]
        att = [_dot_nt(qs[d], kt[d].astype(BF16)) for d in two]
        st = [st_refs[d][...] for d in two]
        qh = [(qt[d] * jnp.exp(c_ref[d])).astype(BF16) for d in two]
        kh = [(kt[d] * jnp.exp(c_end[d] - c_ref[d])).astype(BF16) for d in two]
        o_inter = [_dot_nt(qh[d], st[d].astype(BF16)) for d in two]
        upd = [_dot(v[d].T.astype(BF16), kh[d]) for d in two]
        att = [jnp.where(amask[d], att[d], 0.0).astype(BF16) for d in two]
        r = [_dot(att[d], v[d].astype(BF16)) for d in two]
        outs = (of_ref, ob_ref)
        for d in two:
            o = o_inter[d]
            for hh in range(GLA_HEADS):
                o = o + jnp.where(head_v == hh, r[d][hh * c_len:(hh + 1) * c_len], 0.0)
            outs[d][pl.ds(r0[d], c_len), :] = o
            st_refs[d][...] = st[d] * jnp.exp(c_end[d]) + jnp.where(bd_mask, upd[d], 0.0)
        return carry

    def norm_body(c, carry):
        r0 = pl.multiple_of(c * TILE, TILE)
        o = of_ref[pl.ds(r0, TILE), :] + ob_ref[pl.ds(r0, TILE), :]
        sq_hi, sq_lo = _split_bf16(o * o)
        ms = _dot(sq_hi, ones_bd) + _dot(sq_lo, ones_bd)
        y_ref[0, pl.ds(r0, TILE), :] = (o * lax.rsqrt(ms + NORM_EPS) * gn_ref[...]).astype(y_ref.dtype)
        return carry

    stf_ref[...] = jnp.zeros_like(stf_ref)
    stb_ref[...] = jnp.zeros_like(stb_ref)
    lax.fori_loop(0, n_chunks, scan_body, 0)
    lax.fori_loop(0, n_chunks * c_len // TILE, norm_body, 0)


def _gla(qk, v, lg, layer, g_norm):
    n_b, t_all, _ = qk.shape
    blk = lambda w: pl.BlockSpec((1, t_all, w), lambda b: (b, 0, 0))
    return pl.pallas_call(
        functools.partial(_gla_kernel, n_chunks=t_all // GLA_C, n_ctx_chunks=CTX_LEN // GLA_C),
        out_shape=jax.ShapeDtypeStruct((n_b, t_all, 256), BF16),
        grid=(n_b,),
        in_specs=[blk(256), blk(256), blk(256), _layer_spec((1, 256), layer)],
        out_specs=blk(256),
        scratch_shapes=[pltpu.VMEM((256, 128), F32), pltpu.VMEM((256, 128), F32),
                        pltpu.VMEM((t_all, 256), F32), pltpu.VMEM((t_all, 256), F32)],
        compiler_params=_cp(("parallel",)),
        name="gla_mixer",
    )(qk, v, lg, g_norm)


def _na_kernel(q_ref, kt0_ref, kt1_ref, kt2_ref, v0_ref, v1_ref, v2_ref, ktc_ref, vc_ref, rel_ref, o_ref, *, rows):
    j = pl.program_id(1)
    left = lax.broadcasted_iota(jnp.int32, (1, 128), 1) < NA_DH
    ones_blk = jnp.ones((TILE, 128), BF16)
    ro0 = NA_KR - 1 - NA_ROWS

    def attend(with_window):
        key_tiles = [(ktc_ref, vc_ref, None)]
        if with_window:
            g = j - 1
            q_row = lax.broadcasted_iota(jnp.int32, (TILE, TILE), 0) // GRID_W
            k_row = lax.broadcasted_iota(jnp.int32, (TILE, TILE), 1) // GRID_W
            first = jnp.clip(NA_ROWS * g + q_row - NA_KR // 2, 0, rows - NA_KR) - NA_ROWS * (g - 1)
            for d, (kr, vr) in enumerate(((kt0_ref, v0_ref), (kt1_ref, v1_ref), (kt2_ref, v2_ref))):
                window = None if d == 1 else jnp.where(
                    (k_row + NA_ROWS * d >= first) & (k_row + NA_ROWS * d < first + NA_KR), 0.0, NEG_BIG)
                key_tiles.append((kr, vr, (d, window)))
        for pair in range(NA_HEADS // 2):
            lanes = slice(128 * pair, 128 * (pair + 1))
            qp = q_ref[0, :, lanes]
            vaug = [jnp.concatenate([vr[0, :, lanes], ones_blk], axis=1) for _, vr, _ in key_tiles]
            o_pair = None
            for sub in range(2):
                hh = 2 * pair + sub
                qh = jnp.where(left if sub == 0 else jnp.logical_not(left), qp, jnp.zeros_like(qp))
                scores = []
                for kr, _, info in key_tiles:
                    s = _dot(qh, kr[0, lanes, :])
                    if info is not None:
                        d, window = info
                        rel = jnp.concatenate(
                            [rel_ref[hh, (ro0 - i) % 2, :, GRID_W * ((ro0 - i) // 2 * 2) + d * TILE:
                                     GRID_W * ((ro0 - i) // 2 * 2) + (d + 1) * TILE] for i in range(NA_ROWS)], axis=0)
                        s = s + rel if window is None else s + rel + window
                    scores.append(s)
                mx = scores[0]
                for s in scores[1:]:
                    mx = jnp.maximum(mx, s)
                mx = jnp.max(mx, axis=-1, keepdims=True)
                acc = None
                for s, va in zip(scores, vaug):
                    a = _dot(jnp.exp(s - mx).astype(BF16), va)
                    acc = a if acc is None else acc + a
                o_h = acc[:, 0:128] / acc[:, 128:256]
                o_pair = o_h if o_pair is None else jnp.where(left, o_pair, o_h)
            o_ref[0, :, lanes] = o_pair.astype(o_ref.dtype)

    @pl.when(j == 0)
    def _():
        attend(False)

    @pl.when(j > 0)
    def _():
        attend(True)


def _na(nq, nkt, nv, layer, rel_tab):
    n_b, t_all, _ = nq.shape
    n_tiles = t_all // TILE
    rows = (t_all - CTX_LEN) // GRID_W

    def nbr(delta):
        return lambda b, j: (b, jnp.clip(j + delta, 1, n_tiles - 1), 0)

    def nbr_t(delta):
        return lambda b, j: (b, 0, jnp.clip(j + delta, 1, n_tiles - 1))

    tile = lambda f: pl.BlockSpec((1, TILE, 256), f)
    tile_t = lambda f: pl.BlockSpec((1, 256, TILE), f)
    return pl.pallas_call(
        functools.partial(_na_kernel, rows=rows),
        out_shape=jax.ShapeDtypeStruct((n_b, t_all, 256), BF16),
        grid=(n_b, n_tiles),
        in_specs=[tile(lambda b, j: (b, j, 0)),
                  tile_t(nbr_t(-1)), tile_t(nbr_t(0)), tile_t(nbr_t(1)),
                  tile(nbr(-1)), tile(nbr(0)), tile(nbr(1)),
                  tile_t(lambda b, j: (b, 0, 0)), tile(lambda b, j: (b, 0, 0)),
                  _layer_spec((NA_HEADS, 2, GRID_W, NA_REL_W), layer)],
        out_specs=tile(lambda b, j: (b, j, 0)),
        compiler_params=_cp(("parallel", "arbitrary")),
        name="na_mixer",
    )(nq, nkt, nkt, nkt, nv, nv, nv, nkt, nv, rel_tab)


def _na_rel_table(rpb):
    col = np.arange(GRID_W)
    cs = np.clip(col - NA_KC // 2, 0, GRID_W - NA_KC)
    col_mask = (col[None, :] >= cs[:, None]) & (col[None, :] < cs[:, None] + NA_KC)
    co = np.clip(col[None, :] - col[:, None], -(NA_KC - 1), NA_KC - 1) + (NA_KC - 1)
    onehot = (co[None] == np.arange(2 * NA_KC - 1)[:, None, None]).astype(np.float32)
    toe = jnp.einsum("hrc,cqk->hqrk", rpb.astype(F32), jnp.asarray(onehot),
                     precision=lax.Precision.HIGHEST)
    toe = jnp.where(jnp.asarray(col_mask)[None, :, None, :], toe, NEG_BIG)
    flat = toe.reshape(NA_HEADS, GRID_W, (2 * NA_KR - 1) * GRID_W)
    flat = jnp.pad(flat, ((0, 0), (0, 0), (0, NA_REL_W + GRID_W - flat.shape[-1])), constant_values=NEG_BIG)
    return jnp.stack([flat[:, :, :NA_REL_W], flat[:, :, GRID_W:]], axis=1)


def _s5_kernel(su0_ref, su1_ref, ktoe_ref, bexp_ref, cexp_ref, pw_ref, y0_ref, y1_ref, a_ref, e_ref, x_ref,
               *, n_chunks, n_ctx_chunks):
    blk = 8
    lane_blk = lax.broadcasted_iota(jnp.int32, (blk, 128), 1) // S5_HG
    masks = [lane_blk == m for m in range(8)]
    gp = S5_G * S5_P

    def pick(vals):
        acc = vals[0]
        for k in range(1, 8):
            acc = jnp.where(masks[k], vals[k], acc)
        return acc

    def block_transpose(src):
        rolled = []
        for d in range(8):
            diag = pick([src[(k + d) % 8] for k in range(8)])
            rolled.append(pltpu.roll(diag, S5_HG * d, 1) if d else diag)
        return [pick([rolled[(m - a) % 8] for m in range(8)]) for a in range(8)]

    def regroup_in(rb, carry):
        c0 = pl.multiple_of(rb * blk, blk)
        for hg, ref in enumerate((su0_ref, su1_ref)):
            for o in range(2):
                src = [ref[0, pl.ds(rb * (blk * S5_T) + o * 8 + m, blk, stride=S5_T), :] for m in range(8)]
                for a, val in enumerate(block_transpose(src)):
                    a_ref[hg * 8 + a, pl.ds(c0, blk), o * 128:(o + 1) * 128] = val
        return carry

    lax.fori_loop(0, n_chunks // blk, regroup_in, 0)

    for g in range(S5_G):
        e = _dot(a_ref[g].astype(BF16), bexp_ref[g])
        for kind in range(4):
            e_ref[kind, :, g * S5_P:(g + 1) * S5_P] = e[:, kind * S5_P:(kind + 1) * S5_P]

    afr, afi, abr, abi = pw_ref[0:1, :], pw_ref[1:2, :], pw_ref[2:3, :], pw_ref[3:4, :]

    def scan_body(i, carry):
        fr, fi, br, bi = carry
        cf = i
        cb = jnp.where(i < n_ctx_chunks, n_ctx_chunks - 1 - i, n_chunks - 1 - (i - n_ctx_chunks))
        x_ref[0, pl.ds(cf, 1), :] = fr
        x_ref[1, pl.ds(cf, 1), :] = fi
        x_ref[2, pl.ds(cb, 1), :] = br
        x_ref[3, pl.ds(cb, 1), :] = bi
        er = e_ref[0, pl.ds(cf, 1), :]
        ei = e_ref[1, pl.ds(cf, 1), :]
        gr = e_ref[2, pl.ds(cb, 1), :]
        gi = e_ref[3, pl.ds(cb, 1), :]
        return (afr * fr - afi * fi + er, afr * fi + afi * fr + ei,
                abr * br - abi * bi + gr, abr * bi + abi * br + gi)

    z = jnp.zeros((1, gp), F32)
    lax.fori_loop(0, n_chunks, scan_body, (z, z, z, z))

    for g in range(S5_G):
        y = _dot(a_ref[g].astype(BF16), ktoe_ref[g])
        for kind in range(4):
            y = y + _dot(x_ref[kind, :, g * S5_P:(g + 1) * S5_P].astype(BF16), cexp_ref[g, kind])
        a_ref[g] = y

    def regroup_out(rb, carry):
        c0 = pl.multiple_of(rb * blk, blk)
        for og, ref in enumerate((y0_ref, y1_ref)):
            for ht in range(2):
                src = [a_ref[og * 8 + m, pl.ds(c0, blk), ht * 128:(ht + 1) * 128] for m in range(8)]
                for a, val in enumerate(block_transpose(src)):
                    ref[0, pl.ds(rb * (blk * S5_T) + ht * 8 + a, blk, stride=S5_T), :] = val
        return carry

    lax.fori_loop(0, n_chunks // blk, regroup_out, 0)


def _s5(su0, su1, layer, ktoe, bexp, cexp, pw16):
    n_b, t_all, _ = su0.shape
    n_chunks = t_all // S5_T
    gp = S5_G * S5_P
    blk = pl.BlockSpec((1, t_all, 128), lambda b: (b, 0, 0))
    return pl.pallas_call(
        functools.partial(_s5_kernel, n_chunks=n_chunks, n_ctx_chunks=CTX_LEN // S5_T),
        out_shape=[jax.ShapeDtypeStruct((n_b, t_all, 128), F32)] * 2,
        grid=(n_b,),
        in_specs=[blk, blk, _layer_spec((S5_G, 256, 256), layer), _layer_spec((S5_G, 256, 256), layer),
                  _layer_spec((S5_G, 4, S5_P, 256), layer), _layer_spec((4, gp), layer)],
        out_specs=[blk, blk],
        scratch_shapes=[pltpu.VMEM((S5_G, n_chunks, S5_T * S5_HG), F32),
                        pltpu.VMEM((4, n_chunks, gp), F32),
                        pltpu.VMEM((4, n_chunks, gp), F32)],
        compiler_params=_cp(("parallel",)),
        name="s5_mixer",
    )(su0, su1, ktoe, bexp, cexp, pw16)


def _s5_tables(lam_re, lam_im, log_dt, b_re, b_im, c_re, c_im, d_skip):
    hp = lax.Precision.HIGHEST
    lam_re, lam_im, log_dt = lam_re.astype(F32), lam_im.astype(F32), log_dt.astype(F32)
    b_re, b_im, c_re, c_im = b_re.astype(F32), b_im.astype(F32), c_re.astype(F32), c_im.astype(F32)
    dt = jnp.exp(log_dt)[..., None]
    n = jnp.arange(-S5_T, S5_T + 1, dtype=F32)[:, None, None, None]
    pw_re = jnp.exp(n * lam_re * dt) * jnp.cos(n * lam_im * dt)
    pw_im = jnp.exp(n * lam_re * dt) * jnp.sin(n * lam_im * dt)
    neg_re, neg_im = pw_re[S5_T::-1], pw_im[S5_T::-1]
    pw_re, pw_im = pw_re[S5_T:], pw_im[S5_T:]
    lb_re, lb_im = pw_re[1], pw_im[1]
    num_re, num_im = lb_re - 1.0, lb_im
    den = lam_re * lam_re + lam_im * lam_im
    coef_re = ((num_re * lam_re + num_im * lam_im) / den)[..., None]
    coef_im = ((num_im * lam_re - num_re * lam_im) / den)[..., None]
    bb_re = coef_re * b_re - coef_im * b_im
    bb_im = coef_re * b_im + coef_im * b_re
    steps = np.arange(S5_T)

    def cmul_b(p_re, p_im, d):
        re = p_re[..., None] * bb_re[d][None] - p_im[..., None] * bb_im[d][None]
        im = p_re[..., None] * bb_im[d][None] + p_im[..., None] * bb_re[d][None]
        to = lambda z: z.transpose(1, 0, 3, 2).reshape(S5_G, S5_T * S5_HG, S5_P)
        return to(re), to(im)

    def cmul_c(p_re, p_im, d):
        re = c_re[d][None] * p_re[:, :, None, :] - c_im[d][None] * p_im[:, :, None, :]
        im = c_re[d][None] * p_im[:, :, None, :] + c_im[d][None] * p_re[:, :, None, :]
        to = lambda z: z.transpose(1, 3, 0, 2).reshape(S5_G, S5_P, S5_T * S5_HG)
        return to(re), to(-im)

    def lag_operator(u, v):
        return jnp.einsum("gap,gpb->gab", jnp.concatenate(u, axis=-1), jnp.concatenate(v, axis=1), precision=hp)

    kf = lag_operator(cmul_b(neg_re[steps, 0], neg_im[steps, 0], 0), cmul_c(pw_re[steps, 0], pw_im[steps, 0], 0))
    kb = lag_operator(cmul_b(pw_re[steps, 1], pw_im[steps, 1], 1), cmul_c(neg_re[steps, 1], neg_im[steps, 1], 1))
    tok = np.arange(S5_T * S5_HG) // S5_HG
    skip = jnp.asarray(np.eye(S5_T * S5_HG, dtype=np.float32)) * jnp.tile(
        d_skip.astype(F32).reshape(S5_G, S5_HG), (1, S5_T))[:, None, :]
    ktoe = (jnp.where(jnp.asarray(tok[None, :] >= tok[:, None]), kf, 0.0)
            + jnp.where(jnp.asarray(tok[:, None] >= tok[None, :]), kb, 0.0) + skip)
    bexp = jnp.concatenate(cmul_b(pw_re[S5_T - 1 - steps, 0], pw_im[S5_T - 1 - steps, 0], 0)
                           + cmul_b(pw_re[steps, 1], pw_im[steps, 1], 1), axis=-1)
    cexp = jnp.stack(cmul_c(pw_re[steps + 1, 0], pw_im[steps + 1, 0], 0)
                     + cmul_c(pw_re[S5_T - steps, 1], pw_im[S5_T - steps, 1], 1), axis=1)
    pw16 = jnp.stack([pw_re[S5_T, 0].reshape(-1), pw_im[S5_T, 0].reshape(-1),
                      pw_re[S5_T, 1].reshape(-1), pw_im[S5_T, 1].reshape(-1)], axis=0)
    return ktoe.astype(BF16), bexp.astype(BF16), cexp.astype(BF16), pw16


def _pool_kernel(u_ref, w_ref, scale_ref, o_ref, pad_ref, *, n_lat):
    pad = 8
    seg = ((pad, 0, CTX_LEN), (pad + CTX_LEN + 2 * pad, CTX_LEN, n_lat))
    pad_ref[...] = jnp.zeros_like(pad_ref)
    for p0, u0, n in seg:
        pad_ref[p0:p0 + n, :] = u_ref[0, u0:u0 + n, :]
    for p0, u0, n in seg:
        for t0 in range(0, n, TILE):
            t = t0 + lax.broadcasted_iota(jnp.int32, (TILE, 1), 0)
            outs = []
            for i, w in enumerate(POOL_WINDOWS):
                lo, hi = i * POOL_GW, (i + 1) * POOL_GW
                acc = None
                for jj in range(w):
                    s = p0 + t0 + jj - w // 2
                    piece = pad_ref[s:s + TILE, lo:hi]
                    acc = piece if acc is None else acc + piece
                cnt = (jnp.clip(t - w // 2 + w, 0, n) - jnp.clip(t - w // 2, 0, n)).astype(F32)
                g = u_ref[0, u0 + t0:u0 + t0 + TILE, lo:hi]
                outs.append(_dot((acc / cnt - g).astype(BF16), w_ref[i].astype(BF16)))
            o_ref[0, u0 + t0:u0 + t0 + TILE, :] = (jnp.concatenate(outs, axis=-1) * scale_ref[...]).astype(o_ref.dtype)


def _pool(pu, layer, w_pool, scale):
    n_b, t_all, _ = pu.shape
    blk = pl.BlockSpec((1, t_all, 256), lambda b: (b, 0, 0))
    return pl.pallas_call(
        functools.partial(_pool_kernel, n_lat=t_all - CTX_LEN),
        out_shape=jax.ShapeDtypeStruct((n_b, t_all, 256), BF16),
        grid=(n_b,),
        in_specs=[blk, _layer_spec((len(POOL_WINDOWS), POOL_GW, POOL_GW), layer), _layer_spec((1, 256), layer)],
        out_specs=blk,
        scratch_shapes=[pltpu.VMEM((t_all + 32, 256), F32)],
        compiler_params=_cp(("parallel",)),
        name="pool_mixer",
    )(pu, w_pool, scale)


def _gelu_tanh(x):
    return 0.5 * x * (1.0 + jnp.tanh(math.sqrt(2.0 / math.pi) * (x + 0.044715 * (x * x * x))))


def _outproj_kernel(ygla_ref, yna_ref, ys50_ref, ys51_ref, ypool_ref, gate_ref, x_ref, xc_ref, mod_ref, wout_ref,
                    gpost_ref, wglu_ref, bglu_ref, *out_refs, n_batch, with_ctx):
    b = pl.program_id(0)
    j = pl.program_id(1)
    is_ctx = jnp.logical_and(j == 0, with_ctx)
    gate_m = mod_ref[pl.ds(jnp.where(is_ctx, n_batch, b), 1), :][:, 2 * D_MODEL:3 * D_MODEL]
    xt = jnp.where(is_ctx, xc_ref[0], x_ref[0]) if with_ctx else x_ref[0]
    gc = gate_ref[0].astype(F32)
    sg = gc * jax.nn.sigmoid(gc)
    g5 = _gelu_tanh(jnp.concatenate([ys50_ref[0], ys51_ref[0]], axis=-1))
    s5o = g5 * jax.nn.sigmoid(_dot(g5.astype(BF16), wglu_ref[...]) + bglu_ref[...])
    parts = (ygla_ref[0], yna_ref[0], s5o, ypool_ref[0])
    acc = None
    for i, part in enumerate(parts):
        lo, hi = i * GROUP_W, (i + 1) * GROUP_W
        a = _dot((part * sg[:, lo:hi]).astype(BF16), wout_ref[lo:hi, :])
        acc = a if acc is None else acc + a
    ms = jnp.mean(acc * acc, axis=-1, keepdims=True)
    res = xt + gate_m * (acc * lax.rsqrt(ms + NORM_EPS) * gpost_ref[...])
    if with_ctx:
        xo_ref, xco_ref = out_refs

        @pl.when(j == 0)
        def _():
            xco_ref[0] = res

        @pl.when(j > 0)
        def _():
            xo_ref[0] = res
    else:
        out_refs[0][0] = res


def _outproj(ys, gate_cols, x, xc, layer, mod_all, w_out, g_post, w_glu, b_glu, with_ctx):
    n_b, n_lat, d = x.shape
    n_lat_tiles = n_lat // TILE
    off = 0 if with_ctx else 1
    n_steps = n_lat_tiles + (1 if with_ctx else 0)
    mix_map = lambda b, j: (b, j + off, 0)
    lat_map = (lambda b, j: (b, jnp.maximum(j - 1, 0), 0)) if with_ctx else (lambda b, j: (b, j, 0))
    ctx_map = lambda b, j: (b, 0, 0)
    out_shape = [jax.ShapeDtypeStruct(x.shape, F32)]
    out_specs = [pl.BlockSpec((1, TILE, d), lat_map)]
    if with_ctx:
        out_shape.append(jax.ShapeDtypeStruct(xc.shape, F32))
        out_specs.append(pl.BlockSpec((1, CTX_LEN, d), ctx_map))
    outs = pl.pallas_call(
        functools.partial(_outproj_kernel, n_batch=n_b, with_ctx=with_ctx),
        out_shape=out_shape,
        grid=(n_b, n_steps),
        in_specs=[pl.BlockSpec((1, TILE, w), mix_map) for w in (256, 256, 128, 128, 256)]
                 + [pl.BlockSpec((1, TILE, MIX_W), mix_map),
                    pl.BlockSpec((1, TILE, d), lat_map),
                    pl.BlockSpec((1, CTX_LEN, d), ctx_map),
                    _layer_spec((8, 3 * d), layer),
                    _layer_spec((MIX_W, d), layer),
                    _layer_spec((1, d), layer),
                    _layer_spec((GROUP_W, GROUP_W), layer),
                    _layer_spec((1, GROUP_W), layer)],
        out_specs=out_specs,
        compiler_params=_cp(("parallel", "arbitrary")),
        name="gate_outproj_residual",
    )(*ys, gate_cols, x, xc, mod_all, w_out, g_post, w_glu, b_glu)
    return (outs[0], outs[1]) if with_ctx else (outs[0], None)


def _rope_tables(n_tok):
    t = np.arange(n_tok)
    half = GLA_DK // 2
    freqs = ROPE_BASE ** (-np.arange(0, half, 2, dtype=np.float32) / half)
    d = np.arange(GLA_DK)
    pos = np.where((d // half)[None, :] == 0, (t // GRID_W)[:, None], (t % GRID_W)[:, None]).astype(np.float32)
    ang = jnp.asarray(pos) * jnp.asarray(freqs[(d % half) % (half // 2)])[None, :]
    first = jnp.asarray(((d % half) < half // 2)[None, :])
    cos, sin = jnp.cos(ang), jnp.sin(ang)
    tabs = (cos, jnp.where(first, -sin, 0.0), jnp.where(first, 0.0, sin))
    return tuple(jnp.tile(z, (1, GLA_HEADS)).astype(F32) for z in tabs)


def _pad_w_in(w_in):
    zeros = jnp.zeros(w_in.shape[:2] + (128 - 2 * GLA_RANK,), w_in.dtype)
    return jnp.concatenate([w_in[:, :, :N_GATE_END], zeros, w_in[:, :, N_GATE_END:]], axis=2).astype(BF16)


def _gate_weights(w_gate, b_gate):
    n_l = w_gate.shape[0]
    wg = jnp.zeros((n_l, 128, 256), F32)
    wg = wg.at[:, 0:GLA_RANK, 0:128].set(w_gate[:, 0].astype(F32))
    wg = wg.at[:, GLA_RANK:2 * GLA_RANK, 128:256].set(w_gate[:, 1].astype(F32))
    return wg.astype(BF16), b_gate.astype(F32).reshape(n_l, 1, 256)


def kernel(x, c, ctx, c_ctx, w_mod, b_mod, g_pre, g_post, w_in, w_out, gla_w_gate, gla_b_gate, gla_g_norm, na_rpb,
           s5_lam_re, s5_lam_im, s5_log_dt, s5_b_re, s5_b_im, s5_c_re, s5_c_im, s5_d, s5_w_glu, s5_b_glu,
           pool_w, pool_scale):
    n_b, n_lat, d = x.shape
    depth = w_mod.shape[0]
    rows = n_lat // GRID_W
    assert d == D_MODEL and ctx.shape[1] == CTX_LEN and n_lat % TILE == 0 and n_b < 8 and rows >= NA_KR
    c_rows = jnp.zeros((8, d), F32).at[0:n_b].set(c).at[n_b].set(c_ctx)
    mod_all = _modulation(c_rows, w_mod, b_mod)
    rope_tabs = _rope_tables(n_lat)
    w_pad = _pad_w_in(w_in)
    wg, bg = _gate_weights(gla_w_gate, gla_b_gate)
    g_pre3 = g_pre.astype(F32).reshape(depth, 1, d)
    g_post3 = g_post.astype(F32).reshape(depth, 1, d)
    gn = jnp.tile(gla_g_norm.astype(F32), (1, GLA_HEADS)).reshape(depth, 1, GLA_HEADS * GLA_DV)
    rel_tab = jax.vmap(_na_rel_table)(na_rpb)
    ktoe, bexp, cexp, pw16 = jax.vmap(_s5_tables)(s5_lam_re, s5_lam_im, s5_log_dt, s5_b_re, s5_b_im,
                                                  s5_c_re, s5_c_im, s5_d)
    w_out_b = w_out.astype(BF16)
    w_glu_b = s5_w_glu.astype(BF16)
    b_glu3 = s5_b_glu.astype(F32).reshape(depth, 1, GROUP_W)
    pool_scale3 = pool_scale.astype(F32).reshape(depth, 1, GROUP_W)
    xc = ctx
    for l in range(depth):
        last = l == depth - 1
        qk, gv, lg, nq, nk, nv, su0, su1, pu, gate_cols = _inproj(x, xc, l, mod_all, g_pre3, w_pad, wg, bg,
                                                                  rope_tabs)
        y_gla = _gla(qk, gv, lg, l, gn)
        y_na = _na(nq, nk, nv, l, rel_tab)
        y_s50, y_s51 = _s5(su0, su1, l, ktoe, bexp, cexp, pw16)
        y_pool = _pool(pu, l, pool_w, pool_scale3)
        x, xc = _outproj((y_gla, y_na, y_s50, y_s51, y_pool), gate_cols, x, xc, l, mod_all, w_out_b, g_post3,
                         w_glu_b, b_glu3, with_ctx=not last)
    return x
```

```python
import functools
import math

import numpy as np
import jax
import jax.numpy as jnp
from jax import lax
from jax.experimental import pallas as pl
from jax.experimental.pallas import tpu as pltpu

F32 = jnp.float32
BF16 = jnp.bfloat16

D_MODEL = 1024
GRID_W = 64
CTX_LEN = 256
GROUP_W = 256
MIX_W = 1024
NORM_EPS = 1e-6
GLA_HEADS = 4
GLA_DV = 64
GLA_DK = 32
GLA_RANK = 16
GLA_TAU = 16.0
ROPE_BASE = 10000.0
NA_HEADS = 4
NA_DH = 64
NA_KR = 8
NA_KC = 16
S5_HG = 16
S5_G = 16
S5_P = 64
POOL_WINDOWS = (2, 4, 8, 16)
POOL_GW = 64

TILE = 256
GLA_C = 128
S5_T = 16
NA_ROWS = TILE // GRID_W
NA_REL_W = 1024
NEG_BIG = -1e30

_COL = {}
_off = 0
for _name, _w in (("gk", 128), ("gv", 256), ("gg", 128), ("nk", 256), ("nv", 256), ("su", 256),
                  ("gq", 128), ("nq", 256), ("pu", 256), ("gate", 1024)):
    _COL[_name] = (_off, _off + _w)
    _off += _w
N_PAD_COLS = _off
N_GATE_END = 128 + 256 + 2 * GLA_RANK

_VMEM_LIMIT = 56 * 1024 * 1024


def _cp(sem, vmem=_VMEM_LIMIT):
    return pltpu.CompilerParams(dimension_semantics=sem, vmem_limit_bytes=vmem)


def _dot(a, b):
    return jnp.dot(a, b, preferred_element_type=F32)


def _dot_nt(a, b):
    return lax.dot_general(a, b, (((1,), (1,)), ((), ())), preferred_element_type=F32)


def _split_bf16(a):
    hi = a.astype(BF16)
    lo = (a - hi.astype(F32)).astype(BF16)
    return hi, lo


def _mod_kernel(s_ref, w_ref, b_ref, o_ref):
    s = s_ref[...]
    a = (s * jax.nn.sigmoid(s)).astype(BF16)
    o_ref[0] = _dot(a, w_ref[0].astype(BF16)) + b_ref[0]


def _modulation(c_rows, w_mod, b_mod):
    n_l, d, d3 = w_mod.shape
    return pl.pallas_call(
        _mod_kernel,
        out_shape=jax.ShapeDtypeStruct((n_l, 8, d3), F32),
        grid=(n_l, d3 // d),
        in_specs=[pl.BlockSpec((8, d), lambda l, j: (0, 0)),
                  pl.BlockSpec((1, d, d), lambda l, j: (l, 0, j)),
                  pl.BlockSpec((1, 1, d), lambda l, j: (l, 0, j))],
        out_specs=pl.BlockSpec((1, 8, d), lambda l, j: (l, 0, j)),
        compiler_params=_cp(("parallel", "parallel")),
        name="adaln_modulation",
    )(c_rows, w_mod, b_mod.reshape(n_l, 1, d3))


def _inproj_kernel(x_ref, xc_ref, mod_ref, gpre_ref, w_ref, wg_ref, bg_ref, cos_ref, sa_ref, sb_ref,
                   qk_ref, v_ref, lg_ref, nq_ref, nk_ref, nv_ref, su0_ref, su1_ref, pu_ref, gate_ref, *, n_batch):
    b = pl.program_id(0)
    j = pl.program_id(1)
    is_ctx = j == 0
    xt = jnp.where(is_ctx, xc_ref[0], x_ref[0])
    m = mod_ref[pl.ds(jnp.where(is_ctx, n_batch, b), 1), :]
    shift = m[:, 0:D_MODEL]
    scale = m[:, D_MODEL:2 * D_MODEL]
    ms = jnp.mean(xt * xt, axis=-1, keepdims=True)
    h = xt * lax.rsqrt(ms + NORM_EPS) * gpre_ref[...] * (1.0 + scale) + shift
    hb = h.astype(BF16)

    def proj(name):
        lo, hi = _COL[name]
        return _dot(hb, w_ref[:, lo:hi])

    cos = jnp.where(is_ctx, 1.0, cos_ref[...])
    sa = jnp.where(is_ctx, 0.0, sa_ref[...])
    sb = jnp.where(is_ctx, 0.0, sb_ref[...])

    def rope(t):
        return t * cos + pltpu.roll(t, 128 - 8, 1) * sa + pltpu.roll(t, 8, 1) * sb

    qk_ref[0, :, 0:128] = rope(proj("gq")) * (GLA_DK ** -0.5)
    qk_ref[0, :, 128:256] = rope(proj("gk"))
    v_ref[0] = proj("gv")
    z = _dot(proj("gg").astype(BF16), wg_ref[...]) + bg_ref[...]
    lg_ref[0] = (jnp.minimum(z, 0.0) - jnp.log(1.0 + jnp.exp(-jnp.abs(z)))) * (1.0 / GLA_TAU)
    nq_ref[0] = (proj("nq") * (NA_DH ** -0.5)).astype(BF16)
    nk_ref[0] = proj("nk").T.astype(BF16)
    nv_ref[0] = proj("nv").astype(BF16)
    su = proj("su")
    su0_ref[0] = su[:, 0:128]
    su1_ref[0] = su[:, 128:256]
    pu_ref[0] = proj("pu")
    gc = proj("gate")
    gate_ref[0] = (gc * jax.nn.sigmoid(gc)).astype(BF16)


def _layer_spec(shape, layer):
    return pl.BlockSpec((None,) + tuple(shape), lambda *_: (layer,) + (0,) * len(shape))


def _inproj(x, xc, layer, mod_all, g_pre, w_pad, wg, bg, rope_tabs):
    n_b, n_lat, d = x.shape
    n_tiles = (n_lat + CTX_LEN) // TILE
    t_all = n_lat + CTX_LEN
    lat_map = lambda b, j: (b, jnp.maximum(j - 1, 0), 0)
    tab_map = lambda b, j: (jnp.maximum(j - 1, 0), 0)
    out_map = lambda b, j: (b, j, 0)
    outs = (("qk", 256, F32), ("gv", 256, F32), ("lg", 256, F32), ("nq", 256, BF16), ("nkT", 256, BF16),
            ("nv", 256, BF16), ("su0", 128, F32), ("su1", 128, F32), ("pu", 256, F32), ("gate", 1024, BF16))
    out_shape = [jax.ShapeDtypeStruct((n_b, w, t_all) if n == "nkT" else (n_b, t_all, w), dt) for n, w, dt in outs]
    out_specs = [pl.BlockSpec((1, w, TILE), lambda b, j: (b, 0, j)) if n == "nkT"
                 else pl.BlockSpec((1, TILE, w), out_map) for n, w, dt in outs]
    return pl.pallas_call(
        functools.partial(_inproj_kernel, n_batch=n_b),
        out_shape=out_shape,
        grid=(n_b, n_tiles),
        in_specs=[pl.BlockSpec((1, TILE, d), lat_map),
                  pl.BlockSpec((1, CTX_LEN, d), lambda b, j: (b, 0, 0)),
                  _layer_spec((8, 3 * d), layer),
                  _layer_spec((1, d), layer),
                  _layer_spec((d, N_PAD_COLS), layer),
                  _layer_spec((128, 256), layer),
                  _layer_spec((1, 256), layer),
                  pl.BlockSpec((TILE, 128), tab_map),
                  pl.BlockSpec((TILE, 128), tab_map),
                  pl.BlockSpec((TILE, 128), tab_map)],
        out_specs=out_specs,
        compiler_params=_cp(("parallel", "arbitrary")),
        name="modnorm_inproj",
    )(x, xc, mod_all, g_pre, w_pad, wg, bg, *rope_tabs)


def _gla_kernel(qk_ref, v_ref, lg_ref, gn_ref, y_ref, stf_ref, stb_ref, of_ref, ob_ref, *, n_chunks, n_ctx_chunks):
    c_len = GLA_C
    row = lax.broadcasted_iota(jnp.int32, (c_len, c_len), 0)
    col = lax.broadcasted_iota(jnp.int32, (c_len, c_len), 1)
    tri_f = jnp.concatenate([(col <= row).astype(BF16)] * 2, axis=1)
    tri_b = jnp.concatenate([(col >= row).astype(BF16)] * 2, axis=1)
    arow = lax.broadcasted_iota(jnp.int32, (c_len, 4 * c_len), 0)
    acol = lax.broadcasted_iota(jnp.int32, (c_len, 4 * c_len), 1) % c_len
    amask_f = acol <= arow
    amask_b = acol >= arow
    head_kt = lax.broadcasted_iota(jnp.int32, (128, 1), 0) // GLA_DK
    head_v = lax.broadcasted_iota(jnp.int32, (1, 256), 1) // GLA_DV
    bd_mask = (lax.broadcasted_iota(jnp.int32, (128, 256), 0) // GLA_DK
               == lax.broadcasted_iota(jnp.int32, (128, 256), 1) // GLA_DV)
    ones_bd = jnp.where(lax.broadcasted_iota(jnp.int32, (256, 256), 0) // GLA_DV
                        == lax.broadcasted_iota(jnp.int32, (256, 256), 1) // GLA_DV,
                        1.0 / GLA_DV, 0.0).astype(BF16)
    mid = c_len // 2

    def scan_body(i, carry):
        cb = jnp.where(i < n_ctx_chunks, n_ctx_chunks - 1 - i, n_chunks - 1 - (i - n_ctx_chunks))
        two = (0, 1)
        r0 = [pl.multiple_of(c * c_len, c_len) for c in (i, cb)]
        st_refs = (stf_ref, stb_ref)
        tri = (tri_f, tri_b)
        amask = (amask_f, amask_b)
        end = (c_len - 1, 0)
        q = [qk_ref[0, pl.ds(r0[d], c_len), 0:128] for d in two]
        k = [qk_ref[0, pl.ds(r0[d], c_len), 128:256] for d in two]
        v = [v_ref[0, pl.ds(r0[d], c_len), :] for d in two]
        lg = [_split_bf16(lg_ref[0, pl.ds(r0[d], c_len), 128 * d:128 * (d + 1)]) for d in two]
        cum = [_dot(tri[d], jnp.concatenate(lg[d], axis=0)) for d in two]
        c_ref = [cum[d][mid:mid + 1, :] for d in two]
        qt = [q[d] * jnp.exp(cum[d] - c_ref[d]) for d in two]
        kt_t = [(k[d] * jnp.exp(c_ref[d] - cum[d])).T for d in two]
        cum_t = [cum[d].T for d in two]
        ce_col = [cum_t[d][:, end[d]:end[d] + 1] for d in two]
        cr_col = [cum_t[d][:, mid:mid + 1] for d in two]
        kblk = [jnp.concatenate([jnp.where(head_kt == hh, kt_t[d], 0.0) for hh in range(GLA_HEADS)],
                                axis=1).astype(BF16) for d in two]
        att = [_dot(qt[d].astype(BF16), kblk[d]) for d in two]
        st = [st_refs[d][...] for d in two]
        qh = [(qt[d] * jnp.exp(c_ref[d])).astype(BF16) for d in two]
        kh_t = [(kt_t[d] * jnp.exp(ce_col[d] - cr_col[d])).astype(BF16) for d in two]
        vb = [v[d].astype(BF16) for d in two]
        upd = [_dot(kh_t[d], vb[d]) for d in two]
        vstack = [jnp.concatenate([jnp.where(head_v == hh, vb[d], jnp.zeros_like(vb[d]))
                                   for hh in range(GLA_HEADS)] + [st[d].astype(BF16)], axis=0) for d in two]
        lhs = [jnp.concatenate([jnp.where(amask[d], att[d], 0.0).astype(BF16), qh[d]], axis=1) for d in two]
        outs = (of_ref, ob_ref)
        for d in two:
            outs[d][pl.ds(r0[d], c_len), :] = _dot(lhs[d], vstack[d])
            st_refs[d][...] = st[d] * jnp.exp(ce_col[d]) + jnp.where(bd_mask, upd[d], 0.0)
        return carry

    def norm_body(c, carry):
        r0 = pl.multiple_of(c * TILE, TILE)
        o = of_ref[pl.ds(r0, TILE), :] + ob_ref[pl.ds(r0, TILE), :]
        sq_hi, sq_lo = _split_bf16(o * o)
        ms = _dot(sq_hi, ones_bd) + _dot(sq_lo, ones_bd)
        y_ref[0, pl.ds(r0, TILE), :] = (o * lax.rsqrt(ms + NORM_EPS) * gn_ref[...]).astype(y_ref.dtype)
        return carry

    stf_ref[...] = jnp.zeros_like(stf_ref)
    stb_ref[...] = jnp.zeros_like(stb_ref)
    lax.fori_loop(0, n_chunks, scan_body, 0)
    lax.fori_loop(0, n_chunks * c_len // TILE, norm_body, 0)


def _gla(qk, v, lg, layer, g_norm):
    n_b, t_all, _ = qk.shape
    blk = lambda w: pl.BlockSpec((1, t_all, w), lambda b: (b, 0, 0))
    return pl.pallas_call(
        functools.partial(_gla_kernel, n_chunks=t_all // GLA_C, n_ctx_chunks=CTX_LEN // GLA_C),
        out_shape=jax.ShapeDtypeStruct((n_b, t_all, 256), BF16),
        grid=(n_b,),
        in_specs=[blk(256), blk(256), blk(256), _layer_spec((1, 256), layer)],
        out_specs=blk(256),
        scratch_shapes=[pltpu.VMEM((128, 256), F32), pltpu.VMEM((128, 256), F32),
                        pltpu.VMEM((t_all, 256), F32), pltpu.VMEM((t_all, 256), F32)],
        compiler_params=_cp(("parallel",)),
        name="gla_mixer",
    )(qk, v, lg, g_norm)


def _na_kernel(q_ref, kt0_ref, kt1_ref, kt2_ref, v0_ref, v1_ref, v2_ref, ktc_ref, vc_ref, rel_ref, o_ref, *, rows):
    j = pl.program_id(1)
    left = lax.broadcasted_iota(jnp.int32, (1, 128), 1) < NA_DH
    ones_blk = jnp.ones((TILE, 128), BF16)
    ro0 = NA_KR - 1 - NA_ROWS

    def attend(with_window):
        key_tiles = [(ktc_ref, vc_ref, None)]
        if with_window:
            g = j - 1
            q_row = lax.broadcasted_iota(jnp.int32, (TILE, TILE), 0) // GRID_W
            k_row = lax.broadcasted_iota(jnp.int32, (TILE, TILE), 1) // GRID_W
            first = jnp.clip(NA_ROWS * g + q_row - NA_KR // 2, 0, rows - NA_KR) - NA_ROWS * (g - 1)
            for d, (kr, vr) in enumerate(((kt0_ref, v0_ref), (kt1_ref, v1_ref), (kt2_ref, v2_ref))):
                window = None if d == 1 else jnp.where(
                    (k_row + NA_ROWS * d >= first) & (k_row + NA_ROWS * d < first + NA_KR), 0.0, NEG_BIG)
                key_tiles.append((kr, vr, (d, window)))
        for pair in range(NA_HEADS // 2):
            lanes = slice(128 * pair, 128 * (pair + 1))
            qp = q_ref[0, :, lanes]
            zero = jnp.zeros_like(qp)
            kt_all = jnp.concatenate([kr[0, lanes, :] for kr, _, _ in key_tiles], axis=1)
            vaug = jnp.concatenate([jnp.concatenate([vr[0, :, lanes], ones_blk], axis=1)
                                    for _, vr, _ in key_tiles], axis=0)
            o_pair = None
            for sub in range(2):
                qh = jnp.where(left, qp, zero) if sub == 0 else jnp.where(left, zero, qp)
                s = _dot(qh, kt_all)
                if with_window:
                    parts = [s[:, 0:TILE]]
                    for _, _, (d, window) in key_tiles[1:]:
                        rel = jnp.concatenate(
                            [rel_ref[2 * pair + sub, (ro0 - i) % 2, :, GRID_W * ((ro0 - i) // 2 * 2) + d * TILE:
                                     GRID_W * ((ro0 - i) // 2 * 2) + (d + 1) * TILE] for i in range(NA_ROWS)], axis=0)
                        part = s[:, (d + 1) * TILE:(d + 2) * TILE] + rel
                        parts.append(part if window is None else part + window)
                    s = jnp.concatenate(parts, axis=1)
                mx = jnp.max(s, axis=-1, keepdims=True)
                acc = _dot(jnp.exp(s - mx).astype(BF16), vaug)
                o_h = acc[:, 0:128] / acc[:, 128:256]
                o_pair = o_h if o_pair is None else jnp.where(left, o_pair, o_h)
            o_ref[0, :, lanes] = o_pair.astype(o_ref.dtype)

    @pl.when(j == 0)
    def _():
        attend(False)

    @pl.when(j > 0)
    def _():
        attend(True)


def _na(nq, nkt, nv, layer, rel_tab):
    n_b, t_all, _ = nq.shape
    n_tiles = t_all // TILE
    rows = (t_all - CTX_LEN) // GRID_W

    def nbr(delta):
        return lambda b, j: (b, jnp.clip(j + delta, 1, n_tiles - 1), 0)

    def nbr_t(delta):
        return lambda b, j: (b, 0, jnp.clip(j + delta, 1, n_tiles - 1))

    tile = lambda f: pl.BlockSpec((1, TILE, 256), f)
    tile_t = lambda f: pl.BlockSpec((1, 256, TILE), f)
    return pl.pallas_call(
        functools.partial(_na_kernel, rows=rows),
        out_shape=jax.ShapeDtypeStruct((n_b, t_all, 256), BF16),
        grid=(n_b, n_tiles),
        in_specs=[tile(lambda b, j: (b, j, 0)),
                  tile_t(nbr_t(-1)), tile_t(nbr_t(0)), tile_t(nbr_t(1)),
                  tile(nbr(-1)), tile(nbr(0)), tile(nbr(1)),
                  tile_t(lambda b, j: (b, 0, 0)), tile(lambda b, j: (b, 0, 0)),
                  _layer_spec((NA_HEADS, 2, GRID_W, NA_REL_W), layer)],
        out_specs=tile(lambda b, j: (b, j, 0)),
        compiler_params=_cp(("parallel", "arbitrary")),
        name="na_mixer",
    )(nq, nkt, nkt, nkt, nv, nv, nv, nkt, nv, rel_tab)


def _na_rel_table(rpb):
    col = np.arange(GRID_W)
    cs = np.clip(col - NA_KC // 2, 0, GRID_W - NA_KC)
    col_mask = (col[None, :] >= cs[:, None]) & (col[None, :] < cs[:, None] + NA_KC)
    co = np.clip(col[None, :] - col[:, None], -(NA_KC - 1), NA_KC - 1) + (NA_KC - 1)
    onehot = (co[None] == np.arange(2 * NA_KC - 1)[:, None, None]).astype(np.float32)
    toe = jnp.einsum("hrc,cqk->hqrk", rpb.astype(F32), jnp.asarray(onehot),
                     precision=lax.Precision.HIGHEST)
    toe = jnp.where(jnp.asarray(col_mask)[None, :, None, :], toe, NEG_BIG)
    flat = toe.reshape(NA_HEADS, GRID_W, (2 * NA_KR - 1) * GRID_W)
    flat = jnp.pad(flat, ((0, 0), (0, 0), (0, NA_REL_W + GRID_W - flat.shape[-1])), constant_values=NEG_BIG)
    return jnp.stack([flat[:, :, :NA_REL_W], flat[:, :, GRID_W:]], axis=1)


def _s5_kernel(su0_ref, su1_ref, ktoe_ref, bexp_ref, cexp_ref, pw_ref, y0_ref, y1_ref, a_ref, e_ref, x_ref,
               *, n_chunks, n_ctx_chunks):
    blk = 8
    lane_blk = lax.broadcasted_iota(jnp.int32, (blk, 128), 1) // S5_HG
    masks = [lane_blk == m for m in range(8)]
    gp = S5_G * S5_P

    def pick(vals):
        acc = vals[0]
        for k in range(1, 8):
            acc = jnp.where(masks[k], vals[k], acc)
        return acc

    def block_transpose(src):
        rolled = []
        for d in range(8):
            diag = pick([src[(k + d) % 8] for k in range(8)])
            rolled.append(pltpu.roll(diag, S5_HG * d, 1) if d else diag)
        return [pick([rolled[(m - a) % 8] for m in range(8)]) for a in range(8)]

    def regroup_in(rb, carry):
        c0 = pl.multiple_of(rb * blk, blk)
        for hg, ref in enumerate((su0_ref, su1_ref)):
            for o in range(2):
                src = [ref[0, pl.ds(rb * (blk * S5_T) + o * 8 + m, blk, stride=S5_T), :] for m in range(8)]
                for a, val in enumerate(block_transpose(src)):
                    a_ref[hg * 8 + a, pl.ds(c0, blk), o * 128:(o + 1) * 128] = val
        return carry

    lax.fori_loop(0, n_chunks // blk, regroup_in, 0)

    for g in range(S5_G):
        e = _dot(a_ref[g].astype(BF16), bexp_ref[g])
        for kind in range(4):
            e_ref[kind, :, g * S5_P:(g + 1) * S5_P] = e[:, kind * S5_P:(kind + 1) * S5_P]

    afr, afi, abr, abi = pw_ref[0:1, :], pw_ref[1:2, :], pw_ref[2:3, :], pw_ref[3:4, :]

    def scan_body(i, carry):
        fr, fi, br, bi = carry
        cf = i
        cb = jnp.where(i < n_ctx_chunks, n_ctx_chunks - 1 - i, n_chunks - 1 - (i - n_ctx_chunks))
        x_ref[0, pl.ds(cf, 1), :] = fr
        x_ref[1, pl.ds(cf, 1), :] = fi
        x_ref[2, pl.ds(cb, 1), :] = br
        x_ref[3, pl.ds(cb, 1), :] = bi
        er = e_ref[0, pl.ds(cf, 1), :]
        ei = e_ref[1, pl.ds(cf, 1), :]
        gr = e_ref[2, pl.ds(cb, 1), :]
        gi = e_ref[3, pl.ds(cb, 1), :]
        return (afr * fr - afi * fi + er, afr * fi + afi * fr + ei,
                abr * br - abi * bi + gr, abr * bi + abi * br + gi)

    z = jnp.zeros((1, gp), F32)
    lax.fori_loop(0, n_chunks, scan_body, (z, z, z, z))

    for g in range(S5_G):
        y = _dot(a_ref[g].astype(BF16), ktoe_ref[g])
        for kind in range(4):
            y = y + _dot(x_ref[kind, :, g * S5_P:(g + 1) * S5_P].astype(BF16), cexp_ref[g, kind])
        a_ref[g] = y

    def regroup_out(rb, carry):
        c0 = pl.multiple_of(rb * blk, blk)
        for og, ref in enumerate((y0_ref, y1_ref)):
            for ht in range(2):
                src = [a_ref[og * 8 + m, pl.ds(c0, blk), ht * 128:(ht + 1) * 128] for m in range(8)]
                for a, val in enumerate(block_transpose(src)):
                    ref[0, pl.ds(rb * (blk * S5_T) + ht * 8 + a, blk, stride=S5_T), :] = val
        return carry

    lax.fori_loop(0, n_chunks // blk, regroup_out, 0)


def _s5(su0, su1, layer, ktoe, bexp, cexp, pw16):
    n_b, t_all, _ = su0.shape
    n_chunks = t_all // S5_T
    gp = S5_G * S5_P
    blk = pl.BlockSpec((1, t_all, 128), lambda b: (b, 0, 0))
    return pl.pallas_call(
        functools.partial(_s5_kernel, n_chunks=n_chunks, n_ctx_chunks=CTX_LEN // S5_T),
        out_shape=[jax.ShapeDtypeStruct((n_b, t_all, 128), F32)] * 2,
        grid=(n_b,),
        in_specs=[blk, blk, _layer_spec((S5_G, 256, 256), layer), _layer_spec((S5_G, 256, 256), layer),
                  _layer_spec((S5_G, 4, S5_P, 256), layer), _layer_spec((4, gp), layer)],
        out_specs=[blk, blk],
        scratch_shapes=[pltpu.VMEM((S5_G, n_chunks, S5_T * S5_HG), F32),
                        pltpu.VMEM((4, n_chunks, gp), F32),
                        pltpu.VMEM((4, n_chunks, gp), F32)],
        compiler_params=_cp(("parallel",)),
        name="s5_mixer",
    )(su0, su1, ktoe, bexp, cexp, pw16)


def _s5_tables(lam_re, lam_im, log_dt, b_re, b_im, c_re, c_im, d_skip):
    hp = lax.Precision.HIGH
    lam_re, lam_im, log_dt = lam_re.astype(F32), lam_im.astype(F32), log_dt.astype(F32)
    b_re, b_im, c_re, c_im = b_re.astype(F32), b_im.astype(F32), c_re.astype(F32), c_im.astype(F32)
    dt = jnp.exp(log_dt)[..., None]
    n = jnp.arange(S5_T + 1, dtype=F32)[:, None, None, None]
    pw_re = jnp.exp(n * lam_re * dt) * jnp.cos(n * lam_im * dt)
    pw_im = jnp.exp(n * lam_re * dt) * jnp.sin(n * lam_im * dt)
    neg_re = jnp.exp(-n * lam_re * dt) * jnp.cos(n * lam_im * dt)
    neg_im = -jnp.exp(-n * lam_re * dt) * jnp.sin(n * lam_im * dt)
    m = jnp.arange(S5_T, -1, -1, dtype=F32)[:, None, None, None]
    dsc_re = jnp.exp(m * lam_re * dt) * jnp.cos(m * lam_im * dt)
    dsc_im = jnp.exp(m * lam_re * dt) * jnp.sin(m * lam_im * dt)
    lb_re, lb_im = pw_re[1], pw_im[1]
    num_re, num_im = lb_re - 1.0, lb_im
    den = lam_re * lam_re + lam_im * lam_im
    coef_re = ((num_re * lam_re + num_im * lam_im) / den)[..., None]
    coef_im = ((num_im * lam_re - num_re * lam_im) / den)[..., None]
    bb_re = coef_re * b_re - coef_im * b_im
    bb_im = coef_re * b_im + coef_im * b_re

    def cmul_b(p_re, p_im, d):
        re = p_re[..., None] * bb_re[d][None] - p_im[..., None] * bb_im[d][None]
        im = p_re[..., None] * bb_im[d][None] + p_im[..., None] * bb_re[d][None]
        to = lambda z: z.transpose(1, 0, 3, 2).reshape(S5_G, S5_T * S5_HG, S5_P)
        return to(re), to(im)

    def cmul_c(p_re, p_im, d):
        re = c_re[d][None] * p_re[:, :, None, :] - c_im[d][None] * p_im[:, :, None, :]
        im = c_re[d][None] * p_im[:, :, None, :] + c_im[d][None] * p_re[:, :, None, :]
        to = lambda z: z.transpose(1, 3, 0, 2).reshape(S5_G, S5_P, S5_T * S5_HG)
        return to(re), to(-im)

    def lag_operator(u, v):
        return jnp.einsum("gap,gpb->gab", jnp.concatenate(u, axis=-1), jnp.concatenate(v, axis=1), precision=hp)

    first = slice(0, S5_T)
    kf = lag_operator(cmul_b(neg_re[first, 0], neg_im[first, 0], 0), cmul_c(pw_re[first, 0], pw_im[first, 0], 0))
    kb = lag_operator(cmul_b(pw_re[first, 1], pw_im[first, 1], 1), cmul_c(neg_re[first, 1], neg_im[first, 1], 1))
    tok = np.arange(S5_T * S5_HG) // S5_HG
    skip = jnp.asarray(np.eye(S5_T * S5_HG, dtype=np.float32)) * jnp.tile(
        d_skip.astype(F32).reshape(S5_G, S5_HG), (1, S5_T))[:, None, :]
    ktoe = (jnp.where(jnp.asarray(tok[None, :] >= tok[:, None]), kf, 0.0)
            + jnp.where(jnp.asarray(tok[:, None] >= tok[None, :]), kb, 0.0) + skip)
    last = slice(1, S5_T + 1)
    bexp = jnp.concatenate(cmul_b(dsc_re[last, 0], dsc_im[last, 0], 0)
                           + cmul_b(pw_re[first, 1], pw_im[first, 1], 1), axis=-1)
    cexp = jnp.stack(cmul_c(pw_re[last, 0], pw_im[last, 0], 0)
                     + cmul_c(dsc_re[first, 1], dsc_im[first, 1], 1), axis=1)
    pw16 = jnp.stack([pw_re[S5_T, 0].reshape(-1), pw_im[S5_T, 0].reshape(-1),
                      pw_re[S5_T, 1].reshape(-1), pw_im[S5_T, 1].reshape(-1)], axis=0)
    return ktoe.astype(BF16), bexp.astype(BF16), cexp.astype(BF16), pw16


def _pool_kernel(u_ref, w_ref, scale_ref, o_ref, pad_ref, *, n_lat):
    pad = 8
    seg = ((pad, 0, CTX_LEN), (pad + CTX_LEN + 2 * pad, CTX_LEN, n_lat))
    pad_ref[...] = jnp.zeros_like(pad_ref)
    for p0, u0, n in seg:
        pad_ref[p0:p0 + n, :] = u_ref[0, u0:u0 + n, :]
    for p0, u0, n in seg:
        for t0 in range(0, n, TILE):
            t = t0 + lax.broadcasted_iota(jnp.int32, (TILE, 1), 0)
            outs = []
            for i, w in enumerate(POOL_WINDOWS):
                lo, hi = i * POOL_GW, (i + 1) * POOL_GW
                acc = None
                for jj in range(w):
                    s = p0 + t0 + jj - w // 2
                    piece = pad_ref[s:s + TILE, lo:hi]
                    acc = piece if acc is None else acc + piece
                cnt = (jnp.clip(t - w // 2 + w, 0, n) - jnp.clip(t - w // 2, 0, n)).astype(F32)
                g = u_ref[0, u0 + t0:u0 + t0 + TILE, lo:hi]
                outs.append(_dot((acc / cnt - g).astype(BF16), w_ref[i].astype(BF16)))
            o_ref[0, u0 + t0:u0 + t0 + TILE, :] = (jnp.concatenate(outs, axis=-1) * scale_ref[...]).astype(o_ref.dtype)


def _pool(pu, layer, w_pool, scale):
    n_b, t_all, _ = pu.shape
    blk = pl.BlockSpec((1, t_all, 256), lambda b: (b, 0, 0))
    return pl.pallas_call(
        functools.partial(_pool_kernel, n_lat=t_all - CTX_LEN),
        out_shape=jax.ShapeDtypeStruct((n_b, t_all, 256), BF16),
        grid=(n_b,),
        in_specs=[blk, _layer_spec((len(POOL_WINDOWS), POOL_GW, POOL_GW), layer), _layer_spec((1, 256), layer)],
        out_specs=blk,
        scratch_shapes=[pltpu.VMEM((t_all + 32, 256), F32)],
        compiler_params=_cp(("parallel",)),
        name="pool_mixer",
    )(pu, w_pool, scale)


def _gelu_tanh(x):
    return 0.5 * x * (1.0 + jnp.tanh(math.sqrt(2.0 / math.pi) * (x + 0.044715 * (x * x * x))))


def _outproj_kernel(ygla_ref, yna_ref, ys50_ref, ys51_ref, ypool_ref, gate_ref, x_ref, xc_ref, mod_ref, wout_ref,
                    gpost_ref, wglu_ref, bglu_ref, *out_refs, n_batch, with_ctx):
    b = pl.program_id(0)
    j = pl.program_id(1)
    is_ctx = jnp.logical_and(j == 0, with_ctx)
    gate_m = mod_ref[pl.ds(jnp.where(is_ctx, n_batch, b), 1), :][:, 2 * D_MODEL:3 * D_MODEL]
    xt = jnp.where(is_ctx, xc_ref[0], x_ref[0]) if with_ctx else x_ref[0]
    sg = gate_ref[0].astype(F32)
    g5 = _gelu_tanh(jnp.concatenate([ys50_ref[0], ys51_ref[0]], axis=-1))
    s5o = g5 * jax.nn.sigmoid(_dot(g5.astype(BF16), wglu_ref[...]) + bglu_ref[...])
    mixed = jnp.concatenate([ygla_ref[0].astype(F32), yna_ref[0].astype(F32), s5o, ypool_ref[0].astype(F32)], axis=-1)
    acc = _dot((mixed * sg).astype(BF16), wout_ref[...])
    ms = jnp.mean(acc * acc, axis=-1, keepdims=True)
    res = xt + gate_m * (acc * lax.rsqrt(ms + NORM_EPS) * gpost_ref[...])
    if with_ctx:
        xo_ref, xco_ref = out_refs

        @pl.when(j == 0)
        def _():
            xco_ref[0] = res

        @pl.when(j > 0)
        def _():
            xo_ref[0] = res
    else:
        out_refs[0][0] = res


def _outproj(ys, gate_cols, x, xc, layer, mod_all, w_out, g_post, w_glu, b_glu, with_ctx):
    n_b, n_lat, d = x.shape
    n_lat_tiles = n_lat // TILE
    off = 0 if with_ctx else 1
    n_steps = n_lat_tiles + (1 if with_ctx else 0)
    mix_map = lambda b, j: (b, j + off, 0)
    lat_map = (lambda b, j: (b, jnp.maximum(j - 1, 0), 0)) if with_ctx else (lambda b, j: (b, j, 0))
    ctx_map = lambda b, j: (b, 0, 0)
    out_shape = [jax.ShapeDtypeStruct(x.shape, F32)]
    out_specs = [pl.BlockSpec((1, TILE, d), lat_map)]
    if with_ctx:
        out_shape.append(jax.ShapeDtypeStruct(xc.shape, F32))
        out_specs.append(pl.BlockSpec((1, CTX_LEN, d), ctx_map))
    outs = pl.pallas_call(
        functools.partial(_outproj_kernel, n_batch=n_b, with_ctx=with_ctx),
        out_shape=out_shape,
        grid=(n_b, n_steps),
        in_specs=[pl.BlockSpec((1, TILE, w), mix_map) for w in (256, 256, 128, 128, 256)]
                 + [pl.BlockSpec((1, TILE, MIX_W), mix_map),
                    pl.BlockSpec((1, TILE, d), lat_map),
                    pl.BlockSpec((1, CTX_LEN, d), ctx_map),
                    _layer_spec((8, 3 * d), layer),
                    _layer_spec((MIX_W, d), layer),
                    _layer_spec((1, d), layer),
                    _layer_spec((GROUP_W, GROUP_W), layer),
                    _layer_spec((1, GROUP_W), layer)],
        out_specs=out_specs,
        compiler_params=_cp(("parallel", "arbitrary")),
        name="gate_outproj_residual",
    )(*ys, gate_cols, x, xc, mod_all, w_out, g_post, w_glu, b_glu)
    return (outs[0], outs[1]) if with_ctx else (outs[0], None)


def _rope_tables(n_tok):
    t = np.arange(n_tok)
    half = GLA_DK // 2
    freqs = ROPE_BASE ** (-np.arange(0, half, 2, dtype=np.float32) / half)
    d = np.arange(GLA_DK)
    pos = np.where((d // half)[None, :] == 0, (t // GRID_W)[:, None], (t % GRID_W)[:, None]).astype(np.float32)
    ang = jnp.asarray(pos) * jnp.asarray(freqs[(d % half) % (half // 2)])[None, :]
    first = jnp.asarray(((d % half) < half // 2)[None, :])
    cos, sin = jnp.cos(ang), jnp.sin(ang)
    tabs = (cos, jnp.where(first, -sin, 0.0), jnp.where(first, 0.0, sin))
    return tuple(jnp.tile(z, (1, GLA_HEADS)).astype(F32) for z in tabs)


def _pad_w_in(w_in):
    zeros = jnp.zeros(w_in.shape[:2] + (128 - 2 * GLA_RANK,), w_in.dtype)
    return jnp.concatenate([w_in[:, :, :N_GATE_END], zeros, w_in[:, :, N_GATE_END:]], axis=2).astype(BF16)


def _gate_weights(w_gate, b_gate):
    n_l = w_gate.shape[0]
    wg = jnp.zeros((n_l, 128, 256), F32)
    wg = wg.at[:, 0:GLA_RANK, 0:128].set(w_gate[:, 0].astype(F32))
    wg = wg.at[:, GLA_RANK:2 * GLA_RANK, 128:256].set(w_gate[:, 1].astype(F32))
    return wg.astype(BF16), b_gate.astype(F32).reshape(n_l, 1, 256)


def kernel(x, c, ctx, c_ctx, w_mod, b_mod, g_pre, g_post, w_in, w_out, gla_w_gate, gla_b_gate, gla_g_norm, na_rpb,
           s5_lam_re, s5_lam_im, s5_log_dt, s5_b_re, s5_b_im, s5_c_re, s5_c_im, s5_d, s5_w_glu, s5_b_glu,
           pool_w, pool_scale):
    n_b, n_lat, d = x.shape
    depth = w_mod.shape[0]
    rows = n_lat // GRID_W
    assert d == D_MODEL and ctx.shape[1] == CTX_LEN and n_lat % TILE == 0 and n_b < 8 and rows >= NA_KR
    c_rows = jnp.zeros((8, d), F32).at[0:n_b].set(c).at[n_b].set(c_ctx)
    mod_all = _modulation(c_rows, w_mod, b_mod)
    rope_tabs = _rope_tables(n_lat)
    w_pad = _pad_w_in(w_in)
    wg, bg = _gate_weights(gla_w_gate, gla_b_gate)
    g_pre3 = g_pre.astype(F32).reshape(depth, 1, d)
    g_post3 = g_post.astype(F32).reshape(depth, 1, d)
    gn = jnp.tile(gla_g_norm.astype(F32), (1, GLA_HEADS)).reshape(depth, 1, GLA_HEADS * GLA_DV)
    rel_tab = jax.vmap(_na_rel_table)(na_rpb)
    ktoe, bexp, cexp, pw16 = jax.vmap(_s5_tables)(s5_lam_re, s5_lam_im, s5_log_dt, s5_b_re, s5_b_im,
                                                  s5_c_re, s5_c_im, s5_d)
    w_out_b = w_out.astype(BF16)
    w_glu_b = s5_w_glu.astype(BF16)
    b_glu3 = s5_b_glu.astype(F32).reshape(depth, 1, GROUP_W)
    pool_scale3 = pool_scale.astype(F32).reshape(depth, 1, GROUP_W)
    xc = ctx
    for l in range(depth):
        last = l == depth - 1
        qk, gv, lg, nq, nk, nv, su0, su1, pu, gate_cols = _inproj(x, xc, l, mod_all, g_pre3, w_pad, wg, bg,
                                                                  rope_tabs)
        y_gla = _gla(qk, gv, lg, l, gn)
        y_na = _na(nq, nk, nv, l, rel_tab)
        y_s50, y_s51 = _s5(su0, su1, l, ktoe, bexp, cexp, pw16)
        y_pool = _pool(pu, l, pool_w, pool_scale3)
        x, xc = _outproj((y_gla, y_na, y_s50, y_s51, y_pool), gate_cols, x, xc, l, mod_all, w_out_b, g_post3,
                         w_glu_b, b_glu3, with_ctx=not last)
    return x
```

```python
import functools
import math

import numpy as np
import jax
import jax.numpy as jnp
from jax import lax
from jax.experimental import pallas as pl
from jax.experimental.pallas import tpu as pltpu

F32 = jnp.float32
BF16 = jnp.bfloat16

D_MODEL = 1024
GRID_W = 64
CTX_LEN = 256
GROUP_W = 256
MIX_W = 1024
NORM_EPS = 1e-6
GLA_HEADS = 4
GLA_DV = 64
GLA_DK = 32
GLA_RANK = 16
GLA_TAU = 16.0
ROPE_BASE = 10000.0
NA_HEADS = 4
NA_DH = 64
NA_KR = 8
NA_KC = 16
S5_HG = 16
S5_G = 16
S5_P = 64
POOL_WINDOWS = (2, 4, 8, 16)
POOL_GW = 64

TILE = 256
GLA_C = 128
GLA_STEP = 2
S5_T = 16
NA_ROWS = TILE // GRID_W
NA_REL_W = 1024
NEG_BIG = -1e30

_COL = {}
_off = 0
for _name, _w in (("gk", 128), ("gv", 256), ("gg", 128), ("nk", 256), ("nv", 256), ("su", 256),
                  ("gq", 128), ("nq", 256), ("pu", 256), ("gate", 1024)):
    _COL[_name] = (_off, _off + _w)
    _off += _w
N_PAD_COLS = _off
N_GATE_END = 128 + 256 + 2 * GLA_RANK

_VMEM_LIMIT = 56 * 1024 * 1024


def _cp(sem, vmem=_VMEM_LIMIT):
    return pltpu.CompilerParams(dimension_semantics=sem, vmem_limit_bytes=vmem)


def _dot(a, b):
    return jnp.dot(a, b, preferred_element_type=F32)


def _dot_nt(a, b):
    return lax.dot_general(a, b, (((1,), (1,)), ((), ())), preferred_element_type=F32)


def _split_bf16(a):
    hi = a.astype(BF16)
    lo = (a - hi.astype(F32)).astype(BF16)
    return hi, lo


def _mod_kernel(s_ref, w_ref, b_ref, o_ref):
    s = s_ref[...]
    a = (s * jax.nn.sigmoid(s)).astype(BF16)
    o_ref[0] = _dot(a, w_ref[0].astype(BF16)) + b_ref[0]


def _modulation(c_rows, w_mod, b_mod):
    n_l, d, d3 = w_mod.shape
    return pl.pallas_call(
        _mod_kernel,
        out_shape=jax.ShapeDtypeStruct((n_l, 8, d3), F32),
        grid=(n_l, d3 // d),
        in_specs=[pl.BlockSpec((8, d), lambda l, j: (0, 0)),
                  pl.BlockSpec((1, d, d), lambda l, j: (l, 0, j)),
                  pl.BlockSpec((1, 1, d), lambda l, j: (l, 0, j))],
        out_specs=pl.BlockSpec((1, 8, d), lambda l, j: (l, 0, j)),
        compiler_params=_cp(("parallel", "parallel")),
        name="adaln_modulation",
    )(c_rows, w_mod, b_mod.reshape(n_l, 1, d3))


def _inproj_kernel(x_ref, xc_ref, mod_ref, gpre_ref, w_ref, wg_ref, bg_ref, cos_ref, sa_ref, sb_ref,
                   qk_ref, v_ref, lg_ref, nq_ref, nk_ref, nv_ref, su0_ref, su1_ref, pu_ref, gate_ref, *, n_batch):
    b = pl.program_id(0)
    j = pl.program_id(1)

    def body(src_ref, mod_row, with_rope):
        xt = src_ref[0]
        m = mod_ref[pl.ds(mod_row, 1), :]
        shift = m[:, 0:D_MODEL]
        scale = m[:, D_MODEL:2 * D_MODEL]
        ms = jnp.mean(xt * xt, axis=-1, keepdims=True)
        h = xt * lax.rsqrt(ms + NORM_EPS) * gpre_ref[...] * (1.0 + scale) + shift
        hb = h.astype(BF16)

        def proj(name):
            lo, hi = _COL[name]
            return _dot(hb, w_ref[:, lo:hi])

        def rope(t):
            if not with_rope:
                return t
            return t * cos_ref[...] + pltpu.roll(t, 128 - 8, 1) * sa_ref[...] + pltpu.roll(t, 8, 1) * sb_ref[...]

        qk_ref[0, :, 0:128] = rope(proj("gq")) * (GLA_DK ** -0.5)
        qk_ref[0, :, 128:256] = rope(proj("gk"))
        v_ref[0] = proj("gv")
        z = _dot(proj("gg").astype(BF16), wg_ref[...]) + bg_ref[...]
        lg_ref[0] = (jnp.minimum(z, 0.0) - jnp.log(1.0 + jnp.exp(-jnp.abs(z)))) * (1.0 / GLA_TAU)
        nq_ref[0] = (proj("nq") * (NA_DH ** -0.5)).astype(BF16)
        nk_ref[0] = proj("nk").T.astype(BF16)
        nv_ref[0] = proj("nv").astype(BF16)
        su = proj("su")
        su0_ref[0] = su[:, 0:128]
        su1_ref[0] = su[:, 128:256]
        pu_ref[0] = proj("pu")
        gate_ref[0] = proj("gate").astype(BF16)

    pl.when(j == 0)(lambda: body(xc_ref, n_batch, False))
    pl.when(j > 0)(lambda: body(x_ref, b, True))


def _layer_spec(shape, layer):
    return pl.BlockSpec((None,) + tuple(shape), lambda *_: (layer,) + (0,) * len(shape))


def _inproj(x, xc, layer, mod_all, g_pre, w_pad, wg, bg, rope_tabs):
    n_b, n_lat, d = x.shape
    n_tiles = (n_lat + CTX_LEN) // TILE
    t_all = n_lat + CTX_LEN
    lat_map = lambda b, j: (b, jnp.maximum(j - 1, 0), 0)
    tab_map = lambda b, j: (jnp.maximum(j - 1, 0), 0)
    out_map = lambda b, j: (b, j, 0)
    outs = (("qk", 256, F32), ("gv", 256, F32), ("lg", 256, F32), ("nq", 256, BF16), ("nkT", 256, BF16),
            ("nv", 256, BF16), ("su0", 128, F32), ("su1", 128, F32), ("pu", 256, F32), ("gate", 1024, BF16))
    out_shape = [jax.ShapeDtypeStruct((n_b, w, t_all) if n == "nkT" else (n_b, t_all, w), dt) for n, w, dt in outs]
    out_specs = [pl.BlockSpec((1, w, TILE), lambda b, j: (b, 0, j)) if n == "nkT"
                 else pl.BlockSpec((1, TILE, w), out_map) for n, w, dt in outs]
    return pl.pallas_call(
        functools.partial(_inproj_kernel, n_batch=n_b),
        out_shape=out_shape,
        grid=(n_b, n_tiles),
        in_specs=[pl.BlockSpec((1, TILE, d), lat_map),
                  pl.BlockSpec((1, CTX_LEN, d), lambda b, j: (b, 0, 0)),
                  _layer_spec((8, 3 * d), layer),
                  _layer_spec((1, d), layer),
                  _layer_spec((d, N_PAD_COLS), layer),
                  _layer_spec((128, 256), layer),
                  _layer_spec((1, 256), layer),
                  pl.BlockSpec((TILE, 128), tab_map),
                  pl.BlockSpec((TILE, 128), tab_map),
                  pl.BlockSpec((TILE, 128), tab_map)],
        out_specs=out_specs,
        compiler_params=_cp(("parallel", "arbitrary")),
        name="modnorm_inproj",
    )(x, xc, mod_all, g_pre, w_pad, wg, bg, *rope_tabs)


def _gla_kernel(qk_ref, v_ref, lg_ref, gn_ref, y_ref, stf_ref, stb_ref, of_ref, ob_ref, *, n_chunks, n_ctx_chunks):
    c_len = GLA_C
    row = lax.broadcasted_iota(jnp.int32, (c_len, c_len), 0)
    col = lax.broadcasted_iota(jnp.int32, (c_len, c_len), 1)
    tri_f = (col <= row).astype(BF16)
    tri_b = (col >= row).astype(BF16)
    arow = lax.broadcasted_iota(jnp.int32, (c_len, 4 * c_len), 0)
    acol = lax.broadcasted_iota(jnp.int32, (c_len, 4 * c_len), 1) % c_len
    amask_f = acol <= arow
    amask_b = acol >= arow
    head_k = lax.broadcasted_iota(jnp.int32, (1, 128), 1) // GLA_DK
    head_v = lax.broadcasted_iota(jnp.int32, (1, 256), 1) // GLA_DV
    bd_mask = (lax.broadcasted_iota(jnp.int32, (256, 128), 0) // GLA_DV
               == lax.broadcasted_iota(jnp.int32, (256, 128), 1) // GLA_DK)
    ones_bd = jnp.where(lax.broadcasted_iota(jnp.int32, (256, 256), 0) // GLA_DV
                        == lax.broadcasted_iota(jnp.int32, (256, 256), 1) // GLA_DV,
                        1.0 / GLA_DV, 0.0).astype(BF16)
    mid = c_len // 2

    def scan_body(i, carry):
        def bwd_chunk(pos):
            return jnp.where(pos < n_ctx_chunks, n_ctx_chunks - 1 - pos, n_chunks - 1 - (pos - n_ctx_chunks))

        dirs = [d for d in (0, 1) for _ in range(GLA_STEP)]
        chunks = [i * GLA_STEP + u for u in range(GLA_STEP)] + [bwd_chunk(i * GLA_STEP + u) for u in range(GLA_STEP)]
        jobs = range(len(dirs))
        r0 = [pl.multiple_of(c * c_len, c_len) for c in chunks]
        st_refs = (stf_ref, stb_ref)
        tri = (tri_f, tri_b)
        amask = (amask_f, amask_b)
        end = (c_len - 1, 0)
        q = [qk_ref[0, pl.ds(r0[n], c_len), 0:128] for n in jobs]
        k = [qk_ref[0, pl.ds(r0[n], c_len), 128:256] for n in jobs]
        v = [v_ref[0, pl.ds(r0[n], c_len), :] for n in jobs]
        lg = [_split_bf16(lg_ref[0, pl.ds(r0[n], c_len), 128 * dirs[n]:128 * (dirs[n] + 1)]) for n in jobs]
        cum = [_dot(tri[dirs[n]], lg[n][0]) + _dot(tri[dirs[n]], lg[n][1]) for n in jobs]
        c_ref = [cum[n][mid:mid + 1, :] for n in jobs]
        c_end = [cum[n][end[dirs[n]]:end[dirs[n]] + 1, :] for n in jobs]
        qt = [q[n] * jnp.exp(cum[n] - c_ref[n]) for n in jobs]
        kt = [k[n] * jnp.exp(c_ref[n] - cum[n]) for n in jobs]
        ks = [jnp.concatenate([jnp.where(head_k == hh, kt[n], 0.0) for hh in range(GLA_HEADS)],
                              axis=0).astype(BF16) for n in jobs]
        att = [_dot_nt(qt[n].astype(BF16), ks[n]) for n in jobs]
        qh = [(qt[n] * jnp.exp(c_ref[n])).astype(BF16) for n in jobs]
        kh = [(kt[n] * jnp.exp(c_end[n] - c_ref[n])).astype(BF16) for n in jobs]
        upd = [_dot(v[n].T.astype(BF16), kh[n]) for n in jobs]
        att = [jnp.where(amask[dirs[n]], att[n], 0.0).astype(BF16) for n in jobs]
        vb = [v[n].astype(BF16) for n in jobs]
        vs = [jnp.concatenate([jnp.where(head_v == hh, vb[n], jnp.zeros_like(vb[n])) for hh in range(GLA_HEADS)],
                              axis=0) for n in jobs]
        o_intra = [_dot(att[n], vs[n]) for n in jobs]
        outs = (of_ref, ob_ref)
        st = [st_refs[d][...] for d in (0, 1)]
        for n in jobs:
            d = dirs[n]
            outs[d][pl.ds(r0[n], c_len), :] = o_intra[n] + _dot_nt(qh[n], st[d].astype(BF16))
            st[d] = st[d] * jnp.exp(c_end[n]) + jnp.where(bd_mask, upd[n], 0.0)
        for d in (0, 1):
            st_refs[d][...] = st[d]
        return carry

    def norm_body(c, carry):
        half = TILE // 2
        r0 = [pl.multiple_of(c * TILE + h * half, half) for h in (0, 1)]
        o = [of_ref[pl.ds(r, half), :] + ob_ref[pl.ds(r, half), :] for r in r0]
        sq = [_split_bf16(x * x) for x in o]
        ms = [_dot(hi, ones_bd) + _dot(lo, ones_bd) for hi, lo in sq]
        for r, x, m in zip(r0, o, ms):
            y_ref[0, pl.ds(r, half), :] = (x * lax.rsqrt(m + NORM_EPS) * gn_ref[...]).astype(y_ref.dtype)
        return carry

    stf_ref[...] = jnp.zeros_like(stf_ref)
    stb_ref[...] = jnp.zeros_like(stb_ref)
    lax.fori_loop(0, n_chunks // GLA_STEP, scan_body, 0)
    lax.fori_loop(0, n_chunks * c_len // TILE, norm_body, 0)


def _gla(qk, v, lg, layer, g_norm):
    n_b, t_all, _ = qk.shape
    blk = lambda w: pl.BlockSpec((1, t_all, w), lambda b: (b, 0, 0))
    return pl.pallas_call(
        functools.partial(_gla_kernel, n_chunks=t_all // GLA_C, n_ctx_chunks=CTX_LEN // GLA_C),
        out_shape=jax.ShapeDtypeStruct((n_b, t_all, 256), BF16),
        grid=(n_b,),
        in_specs=[blk(256), blk(256), blk(256), _layer_spec((1, 256), layer)],
        out_specs=blk(256),
        scratch_shapes=[pltpu.VMEM((256, 128), F32), pltpu.VMEM((256, 128), F32),
                        pltpu.VMEM((t_all, 256), F32), pltpu.VMEM((t_all, 256), F32)],
        compiler_params=_cp(("parallel",)),
        name="gla_mixer",
    )(qk, v, lg, g_norm)


def _na_kernel(q_ref, kt0_ref, kt1_ref, kt2_ref, v0_ref, v1_ref, v2_ref, ktc_ref, vc_ref, rel_ref, o_ref, *, rows):
    j = pl.program_id(1)
    left = lax.broadcasted_iota(jnp.int32, (1, 128), 1) < NA_DH
    ones_blk = jnp.ones((TILE, 128), BF16)
    ro0 = NA_KR - 1 - NA_ROWS

    def attend(with_window):
        key_tiles = [(ktc_ref, vc_ref, None)]
        if with_window:
            g = j - 1
            q_row = lax.broadcasted_iota(jnp.int32, (TILE, TILE), 0) // GRID_W
            k_row = lax.broadcasted_iota(jnp.int32, (TILE, TILE), 1) // GRID_W
            first = jnp.clip(NA_ROWS * g + q_row - NA_KR // 2, 0, rows - NA_KR) - NA_ROWS * (g - 1)
            for d, (kr, vr) in enumerate(((kt0_ref, v0_ref), (kt1_ref, v1_ref), (kt2_ref, v2_ref))):
                window = None if d == 1 else jnp.where(
                    (k_row + NA_ROWS * d >= first) & (k_row + NA_ROWS * d < first + NA_KR), 0.0, NEG_BIG)
                key_tiles.append((kr, vr, (d, window)))
        def lanes(hh):
            return slice(128 * (hh // 2), 128 * (hh // 2 + 1))

        def score(hh):
            qp = q_ref[0, :, lanes(hh)]
            zero = jnp.zeros_like(qp)
            qh = jnp.where(left, qp, zero) if hh % 2 == 0 else jnp.where(left, zero, qp)
            kt_all = jnp.concatenate([kr[0, lanes(hh), :] for kr, _, _ in key_tiles], axis=1)
            return _dot(qh, kt_all)

        def probs(hh, s):
            if with_window:
                parts = [s[:, 0:TILE]]
                for _, _, (d, window) in key_tiles[1:]:
                    rel = jnp.concatenate(
                        [rel_ref[hh, (ro0 - i) % 2, :, GRID_W * ((ro0 - i) // 2 * 2) + d * TILE:
                                 GRID_W * ((ro0 - i) // 2 * 2) + (d + 1) * TILE] for i in range(NA_ROWS)], axis=0)
                    part = s[:, (d + 1) * TILE:(d + 2) * TILE] + rel
                    parts.append(part if window is None else part + window)
                s = jnp.concatenate(parts, axis=1)
            return jnp.exp(s - jnp.max(s, axis=-1, keepdims=True)).astype(BF16)

        def attend_values(hh, p):
            vaug = jnp.concatenate([jnp.concatenate([vr[0, :, lanes(hh)], ones_blk], axis=1)
                                    for _, vr, _ in key_tiles], axis=0)
            acc = _dot(p, vaug)
            return acc[:, 0:128] / acc[:, 128:256]

        s_next = score(0)
        outs = []
        for hh in range(NA_HEADS):
            s_cur = s_next
            if hh + 1 < NA_HEADS:
                s_next = score(hh + 1)
            outs.append(attend_values(hh, probs(hh, s_cur)))
        for pair in range(NA_HEADS // 2):
            o_pair = jnp.where(left, outs[2 * pair], outs[2 * pair + 1])
            o_ref[0, :, lanes(2 * pair)] = o_pair.astype(o_ref.dtype)

    @pl.when(j == 0)
    def _():
        attend(False)

    @pl.when(j > 0)
    def _():
        attend(True)


def _na(nq, nkt, nv, layer, rel_tab):
    n_b, t_all, _ = nq.shape
    n_tiles = t_all // TILE
    rows = (t_all - CTX_LEN) // GRID_W

    def nbr(delta):
        return lambda b, j: (b, jnp.clip(j + delta, 1, n_tiles - 1), 0)

    def nbr_t(delta):
        return lambda b, j: (b, 0, jnp.clip(j + delta, 1, n_tiles - 1))

    tile = lambda f: pl.BlockSpec((1, TILE, 256), f)
    tile_t = lambda f: pl.BlockSpec((1, 256, TILE), f)
    return pl.pallas_call(
        functools.partial(_na_kernel, rows=rows),
        out_shape=jax.ShapeDtypeStruct((n_b, t_all, 256), BF16),
        grid=(n_b, n_tiles),
        in_specs=[tile(lambda b, j: (b, j, 0)),
                  tile_t(nbr_t(-1)), tile_t(nbr_t(0)), tile_t(nbr_t(1)),
                  tile(nbr(-1)), tile(nbr(0)), tile(nbr(1)),
                  tile_t(lambda b, j: (b, 0, 0)), tile(lambda b, j: (b, 0, 0)),
                  _layer_spec((NA_HEADS, 2, GRID_W, NA_REL_W), layer)],
        out_specs=tile(lambda b, j: (b, j, 0)),
        compiler_params=_cp(("parallel", "arbitrary")),
        name="na_mixer",
    )(nq, nkt, nkt, nkt, nv, nv, nv, nkt, nv, rel_tab)


def _na_rel_table(rpb):
    col = np.arange(GRID_W)
    cs = np.clip(col - NA_KC // 2, 0, GRID_W - NA_KC)
    col_mask = (col[None, :] >= cs[:, None]) & (col[None, :] < cs[:, None] + NA_KC)
    co = np.clip(col[None, :] - col[:, None], -(NA_KC - 1), NA_KC - 1) + (NA_KC - 1)
    onehot = (co[None] == np.arange(2 * NA_KC - 1)[:, None, None]).astype(np.float32)
    toe = jnp.einsum("hrc,cqk->hqrk", rpb.astype(F32), jnp.asarray(onehot),
                     precision=lax.Precision.HIGHEST)
    toe = jnp.where(jnp.asarray(col_mask)[None, :, None, :], toe, NEG_BIG)
    flat = toe.reshape(NA_HEADS, GRID_W, (2 * NA_KR - 1) * GRID_W)
    flat = jnp.pad(flat, ((0, 0), (0, 0), (0, NA_REL_W + GRID_W - flat.shape[-1])), constant_values=NEG_BIG)
    return jnp.stack([flat[:, :, :NA_REL_W], flat[:, :, GRID_W:]], axis=1)


def _s5_kernel(su0_ref, su1_ref, ktoe_ref, bexp_ref, cexp_ref, pw_ref, y0_ref, y1_ref, a_ref, e_ref, x_ref,
               *, n_chunks, n_ctx_chunks):
    blk = 8
    lane_blk = lax.broadcasted_iota(jnp.int32, (blk, 128), 1) // S5_HG
    masks = [lane_blk == m for m in range(8)]
    gp = S5_G * S5_P

    def pick(vals):
        acc = vals[0]
        for k in range(1, 8):
            acc = jnp.where(masks[k], vals[k], acc)
        return acc

    def block_transpose(src):
        rolled = []
        for d in range(8):
            diag = pick([src[(k + d) % 8] for k in range(8)])
            rolled.append(pltpu.roll(diag, S5_HG * d, 1) if d else diag)
        return [pick([rolled[(m - a) % 8] for m in range(8)]) for a in range(8)]

    def regroup_in(rb, carry):
        c0 = pl.multiple_of(rb * blk, blk)
        for hg, ref in enumerate((su0_ref, su1_ref)):
            for o in range(2):
                src = [ref[0, pl.ds(rb * (blk * S5_T) + o * 8 + m, blk, stride=S5_T), :] for m in range(8)]
                for a, val in enumerate(block_transpose(src)):
                    a_ref[hg * 8 + a, pl.ds(c0, blk), o * 128:(o + 1) * 128] = val
        return carry

    lax.fori_loop(0, n_chunks // blk, regroup_in, 0)

    for g in range(S5_G):
        e = _dot(a_ref[g].astype(BF16), bexp_ref[g])
        for kind in range(4):
            e_ref[kind, :, g * S5_P:(g + 1) * S5_P] = e[:, kind * S5_P:(kind + 1) * S5_P]

    afr, afi, abr, abi = pw_ref[0:1, :], pw_ref[1:2, :], pw_ref[2:3, :], pw_ref[3:4, :]

    def scan_body(i, carry):
        fr, fi, br, bi = carry
        cf = i
        cb = jnp.where(i < n_ctx_chunks, n_ctx_chunks - 1 - i, n_chunks - 1 - (i - n_ctx_chunks))
        x_ref[0, pl.ds(cf, 1), :] = fr
        x_ref[1, pl.ds(cf, 1), :] = fi
        x_ref[2, pl.ds(cb, 1), :] = br
        x_ref[3, pl.ds(cb, 1), :] = bi
        er = e_ref[0, pl.ds(cf, 1), :]
        ei = e_ref[1, pl.ds(cf, 1), :]
        gr = e_ref[2, pl.ds(cb, 1), :]
        gi = e_ref[3, pl.ds(cb, 1), :]
        return (afr * fr - afi * fi + er, afr * fi + afi * fr + ei,
                abr * br - abi * bi + gr, abr * bi + abi * br + gi)

    z = jnp.zeros((1, gp), F32)
    lax.fori_loop(0, n_chunks, scan_body, (z, z, z, z))

    for g in range(S5_G):
        y = _dot(a_ref[g].astype(BF16), ktoe_ref[g])
        for kind in range(4):
            y = y + _dot(x_ref[kind, :, g * S5_P:(g + 1) * S5_P].astype(BF16), cexp_ref[g, kind])
        a_ref[g] = y

    def regroup_out(rb, carry):
        c0 = pl.multiple_of(rb * blk, blk)
        for og, ref in enumerate((y0_ref, y1_ref)):
            for ht in range(2):
                src = [a_ref[og * 8 + m, pl.ds(c0, blk), ht * 128:(ht + 1) * 128] for m in range(8)]
                for a, val in enumerate(block_transpose(src)):
                    ref[0, pl.ds(rb * (blk * S5_T) + ht * 8 + a, blk, stride=S5_T), :] = val
        return carry

    lax.fori_loop(0, n_chunks // blk, regroup_out, 0)


def _s5(su0, su1, layer, ktoe, bexp, cexp, pw16):
    n_b, t_all, _ = su0.shape
    n_chunks = t_all // S5_T
    gp = S5_G * S5_P
    blk = pl.BlockSpec((1, t_all, 128), lambda b: (b, 0, 0))
    return pl.pallas_call(
        functools.partial(_s5_kernel, n_chunks=n_chunks, n_ctx_chunks=CTX_LEN // S5_T),
        out_shape=[jax.ShapeDtypeStruct((n_b, t_all, 128), F32)] * 2,
        grid=(n_b,),
        in_specs=[blk, blk, _layer_spec((S5_G, 256, 256), layer), _layer_spec((S5_G, 256, 256), layer),
                  _layer_spec((S5_G, 4, S5_P, 256), layer), _layer_spec((4, gp), layer)],
        out_specs=[blk, blk],
        scratch_shapes=[pltpu.VMEM((S5_G, n_chunks, S5_T * S5_HG), F32),
                        pltpu.VMEM((4, n_chunks, gp), F32),
                        pltpu.VMEM((4, n_chunks, gp), F32)],
        compiler_params=_cp(("parallel",)),
        name="s5_mixer",
    )(su0, su1, ktoe, bexp, cexp, pw16)


def _s5_tables(lam_re, lam_im, log_dt, b_re, b_im, c_re, c_im, d_skip):
    hp = lax.Precision.HIGH
    lam_re, lam_im, log_dt = lam_re.astype(F32), lam_im.astype(F32), log_dt.astype(F32)
    b_re, b_im, c_re, c_im = b_re.astype(F32), b_im.astype(F32), c_re.astype(F32), c_im.astype(F32)
    dt = jnp.exp(log_dt)[..., None]
    n = jnp.arange(S5_T + 1, dtype=F32)[:, None, None, None]
    pw_re = jnp.exp(n * lam_re * dt) * jnp.cos(n * lam_im * dt)
    pw_im = jnp.exp(n * lam_re * dt) * jnp.sin(n * lam_im * dt)
    neg_re = jnp.exp(-n * lam_re * dt) * jnp.cos(n * lam_im * dt)
    neg_im = -jnp.exp(-n * lam_re * dt) * jnp.sin(n * lam_im * dt)
    m = jnp.arange(S5_T, -1, -1, dtype=F32)[:, None, None, None]
    dsc_re = jnp.exp(m * lam_re * dt) * jnp.cos(m * lam_im * dt)
    dsc_im = jnp.exp(m * lam_re * dt) * jnp.sin(m * lam_im * dt)
    lb_re, lb_im = pw_re[1], pw_im[1]
    num_re, num_im = lb_re - 1.0, lb_im
    den = lam_re * lam_re + lam_im * lam_im
    coef_re = ((num_re * lam_re + num_im * lam_im) / den)[..., None]
    coef_im = ((num_im * lam_re - num_re * lam_im) / den)[..., None]
    bb_re = coef_re * b_re - coef_im * b_im
    bb_im = coef_re * b_im + coef_im * b_re

    def cmul_b(p_re, p_im, d):
        re = p_re[..., None] * bb_re[d][None] - p_im[..., None] * bb_im[d][None]
        im = p_re[..., None] * bb_im[d][None] + p_im[..., None] * bb_re[d][None]
        to = lambda z: z.transpose(1, 0, 3, 2).reshape(S5_G, S5_T * S5_HG, S5_P)
        return to(re), to(im)

    def cmul_c(p_re, p_im, d):
        re = c_re[d][None] * p_re[:, :, None, :] - c_im[d][None] * p_im[:, :, None, :]
        im = c_re[d][None] * p_im[:, :, None, :] + c_im[d][None] * p_re[:, :, None, :]
        to = lambda z: z.transpose(1, 3, 0, 2).reshape(S5_G, S5_P, S5_T * S5_HG)
        return to(re), to(-im)

    def lag_operator(u, v):
        return jnp.einsum("gap,gpb->gab", jnp.concatenate(u, axis=-1), jnp.concatenate(v, axis=1), precision=hp)

    first = slice(0, S5_T)
    kf = lag_operator(cmul_b(neg_re[first, 0], neg_im[first, 0], 0), cmul_c(pw_re[first, 0], pw_im[first, 0], 0))
    kb = lag_operator(cmul_b(pw_re[first, 1], pw_im[first, 1], 1), cmul_c(neg_re[first, 1], neg_im[first, 1], 1))
    tok = np.arange(S5_T * S5_HG) // S5_HG
    skip = jnp.asarray(np.eye(S5_T * S5_HG, dtype=np.float32)) * jnp.tile(
        d_skip.astype(F32).reshape(S5_G, S5_HG), (1, S5_T))[:, None, :]
    ktoe = (jnp.where(jnp.asarray(tok[None, :] >= tok[:, None]), kf, 0.0)
            + jnp.where(jnp.asarray(tok[:, None] >= tok[None, :]), kb, 0.0) + skip)
    last = slice(1, S5_T + 1)
    bexp = jnp.concatenate(cmul_b(dsc_re[last, 0], dsc_im[last, 0], 0)
                           + cmul_b(pw_re[first, 1], pw_im[first, 1], 1), axis=-1)
    cexp = jnp.stack(cmul_c(pw_re[last, 0], pw_im[last, 0], 0)
                     + cmul_c(dsc_re[first, 1], dsc_im[first, 1], 1), axis=1)
    pw16 = jnp.stack([pw_re[S5_T, 0].reshape(-1), pw_im[S5_T, 0].reshape(-1),
                      pw_re[S5_T, 1].reshape(-1), pw_im[S5_T, 1].reshape(-1)], axis=0)
    return ktoe.astype(BF16), bexp.astype(BF16), cexp.astype(BF16), pw16


def _pool_kernel(u_ref, w_ref, scale_ref, o_ref, pad_ref, *, n_lat):
    pad = 8
    seg = ((pad, 0, CTX_LEN), (pad + CTX_LEN + 2 * pad, CTX_LEN, n_lat))
    pad_ref[...] = jnp.zeros_like(pad_ref)
    for p0, u0, n in seg:
        pad_ref[p0:p0 + n, :] = u_ref[0, u0:u0 + n, :]
    for p0, u0, n in seg:
        for t0 in range(0, n, TILE):
            t = t0 + lax.broadcasted_iota(jnp.int32, (TILE, 1), 0)
            outs = []
            for i, w in enumerate(POOL_WINDOWS):
                lo, hi = i * POOL_GW, (i + 1) * POOL_GW
                acc = None
                for jj in range(w):
                    s = p0 + t0 + jj - w // 2
                    piece = pad_ref[s:s + TILE, lo:hi]
                    acc = piece if acc is None else acc + piece
                cnt = (jnp.clip(t - w // 2 + w, 0, n) - jnp.clip(t - w // 2, 0, n)).astype(F32)
                g = u_ref[0, u0 + t0:u0 + t0 + TILE, lo:hi]
                outs.append(_dot((acc / cnt - g).astype(BF16), w_ref[i].astype(BF16)))
            o_ref[0, u0 + t0:u0 + t0 + TILE, :] = (jnp.concatenate(outs, axis=-1) * scale_ref[...]).astype(o_ref.dtype)


def _pool(pu, layer, w_pool, scale):
    n_b, t_all, _ = pu.shape
    blk = pl.BlockSpec((1, t_all, 256), lambda b: (b, 0, 0))
    return pl.pallas_call(
        functools.partial(_pool_kernel, n_lat=t_all - CTX_LEN),
        out_shape=jax.ShapeDtypeStruct((n_b, t_all, 256), BF16),
        grid=(n_b,),
        in_specs=[blk, _layer_spec((len(POOL_WINDOWS), POOL_GW, POOL_GW), layer), _layer_spec((1, 256), layer)],
        out_specs=blk,
        scratch_shapes=[pltpu.VMEM((t_all + 32, 256), F32)],
        compiler_params=_cp(("parallel",)),
        name="pool_mixer",
    )(pu, w_pool, scale)


def _gelu_tanh(x):
    return 0.5 * x * (1.0 + jnp.tanh(math.sqrt(2.0 / math.pi) * (x + 0.044715 * (x * x * x))))


def _outproj_kernel(ygla_ref, yna_ref, ys50_ref, ys51_ref, ypool_ref, gate_ref, x_ref, xc_ref, mod_ref, wout_ref,
                    gpost_ref, wglu_ref, bglu_ref, *out_refs, n_batch, with_ctx):
    b = pl.program_id(0)
    j = pl.program_id(1)

    def gated_proj(part, i):
        lo, hi = i * GROUP_W, (i + 1) * GROUP_W
        gc = gate_ref[0, :, lo:hi].astype(F32)
        return _dot((part * (gc * jax.nn.sigmoid(gc))).astype(BF16), wout_ref[lo:hi, :])

    def body(res_ref, out_ref, mod_row):
        gate_m = mod_ref[pl.ds(mod_row, 1), :][:, 2 * D_MODEL:3 * D_MODEL]
        g5 = _gelu_tanh(jnp.concatenate([ys50_ref[0], ys51_ref[0]], axis=-1))
        z5 = _dot(g5.astype(BF16), wglu_ref[...])
        acc = gated_proj(ygla_ref[0], 0) + gated_proj(yna_ref[0], 1) + gated_proj(ypool_ref[0], 3)
        acc = acc + gated_proj(g5 * jax.nn.sigmoid(z5 + bglu_ref[...]), 2)
        ms = jnp.mean(acc * acc, axis=-1, keepdims=True)
        out_ref[0] = res_ref[0] + gate_m * (acc * lax.rsqrt(ms + NORM_EPS) * gpost_ref[...])

    if with_ctx:
        xo_ref, xco_ref = out_refs
        pl.when(j == 0)(lambda: body(xc_ref, xco_ref, n_batch))
        pl.when(j > 0)(lambda: body(x_ref, xo_ref, b))
    else:
        body(x_ref, out_refs[0], b)


def _outproj(ys, gate_cols, x, xc, layer, mod_all, w_out, g_post, w_glu, b_glu, with_ctx):
    n_b, n_lat, d = x.shape
    n_lat_tiles = n_lat // TILE
    off = 0 if with_ctx else 1
    n_steps = n_lat_tiles + (1 if with_ctx else 0)
    mix_map = lambda b, j: (b, j + off, 0)
    lat_map = (lambda b, j: (b, jnp.maximum(j - 1, 0), 0)) if with_ctx else (lambda b, j: (b, j, 0))
    ctx_map = lambda b, j: (b, 0, 0)
    out_shape = [jax.ShapeDtypeStruct(x.shape, F32)]
    out_specs = [pl.BlockSpec((1, TILE, d), lat_map)]
    if with_ctx:
        out_shape.append(jax.ShapeDtypeStruct(xc.shape, F32))
        out_specs.append(pl.BlockSpec((1, CTX_LEN, d), ctx_map))
    outs = pl.pallas_call(
        functools.partial(_outproj_kernel, n_batch=n_b, with_ctx=with_ctx),
        out_shape=out_shape,
        grid=(n_b, n_steps),
        in_specs=[pl.BlockSpec((1, TILE, w), mix_map) for w in (256, 256, 128, 128, 256)]
                 + [pl.BlockSpec((1, TILE, MIX_W), mix_map),
                    pl.BlockSpec((1, TILE, d), lat_map),
                    pl.BlockSpec((1, CTX_LEN, d), ctx_map),
                    _layer_spec((8, 3 * d), layer),
                    _layer_spec((MIX_W, d), layer),
                    _layer_spec((1, d), layer),
                    _layer_spec((GROUP_W, GROUP_W), layer),
                    _layer_spec((1, GROUP_W), layer)],
        out_specs=out_specs,
        compiler_params=_cp(("parallel", "arbitrary")),
        name="gate_outproj_residual",
    )(*ys, gate_cols, x, xc, mod_all, w_out, g_post, w_glu, b_glu)
    return (outs[0], outs[1]) if with_ctx else (outs[0], None)


def _rope_tables(n_tok):
    t = np.arange(n_tok)
    half = GLA_DK // 2
    freqs = ROPE_BASE ** (-np.arange(0, half, 2, dtype=np.float32) / half)
    d = np.arange(GLA_DK)
    pos = np.where((d // half)[None, :] == 0, (t // GRID_W)[:, None], (t % GRID_W)[:, None]).astype(np.float32)
    ang = jnp.asarray(pos) * jnp.asarray(freqs[(d % half) % (half // 2)])[None, :]
    first = jnp.asarray(((d % half) < half // 2)[None, :])
    cos, sin = jnp.cos(ang), jnp.sin(ang)
    tabs = (cos, jnp.where(first, -sin, 0.0), jnp.where(first, 0.0, sin))
    return tuple(jnp.tile(z, (1, GLA_HEADS)).astype(F32) for z in tabs)


def _pad_w_in(w_in):
    zeros = jnp.zeros(w_in.shape[:2] + (128 - 2 * GLA_RANK,), w_in.dtype)
    return jnp.concatenate([w_in[:, :, :N_GATE_END], zeros, w_in[:, :, N_GATE_END:]], axis=2).astype(BF16)


def _gate_weights(w_gate, b_gate):
    n_l = w_gate.shape[0]
    wg = jnp.zeros((n_l, 128, 256), F32)
    wg = wg.at[:, 0:GLA_RANK, 0:128].set(w_gate[:, 0].astype(F32))
    wg = wg.at[:, GLA_RANK:2 * GLA_RANK, 128:256].set(w_gate[:, 1].astype(F32))
    return wg.astype(BF16), b_gate.astype(F32).reshape(n_l, 1, 256)


def kernel(x, c, ctx, c_ctx, w_mod, b_mod, g_pre, g_post, w_in, w_out, gla_w_gate, gla_b_gate, gla_g_norm, na_rpb,
           s5_lam_re, s5_lam_im, s5_log_dt, s5_b_re, s5_b_im, s5_c_re, s5_c_im, s5_d, s5_w_glu, s5_b_glu,
           pool_w, pool_scale):
    n_b, n_lat, d = x.shape
    depth = w_mod.shape[0]
    rows = n_lat // GRID_W
    assert d == D_MODEL and ctx.shape[1] == CTX_LEN and n_lat % TILE == 0 and n_b < 8 and rows >= NA_KR
    c_rows = jnp.zeros((8, d), F32).at[0:n_b].set(c).at[n_b].set(c_ctx)
    mod_all = _modulation(c_rows, w_mod, b_mod)
    rope_tabs = _rope_tables(n_lat)
    w_pad = _pad_w_in(w_in)
    wg, bg = _gate_weights(gla_w_gate, gla_b_gate)
    g_pre3 = g_pre.astype(F32).reshape(depth, 1, d)
    g_post3 = g_post.astype(F32).reshape(depth, 1, d)
    gn = jnp.tile(gla_g_norm.astype(F32), (1, GLA_HEADS)).reshape(depth, 1, GLA_HEADS * GLA_DV)
    rel_tab = jax.vmap(_na_rel_table)(na_rpb)
    ktoe, bexp, cexp, pw16 = jax.vmap(_s5_tables)(s5_lam_re, s5_lam_im, s5_log_dt, s5_b_re, s5_b_im,
                                                  s5_c_re, s5_c_im, s5_d)
    w_out_b = w_out.astype(BF16)
    w_glu_b = s5_w_glu.astype(BF16)
    b_glu3 = s5_b_glu.astype(F32).reshape(depth, 1, GROUP_W)
    pool_scale3 = pool_scale.astype(F32).reshape(depth, 1, GROUP_W)
    xc = ctx
    for l in range(depth):
        last = l == depth - 1
        qk, gv, lg, nq, nk, nv, su0, su1, pu, gate_cols = _inproj(x, xc, l, mod_all, g_pre3, w_pad, wg, bg,
                                                                  rope_tabs)
        y_gla = _gla(qk, gv, lg, l, gn)
        y_na = _na(nq, nk, nv, l, rel_tab)
        y_s50, y_s51 = _s5(su0, su1, l, ktoe, bexp, cexp, pw16)
        y_pool = _pool(pu, l, pool_w, pool_scale3)
        x, xc = _outproj((y_gla, y_na, y_s50, y_s51, y_pool), gate_cols, x, xc, l, mod_all, w_out_b, g_post3,
                         w_glu_b, b_glu3, with_ctx=not last)
    return x
```

```python
import functools
import math

import numpy as np
import jax
import jax.numpy as jnp
from jax import lax
from jax.experimental import pallas as pl
from jax.experimental.pallas import tpu as pltpu

F32 = jnp.float32
BF16 = jnp.bfloat16

D_MODEL = 1024
GRID_W = 64
CTX_LEN = 256
GROUP_W = 256
MIX_W = 1024
NORM_EPS = 1e-6
GLA_HEADS = 4
GLA_DV = 64
GLA_DK = 32
GLA_RANK = 16
GLA_TAU = 16.0
ROPE_BASE = 10000.0
NA_HEADS = 4
NA_DH = 64
NA_KR = 8
NA_KC = 16
S5_HG = 16
S5_G = 16
S5_P = 64
POOL_WINDOWS = (2, 4, 8, 16)
POOL_GW = 64

TILE = 256
GLA_C = 128
GLA_STEP = 2
S5_T = 16
NA_ROWS = TILE // GRID_W
NA_REL_W = 1024
NEG_BIG = -1e30

_COL = {}
_off = 0
for _name, _w in (("gk", 128), ("gv", 256), ("gg", 128), ("nk", 256), ("nv", 256), ("su", 256),
                  ("gq", 128), ("nq", 256), ("pu", 256), ("gate", 1024)):
    _COL[_name] = (_off, _off + _w)
    _off += _w
N_PAD_COLS = _off
N_GATE_END = 128 + 256 + 2 * GLA_RANK

_VMEM_LIMIT = 56 * 1024 * 1024


def _cp(sem, vmem=_VMEM_LIMIT):
    return pltpu.CompilerParams(dimension_semantics=sem, vmem_limit_bytes=vmem)


def _dot(a, b):
    return jnp.dot(a, b, preferred_element_type=F32)


def _dot_nt(a, b):
    return lax.dot_general(a, b, (((1,), (1,)), ((), ())), preferred_element_type=F32)


def _split_bf16(a):
    hi = a.astype(BF16)
    lo = (a - hi.astype(F32)).astype(BF16)
    return hi, lo


def _mod_kernel(s_ref, w_ref, b_ref, o_ref):
    s = s_ref[...]
    a = (s * jax.nn.sigmoid(s)).astype(BF16)
    o_ref[0] = _dot(a, w_ref[0].astype(BF16)) + b_ref[0]


def _modulation(c_rows, w_mod, b_mod):
    n_l, d, d3 = w_mod.shape
    return pl.pallas_call(
        _mod_kernel,
        out_shape=jax.ShapeDtypeStruct((n_l, 8, d3), F32),
        grid=(n_l, d3 // d),
        in_specs=[pl.BlockSpec((8, d), lambda l, j: (0, 0)),
                  pl.BlockSpec((1, d, d), lambda l, j: (l, 0, j)),
                  pl.BlockSpec((1, 1, d), lambda l, j: (l, 0, j))],
        out_specs=pl.BlockSpec((1, 8, d), lambda l, j: (l, 0, j)),
        compiler_params=_cp(("parallel", "parallel")),
        name="adaln_modulation",
    )(c_rows, w_mod, b_mod.reshape(n_l, 1, d3))


def _inproj_body(xt, mod_row, with_rope, mod_ref, gpre_ref, w_ref, wg_ref, bg_ref, cos_ref, sa_ref, sb_ref,
                 qk_ref, v_ref, lg_ref, nq_ref, nk_ref, nv_ref, su0_ref, su1_ref, pu_ref, gate_ref):
    m = mod_ref[pl.ds(mod_row, 1), :]
    shift = m[:, 0:D_MODEL]
    scale = m[:, D_MODEL:2 * D_MODEL]
    ms = jnp.mean(xt * xt, axis=-1, keepdims=True)
    h = xt * lax.rsqrt(ms + NORM_EPS) * gpre_ref[...] * (1.0 + scale) + shift
    hb = h.astype(BF16)

    def proj(name):
        lo, hi = _COL[name]
        return _dot(hb, w_ref[:, lo:hi])

    def rope(t):
        if not with_rope:
            return t
        return t * cos_ref[...] + pltpu.roll(t, 128 - 8, 1) * sa_ref[...] + pltpu.roll(t, 8, 1) * sb_ref[...]

    qk_ref[0, :, 0:128] = rope(proj("gq")) * (GLA_DK ** -0.5)
    qk_ref[0, :, 128:256] = rope(proj("gk"))
    v_ref[0] = proj("gv")
    z = _dot(proj("gg").astype(BF16), wg_ref[...]) + bg_ref[...]
    lg_ref[0] = (jnp.minimum(z, 0.0) - jnp.log(1.0 + jnp.exp(-jnp.abs(z)))) * (1.0 / GLA_TAU)
    nq_ref[0] = (proj("nq") * (NA_DH ** -0.5)).astype(BF16)
    nk_ref[0] = proj("nk").T.astype(BF16)
    nv_ref[0] = proj("nv").astype(BF16)
    su = proj("su")
    su0_ref[0] = su[:, 0:128]
    su1_ref[0] = su[:, 128:256]
    pu_ref[0] = proj("pu")
    gate_ref[0] = proj("gate").astype(BF16)


def _inproj_kernel(x_ref, xc_ref, *refs, n_batch):
    b = pl.program_id(0)
    j = pl.program_id(1)
    pl.when(j == 0)(lambda: _inproj_body(xc_ref[0], n_batch, False, *refs))
    pl.when(j > 0)(lambda: _inproj_body(x_ref[0], b, True, *refs))


def _layer_spec(shape, layer):
    return pl.BlockSpec((None,) + tuple(shape), lambda *_: (layer,) + (0,) * len(shape))


def _lat_map(b, j):
    return (b, jnp.maximum(j - 1, 0), 0)


def _ctx_map(b, j):
    return (b, 0, 0)


def _inproj_specs(n_b, t_all, d, layer):
    tab_map = lambda b, j: (jnp.maximum(j - 1, 0), 0)
    outs = (("qk", 256, F32), ("gv", 256, F32), ("lg", 256, F32), ("nq", 256, BF16), ("nkT", 256, BF16),
            ("nv", 256, BF16), ("su0", 128, F32), ("su1", 128, F32), ("pu", 256, F32), ("gate", 1024, BF16))
    out_shape = [jax.ShapeDtypeStruct((n_b, w, t_all) if n == "nkT" else (n_b, t_all, w), dt) for n, w, dt in outs]
    out_specs = [pl.BlockSpec((1, w, TILE), lambda b, j: (b, 0, j)) if n == "nkT"
                 else pl.BlockSpec((1, TILE, w), lambda b, j: (b, j, 0)) for n, w, dt in outs]
    in_specs = [_layer_spec((8, 3 * d), layer),
                _layer_spec((1, d), layer),
                _layer_spec((d, N_PAD_COLS), layer),
                _layer_spec((128, 256), layer),
                _layer_spec((1, 256), layer),
                pl.BlockSpec((TILE, 128), tab_map),
                pl.BlockSpec((TILE, 128), tab_map),
                pl.BlockSpec((TILE, 128), tab_map)]
    return in_specs, out_shape, out_specs


def _inproj(x, xc, layer, mod_all, g_pre, w_pad, wg, bg, rope_tabs):
    n_b, n_lat, d = x.shape
    t_all = n_lat + CTX_LEN
    in_specs, out_shape, out_specs = _inproj_specs(n_b, t_all, d, layer)
    return pl.pallas_call(
        functools.partial(_inproj_kernel, n_batch=n_b),
        out_shape=out_shape,
        grid=(n_b, t_all // TILE),
        in_specs=[pl.BlockSpec((1, TILE, d), _lat_map), pl.BlockSpec((1, CTX_LEN, d), _ctx_map)] + in_specs,
        out_specs=out_specs,
        compiler_params=_cp(("parallel", "arbitrary")),
        name="modnorm_inproj",
    )(x, xc, mod_all, g_pre, w_pad, wg, bg, *rope_tabs)


def _gla_kernel(qk_ref, v_ref, lg_ref, gn_ref, y_ref, stf_ref, stb_ref, of_ref, ob_ref, *, n_chunks, n_ctx_chunks):
    c_len = GLA_C
    row = lax.broadcasted_iota(jnp.int32, (c_len, c_len), 0)
    col = lax.broadcasted_iota(jnp.int32, (c_len, c_len), 1)
    tri_f = (col <= row).astype(BF16)
    tri_b = (col >= row).astype(BF16)
    arow = lax.broadcasted_iota(jnp.int32, (c_len, 4 * c_len), 0)
    acol = lax.broadcasted_iota(jnp.int32, (c_len, 4 * c_len), 1) % c_len
    amask_f = acol <= arow
    amask_b = acol >= arow
    head_k = lax.broadcasted_iota(jnp.int32, (1, 128), 1) // GLA_DK
    head_v = lax.broadcasted_iota(jnp.int32, (1, 256), 1) // GLA_DV
    bd_mask = (lax.broadcasted_iota(jnp.int32, (256, 128), 0) // GLA_DV
               == lax.broadcasted_iota(jnp.int32, (256, 128), 1) // GLA_DK)
    ones_bd = jnp.where(lax.broadcasted_iota(jnp.int32, (256, 256), 0) // GLA_DV
                        == lax.broadcasted_iota(jnp.int32, (256, 256), 1) // GLA_DV,
                        1.0 / GLA_DV, 0.0).astype(BF16)
    mid = c_len // 2

    def scan_body(i, carry):
        def bwd_chunk(pos):
            return jnp.where(pos < n_ctx_chunks, n_ctx_chunks - 1 - pos, n_chunks - 1 - (pos - n_ctx_chunks))

        dirs = [d for d in (0, 1) for _ in range(GLA_STEP)]
        chunks = [i * GLA_STEP + u for u in range(GLA_STEP)] + [bwd_chunk(i * GLA_STEP + u) for u in range(GLA_STEP)]
        jobs = range(len(dirs))
        r0 = [pl.multiple_of(c * c_len, c_len) for c in chunks]
        st_refs = (stf_ref, stb_ref)
        tri = (tri_f, tri_b)
        amask = (amask_f, amask_b)
        end = (c_len - 1, 0)
        q = [qk_ref[0, pl.ds(r0[n], c_len), 0:128] for n in jobs]
        k = [qk_ref[0, pl.ds(r0[n], c_len), 128:256] for n in jobs]
        v = [v_ref[0, pl.ds(r0[n], c_len), :] for n in jobs]
        lg = [_split_bf16(lg_ref[0, pl.ds(r0[n], c_len), 128 * dirs[n]:128 * (dirs[n] + 1)]) for n in jobs]
        cum = [_dot(tri[dirs[n]], lg[n][0]) + _dot(tri[dirs[n]], lg[n][1]) for n in jobs]
        c_ref = [cum[n][mid:mid + 1, :] for n in jobs]
        c_end = [cum[n][end[dirs[n]]:end[dirs[n]] + 1, :] for n in jobs]
        qt = [q[n] * jnp.exp(cum[n] - c_ref[n]) for n in jobs]
        kt = [k[n] * jnp.exp(c_ref[n] - cum[n]) for n in jobs]
        ks = [jnp.concatenate([jnp.where(head_k == hh, kt[n], 0.0) for hh in range(GLA_HEADS)],
                              axis=0).astype(BF16) for n in jobs]
        att = [_dot_nt(qt[n].astype(BF16), ks[n]) for n in jobs]
        qh = [(qt[n] * jnp.exp(c_ref[n])).astype(BF16) for n in jobs]
        kh = [(kt[n] * jnp.exp(c_end[n] - c_ref[n])).astype(BF16) for n in jobs]
        upd = [_dot(v[n].T.astype(BF16), kh[n]) for n in jobs]
        att = [jnp.where(amask[dirs[n]], att[n], 0.0).astype(BF16) for n in jobs]
        vb = [v[n].astype(BF16) for n in jobs]
        vs = [jnp.concatenate([jnp.where(head_v == hh, vb[n], jnp.zeros_like(vb[n])) for hh in range(GLA_HEADS)],
                              axis=0) for n in jobs]
        o_intra = [_dot(att[n], vs[n]) for n in jobs]
        outs = (of_ref, ob_ref)
        st = [st_refs[d][...] for d in (0, 1)]
        for n in jobs:
            d = dirs[n]
            outs[d][pl.ds(r0[n], c_len), :] = o_intra[n] + _dot_nt(qh[n], st[d].astype(BF16))
            st[d] = st[d] * jnp.exp(c_end[n]) + jnp.where(bd_mask, upd[n], 0.0)
        for d in (0, 1):
            st_refs[d][...] = st[d]
        return carry

    def norm_body(c, carry):
        half = TILE // 2
        r0 = [pl.multiple_of(c * TILE + h * half, half) for h in (0, 1)]
        o = [of_ref[pl.ds(r, half), :] + ob_ref[pl.ds(r, half), :] for r in r0]
        sq = [_split_bf16(x * x) for x in o]
        ms = [_dot(hi, ones_bd) + _dot(lo, ones_bd) for hi, lo in sq]
        for r, x, m in zip(r0, o, ms):
            y_ref[0, pl.ds(r, half), :] = (x * lax.rsqrt(m + NORM_EPS) * gn_ref[...]).astype(y_ref.dtype)
        return carry

    stf_ref[...] = jnp.zeros_like(stf_ref)
    stb_ref[...] = jnp.zeros_like(stb_ref)
    lax.fori_loop(0, n_chunks // GLA_STEP, scan_body, 0)
    lax.fori_loop(0, n_chunks * c_len // TILE, norm_body, 0)


def _gla(qk, v, lg, layer, g_norm):
    n_b, t_all, _ = qk.shape
    blk = lambda w: pl.BlockSpec((1, t_all, w), lambda b: (b, 0, 0))
    return pl.pallas_call(
        functools.partial(_gla_kernel, n_chunks=t_all // GLA_C, n_ctx_chunks=CTX_LEN // GLA_C),
        out_shape=jax.ShapeDtypeStruct((n_b, t_all, 256), BF16),
        grid=(n_b,),
        in_specs=[blk(256), blk(256), blk(256), _layer_spec((1, 256), layer)],
        out_specs=blk(256),
        scratch_shapes=[pltpu.VMEM((256, 128), F32), pltpu.VMEM((256, 128), F32),
                        pltpu.VMEM((t_all, 256), F32), pltpu.VMEM((t_all, 256), F32)],
        compiler_params=_cp(("parallel",)),
        name="gla_mixer",
    )(qk, v, lg, g_norm)


def _na_kernel(q_ref, kt0_ref, kt1_ref, kt2_ref, v0_ref, v1_ref, v2_ref, ktc_ref, vc_ref, rel_ref, o_ref, *, rows):
    j = pl.program_id(1)
    left = lax.broadcasted_iota(jnp.int32, (1, 128), 1) < NA_DH
    ones_blk = jnp.ones((TILE, 128), BF16)
    ro0 = NA_KR - 1 - NA_ROWS

    def attend(with_window):
        key_tiles = [(ktc_ref, vc_ref, None)]
        if with_window:
            g = j - 1
            q_row = lax.broadcasted_iota(jnp.int32, (TILE, TILE), 0) // GRID_W
            k_row = lax.broadcasted_iota(jnp.int32, (TILE, TILE), 1) // GRID_W
            first = jnp.clip(NA_ROWS * g + q_row - NA_KR // 2, 0, rows - NA_KR) - NA_ROWS * (g - 1)
            for d, (kr, vr) in enumerate(((kt0_ref, v0_ref), (kt1_ref, v1_ref), (kt2_ref, v2_ref))):
                window = None if d == 1 else jnp.where(
                    (k_row + NA_ROWS * d >= first) & (k_row + NA_ROWS * d < first + NA_KR), 0.0, NEG_BIG)
                key_tiles.append((kr, vr, (d, window)))
        def lanes(hh):
            return slice(128 * (hh // 2), 128 * (hh // 2 + 1))

        def score(hh):
            qp = q_ref[0, :, lanes(hh)]
            zero = jnp.zeros_like(qp)
            qh = jnp.where(left, qp, zero) if hh % 2 == 0 else jnp.where(left, zero, qp)
            kt_all = jnp.concatenate([kr[0, lanes(hh), :] for kr, _, _ in key_tiles], axis=1)
            return _dot(qh, kt_all)

        def probs(hh, s):
            if with_window:
                parts = [s[:, 0:TILE]]
                for _, _, (d, window) in key_tiles[1:]:
                    rel = jnp.concatenate(
                        [rel_ref[hh, (ro0 - i) % 2, :, GRID_W * ((ro0 - i) // 2 * 2) + d * TILE:
                                 GRID_W * ((ro0 - i) // 2 * 2) + (d + 1) * TILE] for i in range(NA_ROWS)], axis=0)
                    part = s[:, (d + 1) * TILE:(d + 2) * TILE] + rel
                    parts.append(part if window is None else part + window)
                s = jnp.concatenate(parts, axis=1)
            return jnp.exp(s - jnp.max(s, axis=-1, keepdims=True)).astype(BF16)

        def attend_values(hh, p):
            vaug = jnp.concatenate([jnp.concatenate([vr[0, :, lanes(hh)], ones_blk], axis=1)
                                    for _, vr, _ in key_tiles], axis=0)
            acc = _dot(p, vaug)
            return acc[:, 0:128] / acc[:, 128:256]

        s_next = score(0)
        outs = []
        for hh in range(NA_HEADS):
            s_cur = s_next
            if hh + 1 < NA_HEADS:
                s_next = score(hh + 1)
            outs.append(attend_values(hh, probs(hh, s_cur)))
        for pair in range(NA_HEADS // 2):
            o_pair = jnp.where(left, outs[2 * pair], outs[2 * pair + 1])
            o_ref[0, :, lanes(2 * pair)] = o_pair.astype(o_ref.dtype)

    @pl.when(j == 0)
    def _():
        attend(False)

    @pl.when(j > 0)
    def _():
        attend(True)


def _na(nq, nkt, nv, layer, rel_tab):
    n_b, t_all, _ = nq.shape
    n_tiles = t_all // TILE
    rows = (t_all - CTX_LEN) // GRID_W

    def nbr(delta):
        return lambda b, j: (b, jnp.clip(j + delta, 1, n_tiles - 1), 0)

    def nbr_t(delta):
        return lambda b, j: (b, 0, jnp.clip(j + delta, 1, n_tiles - 1))

    tile = lambda f: pl.BlockSpec((1, TILE, 256), f)
    tile_t = lambda f: pl.BlockSpec((1, 256, TILE), f)
    return pl.pallas_call(
        functools.partial(_na_kernel, rows=rows),
        out_shape=jax.ShapeDtypeStruct((n_b, t_all, 256), BF16),
        grid=(n_b, n_tiles),
        in_specs=[tile(lambda b, j: (b, j, 0)),
                  tile_t(nbr_t(-1)), tile_t(nbr_t(0)), tile_t(nbr_t(1)),
                  tile(nbr(-1)), tile(nbr(0)), tile(nbr(1)),
                  tile_t(lambda b, j: (b, 0, 0)), tile(lambda b, j: (b, 0, 0)),
                  _layer_spec((NA_HEADS, 2, GRID_W, NA_REL_W), layer)],
        out_specs=tile(lambda b, j: (b, j, 0)),
        compiler_params=_cp(("parallel", "arbitrary")),
        name="na_mixer",
    )(nq, nkt, nkt, nkt, nv, nv, nv, nkt, nv, rel_tab)


def _na_rel_table(rpb):
    col = np.arange(GRID_W)
    cs = np.clip(col - NA_KC // 2, 0, GRID_W - NA_KC)
    col_mask = (col[None, :] >= cs[:, None]) & (col[None, :] < cs[:, None] + NA_KC)
    co = np.clip(col[None, :] - col[:, None], -(NA_KC - 1), NA_KC - 1) + (NA_KC - 1)
    onehot = (co[None] == np.arange(2 * NA_KC - 1)[:, None, None]).astype(np.float32)
    toe = jnp.einsum("hrc,cqk->hqrk", rpb.astype(F32), jnp.asarray(onehot),
                     precision=lax.Precision.HIGHEST)
    toe = jnp.where(jnp.asarray(col_mask)[None, :, None, :], toe, NEG_BIG)
    flat = toe.reshape(NA_HEADS, GRID_W, (2 * NA_KR - 1) * GRID_W)
    flat = jnp.pad(flat, ((0, 0), (0, 0), (0, NA_REL_W + GRID_W - flat.shape[-1])), constant_values=NEG_BIG)
    return jnp.stack([flat[:, :, :NA_REL_W], flat[:, :, GRID_W:]], axis=1)


def _s5_kernel(su0_ref, su1_ref, ktoe_ref, bexp_ref, cexp_ref, pw_ref, y0_ref, y1_ref, a_ref, e_ref, x_ref,
               *, n_chunks, n_ctx_chunks):
    blk = 8
    lane_blk = lax.broadcasted_iota(jnp.int32, (blk, 128), 1) // S5_HG
    masks = [lane_blk == m for m in range(8)]
    gp = S5_G * S5_P

    def pick(vals):
        acc = vals[0]
        for k in range(1, 8):
            acc = jnp.where(masks[k], vals[k], acc)
        return acc

    def block_transpose(src):
        rolled = []
        for d in range(8):
            diag = pick([src[(k + d) % 8] for k in range(8)])
            rolled.append(pltpu.roll(diag, S5_HG * d, 1) if d else diag)
        return [pick([rolled[(m - a) % 8] for m in range(8)]) for a in range(8)]

    def regroup_in(rb, carry):
        c0 = pl.multiple_of(rb * blk, blk)
        for hg, ref in enumerate((su0_ref, su1_ref)):
            for o in range(2):
                src = [ref[0, pl.ds(rb * (blk * S5_T) + o * 8 + m, blk, stride=S5_T), :] for m in range(8)]
                for a, val in enumerate(block_transpose(src)):
                    a_ref[hg * 8 + a, pl.ds(c0, blk), o * 128:(o + 1) * 128] = val
        return carry

    lax.fori_loop(0, n_chunks // blk, regroup_in, 0)

    for g in range(S5_G):
        e = _dot(a_ref[g].astype(BF16), bexp_ref[g])
        for kind in range(4):
            e_ref[kind, :, g * S5_P:(g + 1) * S5_P] = e[:, kind * S5_P:(kind + 1) * S5_P]

    afr, afi, abr, abi = pw_ref[0:1, :], pw_ref[1:2, :], pw_ref[2:3, :], pw_ref[3:4, :]

    def scan_body(i, carry):
        fr, fi, br, bi = carry
        cf = i
        cb = jnp.where(i < n_ctx_chunks, n_ctx_chunks - 1 - i, n_chunks - 1 - (i - n_ctx_chunks))
        x_ref[0, pl.ds(cf, 1), :] = fr
        x_ref[1, pl.ds(cf, 1), :] = fi
        x_ref[2, pl.ds(cb, 1), :] = br
        x_ref[3, pl.ds(cb, 1), :] = bi
        er = e_ref[0, pl.ds(cf, 1), :]
        ei = e_ref[1, pl.ds(cf, 1), :]
        gr = e_ref[2, pl.ds(cb, 1), :]
        gi = e_ref[3, pl.ds(cb, 1), :]
        return (afr * fr - afi * fi + er, afr * fi + afi * fr + ei,
                abr * br - abi * bi + gr, abr * bi + abi * br + gi)

    z = jnp.zeros((1, gp), F32)
    lax.fori_loop(0, n_chunks, scan_body, (z, z, z, z))

    for g in range(S5_G):
        y = _dot(a_ref[g].astype(BF16), ktoe_ref[g])
        for kind in range(4):
            y = y + _dot(x_ref[kind, :, g * S5_P:(g + 1) * S5_P].astype(BF16), cexp_ref[g, kind])
        a_ref[g] = y

    def regroup_out(rb, carry):
        c0 = pl.multiple_of(rb * blk, blk)
        for og, ref in enumerate((y0_ref, y1_ref)):
            for ht in range(2):
                src = [a_ref[og * 8 + m, pl.ds(c0, blk), ht * 128:(ht + 1) * 128] for m in range(8)]
                for a, val in enumerate(block_transpose(src)):
                    ref[0, pl.ds(rb * (blk * S5_T) + ht * 8 + a, blk, stride=S5_T), :] = val
        return carry

    lax.fori_loop(0, n_chunks // blk, regroup_out, 0)


def _s5(su0, su1, layer, ktoe, bexp, cexp, pw16):
    n_b, t_all, _ = su0.shape
    n_chunks = t_all // S5_T
    gp = S5_G * S5_P
    blk = pl.BlockSpec((1, t_all, 128), lambda b: (b, 0, 0))
    return pl.pallas_call(
        functools.partial(_s5_kernel, n_chunks=n_chunks, n_ctx_chunks=CTX_LEN // S5_T),
        out_shape=[jax.ShapeDtypeStruct((n_b, t_all, 128), F32)] * 2,
        grid=(n_b,),
        in_specs=[blk, blk, _layer_spec((S5_G, 256, 256), layer), _layer_spec((S5_G, 256, 256), layer),
                  _layer_spec((S5_G, 4, S5_P, 256), layer), _layer_spec((4, gp), layer)],
        out_specs=[blk, blk],
        scratch_shapes=[pltpu.VMEM((S5_G, n_chunks, S5_T * S5_HG), F32),
                        pltpu.VMEM((4, n_chunks, gp), F32),
                        pltpu.VMEM((4, n_chunks, gp), F32)],
        compiler_params=_cp(("parallel",)),
        name="s5_mixer",
    )(su0, su1, ktoe, bexp, cexp, pw16)


def _s5_tables(lam_re, lam_im, log_dt, b_re, b_im, c_re, c_im, d_skip):
    hp = lax.Precision.HIGH
    lam_re, lam_im, log_dt = lam_re.astype(F32), lam_im.astype(F32), log_dt.astype(F32)
    b_re, b_im, c_re, c_im = b_re.astype(F32), b_im.astype(F32), c_re.astype(F32), c_im.astype(F32)
    dt = jnp.exp(log_dt)[..., None]
    n = jnp.arange(S5_T + 1, dtype=F32)[:, None, None, None]
    pw_re = jnp.exp(n * lam_re * dt) * jnp.cos(n * lam_im * dt)
    pw_im = jnp.exp(n * lam_re * dt) * jnp.sin(n * lam_im * dt)
    neg_re = jnp.exp(-n * lam_re * dt) * jnp.cos(n * lam_im * dt)
    neg_im = -jnp.exp(-n * lam_re * dt) * jnp.sin(n * lam_im * dt)
    m = jnp.arange(S5_T, -1, -1, dtype=F32)[:, None, None, None]
    dsc_re = jnp.exp(m * lam_re * dt) * jnp.cos(m * lam_im * dt)
    dsc_im = jnp.exp(m * lam_re * dt) * jnp.sin(m * lam_im * dt)
    lb_re, lb_im = pw_re[1], pw_im[1]
    num_re, num_im = lb_re - 1.0, lb_im
    den = lam_re * lam_re + lam_im * lam_im
    coef_re = ((num_re * lam_re + num_im * lam_im) / den)[..., None]
    coef_im = ((num_im * lam_re - num_re * lam_im) / den)[..., None]
    bb_re = coef_re * b_re - coef_im * b_im
    bb_im = coef_re * b_im + coef_im * b_re

    def cmul_b(p_re, p_im, d):
        re = p_re[..., None] * bb_re[d][None] - p_im[..., None] * bb_im[d][None]
        im = p_re[..., None] * bb_im[d][None] + p_im[..., None] * bb_re[d][None]
        to = lambda z: z.transpose(1, 0, 3, 2).reshape(S5_G, S5_T * S5_HG, S5_P)
        return to(re), to(im)

    def cmul_c(p_re, p_im, d):
        re = c_re[d][None] * p_re[:, :, None, :] - c_im[d][None] * p_im[:, :, None, :]
        im = c_re[d][None] * p_im[:, :, None, :] + c_im[d][None] * p_re[:, :, None, :]
        to = lambda z: z.transpose(1, 3, 0, 2).reshape(S5_G, S5_P, S5_T * S5_HG)
        return to(re), to(-im)

    def lag_operator(u, v):
        return jnp.einsum("gap,gpb->gab", jnp.concatenate(u, axis=-1), jnp.concatenate(v, axis=1), precision=hp)

    first = slice(0, S5_T)
    kf = lag_operator(cmul_b(neg_re[first, 0], neg_im[first, 0], 0), cmul_c(pw_re[first, 0], pw_im[first, 0], 0))
    kb = lag_operator(cmul_b(pw_re[first, 1], pw_im[first, 1], 1), cmul_c(neg_re[first, 1], neg_im[first, 1], 1))
    tok = np.arange(S5_T * S5_HG) // S5_HG
    skip = jnp.asarray(np.eye(S5_T * S5_HG, dtype=np.float32)) * jnp.tile(
        d_skip.astype(F32).reshape(S5_G, S5_HG), (1, S5_T))[:, None, :]
    ktoe = (jnp.where(jnp.asarray(tok[None, :] >= tok[:, None]), kf, 0.0)
            + jnp.where(jnp.asarray(tok[:, None] >= tok[None, :]), kb, 0.0) + skip)
    last = slice(1, S5_T + 1)
    bexp = jnp.concatenate(cmul_b(dsc_re[last, 0], dsc_im[last, 0], 0)
                           + cmul_b(pw_re[first, 1], pw_im[first, 1], 1), axis=-1)
    cexp = jnp.stack(cmul_c(pw_re[last, 0], pw_im[last, 0], 0)
                     + cmul_c(dsc_re[first, 1], dsc_im[first, 1], 1), axis=1)
    pw16 = jnp.stack([pw_re[S5_T, 0].reshape(-1), pw_im[S5_T, 0].reshape(-1),
                      pw_re[S5_T, 1].reshape(-1), pw_im[S5_T, 1].reshape(-1)], axis=0)
    return ktoe.astype(BF16), bexp.astype(BF16), cexp.astype(BF16), pw16


def _pool_kernel(u_ref, w_ref, scale_ref, o_ref, pad_ref, *, n_lat):
    pad = 8
    seg = ((pad, 0, CTX_LEN), (pad + CTX_LEN + 2 * pad, CTX_LEN, n_lat))
    pad_ref[...] = jnp.zeros_like(pad_ref)
    for p0, u0, n in seg:
        pad_ref[p0:p0 + n, :] = u_ref[0, u0:u0 + n, :]
    for p0, u0, n in seg:
        for t0 in range(0, n, TILE):
            t = t0 + lax.broadcasted_iota(jnp.int32, (TILE, 1), 0)
            outs = []
            for i, w in enumerate(POOL_WINDOWS):
                lo, hi = i * POOL_GW, (i + 1) * POOL_GW
                acc = None
                for jj in range(w):
                    s = p0 + t0 + jj - w // 2
                    piece = pad_ref[s:s + TILE, lo:hi]
                    acc = piece if acc is None else acc + piece
                cnt = (jnp.clip(t - w // 2 + w, 0, n) - jnp.clip(t - w // 2, 0, n)).astype(F32)
                g = u_ref[0, u0 + t0:u0 + t0 + TILE, lo:hi]
                outs.append(_dot((acc / cnt - g).astype(BF16), w_ref[i].astype(BF16)))
            o_ref[0, u0 + t0:u0 + t0 + TILE, :] = (jnp.concatenate(outs, axis=-1) * scale_ref[...]).astype(o_ref.dtype)


def _pool(pu, layer, w_pool, scale):
    n_b, t_all, _ = pu.shape
    blk = pl.BlockSpec((1, t_all, 256), lambda b: (b, 0, 0))
    return pl.pallas_call(
        functools.partial(_pool_kernel, n_lat=t_all - CTX_LEN),
        out_shape=jax.ShapeDtypeStruct((n_b, t_all, 256), BF16),
        grid=(n_b,),
        in_specs=[blk, _layer_spec((len(POOL_WINDOWS), POOL_GW, POOL_GW), layer), _layer_spec((1, 256), layer)],
        out_specs=blk,
        scratch_shapes=[pltpu.VMEM((t_all + 32, 256), F32)],
        compiler_params=_cp(("parallel",)),
        name="pool_mixer",
    )(pu, w_pool, scale)


def _gelu_tanh(x):
    return 0.5 * x * (1.0 + jnp.tanh(math.sqrt(2.0 / math.pi) * (x + 0.044715 * (x * x * x))))


def _outproj_res(res, mod_row, ygla_ref, yna_ref, ys50_ref, ys51_ref, ypool_ref, gate_ref, mod_ref, wout_ref,
                 gpost_ref, wglu_ref, bglu_ref):
    def gated_proj(part, i):
        lo, hi = i * GROUP_W, (i + 1) * GROUP_W
        gc = gate_ref[0, :, lo:hi].astype(F32)
        return _dot((part * (gc * jax.nn.sigmoid(gc))).astype(BF16), wout_ref[lo:hi, :])

    gate_m = mod_ref[pl.ds(mod_row, 1), :][:, 2 * D_MODEL:3 * D_MODEL]
    g5 = _gelu_tanh(jnp.concatenate([ys50_ref[0], ys51_ref[0]], axis=-1))
    z5 = _dot(g5.astype(BF16), wglu_ref[...])
    acc = gated_proj(ygla_ref[0], 0) + gated_proj(yna_ref[0], 1) + gated_proj(ypool_ref[0], 3)
    acc = acc + gated_proj(g5 * jax.nn.sigmoid(z5 + bglu_ref[...]), 2)
    ms = jnp.mean(acc * acc, axis=-1, keepdims=True)
    return res + gate_m * (acc * lax.rsqrt(ms + NORM_EPS) * gpost_ref[...])


N_MIX_REFS = 6
N_OUT_PARAM_REFS = 5
N_IN_PARAM_REFS = 8


def _outproj_kernel(*refs):
    mix, x_ref, params, out_ref = refs[:N_MIX_REFS], refs[N_MIX_REFS], refs[N_MIX_REFS + 1:-1], refs[-1]
    out_ref[0] = _outproj_res(x_ref[0], pl.program_id(0), *mix, *params)


def _outproj_inproj_kernel(*refs, n_batch):
    mix = refs[:N_MIX_REFS]
    x_ref, xc_ref = refs[N_MIX_REFS:N_MIX_REFS + 2]
    out_params = refs[N_MIX_REFS + 2:N_MIX_REFS + 2 + N_OUT_PARAM_REFS]
    rest = refs[N_MIX_REFS + 2 + N_OUT_PARAM_REFS:]
    in_params, (xo_ref, xco_ref), in_outs = rest[:N_IN_PARAM_REFS], rest[N_IN_PARAM_REFS:N_IN_PARAM_REFS + 2], \
        rest[N_IN_PARAM_REFS + 2:]
    b = pl.program_id(0)
    j = pl.program_id(1)

    def body(res_ref, out_ref, mod_row, with_rope):
        res = _outproj_res(res_ref[0], mod_row, *mix, *out_params)
        out_ref[0] = res
        _inproj_body(res, mod_row, with_rope, *in_params, *in_outs)

    pl.when(j == 0)(lambda: body(xc_ref, xco_ref, n_batch, False))
    pl.when(j > 0)(lambda: body(x_ref, xo_ref, b, True))


def _outproj_param_specs(d, layer):
    return [_layer_spec((8, 3 * d), layer), _layer_spec((MIX_W, d), layer), _layer_spec((1, d), layer),
            _layer_spec((GROUP_W, GROUP_W), layer), _layer_spec((1, GROUP_W), layer)]


def _mix_specs(index_map):
    return [pl.BlockSpec((1, TILE, w), index_map) for w in (256, 256, 128, 128, 256, MIX_W)]


def _outproj(mix, x, layer, mod_all, w_out, g_post, w_glu, b_glu):
    n_b, n_lat, d = x.shape
    return pl.pallas_call(
        _outproj_kernel,
        out_shape=jax.ShapeDtypeStruct(x.shape, F32),
        grid=(n_b, n_lat // TILE),
        in_specs=_mix_specs(lambda b, j: (b, j + 1, 0)) + [pl.BlockSpec((1, TILE, d), lambda b, j: (b, j, 0))]
                 + _outproj_param_specs(d, layer),
        out_specs=pl.BlockSpec((1, TILE, d), lambda b, j: (b, j, 0)),
        compiler_params=_cp(("parallel", "arbitrary")),
        name="gate_outproj_residual",
    )(*mix, x, mod_all, w_out, g_post, w_glu, b_glu)


def _outproj_inproj(mix, x, xc, layer, mod_all, w_out, g_post, w_glu, b_glu, g_pre, w_pad, wg, bg, rope_tabs):
    n_b, n_lat, d = x.shape
    t_all = n_lat + CTX_LEN
    in_specs, in_shape, in_out_specs = _inproj_specs(n_b, t_all, d, layer + 1)
    outs = pl.pallas_call(
        functools.partial(_outproj_inproj_kernel, n_batch=n_b),
        out_shape=[jax.ShapeDtypeStruct(x.shape, F32), jax.ShapeDtypeStruct(xc.shape, F32)] + in_shape,
        grid=(n_b, t_all // TILE),
        in_specs=_mix_specs(lambda b, j: (b, j, 0))
                 + [pl.BlockSpec((1, TILE, d), _lat_map), pl.BlockSpec((1, CTX_LEN, d), _ctx_map)]
                 + _outproj_param_specs(d, layer) + in_specs,
        out_specs=[pl.BlockSpec((1, TILE, d), _lat_map), pl.BlockSpec((1, CTX_LEN, d), _ctx_map)] + in_out_specs,
        compiler_params=_cp(("parallel", "arbitrary")),
        name="outproj_then_inproj",
    )(*mix, x, xc, mod_all, w_out, g_post, w_glu, b_glu, mod_all, g_pre, w_pad, wg, bg, *rope_tabs)
    return outs[0], outs[1], outs[2:]


def _rope_tables(n_tok):
    t = np.arange(n_tok)
    half = GLA_DK // 2
    freqs = ROPE_BASE ** (-np.arange(0, half, 2, dtype=np.float32) / half)
    d = np.arange(GLA_DK)
    pos = np.where((d // half)[None, :] == 0, (t // GRID_W)[:, None], (t % GRID_W)[:, None]).astype(np.float32)
    ang = jnp.asarray(pos) * jnp.asarray(freqs[(d % half) % (half // 2)])[None, :]
    first = jnp.asarray(((d % half) < half // 2)[None, :])
    cos, sin = jnp.cos(ang), jnp.sin(ang)
    tabs = (cos, jnp.where(first, -sin, 0.0), jnp.where(first, 0.0, sin))
    return tuple(jnp.tile(z, (1, GLA_HEADS)).astype(F32) for z in tabs)


def _pad_w_in_kernel(w_ref, o_ref):
    pad_end = _COL["gg"][1]
    o_ref[:, 0:N_GATE_END] = w_ref[:, 0:N_GATE_END].astype(BF16)
    o_ref[:, N_GATE_END:pad_end] = jnp.zeros((o_ref.shape[0], pad_end - N_GATE_END), BF16)
    o_ref[:, pad_end:N_PAD_COLS] = w_ref[:, N_GATE_END:].astype(BF16)


def _pad_w_in(w_in):
    n_l, d, n_in = w_in.shape
    rows = 128
    return pl.pallas_call(
        _pad_w_in_kernel,
        out_shape=jax.ShapeDtypeStruct((n_l, d, N_PAD_COLS), BF16),
        grid=(n_l, d // rows),
        in_specs=[pl.BlockSpec((None, rows, n_in), lambda l, i: (l, i, 0))],
        out_specs=pl.BlockSpec((None, rows, N_PAD_COLS), lambda l, i: (l, i, 0)),
        compiler_params=_cp(("parallel", "parallel")),
        name="pad_w_in",
    )(w_in)


def _gate_weights(w_gate, b_gate):
    n_l = w_gate.shape[0]
    wg = jnp.zeros((n_l, 128, 256), F32)
    wg = wg.at[:, 0:GLA_RANK, 0:128].set(w_gate[:, 0].astype(F32))
    wg = wg.at[:, GLA_RANK:2 * GLA_RANK, 128:256].set(w_gate[:, 1].astype(F32))
    return wg.astype(BF16), b_gate.astype(F32).reshape(n_l, 1, 256)


def kernel(x, c, ctx, c_ctx, w_mod, b_mod, g_pre, g_post, w_in, w_out, gla_w_gate, gla_b_gate, gla_g_norm, na_rpb,
           s5_lam_re, s5_lam_im, s5_log_dt, s5_b_re, s5_b_im, s5_c_re, s5_c_im, s5_d, s5_w_glu, s5_b_glu,
           pool_w, pool_scale):
    n_b, n_lat, d = x.shape
    depth = w_mod.shape[0]
    rows = n_lat // GRID_W
    assert d == D_MODEL and ctx.shape[1] == CTX_LEN and n_lat % TILE == 0 and n_b < 8 and rows >= NA_KR
    c_rows = jnp.zeros((8, d), F32).at[0:n_b].set(c).at[n_b].set(c_ctx)
    mod_all = _modulation(c_rows, w_mod, b_mod)
    rope_tabs = _rope_tables(n_lat)
    w_pad = _pad_w_in(w_in)
    wg, bg = _gate_weights(gla_w_gate, gla_b_gate)
    g_pre3 = g_pre.astype(F32).reshape(depth, 1, d)
    g_post3 = g_post.astype(F32).reshape(depth, 1, d)
    gn = jnp.tile(gla_g_norm.astype(F32), (1, GLA_HEADS)).reshape(depth, 1, GLA_HEADS * GLA_DV)
    rel_tab = jax.vmap(_na_rel_table)(na_rpb)
    ktoe, bexp, cexp, pw16 = jax.vmap(_s5_tables)(s5_lam_re, s5_lam_im, s5_log_dt, s5_b_re, s5_b_im,
                                                  s5_c_re, s5_c_im, s5_d)
    w_out_b = w_out.astype(BF16)
    w_glu_b = s5_w_glu.astype(BF16)
    b_glu3 = s5_b_glu.astype(F32).reshape(depth, 1, GROUP_W)
    pool_scale3 = pool_scale.astype(F32).reshape(depth, 1, GROUP_W)
    xc = ctx
    proj = _inproj(x, xc, 0, mod_all, g_pre3, w_pad, wg, bg, rope_tabs)
    for l in range(depth):
        qk, gv, lg, nq, nkt, nv, su0, su1, pu, gate_cols = proj
        y_gla = _gla(qk, gv, lg, l, gn)
        y_na = _na(nq, nkt, nv, l, rel_tab)
        y_s50, y_s51 = _s5(su0, su1, l, ktoe, bexp, cexp, pw16)
        y_pool = _pool(pu, l, pool_w, pool_scale3)
        mix = (y_gla, y_na, y_s50, y_s51, y_pool, gate_cols)
        if l == depth - 1:
            x = _outproj(mix, x, l, mod_all, w_out_b, g_post3, w_glu_b, b_glu3)
        else:
            x, xc, proj = _outproj_inproj(mix, x, xc, l, mod_all, w_out_b, g_post3, w_glu_b, b_glu3,
                                          g_pre3, w_pad, wg, bg, rope_tabs)
    return x
```

```python
import functools
import math

import numpy as np
import jax
import jax.numpy as jnp
from jax import lax
from jax.experimental import pallas as pl
from jax.experimental.pallas import tpu as pltpu

F32 = jnp.float32
BF16 = jnp.bfloat16

D_MODEL = 1024
GRID_W = 64
CTX_LEN = 256
GROUP_W = 256
MIX_W = 1024
NORM_EPS = 1e-6
GLA_HEADS = 4
GLA_DV = 64
GLA_DK = 32
GLA_RANK = 16
GLA_TAU = 16.0
ROPE_BASE = 10000.0
NA_HEADS = 4
NA_DH = 64
NA_KR = 8
NA_KC = 16
S5_HG = 16
S5_G = 16
S5_P = 64
POOL_WINDOWS = (2, 4, 8, 16)
POOL_GW = 64

TILE = 256
GLA_C = 128
GLA_STEP = 2
S5_T = 16
NA_ROWS = TILE // GRID_W
NA_REL_W = 1024
NEG_BIG = -1e30

_COL = {}
_off = 0
for _name, _w in (("gk", 128), ("gv", 256), ("gg", 128), ("nk", 256), ("nv", 256), ("su", 256),
                  ("gq", 128), ("nq", 256), ("pu", 256), ("gate", 1024)):
    _COL[_name] = (_off, _off + _w)
    _off += _w
N_PAD_COLS = _off
N_GATE_END = 128 + 256 + 2 * GLA_RANK

_VMEM_LIMIT = 56 * 1024 * 1024


def _cp(sem, vmem=_VMEM_LIMIT):
    return pltpu.CompilerParams(dimension_semantics=sem, vmem_limit_bytes=vmem)


def _dot(a, b):
    return jnp.dot(a, b, preferred_element_type=F32)


def _dot_nt(a, b):
    return lax.dot_general(a, b, (((1,), (1,)), ((), ())), preferred_element_type=F32)


def _split_bf16(a):
    hi = a.astype(BF16)
    lo = (a - hi.astype(F32)).astype(BF16)
    return hi, lo


def _mod_kernel(s_ref, w_ref, b_ref, o_ref):
    s = s_ref[...]
    a = (s * jax.nn.sigmoid(s)).astype(BF16)
    o_ref[0] = _dot(a, w_ref[0].astype(BF16)) + b_ref[0]


def _modulation(c_rows, w_mod, b_mod):
    n_l, d, d3 = w_mod.shape
    return pl.pallas_call(
        _mod_kernel,
        out_shape=jax.ShapeDtypeStruct((n_l, 8, d3), F32),
        grid=(n_l, d3 // d),
        in_specs=[pl.BlockSpec((8, d), lambda l, j: (0, 0)),
                  pl.BlockSpec((1, d, d), lambda l, j: (l, 0, j)),
                  pl.BlockSpec((1, 1, d), lambda l, j: (l, 0, j))],
        out_specs=pl.BlockSpec((1, 8, d), lambda l, j: (l, 0, j)),
        compiler_params=_cp(("parallel", "parallel")),
        name="adaln_modulation",
    )(c_rows, w_mod, b_mod.reshape(n_l, 1, d3))


def _inproj_body(xt, mod_row, with_rope, mod_ref, gpre_ref, w_ref, wg_ref, bg_ref, cos_ref, sa_ref, sb_ref,
                 qk_ref, v_ref, lg_ref, nq_ref, nk_ref, nv_ref, su0_ref, su1_ref, pu_ref, gate_ref):
    m = mod_ref[pl.ds(mod_row, 1), :]
    shift = m[:, 0:D_MODEL]
    scale = m[:, D_MODEL:2 * D_MODEL]
    ms = jnp.mean(xt * xt, axis=-1, keepdims=True)
    h = xt * lax.rsqrt(ms + NORM_EPS) * gpre_ref[...] * (1.0 + scale) + shift
    hb = h.astype(BF16)

    def proj(name):
        lo, hi = _COL[name]
        return _dot(hb, w_ref[:, lo:hi])

    def rope(t):
        if not with_rope:
            return t
        return t * cos_ref[...] + pltpu.roll(t, 128 - 8, 1) * sa_ref[...] + pltpu.roll(t, 8, 1) * sb_ref[...]

    qk_ref[0, :, 0:128] = rope(proj("gq")) * (GLA_DK ** -0.5)
    qk_ref[0, :, 128:256] = rope(proj("gk"))
    v_ref[0] = proj("gv")
    z = _dot(proj("gg").astype(BF16), wg_ref[...]) + bg_ref[...]
    lg_ref[0] = (jnp.minimum(z, 0.0) - jnp.log(1.0 + jnp.exp(-jnp.abs(z)))) * (1.0 / GLA_TAU)
    nq_ref[0] = (proj("nq") * (NA_DH ** -0.5)).astype(BF16)
    nk_ref[0] = proj("nk").T.astype(BF16)
    nv_ref[0] = proj("nv").astype(BF16)
    su = proj("su")
    su0_ref[0] = su[:, 0:128]
    su1_ref[0] = su[:, 128:256]
    pu_ref[0] = proj("pu")
    gate_ref[0] = proj("gate").astype(BF16)


def _inproj_kernel(x_ref, xc_ref, *refs, n_batch):
    b = pl.program_id(0)
    j = pl.program_id(1)
    pl.when(j == 0)(lambda: _inproj_body(xc_ref[0], n_batch, False, *refs))
    pl.when(j > 0)(lambda: _inproj_body(x_ref[0], b, True, *refs))


def _layer_spec(shape, layer):
    return pl.BlockSpec((None,) + tuple(shape), lambda *_: (layer,) + (0,) * len(shape))


def _lat_map(b, j):
    return (b, jnp.maximum(j - 1, 0), 0)


def _ctx_map(b, j):
    return (b, 0, 0)


def _inproj_specs(n_b, t_all, d, layer):
    tab_map = lambda b, j: (jnp.maximum(j - 1, 0), 0)
    outs = (("qk", 256, F32), ("gv", 256, F32), ("lg", 256, F32), ("nq", 256, BF16), ("nkT", 256, BF16),
            ("nv", 256, BF16), ("su0", 128, F32), ("su1", 128, F32), ("pu", 256, F32), ("gate", 1024, BF16))
    out_shape = [jax.ShapeDtypeStruct((n_b, w, t_all) if n == "nkT" else (n_b, t_all, w), dt) for n, w, dt in outs]
    out_specs = [pl.BlockSpec((1, w, TILE), lambda b, j: (b, 0, j)) if n == "nkT"
                 else pl.BlockSpec((1, TILE, w), lambda b, j: (b, j, 0)) for n, w, dt in outs]
    in_specs = [_layer_spec((8, 3 * d), layer),
                _layer_spec((1, d), layer),
                _layer_spec((d, N_PAD_COLS), layer),
                _layer_spec((128, 256), layer),
                _layer_spec((1, 256), layer),
                pl.BlockSpec((TILE, 128), tab_map),
                pl.BlockSpec((TILE, 128), tab_map),
                pl.BlockSpec((TILE, 128), tab_map)]
    return in_specs, out_shape, out_specs


def _inproj(x, xc, layer, mod_all, g_pre, w_pad, wg, bg, rope_tabs):
    n_b, n_lat, d = x.shape
    t_all = n_lat + CTX_LEN
    in_specs, out_shape, out_specs = _inproj_specs(n_b, t_all, d, layer)
    return pl.pallas_call(
        functools.partial(_inproj_kernel, n_batch=n_b),
        out_shape=out_shape,
        grid=(n_b, t_all // TILE),
        in_specs=[pl.BlockSpec((1, TILE, d), _lat_map), pl.BlockSpec((1, CTX_LEN, d), _ctx_map)] + in_specs,
        out_specs=out_specs,
        compiler_params=_cp(("parallel", "arbitrary")),
        name="modnorm_inproj",
    )(x, xc, mod_all, g_pre, w_pad, wg, bg, *rope_tabs)


def _gla_kernel(qk_ref, v_ref, lg_ref, gn_ref, y_ref, stf_ref, stb_ref, of_ref, ob_ref, *, n_chunks, n_ctx_chunks):
    c_len = GLA_C
    row = lax.broadcasted_iota(jnp.int32, (c_len, c_len), 0)
    col = lax.broadcasted_iota(jnp.int32, (c_len, c_len), 1)
    tri_f = (col <= row).astype(BF16)
    tri_b = (col >= row).astype(BF16)
    arow = lax.broadcasted_iota(jnp.int32, (c_len, 4 * c_len), 0)
    acol = lax.broadcasted_iota(jnp.int32, (c_len, 4 * c_len), 1) % c_len
    amask_f = acol <= arow
    amask_b = acol >= arow
    head_k = lax.broadcasted_iota(jnp.int32, (1, 128), 1) // GLA_DK
    head_v = lax.broadcasted_iota(jnp.int32, (1, 256), 1) // GLA_DV
    bd_mask = (lax.broadcasted_iota(jnp.int32, (256, 128), 0) // GLA_DV
               == lax.broadcasted_iota(jnp.int32, (256, 128), 1) // GLA_DK)
    ones_bd = jnp.where(lax.broadcasted_iota(jnp.int32, (256, 256), 0) // GLA_DV
                        == lax.broadcasted_iota(jnp.int32, (256, 256), 1) // GLA_DV,
                        1.0 / GLA_DV, 0.0).astype(BF16)
    mid = c_len // 2

    def scan_body(i, carry):
        def bwd_chunk(pos):
            return jnp.where(pos < n_ctx_chunks, n_ctx_chunks - 1 - pos, n_chunks - 1 - (pos - n_ctx_chunks))

        dirs = [d for d in (0, 1) for _ in range(GLA_STEP)]
        chunks = [i * GLA_STEP + u for u in range(GLA_STEP)] + [bwd_chunk(i * GLA_STEP + u) for u in range(GLA_STEP)]
        jobs = range(len(dirs))
        r0 = [pl.multiple_of(c * c_len, c_len) for c in chunks]
        st_refs = (stf_ref, stb_ref)
        tri = (tri_f, tri_b)
        amask = (amask_f, amask_b)
        end = (c_len - 1, 0)
        q = [qk_ref[0, pl.ds(r0[n], c_len), 0:128] for n in jobs]
        k = [qk_ref[0, pl.ds(r0[n], c_len), 128:256] for n in jobs]
        v = [v_ref[0, pl.ds(r0[n], c_len), :] for n in jobs]
        lg = [_split_bf16(lg_ref[0, pl.ds(r0[n], c_len), 128 * dirs[n]:128 * (dirs[n] + 1)]) for n in jobs]
        cum = [_dot(tri[dirs[n]], lg[n][0]) + _dot(tri[dirs[n]], lg[n][1]) for n in jobs]
        c_ref = [cum[n][mid:mid + 1, :] for n in jobs]
        c_end = [cum[n][end[dirs[n]]:end[dirs[n]] + 1, :] for n in jobs]
        qt = [q[n] * jnp.exp(cum[n] - c_ref[n]) for n in jobs]
        kt = [k[n] * jnp.exp(c_ref[n] - cum[n]) for n in jobs]
        ks = [jnp.concatenate([jnp.where(head_k == hh, kt[n], 0.0) for hh in range(GLA_HEADS)],
                              axis=0).astype(BF16) for n in jobs]
        att = [_dot_nt(qt[n].astype(BF16), ks[n]) for n in jobs]
        qh = [(qt[n] * jnp.exp(c_ref[n])).astype(BF16) for n in jobs]
        kh = [(kt[n] * jnp.exp(c_end[n] - c_ref[n])).astype(BF16) for n in jobs]
        upd = [_dot(v[n].T.astype(BF16), kh[n]) for n in jobs]
        att = [jnp.where(amask[dirs[n]], att[n], 0.0).astype(BF16) for n in jobs]
        vb = [v[n].astype(BF16) for n in jobs]
        vs = [jnp.concatenate([jnp.where(head_v == hh, vb[n], jnp.zeros_like(vb[n])) for hh in range(GLA_HEADS)],
                              axis=0) for n in jobs]
        o_intra = [_dot(att[n], vs[n]) for n in jobs]
        outs = (of_ref, ob_ref)
        st = [st_refs[d][...] for d in (0, 1)]
        for n in jobs:
            d = dirs[n]
            outs[d][pl.ds(r0[n], c_len), :] = o_intra[n] + _dot_nt(qh[n], st[d].astype(BF16))
            st[d] = st[d] * jnp.exp(c_end[n]) + jnp.where(bd_mask, upd[n], 0.0)
        for d in (0, 1):
            st_refs[d][...] = st[d]
        return carry

    def norm_body(c, carry):
        r0 = pl.multiple_of(c * TILE, TILE)
        o = of_ref[pl.ds(r0, TILE), :] + ob_ref[pl.ds(r0, TILE), :]
        ms = _dot((o * o).astype(BF16), ones_bd)
        y_ref[0, pl.ds(r0, TILE), :] = (o * lax.rsqrt(ms + NORM_EPS) * gn_ref[...]).astype(y_ref.dtype)
        return carry

    stf_ref[...] = jnp.zeros_like(stf_ref)
    stb_ref[...] = jnp.zeros_like(stb_ref)
    lax.fori_loop(0, n_chunks // GLA_STEP, scan_body, 0)
    lax.fori_loop(0, n_chunks * c_len // TILE, norm_body, 0)


def _gla(qk, v, lg, layer, g_norm):
    n_b, t_all, _ = qk.shape
    blk = lambda w: pl.BlockSpec((1, t_all, w), lambda b: (b, 0, 0))
    return pl.pallas_call(
        functools.partial(_gla_kernel, n_chunks=t_all // GLA_C, n_ctx_chunks=CTX_LEN // GLA_C),
        out_shape=jax.ShapeDtypeStruct((n_b, t_all, 256), BF16),
        grid=(n_b,),
        in_specs=[blk(256), blk(256), blk(256), _layer_spec((1, 256), layer)],
        out_specs=blk(256),
        scratch_shapes=[pltpu.VMEM((256, 128), F32), pltpu.VMEM((256, 128), F32),
                        pltpu.VMEM((t_all, 256), F32), pltpu.VMEM((t_all, 256), F32)],
        compiler_params=_cp(("parallel",)),
        name="gla_mixer",
    )(qk, v, lg, g_norm)


def _na_kernel(q_ref, kt0_ref, kt1_ref, kt2_ref, v0_ref, v1_ref, v2_ref, ktc_ref, vc_ref, rel_ref, o_ref, *, rows):
    j = pl.program_id(1)
    left = lax.broadcasted_iota(jnp.int32, (1, 128), 1) < NA_DH
    ones_blk = jnp.ones((TILE, 128), BF16)
    ro0 = NA_KR - 1 - NA_ROWS

    def attend(with_window):
        key_tiles = [(ktc_ref, vc_ref, None)]
        if with_window:
            g = j - 1
            q_row = lax.broadcasted_iota(jnp.int32, (TILE, TILE), 0) // GRID_W
            k_row = lax.broadcasted_iota(jnp.int32, (TILE, TILE), 1) // GRID_W
            first = jnp.clip(NA_ROWS * g + q_row - NA_KR // 2, 0, rows - NA_KR) - NA_ROWS * (g - 1)
            for d, (kr, vr) in enumerate(((kt0_ref, v0_ref), (kt1_ref, v1_ref), (kt2_ref, v2_ref))):
                window = None if d == 1 else jnp.where(
                    (k_row + NA_ROWS * d >= first) & (k_row + NA_ROWS * d < first + NA_KR), 0.0, NEG_BIG)
                key_tiles.append((kr, vr, (d, window)))
        def lanes(hh):
            return slice(128 * (hh // 2), 128 * (hh // 2 + 1))

        def score(hh):
            qp = q_ref[0, :, lanes(hh)]
            zero = jnp.zeros_like(qp)
            qh = jnp.where(left, qp, zero) if hh % 2 == 0 else jnp.where(left, zero, qp)
            kt_all = jnp.concatenate([kr[0, lanes(hh), :] for kr, _, _ in key_tiles], axis=1)
            return _dot(qh, kt_all)

        def probs(hh, s):
            if with_window:
                parts = [s[:, 0:TILE]]
                for _, _, (d, window) in key_tiles[1:]:
                    rel = jnp.concatenate(
                        [rel_ref[hh, (ro0 - i) % 2, :, GRID_W * ((ro0 - i) // 2 * 2) + d * TILE:
                                 GRID_W * ((ro0 - i) // 2 * 2) + (d + 1) * TILE] for i in range(NA_ROWS)], axis=0)
                    part = s[:, (d + 1) * TILE:(d + 2) * TILE] + rel
                    parts.append(part if window is None else part + window)
                s = jnp.concatenate(parts, axis=1)
            return jnp.exp(s - jnp.max(s, axis=-1, keepdims=True)).astype(BF16)

        def attend_values(hh, p):
            vaug = jnp.concatenate([jnp.concatenate([vr[0, :, lanes(hh)], ones_blk], axis=1)
                                    for _, vr, _ in key_tiles], axis=0)
            acc = _dot(p, vaug)
            return acc[:, 0:128] / acc[:, 128:256]

        s_next = score(0)
        outs = []
        for hh in range(NA_HEADS):
            s_cur = s_next
            if hh + 1 < NA_HEADS:
                s_next = score(hh + 1)
            outs.append(attend_values(hh, probs(hh, s_cur)))
        for pair in range(NA_HEADS // 2):
            o_pair = jnp.where(left, outs[2 * pair], outs[2 * pair + 1])
            o_ref[0, :, lanes(2 * pair)] = o_pair.astype(o_ref.dtype)

    @pl.when(j == 0)
    def _():
        attend(False)

    @pl.when(j > 0)
    def _():
        attend(True)


def _na(nq, nkt, nv, layer, rel_tab):
    n_b, t_all, _ = nq.shape
    n_tiles = t_all // TILE
    rows = (t_all - CTX_LEN) // GRID_W

    def nbr(delta):
        return lambda b, j: (b, jnp.clip(j + delta, 1, n_tiles - 1), 0)

    def nbr_t(delta):
        return lambda b, j: (b, 0, jnp.clip(j + delta, 1, n_tiles - 1))

    tile = lambda f: pl.BlockSpec((1, TILE, 256), f)
    tile_t = lambda f: pl.BlockSpec((1, 256, TILE), f)
    return pl.pallas_call(
        functools.partial(_na_kernel, rows=rows),
        out_shape=jax.ShapeDtypeStruct((n_b, t_all, 256), BF16),
        grid=(n_b, n_tiles),
        in_specs=[tile(lambda b, j: (b, j, 0)),
                  tile_t(nbr_t(-1)), tile_t(nbr_t(0)), tile_t(nbr_t(1)),
                  tile(nbr(-1)), tile(nbr(0)), tile(nbr(1)),
                  tile_t(lambda b, j: (b, 0, 0)), tile(lambda b, j: (b, 0, 0)),
                  _layer_spec((NA_HEADS, 2, GRID_W, NA_REL_W), layer)],
        out_specs=tile(lambda b, j: (b, j, 0)),
        compiler_params=_cp(("parallel", "arbitrary")),
        name="na_mixer",
    )(nq, nkt, nkt, nkt, nv, nv, nv, nkt, nv, rel_tab)


def _na_rel_table(rpb):
    col = np.arange(GRID_W)
    cs = np.clip(col - NA_KC // 2, 0, GRID_W - NA_KC)
    col_mask = (col[None, :] >= cs[:, None]) & (col[None, :] < cs[:, None] + NA_KC)
    co = np.clip(col[None, :] - col[:, None], -(NA_KC - 1), NA_KC - 1) + (NA_KC - 1)
    onehot = (co[None] == np.arange(2 * NA_KC - 1)[:, None, None]).astype(np.float32)
    toe = jnp.einsum("hrc,cqk->hqrk", rpb.astype(F32), jnp.asarray(onehot),
                     precision=lax.Precision.HIGHEST)
    toe = jnp.where(jnp.asarray(col_mask)[None, :, None, :], toe, NEG_BIG)
    flat = toe.reshape(NA_HEADS, GRID_W, (2 * NA_KR - 1) * GRID_W)
    flat = jnp.pad(flat, ((0, 0), (0, 0), (0, NA_REL_W + GRID_W - flat.shape[-1])), constant_values=NEG_BIG)
    return jnp.stack([flat[:, :, :NA_REL_W], flat[:, :, GRID_W:]], axis=1)


def _s5_kernel(su0_ref, su1_ref, ktoe_ref, bexp_ref, cexp_ref, pw_ref, y0_ref, y1_ref, a_ref, e_ref, x_ref,
               *, n_chunks, n_ctx_chunks):
    blk = 8
    lane_blk = lax.broadcasted_iota(jnp.int32, (blk, 128), 1) // S5_HG
    masks = [lane_blk == m for m in range(8)]
    gp = S5_G * S5_P

    def pick(vals):
        acc = vals[0]
        for k in range(1, 8):
            acc = jnp.where(masks[k], vals[k], acc)
        return acc

    def block_transpose(src):
        rolled = []
        for d in range(8):
            diag = pick([src[(k + d) % 8] for k in range(8)])
            rolled.append(pltpu.roll(diag, S5_HG * d, 1) if d else diag)
        return [pick([rolled[(m - a) % 8] for m in range(8)]) for a in range(8)]

    def regroup_in(rb, carry):
        c0 = pl.multiple_of(rb * blk, blk)
        for hg, ref in enumerate((su0_ref, su1_ref)):
            for o in range(2):
                src = [ref[0, pl.ds(rb * (blk * S5_T) + o * 8 + m, blk, stride=S5_T), :] for m in range(8)]
                for a, val in enumerate(block_transpose(src)):
                    a_ref[hg * 8 + a, pl.ds(c0, blk), o * 128:(o + 1) * 128] = val
        return carry

    lax.fori_loop(0, n_chunks // blk, regroup_in, 0)

    for g in range(S5_G):
        e = _dot(a_ref[g].astype(BF16), bexp_ref[g])
        for kind in range(4):
            e_ref[kind, :, g * S5_P:(g + 1) * S5_P] = e[:, kind * S5_P:(kind + 1) * S5_P]

    afr, afi, abr, abi = pw_ref[0:1, :], pw_ref[1:2, :], pw_ref[2:3, :], pw_ref[3:4, :]

    def scan_body(i, carry):
        fr, fi, br, bi = carry
        cf = i
        cb = jnp.where(i < n_ctx_chunks, n_ctx_chunks - 1 - i, n_chunks - 1 - (i - n_ctx_chunks))
        x_ref[0, pl.ds(cf, 1), :] = fr
        x_ref[1, pl.ds(cf, 1), :] = fi
        x_ref[2, pl.ds(cb, 1), :] = br
        x_ref[3, pl.ds(cb, 1), :] = bi
        er = e_ref[0, pl.ds(cf, 1), :]
        ei = e_ref[1, pl.ds(cf, 1), :]
        gr = e_ref[2, pl.ds(cb, 1), :]
        gi = e_ref[3, pl.ds(cb, 1), :]
        return (afr * fr - afi * fi + er, afr * fi + afi * fr + ei,
                abr * br - abi * bi + gr, abr * bi + abi * br + gi)

    z = jnp.zeros((1, gp), F32)
    lax.fori_loop(0, n_chunks, scan_body, (z, z, z, z))

    for g in range(S5_G):
        y = _dot(a_ref[g].astype(BF16), ktoe_ref[g])
        for kind in range(4):
            y = y + _dot(x_ref[kind, :, g * S5_P:(g + 1) * S5_P].astype(BF16), cexp_ref[g, kind])
        a_ref[g] = y

    def regroup_out(rb, carry):
        c0 = pl.multiple_of(rb * blk, blk)
        for og, ref in enumerate((y0_ref, y1_ref)):
            for ht in range(2):
                src = [a_ref[og * 8 + m, pl.ds(c0, blk), ht * 128:(ht + 1) * 128] for m in range(8)]
                for a, val in enumerate(block_transpose(src)):
                    ref[0, pl.ds(rb * (blk * S5_T) + ht * 8 + a, blk, stride=S5_T), :] = val
        return carry

    lax.fori_loop(0, n_chunks // blk, regroup_out, 0)


def _s5(su0, su1, layer, ktoe, bexp, cexp, pw16):
    n_b, t_all, _ = su0.shape
    n_chunks = t_all // S5_T
    gp = S5_G * S5_P
    blk = pl.BlockSpec((1, t_all, 128), lambda b: (b, 0, 0))
    return pl.pallas_call(
        functools.partial(_s5_kernel, n_chunks=n_chunks, n_ctx_chunks=CTX_LEN // S5_T),
        out_shape=[jax.ShapeDtypeStruct((n_b, t_all, 128), F32)] * 2,
        grid=(n_b,),
        in_specs=[blk, blk, _layer_spec((S5_G, 256, 256), layer), _layer_spec((S5_G, 256, 256), layer),
                  _layer_spec((S5_G, 4, S5_P, 256), layer), _layer_spec((4, gp), layer)],
        out_specs=[blk, blk],
        scratch_shapes=[pltpu.VMEM((S5_G, n_chunks, S5_T * S5_HG), F32),
                        pltpu.VMEM((4, n_chunks, gp), F32),
                        pltpu.VMEM((4, n_chunks, gp), F32)],
        compiler_params=_cp(("parallel",)),
        name="s5_mixer",
    )(su0, su1, ktoe, bexp, cexp, pw16)


def _s5_tables(lam_re, lam_im, log_dt, b_re, b_im, c_re, c_im, d_skip):
    hp = lax.Precision.HIGH
    lam_re, lam_im, log_dt = lam_re.astype(F32), lam_im.astype(F32), log_dt.astype(F32)
    b_re, b_im, c_re, c_im = b_re.astype(F32), b_im.astype(F32), c_re.astype(F32), c_im.astype(F32)
    dt = jnp.exp(log_dt)[..., None]
    n = jnp.arange(S5_T + 1, dtype=F32)[:, None, None, None]
    pw_re = jnp.exp(n * lam_re * dt) * jnp.cos(n * lam_im * dt)
    pw_im = jnp.exp(n * lam_re * dt) * jnp.sin(n * lam_im * dt)
    neg_re = jnp.exp(-n * lam_re * dt) * jnp.cos(n * lam_im * dt)
    neg_im = -jnp.exp(-n * lam_re * dt) * jnp.sin(n * lam_im * dt)
    m = jnp.arange(S5_T, -1, -1, dtype=F32)[:, None, None, None]
    dsc_re = jnp.exp(m * lam_re * dt) * jnp.cos(m * lam_im * dt)
    dsc_im = jnp.exp(m * lam_re * dt) * jnp.sin(m * lam_im * dt)
    lb_re, lb_im = pw_re[1], pw_im[1]
    num_re, num_im = lb_re - 1.0, lb_im
    den = lam_re * lam_re + lam_im * lam_im
    coef_re = ((num_re * lam_re + num_im * lam_im) / den)[..., None]
    coef_im = ((num_im * lam_re - num_re * lam_im) / den)[..., None]
    bb_re = coef_re * b_re - coef_im * b_im
    bb_im = coef_re * b_im + coef_im * b_re

    def cmul_b(p_re, p_im, d):
        re = p_re[..., None] * bb_re[d][None] - p_im[..., None] * bb_im[d][None]
        im = p_re[..., None] * bb_im[d][None] + p_im[..., None] * bb_re[d][None]
        to = lambda z: z.transpose(1, 0, 3, 2).reshape(S5_G, S5_T * S5_HG, S5_P)
        return to(re), to(im)

    def cmul_c(p_re, p_im, d):
        re = c_re[d][None] * p_re[:, :, None, :] - c_im[d][None] * p_im[:, :, None, :]
        im = c_re[d][None] * p_im[:, :, None, :] + c_im[d][None] * p_re[:, :, None, :]
        to = lambda z: z.transpose(1, 3, 0, 2).reshape(S5_G, S5_P, S5_T * S5_HG)
        return to(re), to(-im)

    def lag_operator(u, v):
        return jnp.einsum("gap,gpb->gab", jnp.concatenate(u, axis=-1), jnp.concatenate(v, axis=1), precision=hp)

    first = slice(0, S5_T)
    kf = lag_operator(cmul_b(neg_re[first, 0], neg_im[first, 0], 0), cmul_c(pw_re[first, 0], pw_im[first, 0], 0))
    kb = lag_operator(cmul_b(pw_re[first, 1], pw_im[first, 1], 1), cmul_c(neg_re[first, 1], neg_im[first, 1], 1))
    tok = np.arange(S5_T * S5_HG) // S5_HG
    skip = jnp.asarray(np.eye(S5_T * S5_HG, dtype=np.float32)) * jnp.tile(
        d_skip.astype(F32).reshape(S5_G, S5_HG), (1, S5_T))[:, None, :]
    ktoe = (jnp.where(jnp.asarray(tok[None, :] >= tok[:, None]), kf, 0.0)
            + jnp.where(jnp.asarray(tok[:, None] >= tok[None, :]), kb, 0.0) + skip)
    last = slice(1, S5_T + 1)
    bexp = jnp.concatenate(cmul_b(dsc_re[last, 0], dsc_im[last, 0], 0)
                           + cmul_b(pw_re[first, 1], pw_im[first, 1], 1), axis=-1)
    cexp = jnp.stack(cmul_c(pw_re[last, 0], pw_im[last, 0], 0)
                     + cmul_c(dsc_re[first, 1], dsc_im[first, 1], 1), axis=1)
    pw16 = jnp.stack([pw_re[S5_T, 0].reshape(-1), pw_im[S5_T, 0].reshape(-1),
                      pw_re[S5_T, 1].reshape(-1), pw_im[S5_T, 1].reshape(-1)], axis=0)
    return ktoe.astype(BF16), bexp.astype(BF16), cexp.astype(BF16), pw16


def _pool_kernel(u_ref, w_ref, scale_ref, o_ref, pad_ref, wblk_ref, *, n_lat):
    assert POOL_WINDOWS == (2, 4, 8, 16) and POOL_GW * len(POOL_WINDOWS) == GROUP_W
    pad = max(POOL_WINDOWS) // 2
    seg = ((pad, 0, CTX_LEN), (pad + CTX_LEN + 2 * pad, CTX_LEN, n_lat))
    pad_ref[...] = jnp.zeros_like(pad_ref)
    for p0, u0, n in seg:
        pad_ref[p0:p0 + n, :] = u_ref[0, u0:u0 + n, :]
    wblk_ref[...] = jnp.zeros_like(wblk_ref)
    for i in range(len(POOL_WINDOWS)):
        wblk_ref[i * POOL_GW:(i + 1) * POOL_GW, i * POOL_GW:(i + 1) * POOL_GW] = w_ref[i].astype(BF16)
    grp = lax.broadcasted_iota(jnp.int32, (1, GROUP_W), 1) // POOL_GW
    width = jnp.left_shift(2, grp)
    rows = TILE + 2 * pad

    def up(a, k):
        return pltpu.roll(a, rows - k, 0)

    for p0, u0, n in seg:
        for t0 in range(0, n, TILE):
            w0 = pad_ref[p0 + t0 - pad:p0 + t0 - pad + rows, :]
            a1 = w0 + up(w0, 1)
            a2 = a1 + up(a1, 2)
            a3 = a2 + up(a2, 4)
            a4 = a3 + up(a3, 8)
            total = jnp.where(grp == 0, up(a1, pad - 1), jnp.where(grp == 1, up(a2, pad - 2),
                              jnp.where(grp == 2, up(a3, pad - 4), a4)))[0:TILE]
            if t0 == 0 or t0 + TILE == n:
                t = t0 + lax.broadcasted_iota(jnp.int32, (TILE, 1), 0)
                first = t - width // 2
                cnt = (jnp.clip(first + width, 0, n) - jnp.clip(first, 0, n)).astype(F32)
            else:
                cnt = width.astype(F32)
            diff = total / cnt - w0[pad:pad + TILE]
            o_ref[0, u0 + t0:u0 + t0 + TILE, :] = (_dot(diff.astype(BF16), wblk_ref[...])
                                                   * scale_ref[...]).astype(o_ref.dtype)


def _pool(pu, layer, w_pool, scale):
    n_b, t_all, _ = pu.shape
    blk = pl.BlockSpec((1, t_all, 256), lambda b: (b, 0, 0))
    return pl.pallas_call(
        functools.partial(_pool_kernel, n_lat=t_all - CTX_LEN),
        out_shape=jax.ShapeDtypeStruct((n_b, t_all, 256), BF16),
        grid=(n_b,),
        in_specs=[blk, _layer_spec((len(POOL_WINDOWS), POOL_GW, POOL_GW), layer), _layer_spec((1, 256), layer)],
        out_specs=blk,
        scratch_shapes=[pltpu.VMEM((t_all + 32, 256), F32), pltpu.VMEM((GROUP_W, GROUP_W), BF16)],
        compiler_params=_cp(("parallel",)),
        name="pool_mixer",
    )(pu, w_pool, scale)


def _gelu_tanh(x):
    return 0.5 * x * (1.0 + jnp.tanh(math.sqrt(2.0 / math.pi) * (x + 0.044715 * (x * x * x))))


def _outproj_res(res, mod_row, ygla_ref, yna_ref, ys50_ref, ys51_ref, ypool_ref, gate_ref, mod_ref, wout_ref,
                 gpost_ref, wglu_ref, bglu_ref):
    def gated_proj(part, i):
        lo, hi = i * GROUP_W, (i + 1) * GROUP_W
        gc = gate_ref[0, :, lo:hi].astype(F32)
        return _dot((part * (gc * jax.nn.sigmoid(gc))).astype(BF16), wout_ref[lo:hi, :])

    gate_m = mod_ref[pl.ds(mod_row, 1), :][:, 2 * D_MODEL:3 * D_MODEL]
    g5 = _gelu_tanh(jnp.concatenate([ys50_ref[0], ys51_ref[0]], axis=-1))
    z5 = _dot(g5.astype(BF16), wglu_ref[...])
    acc = gated_proj(ygla_ref[0], 0) + gated_proj(yna_ref[0], 1) + gated_proj(ypool_ref[0], 3)
    acc = acc + gated_proj(g5 * jax.nn.sigmoid(z5 + bglu_ref[...]), 2)
    ms = jnp.mean(acc * acc, axis=-1, keepdims=True)
    return res + gate_m * (acc * lax.rsqrt(ms + NORM_EPS) * gpost_ref[...])


N_MIX_REFS = 6
N_OUT_PARAM_REFS = 5
N_IN_PARAM_REFS = 8


def _outproj_kernel(*refs):
    mix, x_ref, params, out_ref = refs[:N_MIX_REFS], refs[N_MIX_REFS], refs[N_MIX_REFS + 1:-1], refs[-1]
    out_ref[0] = _outproj_res(x_ref[0], pl.program_id(0), *mix, *params)


def _outproj_inproj_kernel(*refs, n_batch):
    mix = refs[:N_MIX_REFS]
    x_ref, xc_ref = refs[N_MIX_REFS:N_MIX_REFS + 2]
    out_params = refs[N_MIX_REFS + 2:N_MIX_REFS + 2 + N_OUT_PARAM_REFS]
    rest = refs[N_MIX_REFS + 2 + N_OUT_PARAM_REFS:]
    in_params, (xo_ref, xco_ref), in_outs = rest[:N_IN_PARAM_REFS], rest[N_IN_PARAM_REFS:N_IN_PARAM_REFS + 2], \
        rest[N_IN_PARAM_REFS + 2:]
    b = pl.program_id(0)
    j = pl.program_id(1)

    def body(res_ref, out_ref, mod_row, with_rope):
        res = _outproj_res(res_ref[0], mod_row, *mix, *out_params)
        out_ref[0] = res
        _inproj_body(res, mod_row, with_rope, *in_params, *in_outs)

    pl.when(j == 0)(lambda: body(xc_ref, xco_ref, n_batch, False))
    pl.when(j > 0)(lambda: body(x_ref, xo_ref, b, True))


def _outproj_param_specs(d, layer):
    return [_layer_spec((8, 3 * d), layer), _layer_spec((MIX_W, d), layer), _layer_spec((1, d), layer),
            _layer_spec((GROUP_W, GROUP_W), layer), _layer_spec((1, GROUP_W), layer)]


def _mix_specs(index_map):
    return [pl.BlockSpec((1, TILE, w), index_map) for w in (256, 256, 128, 128, 256, MIX_W)]


def _outproj(mix, x, layer, mod_all, w_out, g_post, w_glu, b_glu):
    n_b, n_lat, d = x.shape
    return pl.pallas_call(
        _outproj_kernel,
        out_shape=jax.ShapeDtypeStruct(x.shape, F32),
        grid=(n_b, n_lat // TILE),
        in_specs=_mix_specs(lambda b, j: (b, j + 1, 0)) + [pl.BlockSpec((1, TILE, d), lambda b, j: (b, j, 0))]
                 + _outproj_param_specs(d, layer),
        out_specs=pl.BlockSpec((1, TILE, d), lambda b, j: (b, j, 0)),
        compiler_params=_cp(("parallel", "arbitrary")),
        name="gate_outproj_residual",
    )(*mix, x, mod_all, w_out, g_post, w_glu, b_glu)


def _outproj_inproj(mix, x, xc, layer, mod_all, w_out, g_post, w_glu, b_glu, g_pre, w_pad, wg, bg, rope_tabs):
    n_b, n_lat, d = x.shape
    t_all = n_lat + CTX_LEN
    in_specs, in_shape, in_out_specs = _inproj_specs(n_b, t_all, d, layer + 1)
    outs = pl.pallas_call(
        functools.partial(_outproj_inproj_kernel, n_batch=n_b),
        out_shape=[jax.ShapeDtypeStruct(x.shape, F32), jax.ShapeDtypeStruct(xc.shape, F32)] + in_shape,
        grid=(n_b, t_all // TILE),
        in_specs=_mix_specs(lambda b, j: (b, j, 0))
                 + [pl.BlockSpec((1, TILE, d), _lat_map), pl.BlockSpec((1, CTX_LEN, d), _ctx_map)]
                 + _outproj_param_specs(d, layer) + in_specs,
        out_specs=[pl.BlockSpec((1, TILE, d), _lat_map), pl.BlockSpec((1, CTX_LEN, d), _ctx_map)] + in_out_specs,
        compiler_params=_cp(("parallel", "arbitrary")),
        name="outproj_then_inproj",
    )(*mix, x, xc, mod_all, w_out, g_post, w_glu, b_glu, mod_all, g_pre, w_pad, wg, bg, *rope_tabs)
    return outs[0], outs[1], outs[2:]


def _rope_tables(n_tok):
    t = np.arange(n_tok)
    half = GLA_DK // 2
    freqs = ROPE_BASE ** (-np.arange(0, half, 2, dtype=np.float32) / half)
    d = np.arange(GLA_DK)
    pos = np.where((d // half)[None, :] == 0, (t // GRID_W)[:, None], (t % GRID_W)[:, None]).astype(np.float32)
    ang = jnp.asarray(pos) * jnp.asarray(freqs[(d % half) % (half // 2)])[None, :]
    first = jnp.asarray(((d % half) < half // 2)[None, :])
    cos, sin = jnp.cos(ang), jnp.sin(ang)
    tabs = (cos, jnp.where(first, -sin, 0.0), jnp.where(first, 0.0, sin))
    return tuple(jnp.tile(z, (1, GLA_HEADS)).astype(F32) for z in tabs)


def _pad_w_in_kernel(w_ref, o_ref):
    pad_end = _COL["gg"][1]
    o_ref[:, 0:N_GATE_END] = w_ref[:, 0:N_GATE_END]
    o_ref[:, N_GATE_END:pad_end] = jnp.zeros((o_ref.shape[0], pad_end - N_GATE_END), BF16)
    o_ref[:, pad_end:N_PAD_COLS] = w_ref[:, N_GATE_END:]


def _pad_w_in(w_in):
    n_l, d, n_in = w_in.shape
    rows = 128
    return pl.pallas_call(
        _pad_w_in_kernel,
        out_shape=jax.ShapeDtypeStruct((n_l, d, N_PAD_COLS), BF16),
        grid=(n_l, d // rows),
        in_specs=[pl.BlockSpec((None, rows, n_in), lambda l, i: (l, i, 0))],
        out_specs=pl.BlockSpec((None, rows, N_PAD_COLS), lambda l, i: (l, i, 0)),
        compiler_params=_cp(("parallel", "parallel")),
        name="pad_w_in",
    )(w_in.astype(BF16))


def _gate_weights(w_gate, b_gate):
    n_l = w_gate.shape[0]
    wg = jnp.zeros((n_l, 128, 256), F32)
    wg = wg.at[:, 0:GLA_RANK, 0:128].set(w_gate[:, 0].astype(F32))
    wg = wg.at[:, GLA_RANK:2 * GLA_RANK, 128:256].set(w_gate[:, 1].astype(F32))
    return wg.astype(BF16), b_gate.astype(F32).reshape(n_l, 1, 256)


def kernel(x, c, ctx, c_ctx, w_mod, b_mod, g_pre, g_post, w_in, w_out, gla_w_gate, gla_b_gate, gla_g_norm, na_rpb,
           s5_lam_re, s5_lam_im, s5_log_dt, s5_b_re, s5_b_im, s5_c_re, s5_c_im, s5_d, s5_w_glu, s5_b_glu,
           pool_w, pool_scale):
    n_b, n_lat, d = x.shape
    depth = w_mod.shape[0]
    rows = n_lat // GRID_W
    assert d == D_MODEL and ctx.shape[1] == CTX_LEN and n_lat % TILE == 0 and n_b < 8 and rows >= NA_KR
    c_rows = jnp.zeros((8, d), F32).at[0:n_b].set(c).at[n_b].set(c_ctx)
    mod_all = _modulation(c_rows, w_mod, b_mod)
    rope_tabs = _rope_tables(n_lat)
    w_pad = _pad_w_in(w_in)
    wg, bg = _gate_weights(gla_w_gate, gla_b_gate)
    g_pre3 = g_pre.astype(F32).reshape(depth, 1, d)
    g_post3 = g_post.astype(F32).reshape(depth, 1, d)
    gn = jnp.tile(gla_g_norm.astype(F32), (1, GLA_HEADS)).reshape(depth, 1, GLA_HEADS * GLA_DV)
    rel_tab = jax.vmap(_na_rel_table)(na_rpb)
    ktoe, bexp, cexp, pw16 = jax.vmap(_s5_tables)(s5_lam_re, s5_lam_im, s5_log_dt, s5_b_re, s5_b_im,
                                                  s5_c_re, s5_c_im, s5_d)
    w_out_b = w_out.astype(BF16)
    w_glu_b = s5_w_glu.astype(BF16)
    b_glu3 = s5_b_glu.astype(F32).reshape(depth, 1, GROUP_W)
    pool_scale3 = pool_scale.astype(F32).reshape(depth, 1, GROUP_W)
    xc = ctx
    proj = _inproj(x, xc, 0, mod_all, g_pre3, w_pad, wg, bg, rope_tabs)
    for l in range(depth):
        qk, gv, lg, nq, nkt, nv, su0, su1, pu, gate_cols = proj
        y_gla = _gla(qk, gv, lg, l, gn)
        y_na = _na(nq, nkt, nv, l, rel_tab)
        y_s50, y_s51 = _s5(su0, su1, l, ktoe, bexp, cexp, pw16)
        y_pool = _pool(pu, l, pool_w, pool_scale3)
        mix = (y_gla, y_na, y_s50, y_s51, y_pool, gate_cols)
        if l == depth - 1:
            x = _outproj(mix, x, l, mod_all, w_out_b, g_post3, w_glu_b, b_glu3)
        else:
            x, xc, proj = _outproj_inproj(mix, x, xc, l, mod_all, w_out_b, g_post3, w_glu_b, b_glu3,
                                          g_pre3, w_pad, wg, bg, rope_tabs)
    return x
```

```python
import functools
import math

import numpy as np
import jax
import jax.numpy as jnp
from jax import lax
from jax.experimental import pallas as pl
from jax.experimental.pallas import tpu as pltpu

F32 = jnp.float32
BF16 = jnp.bfloat16

D_MODEL = 1024
GRID_W = 64
CTX_LEN = 256
GROUP_W = 256
MIX_W = 1024
NORM_EPS = 1e-6
GLA_HEADS = 4
GLA_DV = 64
GLA_DK = 32
GLA_RANK = 16
GLA_TAU = 16.0
ROPE_BASE = 10000.0
NA_HEADS = 4
NA_DH = 64
NA_KR = 8
NA_KC = 16
S5_HG = 16
S5_G = 16
S5_P = 64
POOL_WINDOWS = (2, 4, 8, 16)
POOL_GW = 64

TILE = 256
GLA_C = 128
GLA_STEP = 2
S5_T = 16
NA_ROWS = TILE // GRID_W
NA_REL_W = 1024
NEG_BIG = -1e30

_COL = {}
_off = 0
for _name, _w in (("gk", 128), ("gv", 256), ("gg", 128), ("nk", 256), ("nv", 256), ("su", 256),
                  ("gq", 128), ("nq", 256), ("pu", 256), ("gate", 1024)):
    _COL[_name] = (_off, _off + _w)
    _off += _w
N_PAD_COLS = _off
N_GATE_END = 128 + 256 + 2 * GLA_RANK
N_HEAD_COLS = _COL["gg"][1]

_VMEM_LIMIT = 56 * 1024 * 1024


def _cp(sem, vmem=_VMEM_LIMIT):
    return pltpu.CompilerParams(dimension_semantics=sem, vmem_limit_bytes=vmem)


def _dot(a, b):
    return jnp.dot(a, b, preferred_element_type=F32)


def _dot_nt(a, b):
    return lax.dot_general(a, b, (((1,), (1,)), ((), ())), preferred_element_type=F32)


def _split_bf16(a):
    hi = a.astype(BF16)
    lo = (a - hi.astype(F32)).astype(BF16)
    return hi, lo


def _mod_kernel(s_ref, w_ref, b_ref, o_ref):
    s = s_ref[...]
    a = (s * jax.nn.sigmoid(s)).astype(BF16)
    o_ref[0] = _dot(a, w_ref[0].astype(BF16)) + b_ref[0]


def _modulation(c_rows, w_mod, b_mod):
    n_l, d, d3 = w_mod.shape
    return pl.pallas_call(
        _mod_kernel,
        out_shape=jax.ShapeDtypeStruct((n_l, 8, d3), F32),
        grid=(n_l, d3 // d),
        in_specs=[pl.BlockSpec((8, d), lambda l, j: (0, 0)),
                  pl.BlockSpec((1, d, d), lambda l, j: (l, 0, j)),
                  pl.BlockSpec((1, 1, d), lambda l, j: (l, 0, j))],
        out_specs=pl.BlockSpec((1, 8, d), lambda l, j: (l, 0, j)),
        compiler_params=_cp(("parallel", "parallel")),
        name="adaln_modulation",
    )(c_rows, w_mod, b_mod.reshape(n_l, 1, d3))


def _inproj_body(xt, mod_row, with_rope, mod_ref, gpre_ref, wa_ref, wb_ref, wg_ref, bg_ref, cos_ref, sa_ref, sb_ref,
                 qk_ref, v_ref, lg_ref, nq_ref, nk_ref, nv_ref, su0_ref, su1_ref, pu_ref, gate_ref):
    m = mod_ref[pl.ds(mod_row, 1), :]
    shift = m[:, 0:D_MODEL]
    scale = m[:, D_MODEL:2 * D_MODEL]
    ms = jnp.mean(xt * xt, axis=-1, keepdims=True)
    h = xt * lax.rsqrt(ms + NORM_EPS) * gpre_ref[...] * (1.0 + scale) + shift
    hb = h.astype(BF16)

    def proj(name):
        lo, hi = _COL[name]
        if hi <= N_HEAD_COLS:
            return _dot(hb, wa_ref[:, lo:hi])
        return _dot(hb, wb_ref[:, lo - N_HEAD_COLS:hi - N_HEAD_COLS])

    def rope(t):
        if not with_rope:
            return t
        return t * cos_ref[...] + pltpu.roll(t, 128 - 8, 1) * sa_ref[...] + pltpu.roll(t, 8, 1) * sb_ref[...]

    qk_ref[0, :, 0:128] = rope(proj("gq")) * (GLA_DK ** -0.5)
    qk_ref[0, :, 128:256] = rope(proj("gk"))
    v_ref[0] = proj("gv")
    z = _dot(proj("gg").astype(BF16), wg_ref[...]) + bg_ref[...]
    lg_ref[0] = (jnp.minimum(z, 0.0) - jnp.log(1.0 + jnp.exp(-jnp.abs(z)))) * (1.0 / GLA_TAU)
    nq_ref[0] = (proj("nq") * (NA_DH ** -0.5)).astype(BF16)
    nk_ref[0] = proj("nk").T.astype(BF16)
    nv_ref[0] = proj("nv").astype(BF16)
    su = proj("su")
    su0_ref[0] = su[:, 0:128]
    su1_ref[0] = su[:, 128:256]
    pu_ref[0] = proj("pu")
    gate_ref[0] = proj("gate").astype(BF16)


def _inproj_kernel(x_ref, xc_ref, *refs, n_batch):
    b = pl.program_id(0)
    j = pl.program_id(1)
    pl.when(j == 0)(lambda: _inproj_body(xc_ref[0], n_batch, False, *refs))
    pl.when(j > 0)(lambda: _inproj_body(x_ref[0], b, True, *refs))


def _layer_spec(shape, layer):
    return pl.BlockSpec((None,) + tuple(shape), lambda *_: (layer,) + (0,) * len(shape))


def _lat_map(b, j):
    return (b, jnp.maximum(j - 1, 0), 0)


def _ctx_map(b, j):
    return (b, 0, 0)


def _inproj_specs(n_b, t_all, d, layer):
    tab_map = lambda b, j: (jnp.maximum(j - 1, 0), 0)
    outs = (("qk", 256, F32), ("gv", 256, F32), ("lg", 256, F32), ("nq", 256, BF16), ("nkT", 256, BF16),
            ("nv", 256, BF16), ("su0", 128, F32), ("su1", 128, F32), ("pu", 256, F32), ("gate", 1024, BF16))
    out_shape = [jax.ShapeDtypeStruct((n_b, w, t_all) if n == "nkT" else (n_b, t_all, w), dt) for n, w, dt in outs]
    out_specs = [pl.BlockSpec((1, w, TILE), lambda b, j: (b, 0, j)) if n == "nkT"
                 else pl.BlockSpec((1, TILE, w), lambda b, j: (b, j, 0)) for n, w, dt in outs]
    in_specs = [_layer_spec((8, 3 * d), layer),
                _layer_spec((1, d), layer),
                _layer_spec((d, N_HEAD_COLS), layer),
                _layer_spec((d, N_PAD_COLS - N_HEAD_COLS), layer),
                _layer_spec((128, 256), layer),
                _layer_spec((1, 256), layer),
                pl.BlockSpec((TILE, 128), tab_map),
                pl.BlockSpec((TILE, 128), tab_map),
                pl.BlockSpec((TILE, 128), tab_map)]
    return in_specs, out_shape, out_specs


def _inproj(x, xc, layer, mod_all, g_pre, w_pad, wg, bg, rope_tabs):
    n_b, n_lat, d = x.shape
    t_all = n_lat + CTX_LEN
    in_specs, out_shape, out_specs = _inproj_specs(n_b, t_all, d, layer)
    return pl.pallas_call(
        functools.partial(_inproj_kernel, n_batch=n_b),
        out_shape=out_shape,
        grid=(n_b, t_all // TILE),
        in_specs=[pl.BlockSpec((1, TILE, d), _lat_map), pl.BlockSpec((1, CTX_LEN, d), _ctx_map)] + in_specs,
        out_specs=out_specs,
        compiler_params=_cp(("parallel", "arbitrary")),
        name="modnorm_inproj",
    )(x, xc, mod_all, g_pre, *w_pad, wg, bg, *rope_tabs)


def _gla_kernel(qk_ref, v_ref, lg_ref, gn_ref, y_ref, stf_ref, stb_ref, of_ref, ob_ref, *, n_chunks, n_ctx_chunks):
    c_len = GLA_C
    row = lax.broadcasted_iota(jnp.int32, (c_len, c_len), 0)
    col = lax.broadcasted_iota(jnp.int32, (c_len, c_len), 1)
    tri_f = (col <= row).astype(BF16)
    tri_b = (col >= row).astype(BF16)
    arow = lax.broadcasted_iota(jnp.int32, (c_len, 4 * c_len), 0)
    acol = lax.broadcasted_iota(jnp.int32, (c_len, 4 * c_len), 1) % c_len
    amask_f = acol <= arow
    amask_b = acol >= arow
    head_k = lax.broadcasted_iota(jnp.int32, (1, 128), 1) // GLA_DK
    head_v = lax.broadcasted_iota(jnp.int32, (1, 256), 1) // GLA_DV
    bd_mask = (lax.broadcasted_iota(jnp.int32, (256, 128), 0) // GLA_DV
               == lax.broadcasted_iota(jnp.int32, (256, 128), 1) // GLA_DK)
    ones_bd = jnp.where(lax.broadcasted_iota(jnp.int32, (256, 256), 0) // GLA_DV
                        == lax.broadcasted_iota(jnp.int32, (256, 256), 1) // GLA_DV,
                        1.0 / GLA_DV, 0.0).astype(BF16)
    mid = c_len // 2

    def scan_body(i, carry):
        def bwd_chunk(pos):
            return jnp.where(pos < n_ctx_chunks, n_ctx_chunks - 1 - pos, n_chunks - 1 - (pos - n_ctx_chunks))

        dirs = [d for d in (0, 1) for _ in range(GLA_STEP)]
        chunks = [i * GLA_STEP + u for u in range(GLA_STEP)] + [bwd_chunk(i * GLA_STEP + u) for u in range(GLA_STEP)]
        jobs = range(len(dirs))
        r0 = [pl.multiple_of(c * c_len, c_len) for c in chunks]
        st_refs = (stf_ref, stb_ref)
        tri = (tri_f, tri_b)
        amask = (amask_f, amask_b)
        end = (c_len - 1, 0)
        q = [qk_ref[0, pl.ds(r0[n], c_len), 0:128] for n in jobs]
        k = [qk_ref[0, pl.ds(r0[n], c_len), 128:256] for n in jobs]
        v = [v_ref[0, pl.ds(r0[n], c_len), :] for n in jobs]
        lg = [_split_bf16(lg_ref[0, pl.ds(r0[n], c_len), 128 * dirs[n]:128 * (dirs[n] + 1)]) for n in jobs]
        cum = [_dot(tri[dirs[n]], lg[n][0]) + _dot(tri[dirs[n]], lg[n][1]) for n in jobs]
        c_ref = [cum[n][mid:mid + 1, :] for n in jobs]
        c_end = [cum[n][end[dirs[n]]:end[dirs[n]] + 1, :] for n in jobs]
        qt = [q[n] * jnp.exp(cum[n] - c_ref[n]) for n in jobs]
        kt = [k[n] * jnp.exp(c_ref[n] - cum[n]) for n in jobs]
        ks = [jnp.concatenate([jnp.where(head_k == hh, kt[n], 0.0) for hh in range(GLA_HEADS)],
                              axis=0).astype(BF16) for n in jobs]
        att = [_dot_nt(qt[n].astype(BF16), ks[n]) for n in jobs]
        qh = [(qt[n] * jnp.exp(c_ref[n])).astype(BF16) for n in jobs]
        kh = [(kt[n] * jnp.exp(c_end[n] - c_ref[n])).astype(BF16) for n in jobs]
        upd = [_dot(v[n].T.astype(BF16), kh[n]) for n in jobs]
        att = [jnp.where(amask[dirs[n]], att[n], 0.0).astype(BF16) for n in jobs]
        vb = [v[n].astype(BF16) for n in jobs]
        vs = [jnp.concatenate([jnp.where(head_v == hh, vb[n], jnp.zeros_like(vb[n])) for hh in range(GLA_HEADS)],
                              axis=0) for n in jobs]
        o_intra = [_dot(att[n], vs[n]) for n in jobs]
        outs = (of_ref, ob_ref)
        st = [st_refs[d][...] for d in (0, 1)]
        for n in jobs:
            d = dirs[n]
            outs[d][pl.ds(r0[n], c_len), :] = o_intra[n] + _dot_nt(qh[n], st[d].astype(BF16))
            st[d] = st[d] * jnp.exp(c_end[n]) + jnp.where(bd_mask, upd[n], 0.0)
        for d in (0, 1):
            st_refs[d][...] = st[d]
        return carry

    def norm_body(c, carry):
        r0 = pl.multiple_of(c * TILE, TILE)
        o = of_ref[pl.ds(r0, TILE), :] + ob_ref[pl.ds(r0, TILE), :]
        ms = _dot((o * o).astype(BF16), ones_bd)
        y_ref[0, pl.ds(r0, TILE), :] = (o * lax.rsqrt(ms + NORM_EPS) * gn_ref[...]).astype(y_ref.dtype)
        return carry

    stf_ref[...] = jnp.zeros_like(stf_ref)
    stb_ref[...] = jnp.zeros_like(stb_ref)
    lax.fori_loop(0, n_chunks // GLA_STEP, scan_body, 0)
    lax.fori_loop(0, n_chunks * c_len // TILE, norm_body, 0)


def _gla(qk, v, lg, layer, g_norm):
    n_b, t_all, _ = qk.shape
    blk = lambda w: pl.BlockSpec((1, t_all, w), lambda b: (b, 0, 0))
    return pl.pallas_call(
        functools.partial(_gla_kernel, n_chunks=t_all // GLA_C, n_ctx_chunks=CTX_LEN // GLA_C),
        out_shape=jax.ShapeDtypeStruct((n_b, t_all, 256), BF16),
        grid=(n_b,),
        in_specs=[blk(256), blk(256), blk(256), _layer_spec((1, 256), layer)],
        out_specs=blk(256),
        scratch_shapes=[pltpu.VMEM((256, 128), F32), pltpu.VMEM((256, 128), F32),
                        pltpu.VMEM((t_all, 256), F32), pltpu.VMEM((t_all, 256), F32)],
        compiler_params=_cp(("parallel",)),
        name="gla_mixer",
    )(qk, v, lg, g_norm)


def _na_kernel(q_ref, kt0_ref, kt1_ref, kt2_ref, v0_ref, v1_ref, v2_ref, ktc_ref, vc_ref, rel_ref, o_ref, *, rows):
    j = pl.program_id(1)
    left = lax.broadcasted_iota(jnp.int32, (1, 128), 1) < NA_DH
    ones_blk = jnp.ones((TILE, 128), BF16)
    ro0 = NA_KR - 1 - NA_ROWS

    def attend(with_window):
        key_tiles = [(ktc_ref, vc_ref, None)]
        if with_window:
            g = j - 1
            q_row = lax.broadcasted_iota(jnp.int32, (TILE, TILE), 0) // GRID_W
            k_row = lax.broadcasted_iota(jnp.int32, (TILE, TILE), 1) // GRID_W
            first = jnp.clip(NA_ROWS * g + q_row - NA_KR // 2, 0, rows - NA_KR) - NA_ROWS * (g - 1)
            for d, (kr, vr) in enumerate(((kt0_ref, v0_ref), (kt1_ref, v1_ref), (kt2_ref, v2_ref))):
                window = None if d == 1 else jnp.where(
                    (k_row + NA_ROWS * d >= first) & (k_row + NA_ROWS * d < first + NA_KR), 0.0, NEG_BIG)
                key_tiles.append((kr, vr, (d, window)))
        def lanes(hh):
            return slice(128 * (hh // 2), 128 * (hh // 2 + 1))

        def score(hh):
            qp = q_ref[0, :, lanes(hh)]
            zero = jnp.zeros_like(qp)
            qh = jnp.where(left, qp, zero) if hh % 2 == 0 else jnp.where(left, zero, qp)
            kt_all = jnp.concatenate([kr[0, lanes(hh), :] for kr, _, _ in key_tiles], axis=1)
            return _dot(qh, kt_all)

        def probs(hh, s):
            if with_window:
                parts = [s[:, 0:TILE]]
                for _, _, (d, window) in key_tiles[1:]:
                    rel = jnp.concatenate(
                        [rel_ref[hh, (ro0 - i) % 2, :, GRID_W * ((ro0 - i) // 2 * 2) + d * TILE:
                                 GRID_W * ((ro0 - i) // 2 * 2) + (d + 1) * TILE] for i in range(NA_ROWS)], axis=0)
                    part = s[:, (d + 1) * TILE:(d + 2) * TILE] + rel
                    parts.append(part if window is None else part + window)
                s = jnp.concatenate(parts, axis=1)
            return jnp.exp(s - jnp.max(s, axis=-1, keepdims=True)).astype(BF16)

        def attend_values(hh, p):
            vaug = jnp.concatenate([jnp.concatenate([vr[0, :, lanes(hh)], ones_blk], axis=1)
                                    for _, vr, _ in key_tiles], axis=0)
            acc = _dot(p, vaug)
            return acc[:, 0:128] / acc[:, 128:256]

        s_next = score(0)
        outs = []
        for hh in range(NA_HEADS):
            s_cur = s_next
            if hh + 1 < NA_HEADS:
                s_next = score(hh + 1)
            outs.append(attend_values(hh, probs(hh, s_cur)))
        for pair in range(NA_HEADS // 2):
            o_pair = jnp.where(left, outs[2 * pair], outs[2 * pair + 1])
            o_ref[0, :, lanes(2 * pair)] = o_pair.astype(o_ref.dtype)

    @pl.when(j == 0)
    def _():
        attend(False)

    @pl.when(j > 0)
    def _():
        attend(True)


def _na(nq, nkt, nv, layer, rel_tab):
    n_b, t_all, _ = nq.shape
    n_tiles = t_all // TILE
    rows = (t_all - CTX_LEN) // GRID_W

    def nbr(delta):
        return lambda b, j: (b, jnp.clip(j + delta, 1, n_tiles - 1), 0)

    def nbr_t(delta):
        return lambda b, j: (b, 0, jnp.clip(j + delta, 1, n_tiles - 1))

    tile = lambda f: pl.BlockSpec((1, TILE, 256), f)
    tile_t = lambda f: pl.BlockSpec((1, 256, TILE), f)
    return pl.pallas_call(
        functools.partial(_na_kernel, rows=rows),
        out_shape=jax.ShapeDtypeStruct((n_b, t_all, 256), BF16),
        grid=(n_b, n_tiles),
        in_specs=[tile(lambda b, j: (b, j, 0)),
                  tile_t(nbr_t(-1)), tile_t(nbr_t(0)), tile_t(nbr_t(1)),
                  tile(nbr(-1)), tile(nbr(0)), tile(nbr(1)),
                  tile_t(lambda b, j: (b, 0, 0)), tile(lambda b, j: (b, 0, 0)),
                  _layer_spec((NA_HEADS, 2, GRID_W, NA_REL_W), layer)],
        out_specs=tile(lambda b, j: (b, j, 0)),
        compiler_params=_cp(("parallel", "arbitrary")),
        name="na_mixer",
    )(nq, nkt, nkt, nkt, nv, nv, nv, nkt, nv, rel_tab)


def _na_rel_table(rpb):
    col = np.arange(GRID_W)
    cs = np.clip(col - NA_KC // 2, 0, GRID_W - NA_KC)
    col_mask = (col[None, :] >= cs[:, None]) & (col[None, :] < cs[:, None] + NA_KC)
    co = np.clip(col[None, :] - col[:, None], -(NA_KC - 1), NA_KC - 1) + (NA_KC - 1)
    onehot = (co[None] == np.arange(2 * NA_KC - 1)[:, None, None]).astype(np.float32)
    toe = jnp.einsum("hrc,cqk->hqrk", rpb.astype(F32), jnp.asarray(onehot),
                     precision=lax.Precision.HIGHEST)
    toe = jnp.where(jnp.asarray(col_mask)[None, :, None, :], toe, NEG_BIG)
    flat = toe.reshape(NA_HEADS, GRID_W, (2 * NA_KR - 1) * GRID_W)
    flat = jnp.pad(flat, ((0, 0), (0, 0), (0, NA_REL_W + GRID_W - flat.shape[-1])), constant_values=NEG_BIG)
    return jnp.stack([flat[:, :, :NA_REL_W], flat[:, :, GRID_W:]], axis=1)


def _s5_kernel(su0_ref, su1_ref, ktoe_ref, bexp_ref, cexp_ref, pw_ref, y0_ref, y1_ref, a_ref, yg_ref, e_ref, x_ref,
               *, n_chunks, n_ctx_chunks):
    blk = 8
    gp = S5_G * S5_P

    def block_transpose(src):
        lane_blk = lax.broadcasted_iota(jnp.int32, src[0].shape, 1) // S5_HG

        def pick(vals):
            acc = vals[0]
            for k in range(1, 8):
                acc = jnp.where(lane_blk == k, vals[k], acc)
            return acc

        rolled = []
        for d in range(8):
            diag = pick([src[(k + d) % 8] for k in range(8)])
            rolled.append(pltpu.roll(diag, S5_HG * d, 1) if d else diag)
        return [pick([rolled[(m - a) % 8] for m in range(8)]) for a in range(8)]

    def regroup_in(rb, carry):
        c0 = pl.multiple_of(rb * 2 * blk, 2 * blk)
        for hg, ref in enumerate((su0_ref, su1_ref)):
            for o in range(2):
                src = [jnp.concatenate(
                    [ref[0, pl.ds((rb * 2 + u) * (blk * S5_T) + o * 8 + m, blk, stride=S5_T), :] for u in range(2)],
                    axis=0).astype(BF16) for m in range(8)]
                for a, val in enumerate(block_transpose(src)):
                    a_ref[hg * 8 + a, pl.ds(c0, 2 * blk), o * 128:(o + 1) * 128] = val
        return carry

    lax.fori_loop(0, n_chunks // (2 * blk), regroup_in, 0)

    for g in range(S5_G):
        e = _dot(a_ref[g], bexp_ref[g])
        for kind in range(4):
            e_ref[kind, :, g * S5_P:(g + 1) * S5_P] = e[:, kind * S5_P:(kind + 1) * S5_P]

    afr, afi, abr, abi = pw_ref[0:1, :], pw_ref[1:2, :], pw_ref[2:3, :], pw_ref[3:4, :]

    def scan_body(i, carry):
        fr, fi, br, bi = carry
        cf = i
        cb = jnp.where(i < n_ctx_chunks, n_ctx_chunks - 1 - i, n_chunks - 1 - (i - n_ctx_chunks))
        x_ref[0, pl.ds(cf, 1), :] = fr
        x_ref[1, pl.ds(cf, 1), :] = fi
        x_ref[2, pl.ds(cb, 1), :] = br
        x_ref[3, pl.ds(cb, 1), :] = bi
        er = e_ref[0, pl.ds(cf, 1), :]
        ei = e_ref[1, pl.ds(cf, 1), :]
        gr = e_ref[2, pl.ds(cb, 1), :]
        gi = e_ref[3, pl.ds(cb, 1), :]
        return (afr * fr - afi * fi + er, afr * fi + afi * fr + ei,
                abr * br - abi * bi + gr, abr * bi + abi * br + gi)

    z = jnp.zeros((1, gp), F32)
    lax.fori_loop(0, n_chunks, scan_body, (z, z, z, z))

    for g in range(S5_G):
        y = _dot(a_ref[g], ktoe_ref[g])
        for kind in range(4):
            y = y + _dot(x_ref[kind, :, g * S5_P:(g + 1) * S5_P].astype(BF16), cexp_ref[g, kind])
        yg_ref[g] = y

    def regroup_out(rb, carry):
        c0 = pl.multiple_of(rb * blk, blk)
        for og, ref in enumerate((y0_ref, y1_ref)):
            for ht in range(2):
                src = [yg_ref[og * 8 + m, pl.ds(c0, blk), ht * 128:(ht + 1) * 128] for m in range(8)]
                for a, val in enumerate(block_transpose(src)):
                    ref[0, pl.ds(rb * (blk * S5_T) + ht * 8 + a, blk, stride=S5_T), :] = val
        return carry

    lax.fori_loop(0, n_chunks // blk, regroup_out, 0)


def _s5(su0, su1, layer, ktoe, bexp, cexp, pw16):
    n_b, t_all, _ = su0.shape
    n_chunks = t_all // S5_T
    gp = S5_G * S5_P
    blk = pl.BlockSpec((1, t_all, 128), lambda b: (b, 0, 0))
    return pl.pallas_call(
        functools.partial(_s5_kernel, n_chunks=n_chunks, n_ctx_chunks=CTX_LEN // S5_T),
        out_shape=[jax.ShapeDtypeStruct((n_b, t_all, 128), F32)] * 2,
        grid=(n_b,),
        in_specs=[blk, blk, _layer_spec((S5_G, 256, 256), layer), _layer_spec((S5_G, 256, 256), layer),
                  _layer_spec((S5_G, 4, S5_P, 256), layer), _layer_spec((4, gp), layer)],
        out_specs=[blk, blk],
        scratch_shapes=[pltpu.VMEM((S5_G, n_chunks, S5_T * S5_HG), BF16),
                        pltpu.VMEM((S5_G, n_chunks, S5_T * S5_HG), F32),
                        pltpu.VMEM((4, n_chunks, gp), F32),
                        pltpu.VMEM((4, n_chunks, gp), F32)],
        compiler_params=_cp(("parallel",)),
        name="s5_mixer",
    )(su0, su1, ktoe, bexp, cexp, pw16)


def _s5_tables(lam_re, lam_im, log_dt, b_re, b_im, c_re, c_im, d_skip):
    hp = lax.Precision.HIGH
    lam_re, lam_im, log_dt = lam_re.astype(F32), lam_im.astype(F32), log_dt.astype(F32)
    b_re, b_im, c_re, c_im = b_re.astype(F32), b_im.astype(F32), c_re.astype(F32), c_im.astype(F32)
    dt = jnp.exp(log_dt)[..., None]
    n = jnp.arange(S5_T + 1, dtype=F32)[:, None, None, None]
    pw_re = jnp.exp(n * lam_re * dt) * jnp.cos(n * lam_im * dt)
    pw_im = jnp.exp(n * lam_re * dt) * jnp.sin(n * lam_im * dt)
    neg_re = jnp.exp(-n * lam_re * dt) * jnp.cos(n * lam_im * dt)
    neg_im = -jnp.exp(-n * lam_re * dt) * jnp.sin(n * lam_im * dt)
    m = jnp.arange(S5_T, -1, -1, dtype=F32)[:, None, None, None]
    dsc_re = jnp.exp(m * lam_re * dt) * jnp.cos(m * lam_im * dt)
    dsc_im = jnp.exp(m * lam_re * dt) * jnp.sin(m * lam_im * dt)
    lb_re, lb_im = pw_re[1], pw_im[1]
    num_re, num_im = lb_re - 1.0, lb_im
    den = lam_re * lam_re + lam_im * lam_im
    coef_re = ((num_re * lam_re + num_im * lam_im) / den)[..., None]
    coef_im = ((num_im * lam_re - num_re * lam_im) / den)[..., None]
    bb_re = coef_re * b_re - coef_im * b_im
    bb_im = coef_re * b_im + coef_im * b_re

    def cmul_b(p_re, p_im, d):
        re = p_re[..., None] * bb_re[d][None] - p_im[..., None] * bb_im[d][None]
        im = p_re[..., None] * bb_im[d][None] + p_im[..., None] * bb_re[d][None]
        to = lambda z: z.transpose(1, 0, 3, 2).reshape(S5_G, S5_T * S5_HG, S5_P)
        return to(re), to(im)

    def cmul_c(p_re, p_im, d):
        re = c_re[d][None] * p_re[:, :, None, :] - c_im[d][None] * p_im[:, :, None, :]
        im = c_re[d][None] * p_im[:, :, None, :] + c_im[d][None] * p_re[:, :, None, :]
        to = lambda z: z.transpose(1, 3, 0, 2).reshape(S5_G, S5_P, S5_T * S5_HG)
        return to(re), to(-im)

    def lag_operator(u, v):
        return jnp.einsum("gap,gpb->gab", jnp.concatenate(u, axis=-1), jnp.concatenate(v, axis=1), precision=hp)

    first = slice(0, S5_T)
    kf = lag_operator(cmul_b(neg_re[first, 0], neg_im[first, 0], 0), cmul_c(pw_re[first, 0], pw_im[first, 0], 0))
    kb = lag_operator(cmul_b(pw_re[first, 1], pw_im[first, 1], 1), cmul_c(neg_re[first, 1], neg_im[first, 1], 1))
    tok = np.arange(S5_T * S5_HG) // S5_HG
    skip = jnp.asarray(np.eye(S5_T * S5_HG, dtype=np.float32)) * jnp.tile(
        d_skip.astype(F32).reshape(S5_G, S5_HG), (1, S5_T))[:, None, :]
    ktoe = (jnp.where(jnp.asarray(tok[None, :] >= tok[:, None]), kf, 0.0)
            + jnp.where(jnp.asarray(tok[:, None] >= tok[None, :]), kb, 0.0) + skip)
    last = slice(1, S5_T + 1)
    bexp = jnp.concatenate(cmul_b(dsc_re[last, 0], dsc_im[last, 0], 0)
                           + cmul_b(pw_re[first, 1], pw_im[first, 1], 1), axis=-1)
    cexp = jnp.stack(cmul_c(pw_re[last, 0], pw_im[last, 0], 0)
                     + cmul_c(dsc_re[first, 1], dsc_im[first, 1], 1), axis=1)
    pw16 = jnp.stack([pw_re[S5_T, 0].reshape(-1), pw_im[S5_T, 0].reshape(-1),
                      pw_re[S5_T, 1].reshape(-1), pw_im[S5_T, 1].reshape(-1)], axis=0)
    return ktoe.astype(BF16), bexp.astype(BF16), cexp.astype(BF16), pw16


def _pool_kernel(u_ref, w_ref, scale_ref, o_ref, pad_ref, wblk_ref, *, n_lat):
    assert POOL_WINDOWS == (2, 4, 8, 16) and POOL_GW * len(POOL_WINDOWS) == GROUP_W
    pad = max(POOL_WINDOWS) // 2
    seg = ((pad, 0, CTX_LEN), (pad + CTX_LEN + 2 * pad, CTX_LEN, n_lat))
    pad_ref[...] = jnp.zeros_like(pad_ref)
    for p0, u0, n in seg:
        pad_ref[p0:p0 + n, :] = u_ref[0, u0:u0 + n, :]
    wblk_ref[...] = jnp.zeros_like(wblk_ref)
    for i in range(len(POOL_WINDOWS)):
        wblk_ref[i * POOL_GW:(i + 1) * POOL_GW, i * POOL_GW:(i + 1) * POOL_GW] = w_ref[i].astype(BF16)
    grp = lax.broadcasted_iota(jnp.int32, (1, GROUP_W), 1) // POOL_GW
    width = jnp.left_shift(2, grp)
    rows = TILE + 2 * pad

    def up(a, k):
        return pltpu.roll(a, rows - k, 0)

    for p0, u0, n in seg:
        for t0 in range(0, n, TILE):
            w0 = pad_ref[p0 + t0 - pad:p0 + t0 - pad + rows, :]
            a1 = w0 + up(w0, 1)
            a2 = a1 + up(a1, 2)
            a3 = a2 + up(a2, 4)
            a4 = a3 + up(a3, 8)
            total = jnp.where(grp == 0, up(a1, pad - 1), jnp.where(grp == 1, up(a2, pad - 2),
                              jnp.where(grp == 2, up(a3, pad - 4), a4)))[0:TILE]
            if t0 == 0 or t0 + TILE == n:
                t = t0 + lax.broadcasted_iota(jnp.int32, (TILE, 1), 0)
                first = t - width // 2
                cnt = (jnp.clip(first + width, 0, n) - jnp.clip(first, 0, n)).astype(F32)
            else:
                cnt = width.astype(F32)
            diff = total / cnt - w0[pad:pad + TILE]
            o_ref[0, u0 + t0:u0 + t0 + TILE, :] = (_dot(diff.astype(BF16), wblk_ref[...])
                                                   * scale_ref[...]).astype(o_ref.dtype)


def _pool(pu, layer, w_pool, scale):
    n_b, t_all, _ = pu.shape
    blk = pl.BlockSpec((1, t_all, 256), lambda b: (b, 0, 0))
    return pl.pallas_call(
        functools.partial(_pool_kernel, n_lat=t_all - CTX_LEN),
        out_shape=jax.ShapeDtypeStruct((n_b, t_all, 256), BF16),
        grid=(n_b,),
        in_specs=[blk, _layer_spec((len(POOL_WINDOWS), POOL_GW, POOL_GW), layer), _layer_spec((1, 256), layer)],
        out_specs=blk,
        scratch_shapes=[pltpu.VMEM((t_all + 32, 256), F32), pltpu.VMEM((GROUP_W, GROUP_W), BF16)],
        compiler_params=_cp(("parallel",)),
        name="pool_mixer",
    )(pu, w_pool, scale)


def _gelu_tanh(x):
    return 0.5 * x * (1.0 + jnp.tanh(math.sqrt(2.0 / math.pi) * (x + 0.044715 * (x * x * x))))


def _outproj_res(res, mod_row, ygla_ref, yna_ref, ys50_ref, ys51_ref, ypool_ref, gate_ref, mod_ref, wout_ref,
                 gpost_ref, wglu_ref, bglu_ref):
    def gated_proj(part, i):
        lo, hi = i * GROUP_W, (i + 1) * GROUP_W
        gc = gate_ref[0, :, lo:hi].astype(F32)
        return _dot((part * (gc * jax.nn.sigmoid(gc))).astype(BF16), wout_ref[lo:hi, :])

    gate_m = mod_ref[pl.ds(mod_row, 1), :][:, 2 * D_MODEL:3 * D_MODEL]
    g5 = _gelu_tanh(jnp.concatenate([ys50_ref[0], ys51_ref[0]], axis=-1))
    z5 = _dot(g5.astype(BF16), wglu_ref[...])
    acc = gated_proj(ygla_ref[0], 0) + gated_proj(yna_ref[0], 1) + gated_proj(ypool_ref[0], 3)
    acc = acc + gated_proj(g5 * jax.nn.sigmoid(z5 + bglu_ref[...]), 2)
    ms = jnp.mean(acc * acc, axis=-1, keepdims=True)
    return res + gate_m * (acc * lax.rsqrt(ms + NORM_EPS) * gpost_ref[...])


N_MIX_REFS = 6
N_OUT_PARAM_REFS = 5
N_IN_PARAM_REFS = 9


def _outproj_kernel(*refs):
    x_ref, params, out_ref = refs[2 * N_MIX_REFS], refs[2 * N_MIX_REFS + 1:-1], refs[-1]
    for half in range(2):
        mix = refs[half * N_MIX_REFS:(half + 1) * N_MIX_REFS]
        rows = slice(half * TILE, (half + 1) * TILE)
        out_ref[0, rows, :] = _outproj_res(x_ref[0, rows, :], pl.program_id(0), *mix, *params)


def _outproj_inproj_kernel(*refs, n_batch):
    mix = refs[:N_MIX_REFS]
    x_ref, xc_ref = refs[N_MIX_REFS:N_MIX_REFS + 2]
    out_params = refs[N_MIX_REFS + 2:N_MIX_REFS + 2 + N_OUT_PARAM_REFS]
    rest = refs[N_MIX_REFS + 2 + N_OUT_PARAM_REFS:]
    in_params, (xo_ref, xco_ref), in_outs = rest[:N_IN_PARAM_REFS], rest[N_IN_PARAM_REFS:N_IN_PARAM_REFS + 2], \
        rest[N_IN_PARAM_REFS + 2:]
    b = pl.program_id(0)
    j = pl.program_id(1)

    def body(res_ref, out_ref, mod_row, with_rope):
        res = _outproj_res(res_ref[0], mod_row, *mix, *out_params)
        out_ref[0] = res
        _inproj_body(res, mod_row, with_rope, *in_params, *in_outs)

    pl.when(j == 0)(lambda: body(xc_ref, xco_ref, n_batch, False))
    pl.when(j > 0)(lambda: body(x_ref, xo_ref, b, True))


def _outproj_param_specs(d, layer):
    return [_layer_spec((8, 3 * d), layer), _layer_spec((MIX_W, d), layer), _layer_spec((1, d), layer),
            _layer_spec((GROUP_W, GROUP_W), layer), _layer_spec((1, GROUP_W), layer)]


def _mix_specs(index_map):
    return [pl.BlockSpec((1, TILE, w), index_map) for w in (256, 256, 128, 128, 256, MIX_W)]


def _outproj(mix, x, layer, mod_all, w_out, g_post, w_glu, b_glu):
    n_b, n_lat, d = x.shape
    assert n_lat % (2 * TILE) == 0
    pair = pl.BlockSpec((1, 2 * TILE, d), lambda b, j: (b, j, 0))
    return pl.pallas_call(
        _outproj_kernel,
        out_shape=jax.ShapeDtypeStruct(x.shape, F32),
        grid=(n_b, n_lat // (2 * TILE)),
        in_specs=_mix_specs(lambda b, j: (b, 2 * j + 1, 0)) + _mix_specs(lambda b, j: (b, 2 * j + 2, 0)) + [pair]
                 + _outproj_param_specs(d, layer),
        out_specs=pair,
        compiler_params=_cp(("parallel", "arbitrary")),
        name="gate_outproj_residual",
    )(*mix, *mix, x, mod_all, w_out, g_post, w_glu, b_glu)


def _outproj_inproj(mix, x, xc, layer, mod_all, w_out, g_post, w_glu, b_glu, g_pre, w_pad, wg, bg, rope_tabs):
    n_b, n_lat, d = x.shape
    t_all = n_lat + CTX_LEN
    in_specs, in_shape, in_out_specs = _inproj_specs(n_b, t_all, d, layer + 1)
    outs = pl.pallas_call(
        functools.partial(_outproj_inproj_kernel, n_batch=n_b),
        out_shape=[jax.ShapeDtypeStruct(x.shape, F32), jax.ShapeDtypeStruct(xc.shape, F32)] + in_shape,
        grid=(n_b, t_all // TILE),
        in_specs=_mix_specs(lambda b, j: (b, j, 0))
                 + [pl.BlockSpec((1, TILE, d), _lat_map), pl.BlockSpec((1, CTX_LEN, d), _ctx_map)]
                 + _outproj_param_specs(d, layer) + in_specs,
        out_specs=[pl.BlockSpec((1, TILE, d), _lat_map), pl.BlockSpec((1, CTX_LEN, d), _ctx_map)] + in_out_specs,
        compiler_params=_cp(("parallel", "arbitrary")),
        name="outproj_then_inproj",
    )(*mix, x, xc, mod_all, w_out, g_post, w_glu, b_glu, mod_all, g_pre, *w_pad, wg, bg, *rope_tabs)
    return outs[0], outs[1], outs[2:]


def _rope_tables(n_tok):
    t = np.arange(n_tok)
    half = GLA_DK // 2
    freqs = ROPE_BASE ** (-np.arange(0, half, 2, dtype=np.float32) / half)
    d = np.arange(GLA_DK)
    pos = np.where((d // half)[None, :] == 0, (t // GRID_W)[:, None], (t % GRID_W)[:, None]).astype(np.float32)
    ang = jnp.asarray(pos) * jnp.asarray(freqs[(d % half) % (half // 2)])[None, :]
    first = jnp.asarray(((d % half) < half // 2)[None, :])
    cos, sin = jnp.cos(ang), jnp.sin(ang)
    tabs = (cos, jnp.where(first, -sin, 0.0), jnp.where(first, 0.0, sin))
    return tuple(jnp.tile(z, (1, GLA_HEADS)).astype(F32) for z in tabs)


def _split_w_in(w_in):
    head = jnp.pad(w_in[:, :, :N_GATE_END], ((0, 0), (0, 0), (0, N_HEAD_COLS - N_GATE_END)))
    return head.astype(BF16), w_in[:, :, N_GATE_END:].astype(BF16)


def _gate_weights(w_gate, b_gate):
    n_l = w_gate.shape[0]
    wg = jnp.zeros((n_l, 128, 256), F32)
    wg = wg.at[:, 0:GLA_RANK, 0:128].set(w_gate[:, 0].astype(F32))
    wg = wg.at[:, GLA_RANK:2 * GLA_RANK, 128:256].set(w_gate[:, 1].astype(F32))
    return wg.astype(BF16), b_gate.astype(F32).reshape(n_l, 1, 256)


def kernel(x, c, ctx, c_ctx, w_mod, b_mod, g_pre, g_post, w_in, w_out, gla_w_gate, gla_b_gate, gla_g_norm, na_rpb,
           s5_lam_re, s5_lam_im, s5_log_dt, s5_b_re, s5_b_im, s5_c_re, s5_c_im, s5_d, s5_w_glu, s5_b_glu,
           pool_w, pool_scale):
    n_b, n_lat, d = x.shape
    depth = w_mod.shape[0]
    rows = n_lat // GRID_W
    assert d == D_MODEL and ctx.shape[1] == CTX_LEN and n_lat % TILE == 0 and n_b < 8 and rows >= NA_KR
    c_rows = jnp.zeros((8, d), F32).at[0:n_b].set(c).at[n_b].set(c_ctx)
    mod_all = _modulation(c_rows, w_mod, b_mod)
    rope_tabs = _rope_tables(n_lat)
    w_pad = _split_w_in(w_in)
    wg, bg = _gate_weights(gla_w_gate, gla_b_gate)
    g_pre3 = g_pre.astype(F32).reshape(depth, 1, d)
    g_post3 = g_post.astype(F32).reshape(depth, 1, d)
    gn = jnp.tile(gla_g_norm.astype(F32), (1, GLA_HEADS)).reshape(depth, 1, GLA_HEADS * GLA_DV)
    rel_tab = jax.vmap(_na_rel_table)(na_rpb)
    ktoe, bexp, cexp, pw16 = jax.vmap(_s5_tables)(s5_lam_re, s5_lam_im, s5_log_dt, s5_b_re, s5_b_im,
                                                  s5_c_re, s5_c_im, s5_d)
    w_out_b = w_out.astype(BF16)
    w_glu_b = s5_w_glu.astype(BF16)
    b_glu3 = s5_b_glu.astype(F32).reshape(depth, 1, GROUP_W)
    pool_scale3 = pool_scale.astype(F32).reshape(depth, 1, GROUP_W)
    xc = ctx
    proj = _inproj(x, xc, 0, mod_all, g_pre3, w_pad, wg, bg, rope_tabs)
    for l in range(depth):
        qk, gv, lg, nq, nkt, nv, su0, su1, pu, gate_cols = proj
        y_gla = _gla(qk, gv, lg, l, gn)
        y_na = _na(nq, nkt, nv, l, rel_tab)
        y_s50, y_s51 = _s5(su0, su1, l, ktoe, bexp, cexp, pw16)
        y_pool = _pool(pu, l, pool_w, pool_scale3)
        mix = (y_gla, y_na, y_s50, y_s51, y_pool, gate_cols)
        if l == depth - 1:
            x = _outproj(mix, x, l, mod_all, w_out_b, g_post3, w_glu_b, b_glu3)
        else:
            x, xc, proj = _outproj_inproj(mix, x, xc, l, mod_all, w_out_b, g_post3, w_glu_b, b_glu3,
                                          g_pre3, w_pad, wg, bg, rope_tabs)
    return x
```

```python
import functools
import math

import numpy as np
import jax
import jax.numpy as jnp
from jax import lax
from jax.experimental import pallas as pl
from jax.experimental.pallas import tpu as pltpu

F32 = jnp.float32
BF16 = jnp.bfloat16

D_MODEL = 1024
GRID_W = 64
CTX_LEN = 256
GROUP_W = 256
MIX_W = 1024
NORM_EPS = 1e-6
GLA_HEADS = 4
GLA_DV = 64
GLA_DK = 32
GLA_RANK = 16
GLA_TAU = 16.0
ROPE_BASE = 10000.0
NA_HEADS = 4
NA_DH = 64
NA_KR = 8
NA_KC = 16
S5_HG = 16
S5_G = 16
S5_P = 64
POOL_WINDOWS = (2, 4, 8, 16)
POOL_GW = 64

TILE = 256
GLA_C = 128
GLA_STEP = 2
S5_T = 16
NA_ROWS = TILE // GRID_W
NA_REL_W = 1024
NEG_BIG = -1e30

_COL = {}
_off = 0
for _name, _w in (("gk", 128), ("gv", 256), ("gg", 128), ("nk", 256), ("nv", 256), ("su", 256),
                  ("gq", 128), ("nq", 256), ("pu", 256), ("gate", 1024)):
    _COL[_name] = (_off, _off + _w)
    _off += _w
N_PAD_COLS = _off
N_GATE_END = 128 + 256 + 2 * GLA_RANK
N_HEAD_COLS = _COL["gg"][1]

_VMEM_LIMIT = 56 * 1024 * 1024


def _cp(sem, vmem=_VMEM_LIMIT):
    return pltpu.CompilerParams(dimension_semantics=sem, vmem_limit_bytes=vmem)


def _dot(a, b):
    return jnp.dot(a, b, preferred_element_type=F32)


def _dot_nt(a, b):
    return lax.dot_general(a, b, (((1,), (1,)), ((), ())), preferred_element_type=F32)


def _split_bf16(a):
    hi = a.astype(BF16)
    lo = (a - hi.astype(F32)).astype(BF16)
    return hi, lo


def _mod_kernel(s_ref, w_ref, b_ref, o_ref):
    s = s_ref[...]
    a = (s * jax.nn.sigmoid(s)).astype(BF16)
    o_ref[0] = _dot(a, w_ref[0].astype(BF16)) + b_ref[0]


def _modulation(c_rows, w_mod, b_mod):
    n_l, d, d3 = w_mod.shape
    return pl.pallas_call(
        _mod_kernel,
        out_shape=jax.ShapeDtypeStruct((n_l, 8, d3), F32),
        grid=(n_l, d3 // d),
        in_specs=[pl.BlockSpec((8, d), lambda l, j: (0, 0)),
                  pl.BlockSpec((1, d, d), lambda l, j: (l, 0, j)),
                  pl.BlockSpec((1, 1, d), lambda l, j: (l, 0, j))],
        out_specs=pl.BlockSpec((1, 8, d), lambda l, j: (l, 0, j)),
        compiler_params=_cp(("parallel", "parallel")),
        name="adaln_modulation",
    )(c_rows, w_mod, b_mod.reshape(n_l, 1, d3))


def _inproj_body(xt, mod_row, with_rope, mod_ref, gpre_ref, wa_ref, wb_ref, wg_ref, bg_ref, cos_ref, sa_ref, sb_ref,
                 qk_ref, v_ref, lg_ref, nq_ref, nk_ref, nv_ref, su0_ref, su1_ref, pu_ref, gate_ref):
    m = mod_ref[pl.ds(mod_row, 1), :]
    shift = m[:, 0:D_MODEL]
    scale = m[:, D_MODEL:2 * D_MODEL]
    ms = jnp.mean(xt * xt, axis=-1, keepdims=True)
    h = xt * lax.rsqrt(ms + NORM_EPS) * gpre_ref[...] * (1.0 + scale) + shift
    hb = h.astype(BF16)

    def proj(name):
        lo, hi = _COL[name]
        if hi <= N_HEAD_COLS:
            return _dot(hb, wa_ref[:, lo:hi])
        return _dot(hb, wb_ref[:, lo - N_HEAD_COLS:hi - N_HEAD_COLS])

    def rope(t):
        if not with_rope:
            return t
        return t * cos_ref[...] + pltpu.roll(t, 128 - 8, 1) * sa_ref[...] + pltpu.roll(t, 8, 1) * sb_ref[...]

    qk_ref[0, :, 0:128] = rope(proj("gq")) * (GLA_DK ** -0.5)
    qk_ref[0, :, 128:256] = rope(proj("gk"))
    v_ref[0] = proj("gv")
    z = _dot(proj("gg").astype(BF16), wg_ref[...]) + bg_ref[...]
    lg_ref[0] = (jnp.minimum(z, 0.0) - jnp.log(1.0 + jnp.exp(-jnp.abs(z)))) * (1.0 / GLA_TAU)
    nq_ref[0] = (proj("nq") * (NA_DH ** -0.5)).astype(BF16)
    nk_ref[0] = proj("nk").T.astype(BF16)
    nv_ref[0] = proj("nv").astype(BF16)
    su = proj("su")
    su0_ref[0] = su[:, 0:128]
    su1_ref[0] = su[:, 128:256]
    pu_ref[0] = proj("pu")
    gate_ref[0] = proj("gate").astype(BF16)


def _inproj_kernel(x_ref, xc_ref, *refs, n_batch):
    b = pl.program_id(0)
    j = pl.program_id(1)
    pl.when(j == 0)(lambda: _inproj_body(xc_ref[0], n_batch, False, *refs))
    pl.when(j > 0)(lambda: _inproj_body(x_ref[0], b, True, *refs))


def _layer_spec(shape, layer):
    return pl.BlockSpec((None,) + tuple(shape), lambda *_: (layer,) + (0,) * len(shape))


def _lat_map(b, j):
    return (b, jnp.maximum(j - 1, 0), 0)


def _ctx_map(b, j):
    return (b, 0, 0)


def _inproj_specs(n_b, t_all, d, layer):
    tab_map = lambda b, j: (jnp.maximum(j - 1, 0), 0)
    outs = (("qk", 256, F32), ("gv", 256, F32), ("lg", 256, F32), ("nq", 256, BF16), ("nkT", 256, BF16),
            ("nv", 256, BF16), ("su0", 128, F32), ("su1", 128, F32), ("pu", 256, F32), ("gate", 1024, BF16))
    out_shape = [jax.ShapeDtypeStruct((n_b, w, t_all) if n == "nkT" else (n_b, t_all, w), dt) for n, w, dt in outs]
    out_specs = [pl.BlockSpec((1, w, TILE), lambda b, j: (b, 0, j)) if n == "nkT"
                 else pl.BlockSpec((1, TILE, w), lambda b, j: (b, j, 0)) for n, w, dt in outs]
    in_specs = [_layer_spec((8, 3 * d), layer),
                _layer_spec((1, d), layer),
                _layer_spec((d, N_HEAD_COLS), layer),
                _layer_spec((d, N_PAD_COLS - N_HEAD_COLS), layer),
                _layer_spec((128, 256), layer),
                _layer_spec((1, 256), layer),
                pl.BlockSpec((TILE, 128), tab_map),
                pl.BlockSpec((TILE, 128), tab_map),
                pl.BlockSpec((TILE, 128), tab_map)]
    return in_specs, out_shape, out_specs


def _inproj(x, xc, layer, mod_all, g_pre, w_pad, wg, bg, rope_tabs):
    n_b, n_lat, d = x.shape
    t_all = n_lat + CTX_LEN
    in_specs, out_shape, out_specs = _inproj_specs(n_b, t_all, d, layer)
    return pl.pallas_call(
        functools.partial(_inproj_kernel, n_batch=n_b),
        out_shape=out_shape,
        grid=(n_b, t_all // TILE),
        in_specs=[pl.BlockSpec((1, TILE, d), _lat_map), pl.BlockSpec((1, CTX_LEN, d), _ctx_map)] + in_specs,
        out_specs=out_specs,
        compiler_params=_cp(("parallel", "arbitrary")),
        name="modnorm_inproj",
    )(x, xc, mod_all, g_pre, *w_pad, wg, bg, *rope_tabs)


def _gla_kernel(qk_ref, v_ref, lg_ref, gn_ref, y_ref, stf_ref, stb_ref, of_ref, ob_ref, *, n_chunks, n_ctx_chunks):
    c_len = GLA_C
    row = lax.broadcasted_iota(jnp.int32, (c_len, c_len), 0)
    col = lax.broadcasted_iota(jnp.int32, (c_len, c_len), 1)
    tri_f = (col <= row).astype(BF16)
    tri_b = (col >= row).astype(BF16)
    arow = lax.broadcasted_iota(jnp.int32, (c_len, 4 * c_len), 0)
    acol = lax.broadcasted_iota(jnp.int32, (c_len, 4 * c_len), 1) % c_len
    amask_f = acol <= arow
    amask_b = acol >= arow
    head_k = lax.broadcasted_iota(jnp.int32, (1, 128), 1) // GLA_DK
    head_v = lax.broadcasted_iota(jnp.int32, (1, 256), 1) // GLA_DV
    bd_mask = (lax.broadcasted_iota(jnp.int32, (256, 128), 0) // GLA_DV
               == lax.broadcasted_iota(jnp.int32, (256, 128), 1) // GLA_DK)
    ones_bd = jnp.where(lax.broadcasted_iota(jnp.int32, (256, 256), 0) // GLA_DV
                        == lax.broadcasted_iota(jnp.int32, (256, 256), 1) // GLA_DV,
                        1.0 / GLA_DV, 0.0).astype(BF16)
    mid = c_len // 2

    def scan_body(i, carry):
        def bwd_chunk(pos):
            return jnp.where(pos < n_ctx_chunks, n_ctx_chunks - 1 - pos, n_chunks - 1 - (pos - n_ctx_chunks))

        dirs = [d for d in (0, 1) for _ in range(GLA_STEP)]
        chunks = [i * GLA_STEP + u for u in range(GLA_STEP)] + [bwd_chunk(i * GLA_STEP + u) for u in range(GLA_STEP)]
        jobs = range(len(dirs))
        r0 = [pl.multiple_of(c * c_len, c_len) for c in chunks]
        st_refs = (stf_ref, stb_ref)
        tri = (tri_f, tri_b)
        amask = (amask_f, amask_b)
        end = (c_len - 1, 0)
        q = [qk_ref[0, pl.ds(r0[n], c_len), 0:128] for n in jobs]
        k = [qk_ref[0, pl.ds(r0[n], c_len), 128:256] for n in jobs]
        v = [v_ref[0, pl.ds(r0[n], c_len), :] for n in jobs]
        lg = [_split_bf16(lg_ref[0, pl.ds(r0[n], c_len), 128 * dirs[n]:128 * (dirs[n] + 1)]) for n in jobs]
        cum = [_dot(tri[dirs[n]], lg[n][0]) + _dot(tri[dirs[n]], lg[n][1]) for n in jobs]
        c_ref = [cum[n][mid:mid + 1, :] for n in jobs]
        c_end = [cum[n][end[dirs[n]]:end[dirs[n]] + 1, :] for n in jobs]
        qt = [q[n] * jnp.exp(cum[n] - c_ref[n]) for n in jobs]
        kt = [k[n] * jnp.exp(c_ref[n] - cum[n]) for n in jobs]
        ks = [jnp.concatenate([jnp.where(head_k == hh, kt[n], 0.0) for hh in range(GLA_HEADS)],
                              axis=0).astype(BF16) for n in jobs]
        att = [_dot_nt(qt[n].astype(BF16), ks[n]) for n in jobs]
        qh = [(qt[n] * jnp.exp(c_ref[n])).astype(BF16) for n in jobs]
        kh = [(kt[n] * jnp.exp(c_end[n] - c_ref[n])).astype(BF16) for n in jobs]
        upd = [_dot(v[n].T.astype(BF16), kh[n]) for n in jobs]
        att = [jnp.where(amask[dirs[n]], att[n], 0.0).astype(BF16) for n in jobs]
        vb = [v[n].astype(BF16) for n in jobs]
        vs = [jnp.concatenate([jnp.where(head_v == hh, vb[n], jnp.zeros_like(vb[n])) for hh in range(GLA_HEADS)],
                              axis=0) for n in jobs]
        o_intra = [_dot(att[n], vs[n]) for n in jobs]
        outs = (of_ref, ob_ref)
        st = [st_refs[d][...] for d in (0, 1)]
        for n in jobs:
            d = dirs[n]
            outs[d][pl.ds(r0[n], c_len), :] = o_intra[n] + _dot_nt(qh[n], st[d].astype(BF16))
            st[d] = st[d] * jnp.exp(c_end[n]) + jnp.where(bd_mask, upd[n], 0.0)
        for d in (0, 1):
            st_refs[d][...] = st[d]
        return carry

    def norm_body(c, carry):
        r0 = pl.multiple_of(c * TILE, TILE)
        o = of_ref[pl.ds(r0, TILE), :] + ob_ref[pl.ds(r0, TILE), :]
        ms = _dot((o * o).astype(BF16), ones_bd)
        y_ref[0, pl.ds(r0, TILE), :] = (o * lax.rsqrt(ms + NORM_EPS) * gn_ref[...]).astype(y_ref.dtype)
        return carry

    stf_ref[...] = jnp.zeros_like(stf_ref)
    stb_ref[...] = jnp.zeros_like(stb_ref)
    lax.fori_loop(0, n_chunks // GLA_STEP, scan_body, 0)
    lax.fori_loop(0, n_chunks * c_len // TILE, norm_body, 0)


def _gla(qk, v, lg, layer, g_norm):
    n_b, t_all, _ = qk.shape
    blk = lambda w: pl.BlockSpec((1, t_all, w), lambda b: (b, 0, 0))
    return pl.pallas_call(
        functools.partial(_gla_kernel, n_chunks=t_all // GLA_C, n_ctx_chunks=CTX_LEN // GLA_C),
        out_shape=jax.ShapeDtypeStruct((n_b, t_all, 256), BF16),
        grid=(n_b,),
        in_specs=[blk(256), blk(256), blk(256), _layer_spec((1, 256), layer)],
        out_specs=blk(256),
        scratch_shapes=[pltpu.VMEM((256, 128), F32), pltpu.VMEM((256, 128), F32),
                        pltpu.VMEM((t_all, 256), F32), pltpu.VMEM((t_all, 256), F32)],
        compiler_params=_cp(("parallel",)),
        name="gla_mixer",
    )(qk, v, lg, g_norm)


def _na_kernel(q_ref, kt0_ref, kt1_ref, kt2_ref, v0_ref, v1_ref, v2_ref, ktc_ref, vc_ref, rel_ref, o_ref, *, rows):
    j = pl.program_id(1)
    left = lax.broadcasted_iota(jnp.int32, (1, 128), 1) < NA_DH
    ones_blk = jnp.ones((TILE, 128), BF16)
    ro0 = NA_KR - 1 - NA_ROWS

    def attend(with_window):
        key_tiles = [(ktc_ref, vc_ref, None)]
        if with_window:
            g = j - 1
            q_row = lax.broadcasted_iota(jnp.int32, (TILE, TILE), 0) // GRID_W
            k_row = lax.broadcasted_iota(jnp.int32, (TILE, TILE), 1) // GRID_W
            first = jnp.clip(NA_ROWS * g + q_row - NA_KR // 2, 0, rows - NA_KR) - NA_ROWS * (g - 1)
            for d, (kr, vr) in enumerate(((kt0_ref, v0_ref), (kt1_ref, v1_ref), (kt2_ref, v2_ref))):
                window = None if d == 1 else jnp.where(
                    (k_row + NA_ROWS * d >= first) & (k_row + NA_ROWS * d < first + NA_KR), 0.0, NEG_BIG)
                key_tiles.append((kr, vr, (d, window)))
        def lanes(hh):
            return slice(128 * (hh // 2), 128 * (hh // 2 + 1))

        def score(hh):
            qp = q_ref[0, :, lanes(hh)]
            zero = jnp.zeros_like(qp)
            qh = jnp.where(left, qp, zero) if hh % 2 == 0 else jnp.where(left, zero, qp)
            kt_all = jnp.concatenate([kr[0, lanes(hh), :] for kr, _, _ in key_tiles], axis=1)
            return _dot(qh, kt_all)

        def probs(hh, s):
            if with_window:
                parts = [s[:, 0:TILE]]
                for _, _, (d, window) in key_tiles[1:]:
                    rel = jnp.concatenate(
                        [rel_ref[hh, (ro0 - i) % 2, :, GRID_W * ((ro0 - i) // 2 * 2) + d * TILE:
                                 GRID_W * ((ro0 - i) // 2 * 2) + (d + 1) * TILE] for i in range(NA_ROWS)], axis=0)
                    part = s[:, (d + 1) * TILE:(d + 2) * TILE] + rel
                    parts.append(part if window is None else part + window)
                s = jnp.concatenate(parts, axis=1)
            return jnp.exp(s - jnp.max(s, axis=-1, keepdims=True)).astype(BF16)

        def attend_values(pair, p_even, p_odd):
            vaug = jnp.concatenate([jnp.concatenate([vr[0, :, lanes(2 * pair)], ones_blk], axis=1)
                                    for _, vr, _ in key_tiles], axis=0)
            acc = _dot(jnp.concatenate([p_even, p_odd], axis=0), vaug)
            o = acc[:, 0:128] / acc[:, 128:256]
            o_ref[0, :, lanes(2 * pair)] = jnp.where(left, o[0:TILE], o[TILE:2 * TILE]).astype(o_ref.dtype)

        s_next = score(0)
        p = []
        for hh in range(NA_HEADS):
            s_cur = s_next
            if hh + 1 < NA_HEADS:
                s_next = score(hh + 1)
            p.append(probs(hh, s_cur))
            if hh % 2 == 1:
                attend_values(hh // 2, p[hh - 1], p[hh])

    @pl.when(j == 0)
    def _():
        attend(False)

    @pl.when(j > 0)
    def _():
        attend(True)


def _na(nq, nkt, nv, layer, rel_tab):
    n_b, t_all, _ = nq.shape
    n_tiles = t_all // TILE
    rows = (t_all - CTX_LEN) // GRID_W

    def nbr(delta):
        return lambda b, j: (b, jnp.clip(j + delta, 1, n_tiles - 1), 0)

    def nbr_t(delta):
        return lambda b, j: (b, 0, jnp.clip(j + delta, 1, n_tiles - 1))

    tile = lambda f: pl.BlockSpec((1, TILE, 256), f)
    tile_t = lambda f: pl.BlockSpec((1, 256, TILE), f)
    return pl.pallas_call(
        functools.partial(_na_kernel, rows=rows),
        out_shape=jax.ShapeDtypeStruct((n_b, t_all, 256), BF16),
        grid=(n_b, n_tiles),
        in_specs=[tile(lambda b, j: (b, j, 0)),
                  tile_t(nbr_t(-1)), tile_t(nbr_t(0)), tile_t(nbr_t(1)),
                  tile(nbr(-1)), tile(nbr(0)), tile(nbr(1)),
                  tile_t(lambda b, j: (b, 0, 0)), tile(lambda b, j: (b, 0, 0)),
                  _layer_spec((NA_HEADS, 2, GRID_W, NA_REL_W), layer)],
        out_specs=tile(lambda b, j: (b, j, 0)),
        compiler_params=_cp(("parallel", "arbitrary")),
        name="na_mixer",
    )(nq, nkt, nkt, nkt, nv, nv, nv, nkt, nv, rel_tab)


def _na_rel_table(rpb):
    col = np.arange(GRID_W)
    cs = np.clip(col - NA_KC // 2, 0, GRID_W - NA_KC)
    col_mask = (col[None, :] >= cs[:, None]) & (col[None, :] < cs[:, None] + NA_KC)
    co = np.clip(col[None, :] - col[:, None], -(NA_KC - 1), NA_KC - 1) + (NA_KC - 1)
    onehot = (co[None] == np.arange(2 * NA_KC - 1)[:, None, None]).astype(np.float32)
    toe = jnp.einsum("hrc,cqk->hqrk", rpb.astype(F32), jnp.asarray(onehot),
                     precision=lax.Precision.HIGHEST)
    toe = jnp.where(jnp.asarray(col_mask)[None, :, None, :], toe, NEG_BIG)
    flat = toe.reshape(NA_HEADS, GRID_W, (2 * NA_KR - 1) * GRID_W)
    flat = jnp.pad(flat, ((0, 0), (0, 0), (0, NA_REL_W + GRID_W - flat.shape[-1])), constant_values=NEG_BIG)
    return jnp.stack([flat[:, :, :NA_REL_W], flat[:, :, GRID_W:]], axis=1)


def _s5_kernel(su0_ref, su1_ref, ktoe_ref, bexp_ref, cexp_ref, pw_ref, y0_ref, y1_ref, a_ref, yg_ref, e_ref, x_ref,
               *, n_chunks, n_ctx_chunks):
    blk = 8
    gp = S5_G * S5_P

    def block_transpose(src):
        lane_blk = lax.broadcasted_iota(jnp.int32, src[0].shape, 1) // S5_HG

        def pick(vals):
            acc = vals[0]
            for k in range(1, 8):
                acc = jnp.where(lane_blk == k, vals[k], acc)
            return acc

        rolled = []
        for d in range(8):
            diag = pick([src[(k + d) % 8] for k in range(8)])
            rolled.append(pltpu.roll(diag, S5_HG * d, 1) if d else diag)
        return [pick([rolled[(m - a) % 8] for m in range(8)]) for a in range(8)]

    def regroup_in(rb, carry):
        c0 = pl.multiple_of(rb * 2 * blk, 2 * blk)
        for hg, ref in enumerate((su0_ref, su1_ref)):
            for o in range(2):
                src = [jnp.concatenate(
                    [ref[0, pl.ds((rb * 2 + u) * (blk * S5_T) + o * 8 + m, blk, stride=S5_T), :] for u in range(2)],
                    axis=0).astype(BF16) for m in range(8)]
                for a, val in enumerate(block_transpose(src)):
                    a_ref[hg * 8 + a, pl.ds(c0, 2 * blk), o * 128:(o + 1) * 128] = val
        return carry

    lax.fori_loop(0, n_chunks // (2 * blk), regroup_in, 0)

    for g in range(S5_G):
        e = _dot(a_ref[g], bexp_ref[g])
        for kind in range(4):
            e_ref[kind, :, g * S5_P:(g + 1) * S5_P] = e[:, kind * S5_P:(kind + 1) * S5_P]

    afr, afi, abr, abi = pw_ref[0:1, :], pw_ref[1:2, :], pw_ref[2:3, :], pw_ref[3:4, :]

    def scan_body(i, carry):
        fr, fi, br, bi = carry
        cf = i
        cb = jnp.where(i < n_ctx_chunks, n_ctx_chunks - 1 - i, n_chunks - 1 - (i - n_ctx_chunks))
        x_ref[0, pl.ds(cf, 1), :] = fr
        x_ref[1, pl.ds(cf, 1), :] = fi
        x_ref[2, pl.ds(cb, 1), :] = br
        x_ref[3, pl.ds(cb, 1), :] = bi
        er = e_ref[0, pl.ds(cf, 1), :]
        ei = e_ref[1, pl.ds(cf, 1), :]
        gr = e_ref[2, pl.ds(cb, 1), :]
        gi = e_ref[3, pl.ds(cb, 1), :]
        return (afr * fr - afi * fi + er, afr * fi + afi * fr + ei,
                abr * br - abi * bi + gr, abr * bi + abi * br + gi)

    z = jnp.zeros((1, gp), F32)
    lax.fori_loop(0, n_chunks, scan_body, (z, z, z, z))

    for g in range(S5_G):
        y = _dot(a_ref[g], ktoe_ref[g])
        for kind in range(4):
            y = y + _dot(x_ref[kind, :, g * S5_P:(g + 1) * S5_P].astype(BF16), cexp_ref[g, kind])
        yg_ref[g] = y

    def regroup_out(rb, carry):
        c0 = pl.multiple_of(rb * 2 * blk, 2 * blk)
        for og, ref in enumerate((y0_ref, y1_ref)):
            for ht in range(2):
                src = [yg_ref[og * 8 + m, pl.ds(c0, 2 * blk), ht * 128:(ht + 1) * 128].astype(BF16) for m in range(8)]
                for a, val in enumerate(block_transpose(src)):
                    val = val.astype(F32)
                    for u in range(2):
                        ref[0, pl.ds((rb * 2 + u) * (blk * S5_T) + ht * 8 + a, blk, stride=S5_T), :] = \
                            val[u * blk:(u + 1) * blk]
        return carry

    lax.fori_loop(0, n_chunks // (2 * blk), regroup_out, 0)


def _s5(su0, su1, layer, ktoe, bexp, cexp, pw16):
    n_b, t_all, _ = su0.shape
    n_chunks = t_all // S5_T
    gp = S5_G * S5_P
    blk = pl.BlockSpec((1, t_all, 128), lambda b: (b, 0, 0))
    return pl.pallas_call(
        functools.partial(_s5_kernel, n_chunks=n_chunks, n_ctx_chunks=CTX_LEN // S5_T),
        out_shape=[jax.ShapeDtypeStruct((n_b, t_all, 128), F32)] * 2,
        grid=(n_b,),
        in_specs=[blk, blk, _layer_spec((S5_G, 256, 256), layer), _layer_spec((S5_G, 256, 256), layer),
                  _layer_spec((S5_G, 4, S5_P, 256), layer), _layer_spec((4, gp), layer)],
        out_specs=[blk, blk],
        scratch_shapes=[pltpu.VMEM((S5_G, n_chunks, S5_T * S5_HG), BF16),
                        pltpu.VMEM((S5_G, n_chunks, S5_T * S5_HG), F32),
                        pltpu.VMEM((4, n_chunks, gp), F32),
                        pltpu.VMEM((4, n_chunks, gp), F32)],
        compiler_params=_cp(("parallel",)),
        name="s5_mixer",
    )(su0, su1, ktoe, bexp, cexp, pw16)


def _s5_tables(lam_re, lam_im, log_dt, b_re, b_im, c_re, c_im, d_skip):
    hp = lax.Precision.HIGH
    lam_re, lam_im, log_dt = lam_re.astype(F32), lam_im.astype(F32), log_dt.astype(F32)
    b_re, b_im, c_re, c_im = b_re.astype(F32), b_im.astype(F32), c_re.astype(F32), c_im.astype(F32)
    dt = jnp.exp(log_dt)[..., None]
    n = jnp.arange(S5_T + 1, dtype=F32)[:, None, None, None]
    pw_re = jnp.exp(n * lam_re * dt) * jnp.cos(n * lam_im * dt)
    pw_im = jnp.exp(n * lam_re * dt) * jnp.sin(n * lam_im * dt)
    neg_re = jnp.exp(-n * lam_re * dt) * jnp.cos(n * lam_im * dt)
    neg_im = -jnp.exp(-n * lam_re * dt) * jnp.sin(n * lam_im * dt)
    m = jnp.arange(S5_T, -1, -1, dtype=F32)[:, None, None, None]
    dsc_re = jnp.exp(m * lam_re * dt) * jnp.cos(m * lam_im * dt)
    dsc_im = jnp.exp(m * lam_re * dt) * jnp.sin(m * lam_im * dt)
    lb_re, lb_im = pw_re[1], pw_im[1]
    num_re, num_im = lb_re - 1.0, lb_im
    den = lam_re * lam_re + lam_im * lam_im
    coef_re = ((num_re * lam_re + num_im * lam_im) / den)[..., None]
    coef_im = ((num_im * lam_re - num_re * lam_im) / den)[..., None]
    bb_re = coef_re * b_re - coef_im * b_im
    bb_im = coef_re * b_im + coef_im * b_re

    def cmul_b(p_re, p_im, d):
        re = p_re[..., None] * bb_re[d][None] - p_im[..., None] * bb_im[d][None]
        im = p_re[..., None] * bb_im[d][None] + p_im[..., None] * bb_re[d][None]
        to = lambda z: z.transpose(1, 0, 3, 2).reshape(S5_G, S5_T * S5_HG, S5_P)
        return to(re), to(im)

    def cmul_c(p_re, p_im, d):
        re = c_re[d][None] * p_re[:, :, None, :] - c_im[d][None] * p_im[:, :, None, :]
        im = c_re[d][None] * p_im[:, :, None, :] + c_im[d][None] * p_re[:, :, None, :]
        to = lambda z: z.transpose(1, 3, 0, 2).reshape(S5_G, S5_P, S5_T * S5_HG)
        return to(re), to(-im)

    def lag_operator(u, v):
        return jnp.einsum("gap,gpb->gab", jnp.concatenate(u, axis=-1), jnp.concatenate(v, axis=1), precision=hp)

    first = slice(0, S5_T)
    kf = lag_operator(cmul_b(neg_re[first, 0], neg_im[first, 0], 0), cmul_c(pw_re[first, 0], pw_im[first, 0], 0))
    kb = lag_operator(cmul_b(pw_re[first, 1], pw_im[first, 1], 1), cmul_c(neg_re[first, 1], neg_im[first, 1], 1))
    tok = np.arange(S5_T * S5_HG) // S5_HG
    skip = jnp.asarray(np.eye(S5_T * S5_HG, dtype=np.float32)) * jnp.tile(
        d_skip.astype(F32).reshape(S5_G, S5_HG), (1, S5_T))[:, None, :]
    ktoe = (jnp.where(jnp.asarray(tok[None, :] >= tok[:, None]), kf, 0.0)
            + jnp.where(jnp.asarray(tok[:, None] >= tok[None, :]), kb, 0.0) + skip)
    last = slice(1, S5_T + 1)
    bexp = jnp.concatenate(cmul_b(dsc_re[last, 0], dsc_im[last, 0], 0)
                           + cmul_b(pw_re[first, 1], pw_im[first, 1], 1), axis=-1)
    cexp = jnp.stack(cmul_c(pw_re[last, 0], pw_im[last, 0], 0)
                     + cmul_c(dsc_re[first, 1], dsc_im[first, 1], 1), axis=1)
    pw16 = jnp.stack([pw_re[S5_T, 0].reshape(-1), pw_im[S5_T, 0].reshape(-1),
                      pw_re[S5_T, 1].reshape(-1), pw_im[S5_T, 1].reshape(-1)], axis=0)
    return ktoe.astype(BF16), bexp.astype(BF16), cexp.astype(BF16), pw16


def _pool_kernel(u_ref, w_ref, scale_ref, o_ref, pad_ref, wblk_ref, *, n_lat):
    assert POOL_WINDOWS == (2, 4, 8, 16) and POOL_GW * len(POOL_WINDOWS) == GROUP_W
    pad = max(POOL_WINDOWS) // 2
    seg = ((pad, 0, CTX_LEN), (pad + CTX_LEN + 2 * pad, CTX_LEN, n_lat))
    pad_ref[...] = jnp.zeros_like(pad_ref)
    for p0, u0, n in seg:
        pad_ref[p0:p0 + n, :] = u_ref[0, u0:u0 + n, :]
    wblk_ref[...] = jnp.zeros_like(wblk_ref)
    for i in range(len(POOL_WINDOWS)):
        wblk_ref[i * POOL_GW:(i + 1) * POOL_GW, i * POOL_GW:(i + 1) * POOL_GW] = w_ref[i].astype(BF16)
    grp = lax.broadcasted_iota(jnp.int32, (1, GROUP_W), 1) // POOL_GW
    width = jnp.left_shift(2, grp)
    rows = TILE + 2 * pad

    def up(a, k):
        return pltpu.roll(a, rows - k, 0)

    for p0, u0, n in seg:
        for t0 in range(0, n, TILE):
            w0 = pad_ref[p0 + t0 - pad:p0 + t0 - pad + rows, :]
            a1 = w0 + up(w0, 1)
            a2 = a1 + up(a1, 2)
            a3 = a2 + up(a2, 4)
            a4 = a3 + up(a3, 8)
            total = jnp.where(grp == 0, up(a1, pad - 1), jnp.where(grp == 1, up(a2, pad - 2),
                              jnp.where(grp == 2, up(a3, pad - 4), a4)))[0:TILE]
            if t0 == 0 or t0 + TILE == n:
                t = t0 + lax.broadcasted_iota(jnp.int32, (TILE, 1), 0)
                first = t - width // 2
                cnt = (jnp.clip(first + width, 0, n) - jnp.clip(first, 0, n)).astype(F32)
            else:
                cnt = width.astype(F32)
            diff = total / cnt - w0[pad:pad + TILE]
            o_ref[0, u0 + t0:u0 + t0 + TILE, :] = (_dot(diff.astype(BF16), wblk_ref[...])
                                                   * scale_ref[...]).astype(o_ref.dtype)


def _pool(pu, layer, w_pool, scale):
    n_b, t_all, _ = pu.shape
    blk = pl.BlockSpec((1, t_all, 256), lambda b: (b, 0, 0))
    return pl.pallas_call(
        functools.partial(_pool_kernel, n_lat=t_all - CTX_LEN),
        out_shape=jax.ShapeDtypeStruct((n_b, t_all, 256), BF16),
        grid=(n_b,),
        in_specs=[blk, _layer_spec((len(POOL_WINDOWS), POOL_GW, POOL_GW), layer), _layer_spec((1, 256), layer)],
        out_specs=blk,
        scratch_shapes=[pltpu.VMEM((t_all + 32, 256), F32), pltpu.VMEM((GROUP_W, GROUP_W), BF16)],
        compiler_params=_cp(("parallel",)),
        name="pool_mixer",
    )(pu, w_pool, scale)


def _gelu_tanh(x):
    return 0.5 * x * (1.0 + jnp.tanh(math.sqrt(2.0 / math.pi) * (x + 0.044715 * (x * x * x))))


def _outproj_res(res, mod_row, ygla_ref, yna_ref, ys50_ref, ys51_ref, ypool_ref, gate_ref, mod_ref, wout_ref,
                 gpost_ref, wglu_ref, bglu_ref):
    def gated_proj(part, i):
        lo, hi = i * GROUP_W, (i + 1) * GROUP_W
        gc = gate_ref[0, :, lo:hi].astype(F32)
        return _dot((part * (gc * jax.nn.sigmoid(gc))).astype(BF16), wout_ref[lo:hi, :])

    gate_m = mod_ref[pl.ds(mod_row, 1), :][:, 2 * D_MODEL:3 * D_MODEL]
    g5 = _gelu_tanh(jnp.concatenate([ys50_ref[0], ys51_ref[0]], axis=-1))
    z5 = _dot(g5.astype(BF16), wglu_ref[...])
    acc = gated_proj(ygla_ref[0], 0) + gated_proj(yna_ref[0], 1) + gated_proj(ypool_ref[0], 3)
    acc = acc + gated_proj(g5 * jax.nn.sigmoid(z5 + bglu_ref[...]), 2)
    ms = jnp.mean(acc * acc, axis=-1, keepdims=True)
    return res + gate_m * (acc * lax.rsqrt(ms + NORM_EPS) * gpost_ref[...])


N_MIX_REFS = 6
N_OUT_PARAM_REFS = 5
N_IN_PARAM_REFS = 9


def _outproj_kernel(*refs):
    x_ref, params, out_ref = refs[2 * N_MIX_REFS], refs[2 * N_MIX_REFS + 1:-1], refs[-1]
    for half in range(2):
        mix = refs[half * N_MIX_REFS:(half + 1) * N_MIX_REFS]
        rows = slice(half * TILE, (half + 1) * TILE)
        out_ref[0, rows, :] = _outproj_res(x_ref[0, rows, :], pl.program_id(0), *mix, *params)


def _outproj_inproj_kernel(*refs, n_batch):
    mix = refs[:N_MIX_REFS]
    x_ref, xc_ref = refs[N_MIX_REFS:N_MIX_REFS + 2]
    out_params = refs[N_MIX_REFS + 2:N_MIX_REFS + 2 + N_OUT_PARAM_REFS]
    rest = refs[N_MIX_REFS + 2 + N_OUT_PARAM_REFS:]
    in_params, (xo_ref, xco_ref), in_outs = rest[:N_IN_PARAM_REFS], rest[N_IN_PARAM_REFS:N_IN_PARAM_REFS + 2], \
        rest[N_IN_PARAM_REFS + 2:]
    b = pl.program_id(0)
    j = pl.program_id(1)

    def body(res_ref, out_ref, mod_row, with_rope):
        res = _outproj_res(res_ref[0], mod_row, *mix, *out_params)
        out_ref[0] = res
        _inproj_body(res, mod_row, with_rope, *in_params, *in_outs)

    pl.when(j == 0)(lambda: body(xc_ref, xco_ref, n_batch, False))
    pl.when(j > 0)(lambda: body(x_ref, xo_ref, b, True))


def _outproj_param_specs(d, layer):
    return [_layer_spec((8, 3 * d), layer), _layer_spec((MIX_W, d), layer), _layer_spec((1, d), layer),
            _layer_spec((GROUP_W, GROUP_W), layer), _layer_spec((1, GROUP_W), layer)]


def _mix_specs(index_map):
    return [pl.BlockSpec((1, TILE, w), index_map) for w in (256, 256, 128, 128, 256, MIX_W)]


def _outproj(mix, x, layer, mod_all, w_out, g_post, w_glu, b_glu):
    n_b, n_lat, d = x.shape
    assert n_lat % (2 * TILE) == 0
    pair = pl.BlockSpec((1, 2 * TILE, d), lambda b, j: (b, j, 0))
    return pl.pallas_call(
        _outproj_kernel,
        out_shape=jax.ShapeDtypeStruct(x.shape, F32),
        grid=(n_b, n_lat // (2 * TILE)),
        in_specs=_mix_specs(lambda b, j: (b, 2 * j + 1, 0)) + _mix_specs(lambda b, j: (b, 2 * j + 2, 0)) + [pair]
                 + _outproj_param_specs(d, layer),
        out_specs=pair,
        compiler_params=_cp(("parallel", "arbitrary")),
        name="gate_outproj_residual",
    )(*mix, *mix, x, mod_all, w_out, g_post, w_glu, b_glu)


def _outproj_inproj(mix, x, xc, layer, mod_all, w_out, g_post, w_glu, b_glu, g_pre, w_pad, wg, bg, rope_tabs):
    n_b, n_lat, d = x.shape
    t_all = n_lat + CTX_LEN
    in_specs, in_shape, in_out_specs = _inproj_specs(n_b, t_all, d, layer + 1)
    outs = pl.pallas_call(
        functools.partial(_outproj_inproj_kernel, n_batch=n_b),
        out_shape=[jax.ShapeDtypeStruct(x.shape, F32), jax.ShapeDtypeStruct(xc.shape, F32)] + in_shape,
        grid=(n_b, t_all // TILE),
        in_specs=_mix_specs(lambda b, j: (b, j, 0))
                 + [pl.BlockSpec((1, TILE, d), _lat_map), pl.BlockSpec((1, CTX_LEN, d), _ctx_map)]
                 + _outproj_param_specs(d, layer) + in_specs,
        out_specs=[pl.BlockSpec((1, TILE, d), _lat_map), pl.BlockSpec((1, CTX_LEN, d), _ctx_map)] + in_out_specs,
        compiler_params=_cp(("parallel", "arbitrary")),
        name="outproj_then_inproj",
    )(*mix, x, xc, mod_all, w_out, g_post, w_glu, b_glu, mod_all, g_pre, *w_pad, wg, bg, *rope_tabs)
    return outs[0], outs[1], outs[2:]


def _rope_tables(n_tok):
    t = np.arange(n_tok)
    half = GLA_DK // 2
    freqs = ROPE_BASE ** (-np.arange(0, half, 2, dtype=np.float32) / half)
    d = np.arange(GLA_DK)
    pos = np.where((d // half)[None, :] == 0, (t // GRID_W)[:, None], (t % GRID_W)[:, None]).astype(np.float32)
    ang = jnp.asarray(pos) * jnp.asarray(freqs[(d % half) % (half // 2)])[None, :]
    first = jnp.asarray(((d % half) < half // 2)[None, :])
    cos, sin = jnp.cos(ang), jnp.sin(ang)
    tabs = (cos, jnp.where(first, -sin, 0.0), jnp.where(first, 0.0, sin))
    return tuple(jnp.tile(z, (1, GLA_HEADS)).astype(F32) for z in tabs)


def _split_w_in(w_in):
    head = jnp.pad(w_in[:, :, :N_GATE_END], ((0, 0), (0, 0), (0, N_HEAD_COLS - N_GATE_END)))
    return head.astype(BF16), w_in[:, :, N_GATE_END:].astype(BF16)


def _gate_weights(w_gate, b_gate):
    n_l = w_gate.shape[0]
    wg = jnp.zeros((n_l, 128, 256), F32)
    wg = wg.at[:, 0:GLA_RANK, 0:128].set(w_gate[:, 0].astype(F32))
    wg = wg.at[:, GLA_RANK:2 * GLA_RANK, 128:256].set(w_gate[:, 1].astype(F32))
    return wg.astype(BF16), b_gate.astype(F32).reshape(n_l, 1, 256)


def kernel(x, c, ctx, c_ctx, w_mod, b_mod, g_pre, g_post, w_in, w_out, gla_w_gate, gla_b_gate, gla_g_norm, na_rpb,
           s5_lam_re, s5_lam_im, s5_log_dt, s5_b_re, s5_b_im, s5_c_re, s5_c_im, s5_d, s5_w_glu, s5_b_glu,
           pool_w, pool_scale):
    n_b, n_lat, d = x.shape
    depth = w_mod.shape[0]
    rows = n_lat // GRID_W
    assert d == D_MODEL and ctx.shape[1] == CTX_LEN and n_lat % TILE == 0 and n_b < 8 and rows >= NA_KR
    c_rows = jnp.zeros((8, d), F32).at[0:n_b].set(c).at[n_b].set(c_ctx)
    mod_all = _modulation(c_rows, w_mod, b_mod)
    rope_tabs = _rope_tables(n_lat)
    w_pad = _split_w_in(w_in)
    wg, bg = _gate_weights(gla_w_gate, gla_b_gate)
    g_pre3 = g_pre.astype(F32).reshape(depth, 1, d)
    g_post3 = g_post.astype(F32).reshape(depth, 1, d)
    gn = jnp.tile(gla_g_norm.astype(F32), (1, GLA_HEADS)).reshape(depth, 1, GLA_HEADS * GLA_DV)
    rel_tab = jax.vmap(_na_rel_table)(na_rpb)
    ktoe, bexp, cexp, pw16 = jax.vmap(_s5_tables)(s5_lam_re, s5_lam_im, s5_log_dt, s5_b_re, s5_b_im,
                                                  s5_c_re, s5_c_im, s5_d)
    w_out_b = w_out.astype(BF16)
    w_glu_b = s5_w_glu.astype(BF16)
    b_glu3 = s5_b_glu.astype(F32).reshape(depth, 1, GROUP_W)
    pool_scale3 = pool_scale.astype(F32).reshape(depth, 1, GROUP_W)
    xc = ctx
    proj = _inproj(x, xc, 0, mod_all, g_pre3, w_pad, wg, bg, rope_tabs)
    for l in range(depth):
        qk, gv, lg, nq, nkt, nv, su0, su1, pu, gate_cols = proj
        y_gla = _gla(qk, gv, lg, l, gn)
        y_na = _na(nq, nkt, nv, l, rel_tab)
        y_s50, y_s51 = _s5(su0, su1, l, ktoe, bexp, cexp, pw16)
        y_pool = _pool(pu, l, pool_w, pool_scale3)
        mix = (y_gla, y_na, y_s50, y_s51, y_pool, gate_cols)
        if l == depth - 1:
            x = _outproj(mix, x, l, mod_all, w_out_b, g_post3, w_glu_b, b_glu3)
        else:
            x, xc, proj = _outproj_inproj(mix, x, xc, l, mod_all, w_out_b, g_post3, w_glu_b, b_glu3,
                                          g_pre3, w_pad, wg, bg, rope_tabs)
    return x
```

```python
import functools
import math

import numpy as np
import jax
import jax.numpy as jnp
from jax import lax
from jax.experimental import pallas as pl
from jax.experimental.pallas import tpu as pltpu

F32 = jnp.float32
BF16 = jnp.bfloat16

D_MODEL = 1024
GRID_W = 64
CTX_LEN = 256
GROUP_W = 256
MIX_W = 1024
NORM_EPS = 1e-6
GLA_HEADS = 4
GLA_DV = 64
GLA_DK = 32
GLA_RANK = 16
GLA_TAU = 16.0
ROPE_BASE = 10000.0
NA_HEADS = 4
NA_DH = 64
NA_KR = 8
NA_KC = 16
S5_HG = 16
S5_G = 16
S5_P = 64
POOL_WINDOWS = (2, 4, 8, 16)
POOL_GW = 64

TILE = 256
GLA_C = 128
GLA_STEP = 2
S5_T = 16
NA_ROWS = TILE // GRID_W
NA_REL_W = 1024
NEG_BIG = -1e30

_COL = {}
_off = 0
for _name, _w in (("gk", 128), ("gv", 256), ("gg", 128), ("nk", 256), ("nv", 256), ("su", 256),
                  ("gq", 128), ("nq", 256), ("pu", 256), ("gate", 1024)):
    _COL[_name] = (_off, _off + _w)
    _off += _w
N_PAD_COLS = _off
N_GATE_END = 128 + 256 + 2 * GLA_RANK
N_HEAD_COLS = _COL["gg"][1]

_VMEM_LIMIT = 56 * 1024 * 1024


def _cp(sem, vmem=_VMEM_LIMIT):
    return pltpu.CompilerParams(dimension_semantics=sem, vmem_limit_bytes=vmem)


def _dot(a, b):
    return jnp.dot(a, b, preferred_element_type=F32)


def _dot_nt(a, b):
    return lax.dot_general(a, b, (((1,), (1,)), ((), ())), preferred_element_type=F32)


def _split_bf16(a):
    hi = a.astype(BF16)
    lo = (a - hi.astype(F32)).astype(BF16)
    return hi, lo


def _mod_kernel(s_ref, w_ref, b_ref, o_ref):
    s = s_ref[...]
    a = (s * jax.nn.sigmoid(s)).astype(BF16)
    o_ref[0] = _dot(a, w_ref[0].astype(BF16)) + b_ref[0]


def _modulation(c_rows, w_mod, b_mod):
    n_l, d, d3 = w_mod.shape
    return pl.pallas_call(
        _mod_kernel,
        out_shape=jax.ShapeDtypeStruct((n_l, 8, d3), F32),
        grid=(n_l, d3 // d),
        in_specs=[pl.BlockSpec((8, d), lambda l, j: (0, 0)),
                  pl.BlockSpec((1, d, d), lambda l, j: (l, 0, j)),
                  pl.BlockSpec((1, 1, d), lambda l, j: (l, 0, j))],
        out_specs=pl.BlockSpec((1, 8, d), lambda l, j: (l, 0, j)),
        compiler_params=_cp(("parallel", "parallel")),
        name="adaln_modulation",
    )(c_rows, w_mod, b_mod.reshape(n_l, 1, d3))


def _inproj_body(xt, mod_row, with_rope, mod_ref, gpre_ref, wa_ref, wb_ref, wg_ref, bg_ref, cos_ref, sa_ref, sb_ref,
                 qk_ref, v_ref, lg_ref, nq_ref, nk_ref, nv_ref, su0_ref, su1_ref, pu_ref, gate_ref):
    m = mod_ref[pl.ds(mod_row, 1), :]
    shift = m[:, 0:D_MODEL]
    scale = m[:, D_MODEL:2 * D_MODEL]
    ms = jnp.mean(xt * xt, axis=-1, keepdims=True)
    h = xt * lax.rsqrt(ms + NORM_EPS) * gpre_ref[...] * (1.0 + scale) + shift
    hb = h.astype(BF16)

    def proj(name):
        lo, hi = _COL[name]
        if hi <= N_HEAD_COLS:
            return _dot(hb, wa_ref[:, lo:hi])
        return _dot(hb, wb_ref[:, lo - N_HEAD_COLS:hi - N_HEAD_COLS])

    def rope(t):
        if not with_rope:
            return t
        return t * cos_ref[...] + pltpu.roll(t, 128 - 8, 1) * sa_ref[...] + pltpu.roll(t, 8, 1) * sb_ref[...]

    qk_ref[0, :, 0:128] = rope(proj("gq")) * (GLA_DK ** -0.5)
    qk_ref[0, :, 128:256] = rope(proj("gk"))
    v_ref[0] = proj("gv")
    z = _dot(proj("gg").astype(BF16), wg_ref[...]) + bg_ref[...]
    lg_ref[0] = (jnp.minimum(z, 0.0) - jnp.log(1.0 + jnp.exp(-jnp.abs(z)))) * (1.0 / GLA_TAU)
    nq_ref[0] = (proj("nq") * (NA_DH ** -0.5)).astype(BF16)
    nk_ref[0] = proj("nk").T.astype(BF16)
    nv_ref[0] = proj("nv").astype(BF16)
    su = proj("su")
    su0_ref[0] = su[:, 0:128]
    su1_ref[0] = su[:, 128:256]
    pu_ref[0] = proj("pu")
    gate_ref[0] = proj("gate").astype(BF16)


def _inproj_kernel(x_ref, xc_ref, *refs, n_batch):
    b = pl.program_id(0)
    j = pl.program_id(1)
    pl.when(j == 0)(lambda: _inproj_body(xc_ref[0], n_batch, False, *refs))
    pl.when(j > 0)(lambda: _inproj_body(x_ref[0], b, True, *refs))


def _layer_spec(shape, layer):
    return pl.BlockSpec((None,) + tuple(shape), lambda *_: (layer,) + (0,) * len(shape))


def _lat_map(b, j):
    return (b, jnp.maximum(j - 1, 0), 0)


def _ctx_map(b, j):
    return (b, 0, 0)


def _inproj_specs(n_b, t_all, d, layer):
    tab_map = lambda b, j: (jnp.maximum(j - 1, 0), 0)
    outs = (("qk", 256, F32), ("gv", 256, F32), ("lg", 256, F32), ("nq", 256, BF16), ("nkT", 256, BF16),
            ("nv", 256, BF16), ("su0", 128, F32), ("su1", 128, F32), ("pu", 256, F32), ("gate", 1024, BF16))
    out_shape = [jax.ShapeDtypeStruct((n_b, w, t_all) if n == "nkT" else (n_b, t_all, w), dt) for n, w, dt in outs]
    out_specs = [pl.BlockSpec((1, w, TILE), lambda b, j: (b, 0, j)) if n == "nkT"
                 else pl.BlockSpec((1, TILE, w), lambda b, j: (b, j, 0)) for n, w, dt in outs]
    in_specs = [_layer_spec((8, 3 * d), layer),
                _layer_spec((1, d), layer),
                _layer_spec((d, N_HEAD_COLS), layer),
                _layer_spec((d, N_PAD_COLS - N_HEAD_COLS), layer),
                _layer_spec((128, 256), layer),
                _layer_spec((1, 256), layer),
                pl.BlockSpec((TILE, 128), tab_map),
                pl.BlockSpec((TILE, 128), tab_map),
                pl.BlockSpec((TILE, 128), tab_map)]
    return in_specs, out_shape, out_specs


def _inproj(x, xc, layer, mod_all, g_pre, w_pad, wg, bg, rope_tabs):
    n_b, n_lat, d = x.shape
    t_all = n_lat + CTX_LEN
    in_specs, out_shape, out_specs = _inproj_specs(n_b, t_all, d, layer)
    return pl.pallas_call(
        functools.partial(_inproj_kernel, n_batch=n_b),
        out_shape=out_shape,
        grid=(n_b, t_all // TILE),
        in_specs=[pl.BlockSpec((1, TILE, d), _lat_map), pl.BlockSpec((1, CTX_LEN, d), _ctx_map)] + in_specs,
        out_specs=out_specs,
        compiler_params=_cp(("parallel", "arbitrary")),
        name="modnorm_inproj",
    )(x, xc, mod_all, g_pre, *w_pad, wg, bg, *rope_tabs)


def _gla_kernel(qk_ref, v_ref, lg_ref, gn_ref, y_ref, stf_ref, stb_ref, of_ref, ob_ref, *, n_chunks, n_ctx_chunks):
    c_len = GLA_C
    row = lax.broadcasted_iota(jnp.int32, (c_len, c_len), 0)
    col = lax.broadcasted_iota(jnp.int32, (c_len, c_len), 1)
    tri_f = (col <= row).astype(BF16)
    tri_b = (col >= row).astype(BF16)
    arow = lax.broadcasted_iota(jnp.int32, (c_len, 4 * c_len), 0)
    acol = lax.broadcasted_iota(jnp.int32, (c_len, 4 * c_len), 1) % c_len
    amask_f = acol <= arow
    amask_b = acol >= arow
    head_k = lax.broadcasted_iota(jnp.int32, (1, 128), 1) // GLA_DK
    head_v = lax.broadcasted_iota(jnp.int32, (1, 256), 1) // GLA_DV
    bd_mask = (lax.broadcasted_iota(jnp.int32, (256, 128), 0) // GLA_DV
               == lax.broadcasted_iota(jnp.int32, (256, 128), 1) // GLA_DK)
    ones_bd = jnp.where(lax.broadcasted_iota(jnp.int32, (256, 256), 0) // GLA_DV
                        == lax.broadcasted_iota(jnp.int32, (256, 256), 1) // GLA_DV,
                        1.0 / GLA_DV, 0.0).astype(BF16)
    mid = c_len // 2

    def scan_body(i, carry):
        def bwd_chunk(pos):
            return jnp.where(pos < n_ctx_chunks, n_ctx_chunks - 1 - pos, n_chunks - 1 - (pos - n_ctx_chunks))

        dirs = [d for d in (0, 1) for _ in range(GLA_STEP)]
        chunks = [i * GLA_STEP + u for u in range(GLA_STEP)] + [bwd_chunk(i * GLA_STEP + u) for u in range(GLA_STEP)]
        jobs = range(len(dirs))
        r0 = [pl.multiple_of(c * c_len, c_len) for c in chunks]
        st_refs = (stf_ref, stb_ref)
        tri = (tri_f, tri_b)
        amask = (amask_f, amask_b)
        end = (c_len - 1, 0)
        q = [qk_ref[0, pl.ds(r0[n], c_len), 0:128] for n in jobs]
        k = [qk_ref[0, pl.ds(r0[n], c_len), 128:256] for n in jobs]
        v = [v_ref[0, pl.ds(r0[n], c_len), :] for n in jobs]
        lg = [_split_bf16(lg_ref[0, pl.ds(r0[n], c_len), 128 * dirs[n]:128 * (dirs[n] + 1)]) for n in jobs]
        cum = [_dot(tri[dirs[n]], lg[n][0]) + _dot(tri[dirs[n]], lg[n][1]) for n in jobs]
        c_ref = [cum[n][mid:mid + 1, :] for n in jobs]
        c_end = [cum[n][end[dirs[n]]:end[dirs[n]] + 1, :] for n in jobs]
        qt = [q[n] * jnp.exp(cum[n] - c_ref[n]) for n in jobs]
        kt = [k[n] * jnp.exp(c_ref[n] - cum[n]) for n in jobs]
        ks = [jnp.concatenate([jnp.where(head_k == hh, kt[n], 0.0) for hh in range(GLA_HEADS)],
                              axis=0).astype(BF16) for n in jobs]
        att = [_dot_nt(qt[n].astype(BF16), ks[n]) for n in jobs]
        qh = [(qt[n] * jnp.exp(c_ref[n])).astype(BF16) for n in jobs]
        kh = [(kt[n] * jnp.exp(c_end[n] - c_ref[n])).astype(BF16) for n in jobs]
        upd = [_dot(v[n].T.astype(BF16), kh[n]) for n in jobs]
        att = [jnp.where(amask[dirs[n]], att[n], 0.0).astype(BF16) for n in jobs]
        vb = [v[n].astype(BF16) for n in jobs]
        vs = [jnp.concatenate([jnp.where(head_v == hh, vb[n], jnp.zeros_like(vb[n])) for hh in range(GLA_HEADS)],
                              axis=0) for n in jobs]
        o_intra = [_dot(att[n], vs[n]) for n in jobs]
        outs = (of_ref, ob_ref)
        st = [st_refs[d][...] for d in (0, 1)]
        for n in jobs:
            d = dirs[n]
            outs[d][pl.ds(r0[n], c_len), :] = o_intra[n] + _dot_nt(qh[n], st[d].astype(BF16))
            st[d] = st[d] * jnp.exp(c_end[n]) + jnp.where(bd_mask, upd[n], 0.0)
        for d in (0, 1):
            st_refs[d][...] = st[d]
        return carry

    def norm_body(c, carry):
        r0 = pl.multiple_of(c * TILE, TILE)
        o = of_ref[pl.ds(r0, TILE), :] + ob_ref[pl.ds(r0, TILE), :]
        ms = _dot((o * o).astype(BF16), ones_bd)
        y_ref[0, pl.ds(r0, TILE), :] = (o * lax.rsqrt(ms + NORM_EPS) * gn_ref[...]).astype(y_ref.dtype)
        return carry

    stf_ref[...] = jnp.zeros_like(stf_ref)
    stb_ref[...] = jnp.zeros_like(stb_ref)
    lax.fori_loop(0, n_chunks // GLA_STEP, scan_body, 0)
    lax.fori_loop(0, n_chunks * c_len // TILE, norm_body, 0)


def _gla(qk, v, lg, layer, g_norm):
    n_b, t_all, _ = qk.shape
    blk = lambda w: pl.BlockSpec((1, t_all, w), lambda b: (b, 0, 0))
    return pl.pallas_call(
        functools.partial(_gla_kernel, n_chunks=t_all // GLA_C, n_ctx_chunks=CTX_LEN // GLA_C),
        out_shape=jax.ShapeDtypeStruct((n_b, t_all, 256), BF16),
        grid=(n_b,),
        in_specs=[blk(256), blk(256), blk(256), _layer_spec((1, 256), layer)],
        out_specs=blk(256),
        scratch_shapes=[pltpu.VMEM((256, 128), F32), pltpu.VMEM((256, 128), F32),
                        pltpu.VMEM((t_all, 256), F32), pltpu.VMEM((t_all, 256), F32)],
        compiler_params=_cp(("parallel",)),
        name="gla_mixer",
    )(qk, v, lg, g_norm)


def _na_kernel(q_ref, kt0_ref, kt1_ref, kt2_ref, v0_ref, v1_ref, v2_ref, ktc_ref, vc_ref, rel_ref, o_ref, *, rows):
    j = pl.program_id(1)
    left = lax.broadcasted_iota(jnp.int32, (1, 128), 1) < NA_DH
    ones_blk = jnp.ones((TILE, 128), BF16)
    ro0 = NA_KR - 1 - NA_ROWS

    def attend(with_window):
        key_tiles = [(ktc_ref, vc_ref, None)]
        if with_window:
            g = j - 1
            q_row = lax.broadcasted_iota(jnp.int32, (TILE, TILE), 0) // GRID_W
            k_row = lax.broadcasted_iota(jnp.int32, (TILE, TILE), 1) // GRID_W
            first = jnp.clip(NA_ROWS * g + q_row - NA_KR // 2, 0, rows - NA_KR) - NA_ROWS * (g - 1)
            for d, (kr, vr) in enumerate(((kt0_ref, v0_ref), (kt1_ref, v1_ref), (kt2_ref, v2_ref))):
                window = None if d == 1 else jnp.where(
                    (k_row + NA_ROWS * d >= first) & (k_row + NA_ROWS * d < first + NA_KR), 0.0, NEG_BIG)
                key_tiles.append((kr, vr, (d, window)))
        def lanes(hh):
            return slice(128 * (hh // 2), 128 * (hh // 2 + 1))

        def score(hh):
            qp = q_ref[0, :, lanes(hh)]
            zero = jnp.zeros_like(qp)
            qh = jnp.where(left, qp, zero) if hh % 2 == 0 else jnp.where(left, zero, qp)
            kt_all = jnp.concatenate([kr[0, lanes(hh), :] for kr, _, _ in key_tiles], axis=1)
            return _dot(qh, kt_all)

        def probs(hh, s):
            if with_window:
                parts = [s[:, 0:TILE]]
                for _, _, (d, window) in key_tiles[1:]:
                    rel = jnp.concatenate(
                        [rel_ref[hh, (ro0 - i) % 2, :, GRID_W * ((ro0 - i) // 2 * 2) + d * TILE:
                                 GRID_W * ((ro0 - i) // 2 * 2) + (d + 1) * TILE] for i in range(NA_ROWS)], axis=0)
                    part = s[:, (d + 1) * TILE:(d + 2) * TILE] + rel
                    parts.append(part if window is None else part + window)
                s = jnp.concatenate(parts, axis=1)
            return jnp.exp(s - jnp.max(s, axis=-1, keepdims=True)).astype(BF16)

        def attend_values(pair, p_even, p_odd):
            vaug = jnp.concatenate([jnp.concatenate([vr[0, :, lanes(2 * pair)], ones_blk], axis=1)
                                    for _, vr, _ in key_tiles], axis=0)
            acc = _dot(jnp.concatenate([p_even, p_odd], axis=0), vaug)
            o = acc[:, 0:128] / acc[:, 128:256]
            o_ref[0, :, lanes(2 * pair)] = jnp.where(left, o[0:TILE], o[TILE:2 * TILE]).astype(o_ref.dtype)

        s_next = score(0)
        p = []
        for hh in range(NA_HEADS):
            s_cur = s_next
            if hh + 1 < NA_HEADS:
                s_next = score(hh + 1)
            p.append(probs(hh, s_cur))
            if hh % 2 == 1:
                attend_values(hh // 2, p[hh - 1], p[hh])

    @pl.when(j == 0)
    def _():
        attend(False)

    @pl.when(j > 0)
    def _():
        attend(True)


def _na(nq, nkt, nv, layer, rel_tab):
    n_b, t_all, _ = nq.shape
    n_tiles = t_all // TILE
    rows = (t_all - CTX_LEN) // GRID_W

    def nbr(delta):
        return lambda b, j: (b, jnp.clip(j + delta, 1, n_tiles - 1), 0)

    def nbr_t(delta):
        return lambda b, j: (b, 0, jnp.clip(j + delta, 1, n_tiles - 1))

    tile = lambda f: pl.BlockSpec((1, TILE, 256), f)
    tile_t = lambda f: pl.BlockSpec((1, 256, TILE), f)
    return pl.pallas_call(
        functools.partial(_na_kernel, rows=rows),
        out_shape=jax.ShapeDtypeStruct((n_b, t_all, 256), BF16),
        grid=(n_b, n_tiles),
        in_specs=[tile(lambda b, j: (b, j, 0)),
                  tile_t(nbr_t(-1)), tile_t(nbr_t(0)), tile_t(nbr_t(1)),
                  tile(nbr(-1)), tile(nbr(0)), tile(nbr(1)),
                  tile_t(lambda b, j: (b, 0, 0)), tile(lambda b, j: (b, 0, 0)),
                  _layer_spec((NA_HEADS, 2, GRID_W, NA_REL_W), layer)],
        out_specs=tile(lambda b, j: (b, j, 0)),
        compiler_params=_cp(("parallel", "arbitrary")),
        name="na_mixer",
    )(nq, nkt, nkt, nkt, nv, nv, nv, nkt, nv, rel_tab)


def _na_rel_table(rpb):
    col = np.arange(GRID_W)
    cs = np.clip(col - NA_KC // 2, 0, GRID_W - NA_KC)
    col_mask = (col[None, :] >= cs[:, None]) & (col[None, :] < cs[:, None] + NA_KC)
    co = np.clip(col[None, :] - col[:, None], -(NA_KC - 1), NA_KC - 1) + (NA_KC - 1)
    onehot = (co[None] == np.arange(2 * NA_KC - 1)[:, None, None]).astype(np.float32)
    toe = jnp.einsum("hrc,cqk->hqrk", rpb.astype(F32), jnp.asarray(onehot),
                     precision=lax.Precision.HIGHEST)
    toe = jnp.where(jnp.asarray(col_mask)[None, :, None, :], toe, NEG_BIG)
    flat = toe.reshape(NA_HEADS, GRID_W, (2 * NA_KR - 1) * GRID_W)
    flat = jnp.pad(flat, ((0, 0), (0, 0), (0, NA_REL_W + GRID_W - flat.shape[-1])), constant_values=NEG_BIG)
    return jnp.stack([flat[:, :, :NA_REL_W], flat[:, :, GRID_W:]], axis=1)


def _s5_operators(b32_ref, c32_ref, pw_ref, skip_ref, ktoe_ref, bexp_ref, cexp_ref):
    tok_s = lax.broadcasted_iota(jnp.int32, (S5_T * S5_HG, S5_T * S5_HG), 0) // S5_HG
    tok_t = lax.broadcasted_iota(jnp.int32, (S5_T * S5_HG, S5_T * S5_HG), 1) // S5_HG
    diag = (lax.broadcasted_iota(jnp.int32, (S5_T * S5_HG, S5_T * S5_HG), 0)
            == lax.broadcasted_iota(jnp.int32, (S5_T * S5_HG, S5_T * S5_HG), 1))

    def dot3(a, b):
        a_hi, a_lo = _split_bf16(a)
        b_hi, b_lo = _split_bf16(b)
        return _dot(a_hi, b_hi) + _dot(a_lo, b_hi) + _dot(a_hi, b_lo)

    for g in range(S5_G):
        b32 = b32_ref[g]
        ktoe = jnp.where(diag, skip_ref[g], 0.0)
        for d, keep in ((0, tok_t >= tok_s), (1, tok_s >= tok_t)):
            p_re = pw_ref[2 * d:2 * d + 1, g * S5_P:(g + 1) * S5_P]
            p_im = pw_ref[2 * d + 1:2 * d + 2, g * S5_P:(g + 1) * S5_P]
            inv = 1.0 / (p_re * p_re + p_im * p_im)
            q_re, q_im = p_re * inv, -p_im * inv
            b_re = b32[:, (2 * d) * S5_P:(2 * d + 1) * S5_P]
            b_im = b32[:, (2 * d + 1) * S5_P:(2 * d + 2) * S5_P]
            lag = (dot3(b_re * q_re - b_im * q_im, c32_ref[g, 2 * d])
                   + dot3(b_re * q_im + b_im * q_re, c32_ref[g, 2 * d + 1]))
            ktoe = ktoe + jnp.where(keep, lag, 0.0)
        ktoe_ref[g] = ktoe.astype(BF16)
        bexp_ref[g] = b32.astype(BF16)
        cexp_ref[g] = c32_ref[g].astype(BF16)


def _s5_kernel(su0_ref, su1_ref, b32_ref, c32_ref, pw_ref, skip_ref, y0_ref, y1_ref, ktoe_ref, bexp_ref, cexp_ref,
               a_ref, yg_ref, e_ref, x_ref, *, n_chunks, n_ctx_chunks):
    blk = 8
    gp = S5_G * S5_P

    @pl.when(pl.program_id(0) == 0)
    def _():
        _s5_operators(b32_ref, c32_ref, pw_ref, skip_ref, ktoe_ref, bexp_ref, cexp_ref)

    def block_transpose(src):
        lane_blk = lax.broadcasted_iota(jnp.int32, src[0].shape, 1) // S5_HG

        def pick(vals):
            acc = vals[0]
            for k in range(1, 8):
                acc = jnp.where(lane_blk == k, vals[k], acc)
            return acc

        rolled = []
        for d in range(8):
            diag = pick([src[(k + d) % 8] for k in range(8)])
            rolled.append(pltpu.roll(diag, S5_HG * d, 1) if d else diag)
        return [pick([rolled[(m - a) % 8] for m in range(8)]) for a in range(8)]

    def regroup_in(rb, carry):
        c0 = pl.multiple_of(rb * 2 * blk, 2 * blk)
        for hg, ref in enumerate((su0_ref, su1_ref)):
            for o in range(2):
                src = [jnp.concatenate(
                    [ref[0, pl.ds((rb * 2 + u) * (blk * S5_T) + o * 8 + m, blk, stride=S5_T), :] for u in range(2)],
                    axis=0).astype(BF16) for m in range(8)]
                for a, val in enumerate(block_transpose(src)):
                    a_ref[hg * 8 + a, pl.ds(c0, 2 * blk), o * 128:(o + 1) * 128] = val
        return carry

    lax.fori_loop(0, n_chunks // (2 * blk), regroup_in, 0)

    for g in range(S5_G):
        e = _dot(a_ref[g], bexp_ref[g])
        for kind in range(4):
            e_ref[kind, :, g * S5_P:(g + 1) * S5_P] = e[:, kind * S5_P:(kind + 1) * S5_P]

    afr, afi, abr, abi = pw_ref[0:1, :], pw_ref[1:2, :], pw_ref[2:3, :], pw_ref[3:4, :]

    def scan_body(i, carry):
        fr, fi, br, bi = carry
        cf = i
        cb = jnp.where(i < n_ctx_chunks, n_ctx_chunks - 1 - i, n_chunks - 1 - (i - n_ctx_chunks))
        x_ref[0, pl.ds(cf, 1), :] = fr
        x_ref[1, pl.ds(cf, 1), :] = fi
        x_ref[2, pl.ds(cb, 1), :] = br
        x_ref[3, pl.ds(cb, 1), :] = bi
        er = e_ref[0, pl.ds(cf, 1), :]
        ei = e_ref[1, pl.ds(cf, 1), :]
        gr = e_ref[2, pl.ds(cb, 1), :]
        gi = e_ref[3, pl.ds(cb, 1), :]
        return (afr * fr - afi * fi + er, afr * fi + afi * fr + ei,
                abr * br - abi * bi + gr, abr * bi + abi * br + gi)

    z = jnp.zeros((1, gp), F32)
    lax.fori_loop(0, n_chunks, scan_body, (z, z, z, z))

    for g in range(S5_G):
        y = _dot(a_ref[g], ktoe_ref[g])
        for kind in range(4):
            y = y + _dot(x_ref[kind, :, g * S5_P:(g + 1) * S5_P].astype(BF16), cexp_ref[g, kind])
        yg_ref[g] = y

    def regroup_out(rb, carry):
        c0 = pl.multiple_of(rb * 2 * blk, 2 * blk)
        for og, ref in enumerate((y0_ref, y1_ref)):
            for ht in range(2):
                src = [yg_ref[og * 8 + m, pl.ds(c0, 2 * blk), ht * 128:(ht + 1) * 128].astype(BF16) for m in range(8)]
                for a, val in enumerate(block_transpose(src)):
                    val = val.astype(F32)
                    for u in range(2):
                        ref[0, pl.ds((rb * 2 + u) * (blk * S5_T) + ht * 8 + a, blk, stride=S5_T), :] = \
                            val[u * blk:(u + 1) * blk]
        return carry

    lax.fori_loop(0, n_chunks // (2 * blk), regroup_out, 0)


def _s5(su0, su1, layer, bexp, cexp, pw16, skip):
    n_b, t_all, _ = su0.shape
    n_chunks = t_all // S5_T
    gp = S5_G * S5_P
    width = S5_T * S5_HG
    blk = pl.BlockSpec((1, t_all, 128), lambda b: (b, 0, 0))
    return pl.pallas_call(
        functools.partial(_s5_kernel, n_chunks=n_chunks, n_ctx_chunks=CTX_LEN // S5_T),
        out_shape=[jax.ShapeDtypeStruct((n_b, t_all, 128), F32)] * 2,
        grid=(n_b,),
        in_specs=[blk, blk, _layer_spec((S5_G, width, 4 * S5_P), layer), _layer_spec((S5_G, 4, S5_P, width), layer),
                  _layer_spec((4, gp), layer), _layer_spec((S5_G, 1, width), layer)],
        out_specs=[blk, blk],
        scratch_shapes=[pltpu.VMEM((S5_G, width, width), BF16),
                        pltpu.VMEM((S5_G, width, 4 * S5_P), BF16),
                        pltpu.VMEM((S5_G, 4, S5_P, width), BF16),
                        pltpu.VMEM((S5_G, n_chunks, width), BF16),
                        pltpu.VMEM((S5_G, n_chunks, width), F32),
                        pltpu.VMEM((4, n_chunks, gp), F32),
                        pltpu.VMEM((4, n_chunks, gp), F32)],
        compiler_params=_cp(("arbitrary",)),
        name="s5_mixer",
    )(su0, su1, bexp, cexp, pw16, skip)


def _s5_tables(lam_re, lam_im, log_dt, b_re, b_im, c_re, c_im, d_skip):
    lam_re, lam_im, log_dt = lam_re.astype(F32), lam_im.astype(F32), log_dt.astype(F32)
    b_re, b_im, c_re, c_im = b_re.astype(F32), b_im.astype(F32), c_re.astype(F32), c_im.astype(F32)
    dt = jnp.exp(log_dt)[..., None]
    n = jnp.arange(S5_T + 1, dtype=F32)[:, None, None, None]
    pw_re = jnp.exp(n * lam_re * dt) * jnp.cos(n * lam_im * dt)
    pw_im = jnp.exp(n * lam_re * dt) * jnp.sin(n * lam_im * dt)
    m = jnp.arange(S5_T, -1, -1, dtype=F32)[:, None, None, None]
    dsc_re = jnp.exp(m * lam_re * dt) * jnp.cos(m * lam_im * dt)
    dsc_im = jnp.exp(m * lam_re * dt) * jnp.sin(m * lam_im * dt)
    lb_re, lb_im = pw_re[1], pw_im[1]
    num_re, num_im = lb_re - 1.0, lb_im
    den = lam_re * lam_re + lam_im * lam_im
    coef_re = ((num_re * lam_re + num_im * lam_im) / den)[..., None]
    coef_im = ((num_im * lam_re - num_re * lam_im) / den)[..., None]
    bb_re = coef_re * b_re - coef_im * b_im
    bb_im = coef_re * b_im + coef_im * b_re

    def cmul_b(p_re, p_im, d):
        re = p_re[..., None] * bb_re[d][None] - p_im[..., None] * bb_im[d][None]
        im = p_re[..., None] * bb_im[d][None] + p_im[..., None] * bb_re[d][None]
        to = lambda z: z.transpose(1, 0, 3, 2).reshape(S5_G, S5_T * S5_HG, S5_P)
        return to(re), to(im)

    def cmul_c(p_re, p_im, d):
        re = c_re[d][None] * p_re[:, :, None, :] - c_im[d][None] * p_im[:, :, None, :]
        im = c_re[d][None] * p_im[:, :, None, :] + c_im[d][None] * p_re[:, :, None, :]
        to = lambda z: z.transpose(1, 3, 0, 2).reshape(S5_G, S5_P, S5_T * S5_HG)
        return to(re), to(-im)

    first = slice(0, S5_T)
    last = slice(1, S5_T + 1)
    bexp = jnp.concatenate(cmul_b(dsc_re[last, 0], dsc_im[last, 0], 0)
                           + cmul_b(pw_re[first, 1], pw_im[first, 1], 1), axis=-1)
    cexp = jnp.stack(cmul_c(pw_re[last, 0], pw_im[last, 0], 0)
                     + cmul_c(dsc_re[first, 1], dsc_im[first, 1], 1), axis=1)
    pw16 = jnp.stack([pw_re[S5_T, 0].reshape(-1), pw_im[S5_T, 0].reshape(-1),
                      pw_re[S5_T, 1].reshape(-1), pw_im[S5_T, 1].reshape(-1)], axis=0)
    skip = jnp.tile(d_skip.astype(F32).reshape(S5_G, 1, S5_HG), (1, 1, S5_T))
    return bexp, cexp, pw16, skip


def _pool_kernel(u_ref, w_ref, scale_ref, o_ref, pad_ref, wblk_ref, *, n_lat):
    assert POOL_WINDOWS == (2, 4, 8, 16) and POOL_GW * len(POOL_WINDOWS) == GROUP_W
    pad = max(POOL_WINDOWS) // 2
    seg = ((pad, 0, CTX_LEN), (pad + CTX_LEN + 2 * pad, CTX_LEN, n_lat))
    pad_ref[...] = jnp.zeros_like(pad_ref)
    for p0, u0, n in seg:
        pad_ref[p0:p0 + n, :] = u_ref[0, u0:u0 + n, :]
    wblk_ref[...] = jnp.zeros_like(wblk_ref)
    for i in range(len(POOL_WINDOWS)):
        wblk_ref[i * POOL_GW:(i + 1) * POOL_GW, i * POOL_GW:(i + 1) * POOL_GW] = w_ref[i].astype(BF16)
    grp = lax.broadcasted_iota(jnp.int32, (1, GROUP_W), 1) // POOL_GW
    width = jnp.left_shift(2, grp)
    rows = TILE + 2 * pad

    def up(a, k):
        return pltpu.roll(a, rows - k, 0)

    for p0, u0, n in seg:
        for t0 in range(0, n, TILE):
            w0 = pad_ref[p0 + t0 - pad:p0 + t0 - pad + rows, :]
            a1 = w0 + up(w0, 1)
            a2 = a1 + up(a1, 2)
            a3 = a2 + up(a2, 4)
            a4 = a3 + up(a3, 8)
            total = jnp.where(grp == 0, up(a1, pad - 1), jnp.where(grp == 1, up(a2, pad - 2),
                              jnp.where(grp == 2, up(a3, pad - 4), a4)))[0:TILE]
            if t0 == 0 or t0 + TILE == n:
                t = t0 + lax.broadcasted_iota(jnp.int32, (TILE, 1), 0)
                first = t - width // 2
                cnt = (jnp.clip(first + width, 0, n) - jnp.clip(first, 0, n)).astype(F32)
            else:
                cnt = width.astype(F32)
            diff = total / cnt - w0[pad:pad + TILE]
            o_ref[0, u0 + t0:u0 + t0 + TILE, :] = (_dot(diff.astype(BF16), wblk_ref[...])
                                                   * scale_ref[...]).astype(o_ref.dtype)


def _pool(pu, layer, w_pool, scale):
    n_b, t_all, _ = pu.shape
    blk = pl.BlockSpec((1, t_all, 256), lambda b: (b, 0, 0))
    return pl.pallas_call(
        functools.partial(_pool_kernel, n_lat=t_all - CTX_LEN),
        out_shape=jax.ShapeDtypeStruct((n_b, t_all, 256), BF16),
        grid=(n_b,),
        in_specs=[blk, _layer_spec((len(POOL_WINDOWS), POOL_GW, POOL_GW), layer), _layer_spec((1, 256), layer)],
        out_specs=blk,
        scratch_shapes=[pltpu.VMEM((t_all + 32, 256), F32), pltpu.VMEM((GROUP_W, GROUP_W), BF16)],
        compiler_params=_cp(("parallel",)),
        name="pool_mixer",
    )(pu, w_pool, scale)


def _gelu_tanh(x):
    return 0.5 * x * (1.0 + jnp.tanh(math.sqrt(2.0 / math.pi) * (x + 0.044715 * (x * x * x))))


def _outproj_res(res, mod_row, ygla_ref, yna_ref, ys50_ref, ys51_ref, ypool_ref, gate_ref, mod_ref, wout_ref,
                 gpost_ref, wglu_ref, bglu_ref):
    def gated_proj(part, i):
        lo, hi = i * GROUP_W, (i + 1) * GROUP_W
        gc = gate_ref[0, :, lo:hi].astype(F32)
        return _dot((part * (gc * jax.nn.sigmoid(gc))).astype(BF16), wout_ref[lo:hi, :])

    gate_m = mod_ref[pl.ds(mod_row, 1), :][:, 2 * D_MODEL:3 * D_MODEL]
    g5 = _gelu_tanh(jnp.concatenate([ys50_ref[0], ys51_ref[0]], axis=-1))
    z5 = _dot(g5.astype(BF16), wglu_ref[...])
    acc = gated_proj(ygla_ref[0], 0) + gated_proj(yna_ref[0], 1) + gated_proj(ypool_ref[0], 3)
    acc = acc + gated_proj(g5 * jax.nn.sigmoid(z5 + bglu_ref[...]), 2)
    ms = jnp.mean(acc * acc, axis=-1, keepdims=True)
    return res + gate_m * (acc * lax.rsqrt(ms + NORM_EPS) * gpost_ref[...])


N_MIX_REFS = 6
N_OUT_PARAM_REFS = 5
N_IN_PARAM_REFS = 9


def _outproj_kernel(*refs):
    x_ref, params, out_ref = refs[2 * N_MIX_REFS], refs[2 * N_MIX_REFS + 1:-1], refs[-1]
    for half in range(2):
        mix = refs[half * N_MIX_REFS:(half + 1) * N_MIX_REFS]
        rows = slice(half * TILE, (half + 1) * TILE)
        out_ref[0, rows, :] = _outproj_res(x_ref[0, rows, :], pl.program_id(0), *mix, *params)


def _outproj_inproj_kernel(*refs, n_batch):
    mix = refs[:N_MIX_REFS]
    x_ref, xc_ref = refs[N_MIX_REFS:N_MIX_REFS + 2]
    out_params = refs[N_MIX_REFS + 2:N_MIX_REFS + 2 + N_OUT_PARAM_REFS]
    rest = refs[N_MIX_REFS + 2 + N_OUT_PARAM_REFS:]
    in_params, (xo_ref, xco_ref), in_outs = rest[:N_IN_PARAM_REFS], rest[N_IN_PARAM_REFS:N_IN_PARAM_REFS + 2], \
        rest[N_IN_PARAM_REFS + 2:]
    b = pl.program_id(0)
    j = pl.program_id(1)

    def body(res_ref, out_ref, mod_row, with_rope):
        res = _outproj_res(res_ref[0], mod_row, *mix, *out_params)
        out_ref[0] = res
        _inproj_body(res, mod_row, with_rope, *in_params, *in_outs)

    pl.when(j == 0)(lambda: body(xc_ref, xco_ref, n_batch, False))
    pl.when(j > 0)(lambda: body(x_ref, xo_ref, b, True))


def _outproj_param_specs(d, layer):
    return [_layer_spec((8, 3 * d), layer), _layer_spec((MIX_W, d), layer), _layer_spec((1, d), layer),
            _layer_spec((GROUP_W, GROUP_W), layer), _layer_spec((1, GROUP_W), layer)]


def _mix_specs(index_map):
    return [pl.BlockSpec((1, TILE, w), index_map) for w in (256, 256, 128, 128, 256, MIX_W)]


def _outproj(mix, x, layer, mod_all, w_out, g_post, w_glu, b_glu):
    n_b, n_lat, d = x.shape
    assert n_lat % (2 * TILE) == 0
    pair = pl.BlockSpec((1, 2 * TILE, d), lambda b, j: (b, j, 0))
    return pl.pallas_call(
        _outproj_kernel,
        out_shape=jax.ShapeDtypeStruct(x.shape, F32),
        grid=(n_b, n_lat // (2 * TILE)),
        in_specs=_mix_specs(lambda b, j: (b, 2 * j + 1, 0)) + _mix_specs(lambda b, j: (b, 2 * j + 2, 0)) + [pair]
                 + _outproj_param_specs(d, layer),
        out_specs=pair,
        compiler_params=_cp(("parallel", "arbitrary")),
        name="gate_outproj_residual",
    )(*mix, *mix, x, mod_all, w_out, g_post, w_glu, b_glu)


def _outproj_inproj(mix, x, xc, layer, mod_all, w_out, g_post, w_glu, b_glu, g_pre, w_pad, wg, bg, rope_tabs):
    n_b, n_lat, d = x.shape
    t_all = n_lat + CTX_LEN
    in_specs, in_shape, in_out_specs = _inproj_specs(n_b, t_all, d, layer + 1)
    outs = pl.pallas_call(
        functools.partial(_outproj_inproj_kernel, n_batch=n_b),
        out_shape=[jax.ShapeDtypeStruct(x.shape, F32), jax.ShapeDtypeStruct(xc.shape, F32)] + in_shape,
        grid=(n_b, t_all // TILE),
        in_specs=_mix_specs(lambda b, j: (b, j, 0))
                 + [pl.BlockSpec((1, TILE, d), _lat_map), pl.BlockSpec((1, CTX_LEN, d), _ctx_map)]
                 + _outproj_param_specs(d, layer) + in_specs,
        out_specs=[pl.BlockSpec((1, TILE, d), _lat_map), pl.BlockSpec((1, CTX_LEN, d), _ctx_map)] + in_out_specs,
        compiler_params=_cp(("parallel", "arbitrary")),
        name="outproj_then_inproj",
    )(*mix, x, xc, mod_all, w_out, g_post, w_glu, b_glu, mod_all, g_pre, *w_pad, wg, bg, *rope_tabs)
    return outs[0], outs[1], outs[2:]


def _rope_tables(n_tok):
    t = np.arange(n_tok)
    half = GLA_DK // 2
    freqs = ROPE_BASE ** (-np.arange(0, half, 2, dtype=np.float32) / half)
    d = np.arange(GLA_DK)
    pos = np.where((d // half)[None, :] == 0, (t // GRID_W)[:, None], (t % GRID_W)[:, None]).astype(np.float32)
    ang = jnp.asarray(pos) * jnp.asarray(freqs[(d % half) % (half // 2)])[None, :]
    first = jnp.asarray(((d % half) < half // 2)[None, :])
    cos, sin = jnp.cos(ang), jnp.sin(ang)
    tabs = (cos, jnp.where(first, -sin, 0.0), jnp.where(first, 0.0, sin))
    return tuple(jnp.tile(z, (1, GLA_HEADS)).astype(F32) for z in tabs)


def _split_w_in(w_in):
    head = jnp.pad(w_in[:, :, :N_GATE_END], ((0, 0), (0, 0), (0, N_HEAD_COLS - N_GATE_END)))
    return head.astype(BF16), w_in[:, :, N_GATE_END:].astype(BF16)


def _gate_weights(w_gate, b_gate):
    n_l = w_gate.shape[0]
    wg = jnp.zeros((n_l, 128, 256), F32)
    wg = wg.at[:, 0:GLA_RANK, 0:128].set(w_gate[:, 0].astype(F32))
    wg = wg.at[:, GLA_RANK:2 * GLA_RANK, 128:256].set(w_gate[:, 1].astype(F32))
    return wg.astype(BF16), b_gate.astype(F32).reshape(n_l, 1, 256)


def kernel(x, c, ctx, c_ctx, w_mod, b_mod, g_pre, g_post, w_in, w_out, gla_w_gate, gla_b_gate, gla_g_norm, na_rpb,
           s5_lam_re, s5_lam_im, s5_log_dt, s5_b_re, s5_b_im, s5_c_re, s5_c_im, s5_d, s5_w_glu, s5_b_glu,
           pool_w, pool_scale):
    n_b, n_lat, d = x.shape
    depth = w_mod.shape[0]
    rows = n_lat // GRID_W
    assert d == D_MODEL and ctx.shape[1] == CTX_LEN and n_lat % TILE == 0 and n_b < 8 and rows >= NA_KR
    c_rows = jnp.zeros((8, d), F32).at[0:n_b].set(c).at[n_b].set(c_ctx)
    mod_all = _modulation(c_rows, w_mod, b_mod)
    rope_tabs = _rope_tables(n_lat)
    w_pad = _split_w_in(w_in)
    wg, bg = _gate_weights(gla_w_gate, gla_b_gate)
    g_pre3 = g_pre.astype(F32).reshape(depth, 1, d)
    g_post3 = g_post.astype(F32).reshape(depth, 1, d)
    gn = jnp.tile(gla_g_norm.astype(F32), (1, GLA_HEADS)).reshape(depth, 1, GLA_HEADS * GLA_DV)
    rel_tab = jax.vmap(_na_rel_table)(na_rpb)
    bexp, cexp, pw16, skip = jax.vmap(_s5_tables)(s5_lam_re, s5_lam_im, s5_log_dt, s5_b_re, s5_b_im,
                                                  s5_c_re, s5_c_im, s5_d)
    w_out_b = w_out.astype(BF16)
    w_glu_b = s5_w_glu.astype(BF16)
    b_glu3 = s5_b_glu.astype(F32).reshape(depth, 1, GROUP_W)
    pool_scale3 = pool_scale.astype(F32).reshape(depth, 1, GROUP_W)
    xc = ctx
    proj = _inproj(x, xc, 0, mod_all, g_pre3, w_pad, wg, bg, rope_tabs)
    for l in range(depth):
        qk, gv, lg, nq, nkt, nv, su0, su1, pu, gate_cols = proj
        y_gla = _gla(qk, gv, lg, l, gn)
        y_na = _na(nq, nkt, nv, l, rel_tab)
        y_s50, y_s51 = _s5(su0, su1, l, bexp, cexp, pw16, skip)
        y_pool = _pool(pu, l, pool_w, pool_scale3)
        mix = (y_gla, y_na, y_s50, y_s51, y_pool, gate_cols)
        if l == depth - 1:
            x = _outproj(mix, x, l, mod_all, w_out_b, g_post3, w_glu_b, b_glu3)
        else:
            x, xc, proj = _outproj_inproj(mix, x, xc, l, mod_all, w_out_b, g_post3, w_glu_b, b_glu3,
                                          g_pre3, w_pad, wg, bg, rope_tabs)
    return x
```

```python
import functools
import math

import numpy as np
import jax
import jax.numpy as jnp
from jax import lax
from jax.experimental import pallas as pl
from jax.experimental.pallas import tpu as pltpu

F32 = jnp.float32
BF16 = jnp.bfloat16

D_MODEL = 1024
GRID_W = 64
CTX_LEN = 256
GROUP_W = 256
MIX_W = 1024
NORM_EPS = 1e-6
GLA_HEADS = 4
GLA_DV = 64
GLA_DK = 32
GLA_RANK = 16
GLA_TAU = 16.0
ROPE_BASE = 10000.0
NA_HEADS = 4
NA_DH = 64
NA_KR = 8
NA_KC = 16
S5_HG = 16
S5_G = 16
S5_P = 64
POOL_WINDOWS = (2, 4, 8, 16)
POOL_GW = 64

TILE = 256
GLA_C = 128
GLA_STEP = 2
S5_T = 16
NA_ROWS = TILE // GRID_W
NA_REL_W = 1024
NEG_BIG = -1e30

N_GATE_END = 128 + 256 + 2 * GLA_RANK
_HEAD_COL = {"gqk": (0, 256), "gv": (256, 512), "gg": (512, 640)}
_TAIL_COL = {}
_off = 0
for _name, _w in (("nk", 256), ("nv", 256), ("su", 256), ("gq_unused", 128), ("nq", 256), ("pu", 256),
                  ("gate", 1024)):
    _TAIL_COL[_name] = (_off, _off + _w)
    _off += _w
N_HEAD_COLS = _HEAD_COL["gg"][1]
N_TAIL_COLS = _off
GQ_COL = N_GATE_END + _TAIL_COL["gq_unused"][0]

_VMEM_LIMIT = 56 * 1024 * 1024


def _cp(sem, vmem=_VMEM_LIMIT):
    return pltpu.CompilerParams(dimension_semantics=sem, vmem_limit_bytes=vmem)


def _dot(a, b):
    return jnp.dot(a, b, preferred_element_type=F32)


def _dot_nt(a, b):
    return lax.dot_general(a, b, (((1,), (1,)), ((), ())), preferred_element_type=F32)


def _split_bf16(a):
    hi = a.astype(BF16)
    lo = (a - hi.astype(F32)).astype(BF16)
    return hi, lo


def _mod_kernel(s_ref, w_ref, b_ref, o_ref):
    s = s_ref[...]
    a = (s * jax.nn.sigmoid(s)).astype(BF16)
    o_ref[0] = _dot(a, w_ref[0].astype(BF16)) + b_ref[0]


def _modulation(c_rows, w_mod, b_mod):
    n_l, d, d3 = w_mod.shape
    return pl.pallas_call(
        _mod_kernel,
        out_shape=jax.ShapeDtypeStruct((n_l, 8, d3), F32),
        grid=(n_l, d3 // d),
        in_specs=[pl.BlockSpec((8, d), lambda l, j: (0, 0)),
                  pl.BlockSpec((1, d, d), lambda l, j: (l, 0, j)),
                  pl.BlockSpec((1, 1, d), lambda l, j: (l, 0, j))],
        out_specs=pl.BlockSpec((1, 8, d), lambda l, j: (l, 0, j)),
        compiler_params=_cp(("parallel", "parallel")),
        name="adaln_modulation",
    )(c_rows, w_mod, b_mod.reshape(n_l, 1, d3))


def _inproj_body(xt, mod_row, with_rope, mod_ref, gpre_ref, wa_ref, wb_ref, wg_ref, bg_ref, cos_ref, sa_ref, sb_ref,
                 qk_ref, v_ref, lg_ref, nq_ref, nk_ref, nv_ref, su0_ref, su1_ref, pu_ref, gate_ref):
    m = mod_ref[pl.ds(mod_row, 1), :]
    shift = m[:, 0:D_MODEL]
    scale = m[:, D_MODEL:2 * D_MODEL]
    ms = jnp.mean(xt * xt, axis=-1, keepdims=True)
    h = xt * lax.rsqrt(ms + NORM_EPS) * gpre_ref[...] * (1.0 + scale) + shift
    hb = h.astype(BF16)

    def proj(name):
        if name in _HEAD_COL:
            lo, hi = _HEAD_COL[name]
            return _dot(hb, wa_ref[:, lo:hi])
        lo, hi = _TAIL_COL[name]
        return _dot(hb, wb_ref[:, lo:hi])

    def rope(t):
        if not with_rope:
            return t
        return t * cos_ref[...] + pltpu.roll(t, 128 - 8, 1) * sa_ref[...] + pltpu.roll(t, 8, 1) * sb_ref[...]

    qk = proj("gqk")
    qk_ref[0, :, 0:128] = rope(qk[:, 0:128]) * (GLA_DK ** -0.5)
    qk_ref[0, :, 128:256] = rope(qk[:, 128:256])
    v_ref[0] = proj("gv")
    z = _dot(proj("gg").astype(BF16), wg_ref[...]) + bg_ref[...]
    lg_ref[0] = (jnp.minimum(z, 0.0) - jnp.log(1.0 + jnp.exp(-jnp.abs(z)))) * (1.0 / GLA_TAU)
    nq_ref[0] = (proj("nq") * (NA_DH ** -0.5)).astype(BF16)
    nk_ref[0] = proj("nk").T.astype(BF16)
    nv_ref[0] = proj("nv").astype(BF16)
    su = proj("su")
    su0_ref[0] = su[:, 0:128]
    su1_ref[0] = su[:, 128:256]
    pu_ref[0] = proj("pu")
    gate_ref[0] = proj("gate").astype(BF16)


def _inproj_kernel(x_ref, xc_ref, *refs, n_batch):
    b = pl.program_id(0)
    j = pl.program_id(1)
    pl.when(j == 0)(lambda: _inproj_body(xc_ref[0], n_batch, False, *refs))
    pl.when(j > 0)(lambda: _inproj_body(x_ref[0], b, True, *refs))


def _layer_spec(shape, layer):
    return pl.BlockSpec((None,) + tuple(shape), lambda *_: (layer,) + (0,) * len(shape))


def _lat_map(b, j):
    return (b, jnp.maximum(j - 1, 0), 0)


def _ctx_map(b, j):
    return (b, 0, 0)


def _inproj_specs(n_b, t_all, d, layer):
    tab_map = lambda b, j: (jnp.maximum(j - 1, 0), 0)
    outs = (("qk", 256, F32), ("gv", 256, F32), ("lg", 256, F32), ("nq", 256, BF16), ("nkT", 256, BF16),
            ("nv", 256, BF16), ("su0", 128, F32), ("su1", 128, F32), ("pu", 256, F32), ("gate", 1024, BF16))
    out_shape = [jax.ShapeDtypeStruct((n_b, w, t_all) if n == "nkT" else (n_b, t_all, w), dt) for n, w, dt in outs]
    out_specs = [pl.BlockSpec((1, w, TILE), lambda b, j: (b, 0, j)) if n == "nkT"
                 else pl.BlockSpec((1, TILE, w), lambda b, j: (b, j, 0)) for n, w, dt in outs]
    in_specs = [_layer_spec((8, 3 * d), layer),
                _layer_spec((1, d), layer),
                _layer_spec((d, N_HEAD_COLS), layer),
                _layer_spec((d, N_TAIL_COLS), layer),
                _layer_spec((128, 256), layer),
                _layer_spec((1, 256), layer),
                pl.BlockSpec((TILE, 128), tab_map),
                pl.BlockSpec((TILE, 128), tab_map),
                pl.BlockSpec((TILE, 128), tab_map)]
    return in_specs, out_shape, out_specs


def _inproj(x, xc, layer, mod_all, g_pre, w_pad, wg, bg, rope_tabs):
    n_b, n_lat, d = x.shape
    t_all = n_lat + CTX_LEN
    in_specs, out_shape, out_specs = _inproj_specs(n_b, t_all, d, layer)
    return pl.pallas_call(
        functools.partial(_inproj_kernel, n_batch=n_b),
        out_shape=out_shape,
        grid=(n_b, t_all // TILE),
        in_specs=[pl.BlockSpec((1, TILE, d), _lat_map), pl.BlockSpec((1, CTX_LEN, d), _ctx_map)] + in_specs,
        out_specs=out_specs,
        compiler_params=_cp(("parallel", "arbitrary")),
        name="modnorm_inproj",
    )(x, xc, mod_all, g_pre, *w_pad, wg, bg, *rope_tabs)


def _gla_kernel(qk_ref, v_ref, lg_ref, gn_ref, y_ref, stf_ref, stb_ref, of_ref, ob_ref, *, n_chunks, n_ctx_chunks):
    c_len = GLA_C
    row = lax.broadcasted_iota(jnp.int32, (c_len, c_len), 0)
    col = lax.broadcasted_iota(jnp.int32, (c_len, c_len), 1)
    tri_f = (col <= row).astype(BF16)
    tri_b = (col >= row).astype(BF16)
    arow = lax.broadcasted_iota(jnp.int32, (c_len, 4 * c_len), 0)
    acol = lax.broadcasted_iota(jnp.int32, (c_len, 4 * c_len), 1) % c_len
    amask_f = acol <= arow
    amask_b = acol >= arow
    head_k = lax.broadcasted_iota(jnp.int32, (1, 128), 1) // GLA_DK
    head_v = lax.broadcasted_iota(jnp.int32, (1, 256), 1) // GLA_DV
    bd_mask = (lax.broadcasted_iota(jnp.int32, (256, 128), 0) // GLA_DV
               == lax.broadcasted_iota(jnp.int32, (256, 128), 1) // GLA_DK)
    ones_bd = jnp.where(lax.broadcasted_iota(jnp.int32, (256, 256), 0) // GLA_DV
                        == lax.broadcasted_iota(jnp.int32, (256, 256), 1) // GLA_DV,
                        1.0 / GLA_DV, 0.0).astype(BF16)
    mid = c_len // 2

    def scan_body(i, carry):
        def bwd_chunk(pos):
            return jnp.where(pos < n_ctx_chunks, n_ctx_chunks - 1 - pos, n_chunks - 1 - (pos - n_ctx_chunks))

        dirs = [d for d in (0, 1) for _ in range(GLA_STEP)]
        chunks = [i * GLA_STEP + u for u in range(GLA_STEP)] + [bwd_chunk(i * GLA_STEP + u) for u in range(GLA_STEP)]
        jobs = range(len(dirs))
        r0 = [pl.multiple_of(c * c_len, c_len) for c in chunks]
        st_refs = (stf_ref, stb_ref)
        tri = (tri_f, tri_b)
        amask = (amask_f, amask_b)
        end = (c_len - 1, 0)
        q = [qk_ref[0, pl.ds(r0[n], c_len), 0:128] for n in jobs]
        k = [qk_ref[0, pl.ds(r0[n], c_len), 128:256] for n in jobs]
        v = [v_ref[0, pl.ds(r0[n], c_len), :] for n in jobs]
        lg = [_split_bf16(lg_ref[0, pl.ds(r0[n], c_len), 128 * dirs[n]:128 * (dirs[n] + 1)]) for n in jobs]
        cum = [_dot(tri[dirs[n]], lg[n][0]) + _dot(tri[dirs[n]], lg[n][1]) for n in jobs]
        c_ref = [cum[n][mid:mid + 1, :] for n in jobs]
        c_end = [cum[n][end[dirs[n]]:end[dirs[n]] + 1, :] for n in jobs]
        qt = [q[n] * jnp.exp(cum[n] - c_ref[n]) for n in jobs]
        kt = [k[n] * jnp.exp(c_ref[n] - cum[n]) for n in jobs]
        ks = [jnp.concatenate([jnp.where(head_k == hh, kt[n], 0.0) for hh in range(GLA_HEADS)],
                              axis=0).astype(BF16) for n in jobs]
        att = [_dot_nt(qt[n].astype(BF16), ks[n]) for n in jobs]
        qh = [(qt[n] * jnp.exp(c_ref[n])).astype(BF16) for n in jobs]
        kh = [(kt[n] * jnp.exp(c_end[n] - c_ref[n])).astype(BF16) for n in jobs]
        upd = [_dot(v[n].T.astype(BF16), kh[n]) for n in jobs]
        att = [jnp.where(amask[dirs[n]], att[n], 0.0).astype(BF16) for n in jobs]
        vb = [v[n].astype(BF16) for n in jobs]
        vs = [jnp.concatenate([jnp.where(head_v == hh, vb[n], jnp.zeros_like(vb[n])) for hh in range(GLA_HEADS)],
                              axis=0) for n in jobs]
        o_intra = [_dot(att[n], vs[n]) for n in jobs]
        outs = (of_ref, ob_ref)
        st = [st_refs[d][...] for d in (0, 1)]
        for n in jobs:
            d = dirs[n]
            outs[d][pl.ds(r0[n], c_len), :] = o_intra[n] + _dot_nt(qh[n], st[d].astype(BF16))
            st[d] = st[d] * jnp.exp(c_end[n]) + jnp.where(bd_mask, upd[n], 0.0)
        for d in (0, 1):
            st_refs[d][...] = st[d]
        return carry

    def norm_body(c, carry):
        r0 = pl.multiple_of(c * TILE, TILE)
        o = of_ref[pl.ds(r0, TILE), :] + ob_ref[pl.ds(r0, TILE), :]
        ms = _dot((o * o).astype(BF16), ones_bd)
        y_ref[0, pl.ds(r0, TILE), :] = (o * lax.rsqrt(ms + NORM_EPS) * gn_ref[...]).astype(y_ref.dtype)
        return carry

    stf_ref[...] = jnp.zeros_like(stf_ref)
    stb_ref[...] = jnp.zeros_like(stb_ref)
    lax.fori_loop(0, n_chunks // GLA_STEP, scan_body, 0)
    lax.fori_loop(0, n_chunks * c_len // TILE, norm_body, 0)


def _gla(qk, v, lg, layer, g_norm):
    n_b, t_all, _ = qk.shape
    blk = lambda w: pl.BlockSpec((1, t_all, w), lambda b: (b, 0, 0))
    return pl.pallas_call(
        functools.partial(_gla_kernel, n_chunks=t_all // GLA_C, n_ctx_chunks=CTX_LEN // GLA_C),
        out_shape=jax.ShapeDtypeStruct((n_b, t_all, 256), BF16),
        grid=(n_b,),
        in_specs=[blk(256), blk(256), blk(256), _layer_spec((1, 256), layer)],
        out_specs=blk(256),
        scratch_shapes=[pltpu.VMEM((256, 128), F32), pltpu.VMEM((256, 128), F32),
                        pltpu.VMEM((t_all, 256), F32), pltpu.VMEM((t_all, 256), F32)],
        compiler_params=_cp(("parallel",)),
        name="gla_mixer",
    )(qk, v, lg, g_norm)


def _na_kernel(q_ref, kt0_ref, kt1_ref, kt2_ref, v0_ref, v1_ref, v2_ref, ktc_ref, vc_ref, rel_ref, o_ref, *, rows):
    j = pl.program_id(1)
    left = lax.broadcasted_iota(jnp.int32, (1, 128), 1) < NA_DH
    ones_blk = jnp.ones((TILE, 128), BF16)
    ro0 = NA_KR - 1 - NA_ROWS

    def attend(with_window):
        key_tiles = [(ktc_ref, vc_ref, None)]
        if with_window:
            g = j - 1
            q_row = lax.broadcasted_iota(jnp.int32, (TILE, TILE), 0) // GRID_W
            k_row = lax.broadcasted_iota(jnp.int32, (TILE, TILE), 1) // GRID_W
            first = jnp.clip(NA_ROWS * g + q_row - NA_KR // 2, 0, rows - NA_KR) - NA_ROWS * (g - 1)
            for d, (kr, vr) in enumerate(((kt0_ref, v0_ref), (kt1_ref, v1_ref), (kt2_ref, v2_ref))):
                window = None if d == 1 else jnp.where(
                    (k_row + NA_ROWS * d >= first) & (k_row + NA_ROWS * d < first + NA_KR), 0.0, NEG_BIG)
                key_tiles.append((kr, vr, (d, window)))
        def lanes(hh):
            return slice(128 * (hh // 2), 128 * (hh // 2 + 1))

        def score(hh):
            qp = q_ref[0, :, lanes(hh)]
            zero = jnp.zeros_like(qp)
            qh = jnp.where(left, qp, zero) if hh % 2 == 0 else jnp.where(left, zero, qp)
            kt_all = jnp.concatenate([kr[0, lanes(hh), :] for kr, _, _ in key_tiles], axis=1)
            return _dot(qh, kt_all)

        def probs(hh, s):
            if with_window:
                parts = [s[:, 0:TILE]]
                for _, _, (d, window) in key_tiles[1:]:
                    rel = jnp.concatenate(
                        [rel_ref[hh, (ro0 - i) % 2, :, GRID_W * ((ro0 - i) // 2 * 2) + d * TILE:
                                 GRID_W * ((ro0 - i) // 2 * 2) + (d + 1) * TILE] for i in range(NA_ROWS)], axis=0)
                    part = s[:, (d + 1) * TILE:(d + 2) * TILE] + rel
                    parts.append(part if window is None else part + window)
                s = jnp.concatenate(parts, axis=1)
            return jnp.exp(s - jnp.max(s, axis=-1, keepdims=True)).astype(BF16)

        def attend_values(pair, p_even, p_odd):
            vaug = jnp.concatenate([jnp.concatenate([vr[0, :, lanes(2 * pair)], ones_blk], axis=1)
                                    for _, vr, _ in key_tiles], axis=0)
            acc = _dot(jnp.concatenate([p_even, p_odd], axis=0), vaug)
            o = acc[:, 0:128] / acc[:, 128:256]
            o_ref[0, :, lanes(2 * pair)] = jnp.where(left, o[0:TILE], o[TILE:2 * TILE]).astype(o_ref.dtype)

        s_next = score(0)
        p = []
        for hh in range(NA_HEADS):
            s_cur = s_next
            if hh + 1 < NA_HEADS:
                s_next = score(hh + 1)
            p.append(probs(hh, s_cur))
            if hh % 2 == 1:
                attend_values(hh // 2, p[hh - 1], p[hh])

    @pl.when(j == 0)
    def _():
        attend(False)

    @pl.when(j > 0)
    def _():
        attend(True)


def _na(nq, nkt, nv, layer, rel_tab):
    n_b, t_all, _ = nq.shape
    n_tiles = t_all // TILE
    rows = (t_all - CTX_LEN) // GRID_W

    def nbr(delta):
        return lambda b, j: (b, jnp.clip(j + delta, 1, n_tiles - 1), 0)

    def nbr_t(delta):
        return lambda b, j: (b, 0, jnp.clip(j + delta, 1, n_tiles - 1))

    tile = lambda f: pl.BlockSpec((1, TILE, 256), f)
    tile_t = lambda f: pl.BlockSpec((1, 256, TILE), f)
    return pl.pallas_call(
        functools.partial(_na_kernel, rows=rows),
        out_shape=jax.ShapeDtypeStruct((n_b, t_all, 256), BF16),
        grid=(n_b, n_tiles),
        in_specs=[tile(lambda b, j: (b, j, 0)),
                  tile_t(nbr_t(-1)), tile_t(nbr_t(0)), tile_t(nbr_t(1)),
                  tile(nbr(-1)), tile(nbr(0)), tile(nbr(1)),
                  tile_t(lambda b, j: (b, 0, 0)), tile(lambda b, j: (b, 0, 0)),
                  _layer_spec((NA_HEADS, 2, GRID_W, NA_REL_W), layer)],
        out_specs=tile(lambda b, j: (b, j, 0)),
        compiler_params=_cp(("parallel", "arbitrary")),
        name="na_mixer",
    )(nq, nkt, nkt, nkt, nv, nv, nv, nkt, nv, rel_tab)


def _na_rel_table(rpb):
    col = np.arange(GRID_W)
    cs = np.clip(col - NA_KC // 2, 0, GRID_W - NA_KC)
    col_mask = (col[None, :] >= cs[:, None]) & (col[None, :] < cs[:, None] + NA_KC)
    co = np.clip(col[None, :] - col[:, None], -(NA_KC - 1), NA_KC - 1) + (NA_KC - 1)
    onehot = (co[None] == np.arange(2 * NA_KC - 1)[:, None, None]).astype(np.float32)
    toe = jnp.einsum("hrc,cqk->hqrk", rpb.astype(F32), jnp.asarray(onehot),
                     precision=lax.Precision.HIGHEST)
    toe = jnp.where(jnp.asarray(col_mask)[None, :, None, :], toe, NEG_BIG)
    flat = toe.reshape(NA_HEADS, GRID_W, (2 * NA_KR - 1) * GRID_W)
    flat = jnp.pad(flat, ((0, 0), (0, 0), (0, NA_REL_W + GRID_W - flat.shape[-1])), constant_values=NEG_BIG)
    return jnp.stack([flat[:, :, :NA_REL_W], flat[:, :, GRID_W:]], axis=1)


def _s5_operators(b32_ref, c32_ref, pw_ref, skip_ref, ktoe_ref, bexp_ref, cexp_ref):
    tok_s = lax.broadcasted_iota(jnp.int32, (S5_T * S5_HG, S5_T * S5_HG), 0) // S5_HG
    tok_t = lax.broadcasted_iota(jnp.int32, (S5_T * S5_HG, S5_T * S5_HG), 1) // S5_HG
    diag = (lax.broadcasted_iota(jnp.int32, (S5_T * S5_HG, S5_T * S5_HG), 0)
            == lax.broadcasted_iota(jnp.int32, (S5_T * S5_HG, S5_T * S5_HG), 1))

    def dot3(a, b):
        a_hi, a_lo = _split_bf16(a)
        b_hi, b_lo = _split_bf16(b)
        return _dot(a_hi, b_hi) + _dot(a_lo, b_hi) + _dot(a_hi, b_lo)

    for g in range(S5_G):
        b32 = b32_ref[g]
        ktoe = jnp.where(diag, skip_ref[g], 0.0)
        for d, keep in ((0, tok_t >= tok_s), (1, tok_s >= tok_t)):
            p_re = pw_ref[2 * d:2 * d + 1, g * S5_P:(g + 1) * S5_P]
            p_im = pw_ref[2 * d + 1:2 * d + 2, g * S5_P:(g + 1) * S5_P]
            inv = 1.0 / (p_re * p_re + p_im * p_im)
            q_re, q_im = p_re * inv, -p_im * inv
            b_re = b32[:, (2 * d) * S5_P:(2 * d + 1) * S5_P]
            b_im = b32[:, (2 * d + 1) * S5_P:(2 * d + 2) * S5_P]
            lag = (dot3(b_re * q_re - b_im * q_im, c32_ref[g, 2 * d])
                   + dot3(b_re * q_im + b_im * q_re, c32_ref[g, 2 * d + 1]))
            ktoe = ktoe + jnp.where(keep, lag, 0.0)
        ktoe_ref[g] = ktoe.astype(BF16)
        bexp_ref[g] = b32.astype(BF16)
        cexp_ref[g] = c32_ref[g].astype(BF16)


def _s5_kernel(su0_ref, su1_ref, b32_ref, c32_ref, pw_ref, skip_ref, y0_ref, y1_ref, ktoe_ref, bexp_ref, cexp_ref,
               a_ref, yg_ref, e_ref, x_ref, *, n_chunks, n_ctx_chunks):
    blk = 8
    gp = S5_G * S5_P

    @pl.when(pl.program_id(0) == 0)
    def _():
        _s5_operators(b32_ref, c32_ref, pw_ref, skip_ref, ktoe_ref, bexp_ref, cexp_ref)

    def block_transpose(src):
        lane_blk = lax.broadcasted_iota(jnp.int32, src[0].shape, 1) // S5_HG

        def pick(vals):
            acc = vals[0]
            for k in range(1, 8):
                acc = jnp.where(lane_blk == k, vals[k], acc)
            return acc

        rolled = []
        for d in range(8):
            diag = pick([src[(k + d) % 8] for k in range(8)])
            rolled.append(pltpu.roll(diag, S5_HG * d, 1) if d else diag)
        return [pick([rolled[(m - a) % 8] for m in range(8)]) for a in range(8)]

    def regroup_in(rb, carry):
        c0 = pl.multiple_of(rb * 2 * blk, 2 * blk)
        for hg, ref in enumerate((su0_ref, su1_ref)):
            for o in range(2):
                src = [jnp.concatenate(
                    [ref[0, pl.ds((rb * 2 + u) * (blk * S5_T) + o * 8 + m, blk, stride=S5_T), :] for u in range(2)],
                    axis=0).astype(BF16) for m in range(8)]
                for a, val in enumerate(block_transpose(src)):
                    a_ref[hg * 8 + a, pl.ds(c0, 2 * blk), o * 128:(o + 1) * 128] = val
        return carry

    lax.fori_loop(0, n_chunks // (2 * blk), regroup_in, 0)

    for g in range(S5_G):
        e = _dot(a_ref[g], bexp_ref[g])
        for kind in range(4):
            e_ref[kind, :, g * S5_P:(g + 1) * S5_P] = e[:, kind * S5_P:(kind + 1) * S5_P]

    afr, afi, abr, abi = pw_ref[0:1, :], pw_ref[1:2, :], pw_ref[2:3, :], pw_ref[3:4, :]

    def scan_body(i, carry):
        fr, fi, br, bi = carry
        cf = i
        cb = jnp.where(i < n_ctx_chunks, n_ctx_chunks - 1 - i, n_chunks - 1 - (i - n_ctx_chunks))
        x_ref[0, pl.ds(cf, 1), :] = fr
        x_ref[1, pl.ds(cf, 1), :] = fi
        x_ref[2, pl.ds(cb, 1), :] = br
        x_ref[3, pl.ds(cb, 1), :] = bi
        er = e_ref[0, pl.ds(cf, 1), :]
        ei = e_ref[1, pl.ds(cf, 1), :]
        gr = e_ref[2, pl.ds(cb, 1), :]
        gi = e_ref[3, pl.ds(cb, 1), :]
        return (afr * fr - afi * fi + er, afr * fi + afi * fr + ei,
                abr * br - abi * bi + gr, abr * bi + abi * br + gi)

    z = jnp.zeros((1, gp), F32)
    lax.fori_loop(0, n_chunks, scan_body, (z, z, z, z))

    for g in range(S5_G):
        y = _dot(a_ref[g], ktoe_ref[g])
        for kind in range(4):
            y = y + _dot(x_ref[kind, :, g * S5_P:(g + 1) * S5_P].astype(BF16), cexp_ref[g, kind])
        yg_ref[g] = y

    def regroup_out(rb, carry):
        c0 = pl.multiple_of(rb * 2 * blk, 2 * blk)
        for og, ref in enumerate((y0_ref, y1_ref)):
            for ht in range(2):
                src = [yg_ref[og * 8 + m, pl.ds(c0, 2 * blk), ht * 128:(ht + 1) * 128].astype(BF16) for m in range(8)]
                for a, val in enumerate(block_transpose(src)):
                    val = val.astype(F32)
                    for u in range(2):
                        ref[0, pl.ds((rb * 2 + u) * (blk * S5_T) + ht * 8 + a, blk, stride=S5_T), :] = \
                            val[u * blk:(u + 1) * blk]
        return carry

    lax.fori_loop(0, n_chunks // (2 * blk), regroup_out, 0)


def _s5(su0, su1, layer, bexp, cexp, pw16, skip):
    n_b, t_all, _ = su0.shape
    n_chunks = t_all // S5_T
    gp = S5_G * S5_P
    width = S5_T * S5_HG
    blk = pl.BlockSpec((1, t_all, 128), lambda b: (b, 0, 0))
    return pl.pallas_call(
        functools.partial(_s5_kernel, n_chunks=n_chunks, n_ctx_chunks=CTX_LEN // S5_T),
        out_shape=[jax.ShapeDtypeStruct((n_b, t_all, 128), F32)] * 2,
        grid=(n_b,),
        in_specs=[blk, blk, _layer_spec((S5_G, width, 4 * S5_P), layer), _layer_spec((S5_G, 4, S5_P, width), layer),
                  _layer_spec((4, gp), layer), _layer_spec((S5_G, 1, width), layer)],
        out_specs=[blk, blk],
        scratch_shapes=[pltpu.VMEM((S5_G, width, width), BF16),
                        pltpu.VMEM((S5_G, width, 4 * S5_P), BF16),
                        pltpu.VMEM((S5_G, 4, S5_P, width), BF16),
                        pltpu.VMEM((S5_G, n_chunks, width), BF16),
                        pltpu.VMEM((S5_G, n_chunks, width), F32),
                        pltpu.VMEM((4, n_chunks, gp), F32),
                        pltpu.VMEM((4, n_chunks, gp), F32)],
        compiler_params=_cp(("arbitrary",)),
        name="s5_mixer",
    )(su0, su1, bexp, cexp, pw16, skip)


def _s5_tables(lam_re, lam_im, log_dt, b_re, b_im, c_re, c_im, d_skip):
    lam_re, lam_im, log_dt = lam_re.astype(F32), lam_im.astype(F32), log_dt.astype(F32)
    b_re, b_im, c_re, c_im = b_re.astype(F32), b_im.astype(F32), c_re.astype(F32), c_im.astype(F32)
    dt = jnp.exp(log_dt)[..., None]
    n = jnp.arange(S5_T + 1, dtype=F32)[:, None, None, None]
    pw_re = jnp.exp(n * lam_re * dt) * jnp.cos(n * lam_im * dt)
    pw_im = jnp.exp(n * lam_re * dt) * jnp.sin(n * lam_im * dt)
    m = jnp.arange(S5_T, -1, -1, dtype=F32)[:, None, None, None]
    dsc_re = jnp.exp(m * lam_re * dt) * jnp.cos(m * lam_im * dt)
    dsc_im = jnp.exp(m * lam_re * dt) * jnp.sin(m * lam_im * dt)
    lb_re, lb_im = pw_re[1], pw_im[1]
    num_re, num_im = lb_re - 1.0, lb_im
    den = lam_re * lam_re + lam_im * lam_im
    coef_re = ((num_re * lam_re + num_im * lam_im) / den)[..., None]
    coef_im = ((num_im * lam_re - num_re * lam_im) / den)[..., None]
    bb_re = coef_re * b_re - coef_im * b_im
    bb_im = coef_re * b_im + coef_im * b_re

    def cmul_b(p_re, p_im, d):
        re = p_re[..., None] * bb_re[d][None] - p_im[..., None] * bb_im[d][None]
        im = p_re[..., None] * bb_im[d][None] + p_im[..., None] * bb_re[d][None]
        to = lambda z: z.transpose(1, 0, 3, 2).reshape(S5_G, S5_T * S5_HG, S5_P)
        return to(re), to(im)

    def cmul_c(p_re, p_im, d):
        re = c_re[d][None] * p_re[:, :, None, :] - c_im[d][None] * p_im[:, :, None, :]
        im = c_re[d][None] * p_im[:, :, None, :] + c_im[d][None] * p_re[:, :, None, :]
        to = lambda z: z.transpose(1, 3, 0, 2).reshape(S5_G, S5_P, S5_T * S5_HG)
        return to(re), to(-im)

    first = slice(0, S5_T)
    last = slice(1, S5_T + 1)
    bexp = jnp.concatenate(cmul_b(dsc_re[last, 0], dsc_im[last, 0], 0)
                           + cmul_b(pw_re[first, 1], pw_im[first, 1], 1), axis=-1)
    cexp = jnp.stack(cmul_c(pw_re[last, 0], pw_im[last, 0], 0)
                     + cmul_c(dsc_re[first, 1], dsc_im[first, 1], 1), axis=1)
    pw16 = jnp.stack([pw_re[S5_T, 0].reshape(-1), pw_im[S5_T, 0].reshape(-1),
                      pw_re[S5_T, 1].reshape(-1), pw_im[S5_T, 1].reshape(-1)], axis=0)
    skip = jnp.tile(d_skip.astype(F32).reshape(S5_G, 1, S5_HG), (1, 1, S5_T))
    return bexp, cexp, pw16, skip


def _pool_kernel(u_ref, w_ref, scale_ref, o_ref, pad_ref, wblk_ref, *, n_lat):
    assert POOL_WINDOWS == (2, 4, 8, 16) and POOL_GW * len(POOL_WINDOWS) == GROUP_W
    pad = max(POOL_WINDOWS) // 2
    seg = ((pad, 0, CTX_LEN), (pad + CTX_LEN + 2 * pad, CTX_LEN, n_lat))
    pad_ref[...] = jnp.zeros_like(pad_ref)
    for p0, u0, n in seg:
        pad_ref[p0:p0 + n, :] = u_ref[0, u0:u0 + n, :]
    wblk_ref[...] = jnp.zeros_like(wblk_ref)
    for i in range(len(POOL_WINDOWS)):
        wblk_ref[i * POOL_GW:(i + 1) * POOL_GW, i * POOL_GW:(i + 1) * POOL_GW] = w_ref[i].astype(BF16)
    grp = lax.broadcasted_iota(jnp.int32, (1, GROUP_W), 1) // POOL_GW
    width = jnp.left_shift(2, grp)
    rows = TILE + 2 * pad

    def up(a, k):
        return pltpu.roll(a, rows - k, 0)

    for p0, u0, n in seg:
        for t0 in range(0, n, TILE):
            w0 = pad_ref[p0 + t0 - pad:p0 + t0 - pad + rows, :]
            a1 = w0 + up(w0, 1)
            a2 = a1 + up(a1, 2)
            a3 = a2 + up(a2, 4)
            a4 = a3 + up(a3, 8)
            total = jnp.where(grp == 0, up(a1, pad - 1), jnp.where(grp == 1, up(a2, pad - 2),
                              jnp.where(grp == 2, up(a3, pad - 4), a4)))[0:TILE]
            if t0 == 0 or t0 + TILE == n:
                t = t0 + lax.broadcasted_iota(jnp.int32, (TILE, 1), 0)
                first = t - width // 2
                cnt = (jnp.clip(first + width, 0, n) - jnp.clip(first, 0, n)).astype(F32)
            else:
                cnt = width.astype(F32)
            diff = total / cnt - w0[pad:pad + TILE]
            o_ref[0, u0 + t0:u0 + t0 + TILE, :] = (_dot(diff.astype(BF16), wblk_ref[...])
                                                   * scale_ref[...]).astype(o_ref.dtype)


def _pool(pu, layer, w_pool, scale):
    n_b, t_all, _ = pu.shape
    blk = pl.BlockSpec((1, t_all, 256), lambda b: (b, 0, 0))
    return pl.pallas_call(
        functools.partial(_pool_kernel, n_lat=t_all - CTX_LEN),
        out_shape=jax.ShapeDtypeStruct((n_b, t_all, 256), BF16),
        grid=(n_b,),
        in_specs=[blk, _layer_spec((len(POOL_WINDOWS), POOL_GW, POOL_GW), layer), _layer_spec((1, 256), layer)],
        out_specs=blk,
        scratch_shapes=[pltpu.VMEM((t_all + 32, 256), F32), pltpu.VMEM((GROUP_W, GROUP_W), BF16)],
        compiler_params=_cp(("parallel",)),
        name="pool_mixer",
    )(pu, w_pool, scale)


def _gelu_tanh(x):
    return 0.5 * x * (1.0 + jnp.tanh(math.sqrt(2.0 / math.pi) * (x + 0.044715 * (x * x * x))))


def _outproj_res(res, mod_row, ygla_ref, yna_ref, ys50_ref, ys51_ref, ypool_ref, gate_ref, mod_ref, wout_ref,
                 gpost_ref, wglu_ref, bglu_ref):
    def gated_proj(part, i):
        lo, hi = i * GROUP_W, (i + 1) * GROUP_W
        gc = gate_ref[0, :, lo:hi].astype(F32)
        return _dot((part * (gc * jax.nn.sigmoid(gc))).astype(BF16), wout_ref[lo:hi, :])

    gate_m = mod_ref[pl.ds(mod_row, 1), :][:, 2 * D_MODEL:3 * D_MODEL]
    g5 = _gelu_tanh(jnp.concatenate([ys50_ref[0], ys51_ref[0]], axis=-1))
    z5 = _dot(g5.astype(BF16), wglu_ref[...])
    acc = gated_proj(ygla_ref[0], 0) + gated_proj(yna_ref[0], 1) + gated_proj(ypool_ref[0], 3)
    acc = acc + gated_proj(g5 * jax.nn.sigmoid(z5 + bglu_ref[...]), 2)
    ms = jnp.mean(acc * acc, axis=-1, keepdims=True)
    return res + gate_m * (acc * lax.rsqrt(ms + NORM_EPS) * gpost_ref[...])


N_MIX_REFS = 6
N_OUT_PARAM_REFS = 5
N_IN_PARAM_REFS = 9


def _outproj_kernel(*refs):
    x_ref, params, out_ref = refs[2 * N_MIX_REFS], refs[2 * N_MIX_REFS + 1:-1], refs[-1]
    for half in range(2):
        mix = refs[half * N_MIX_REFS:(half + 1) * N_MIX_REFS]
        rows = slice(half * TILE, (half + 1) * TILE)
        out_ref[0, rows, :] = _outproj_res(x_ref[0, rows, :], pl.program_id(0), *mix, *params)


def _outproj_inproj_kernel(*refs, n_batch):
    mix = refs[:N_MIX_REFS]
    x_ref, xc_ref = refs[N_MIX_REFS:N_MIX_REFS + 2]
    out_params = refs[N_MIX_REFS + 2:N_MIX_REFS + 2 + N_OUT_PARAM_REFS]
    rest = refs[N_MIX_REFS + 2 + N_OUT_PARAM_REFS:]
    in_params, (xo_ref, xco_ref), in_outs = rest[:N_IN_PARAM_REFS], rest[N_IN_PARAM_REFS:N_IN_PARAM_REFS + 2], \
        rest[N_IN_PARAM_REFS + 2:]
    b = pl.program_id(0)
    j = pl.program_id(1)

    def body(res_ref, out_ref, mod_row, with_rope):
        res = _outproj_res(res_ref[0], mod_row, *mix, *out_params)
        out_ref[0] = res
        _inproj_body(res, mod_row, with_rope, *in_params, *in_outs)

    pl.when(j == 0)(lambda: body(xc_ref, xco_ref, n_batch, False))
    pl.when(j > 0)(lambda: body(x_ref, xo_ref, b, True))


def _outproj_param_specs(d, layer):
    return [_layer_spec((8, 3 * d), layer), _layer_spec((MIX_W, d), layer), _layer_spec((1, d), layer),
            _layer_spec((GROUP_W, GROUP_W), layer), _layer_spec((1, GROUP_W), layer)]


def _mix_specs(index_map):
    return [pl.BlockSpec((1, TILE, w), index_map) for w in (256, 256, 128, 128, 256, MIX_W)]


def _outproj(mix, x, layer, mod_all, w_out, g_post, w_glu, b_glu):
    n_b, n_lat, d = x.shape
    assert n_lat % (2 * TILE) == 0
    pair = pl.BlockSpec((1, 2 * TILE, d), lambda b, j: (b, j, 0))
    return pl.pallas_call(
        _outproj_kernel,
        out_shape=jax.ShapeDtypeStruct(x.shape, F32),
        grid=(n_b, n_lat // (2 * TILE)),
        in_specs=_mix_specs(lambda b, j: (b, 2 * j + 1, 0)) + _mix_specs(lambda b, j: (b, 2 * j + 2, 0)) + [pair]
                 + _outproj_param_specs(d, layer),
        out_specs=pair,
        compiler_params=_cp(("parallel", "arbitrary")),
        name="gate_outproj_residual",
    )(*mix, *mix, x, mod_all, w_out, g_post, w_glu, b_glu)


def _outproj_inproj(mix, x, xc, layer, mod_all, w_out, g_post, w_glu, b_glu, g_pre, w_pad, wg, bg, rope_tabs):
    n_b, n_lat, d = x.shape
    t_all = n_lat + CTX_LEN
    in_specs, in_shape, in_out_specs = _inproj_specs(n_b, t_all, d, layer + 1)
    outs = pl.pallas_call(
        functools.partial(_outproj_inproj_kernel, n_batch=n_b),
        out_shape=[jax.ShapeDtypeStruct(x.shape, F32), jax.ShapeDtypeStruct(xc.shape, F32)] + in_shape,
        grid=(n_b, t_all // TILE),
        in_specs=_mix_specs(lambda b, j: (b, j, 0))
                 + [pl.BlockSpec((1, TILE, d), _lat_map), pl.BlockSpec((1, CTX_LEN, d), _ctx_map)]
                 + _outproj_param_specs(d, layer) + in_specs,
        out_specs=[pl.BlockSpec((1, TILE, d), _lat_map), pl.BlockSpec((1, CTX_LEN, d), _ctx_map)] + in_out_specs,
        compiler_params=_cp(("parallel", "arbitrary")),
        name="outproj_then_inproj",
    )(*mix, x, xc, mod_all, w_out, g_post, w_glu, b_glu, mod_all, g_pre, *w_pad, wg, bg, *rope_tabs)
    return outs[0], outs[1], outs[2:]


def _rope_tables(n_tok):
    t = np.arange(n_tok)
    half = GLA_DK // 2
    freqs = ROPE_BASE ** (-np.arange(0, half, 2, dtype=np.float32) / half)
    d = np.arange(GLA_DK)
    pos = np.where((d // half)[None, :] == 0, (t // GRID_W)[:, None], (t % GRID_W)[:, None]).astype(np.float32)
    ang = jnp.asarray(pos) * jnp.asarray(freqs[(d % half) % (half // 2)])[None, :]
    first = jnp.asarray(((d % half) < half // 2)[None, :])
    cos, sin = jnp.cos(ang), jnp.sin(ang)
    tabs = (cos, jnp.where(first, -sin, 0.0), jnp.where(first, 0.0, sin))
    return tuple(jnp.tile(z, (1, GLA_HEADS)).astype(F32) for z in tabs)


def _split_w_in(w_in):
    pad = jnp.zeros(w_in.shape[:2] + (N_HEAD_COLS - N_GATE_END - 128,), w_in.dtype)
    head = jnp.concatenate([w_in[:, :, GQ_COL:GQ_COL + 128], w_in[:, :, :N_GATE_END], pad], axis=2)
    return head.astype(BF16), w_in[:, :, N_GATE_END:].astype(BF16)


def _gate_weights(w_gate, b_gate):
    n_l = w_gate.shape[0]
    wg = jnp.zeros((n_l, 128, 256), F32)
    wg = wg.at[:, 0:GLA_RANK, 0:128].set(w_gate[:, 0].astype(F32))
    wg = wg.at[:, GLA_RANK:2 * GLA_RANK, 128:256].set(w_gate[:, 1].astype(F32))
    return wg.astype(BF16), b_gate.astype(F32).reshape(n_l, 1, 256)


def kernel(x, c, ctx, c_ctx, w_mod, b_mod, g_pre, g_post, w_in, w_out, gla_w_gate, gla_b_gate, gla_g_norm, na_rpb,
           s5_lam_re, s5_lam_im, s5_log_dt, s5_b_re, s5_b_im, s5_c_re, s5_c_im, s5_d, s5_w_glu, s5_b_glu,
           pool_w, pool_scale):
    n_b, n_lat, d = x.shape
    depth = w_mod.shape[0]
    rows = n_lat // GRID_W
    assert d == D_MODEL and ctx.shape[1] == CTX_LEN and n_lat % TILE == 0 and n_b < 8 and rows >= NA_KR
    c_rows = jnp.zeros((8, d), F32).at[0:n_b].set(c).at[n_b].set(c_ctx)
    mod_all = _modulation(c_rows, w_mod, b_mod)
    rope_tabs = _rope_tables(n_lat)
    w_pad = _split_w_in(w_in)
    wg, bg = _gate_weights(gla_w_gate, gla_b_gate)
    g_pre3 = g_pre.astype(F32).reshape(depth, 1, d)
    g_post3 = g_post.astype(F32).reshape(depth, 1, d)
    gn = jnp.tile(gla_g_norm.astype(F32), (1, GLA_HEADS)).reshape(depth, 1, GLA_HEADS * GLA_DV)
    rel_tab = jax.vmap(_na_rel_table)(na_rpb)
    bexp, cexp, pw16, skip = jax.vmap(_s5_tables)(s5_lam_re, s5_lam_im, s5_log_dt, s5_b_re, s5_b_im,
                                                  s5_c_re, s5_c_im, s5_d)
    w_out_b = w_out.astype(BF16)
    w_glu_b = s5_w_glu.astype(BF16)
    b_glu3 = s5_b_glu.astype(F32).reshape(depth, 1, GROUP_W)
    pool_scale3 = pool_scale.astype(F32).reshape(depth, 1, GROUP_W)
    xc = ctx
    proj = _inproj(x, xc, 0, mod_all, g_pre3, w_pad, wg, bg, rope_tabs)
    for l in range(depth):
        qk, gv, lg, nq, nkt, nv, su0, su1, pu, gate_cols = proj
        y_gla = _gla(qk, gv, lg, l, gn)
        y_na = _na(nq, nkt, nv, l, rel_tab)
        y_s50, y_s51 = _s5(su0, su1, l, bexp, cexp, pw16, skip)
        y_pool = _pool(pu, l, pool_w, pool_scale3)
        mix = (y_gla, y_na, y_s50, y_s51, y_pool, gate_cols)
        if l == depth - 1:
            x = _outproj(mix, x, l, mod_all, w_out_b, g_post3, w_glu_b, b_glu3)
        else:
            x, xc, proj = _outproj_inproj(mix, x, xc, l, mod_all, w_out_b, g_post3, w_glu_b, b_glu3,
                                          g_pre3, w_pad, wg, bg, rope_tabs)
    return x
```

```python
import functools
import math

import numpy as np
import jax
import jax.numpy as jnp
from jax import lax
from jax.experimental import pallas as pl
from jax.experimental.pallas import tpu as pltpu

F32 = jnp.float32
BF16 = jnp.bfloat16

D_MODEL = 1024
GRID_W = 64
CTX_LEN = 256
GROUP_W = 256
MIX_W = 1024
NORM_EPS = 1e-6
GLA_HEADS = 4
GLA_DV = 64
GLA_DK = 32
GLA_RANK = 16
GLA_TAU = 16.0
ROPE_BASE = 10000.0
NA_HEADS = 4
NA_DH = 64
NA_KR = 8
NA_KC = 16
S5_HG = 16
S5_G = 16
S5_P = 64
POOL_WINDOWS = (2, 4, 8, 16)
POOL_GW = 64

TILE = 256
GLA_C = 128
GLA_STEP = 2
S5_T = 16
NA_ROWS = TILE // GRID_W
NA_REL_W = 1024
NEG_BIG = -1e30

N_GATE_END = 128 + 256 + 2 * GLA_RANK
_HEAD_COL = {"gqk": (0, 256), "gv": (256, 512), "gg": (512, 640)}
_TAIL_COL = {}
_off = 0
for _name, _w in (("nk", 256), ("nv", 256), ("su", 256), ("gq_unused", 128), ("nq", 256), ("pu", 256),
                  ("gate", 1024)):
    _TAIL_COL[_name] = (_off, _off + _w)
    _off += _w
N_HEAD_COLS = _HEAD_COL["gg"][1]
N_TAIL_COLS = _off
GQ_COL = N_GATE_END + _TAIL_COL["gq_unused"][0]

_VMEM_LIMIT = 56 * 1024 * 1024


def _cp(sem, vmem=_VMEM_LIMIT):
    return pltpu.CompilerParams(dimension_semantics=sem, vmem_limit_bytes=vmem)


def _dot(a, b):
    return jnp.dot(a, b, preferred_element_type=F32)


def _dot_nt(a, b):
    return lax.dot_general(a, b, (((1,), (1,)), ((), ())), preferred_element_type=F32)


def _split_bf16(a):
    hi = a.astype(BF16)
    lo = (a - hi.astype(F32)).astype(BF16)
    return hi, lo


def _mod_kernel(s_ref, w_ref, b_ref, o_ref):
    s = s_ref[...]
    a = (s * jax.nn.sigmoid(s)).astype(BF16)
    o_ref[0] = _dot(a, w_ref[0].astype(BF16)) + b_ref[0]


def _modulation(c_rows, w_mod, b_mod):
    n_l, d, d3 = w_mod.shape
    return pl.pallas_call(
        _mod_kernel,
        out_shape=jax.ShapeDtypeStruct((n_l, 8, d3), F32),
        grid=(n_l, d3 // d),
        in_specs=[pl.BlockSpec((8, d), lambda l, j: (0, 0)),
                  pl.BlockSpec((1, d, d), lambda l, j: (l, 0, j)),
                  pl.BlockSpec((1, 1, d), lambda l, j: (l, 0, j))],
        out_specs=pl.BlockSpec((1, 8, d), lambda l, j: (l, 0, j)),
        compiler_params=_cp(("parallel", "parallel")),
        name="adaln_modulation",
    )(c_rows, w_mod, b_mod.reshape(n_l, 1, d3))


def _inproj_body(xt, mod_row, with_rope, mod_ref, gpre_ref, wa_ref, wb_ref, wg_ref, bg_ref, cos_ref, sa_ref, sb_ref,
                 qk_ref, v_ref, lg_ref, nq_ref, nk_ref, nv_ref, su0_ref, su1_ref, pu_ref, gate_ref):
    m = mod_ref[pl.ds(mod_row, 1), :]
    shift = m[:, 0:D_MODEL]
    scale = m[:, D_MODEL:2 * D_MODEL]
    ms = jnp.mean(xt * xt, axis=-1, keepdims=True)
    h = xt * lax.rsqrt(ms + NORM_EPS) * gpre_ref[...] * (1.0 + scale) + shift
    hb = h.astype(BF16)

    def proj(name):
        if name in _HEAD_COL:
            lo, hi = _HEAD_COL[name]
            return _dot(hb, wa_ref[:, lo:hi])
        lo, hi = _TAIL_COL[name]
        return _dot(hb, wb_ref[:, lo:hi])

    def rope(t):
        if not with_rope:
            return t
        return t * cos_ref[...] + pltpu.roll(t, 128 - 8, 1) * sa_ref[...] + pltpu.roll(t, 8, 1) * sb_ref[...]

    qk = proj("gqk")
    qk_ref[0, :, 0:128] = rope(qk[:, 0:128]) * (GLA_DK ** -0.5)
    qk_ref[0, :, 128:256] = rope(qk[:, 128:256])
    v_ref[0] = proj("gv")
    z = _dot(proj("gg").astype(BF16), wg_ref[...]) + bg_ref[...]
    lg_ref[0] = (jnp.minimum(z, 0.0) - jnp.log(1.0 + jnp.exp(-jnp.abs(z)))) * (1.0 / GLA_TAU)
    nq_ref[0] = (proj("nq") * (NA_DH ** -0.5)).astype(BF16)
    nk_ref[0] = proj("nk").T.astype(BF16)
    nv_ref[0] = proj("nv").astype(BF16)
    su = proj("su")
    su0_ref[0] = su[:, 0:128]
    su1_ref[0] = su[:, 128:256]
    pu_ref[0] = proj("pu")
    gate_ref[0] = proj("gate").astype(BF16)


def _inproj_kernel(x_ref, xc_ref, *refs, n_batch):
    b = pl.program_id(0)
    j = pl.program_id(1)
    pl.when(j == 0)(lambda: _inproj_body(xc_ref[0], n_batch, False, *refs))
    pl.when(j > 0)(lambda: _inproj_body(x_ref[0], b, True, *refs))


def _layer_spec(shape, layer):
    return pl.BlockSpec((None,) + tuple(shape), lambda *_: (layer,) + (0,) * len(shape))


def _lat_map(b, j):
    return (b, jnp.maximum(j - 1, 0), 0)


def _ctx_map(b, j):
    return (b, 0, 0)


def _inproj_specs(n_b, t_all, d, layer):
    tab_map = lambda b, j: (jnp.maximum(j - 1, 0), 0)
    outs = (("qk", 256, F32), ("gv", 256, F32), ("lg", 256, F32), ("nq", 256, BF16), ("nkT", 256, BF16),
            ("nv", 256, BF16), ("su0", 128, F32), ("su1", 128, F32), ("pu", 256, F32), ("gate", 1024, BF16))
    out_shape = [jax.ShapeDtypeStruct((n_b, w, t_all) if n == "nkT" else (n_b, t_all, w), dt) for n, w, dt in outs]
    out_specs = [pl.BlockSpec((1, w, TILE), lambda b, j: (b, 0, j)) if n == "nkT"
                 else pl.BlockSpec((1, TILE, w), lambda b, j: (b, j, 0)) for n, w, dt in outs]
    in_specs = [_layer_spec((8, 3 * d), layer),
                _layer_spec((1, d), layer),
                _layer_spec((d, N_HEAD_COLS), layer),
                _layer_spec((d, N_TAIL_COLS), layer),
                _layer_spec((128, 256), layer),
                _layer_spec((1, 256), layer),
                pl.BlockSpec((TILE, 128), tab_map),
                pl.BlockSpec((TILE, 128), tab_map),
                pl.BlockSpec((TILE, 128), tab_map)]
    return in_specs, out_shape, out_specs


def _inproj(x, xc, layer, mod_all, g_pre, w_pad, wg, bg, rope_tabs):
    n_b, n_lat, d = x.shape
    t_all = n_lat + CTX_LEN
    in_specs, out_shape, out_specs = _inproj_specs(n_b, t_all, d, layer)
    return pl.pallas_call(
        functools.partial(_inproj_kernel, n_batch=n_b),
        out_shape=out_shape,
        grid=(n_b, t_all // TILE),
        in_specs=[pl.BlockSpec((1, TILE, d), _lat_map), pl.BlockSpec((1, CTX_LEN, d), _ctx_map)] + in_specs,
        out_specs=out_specs,
        compiler_params=_cp(("parallel", "arbitrary")),
        name="modnorm_inproj",
    )(x, xc, mod_all, g_pre, *w_pad, wg, bg, *rope_tabs)


def _gla_kernel(qk_ref, v_ref, lg_ref, gn_ref, y_ref, stf_ref, stb_ref, of_ref, ob_ref, *, n_chunks, n_ctx_chunks):
    c_len = GLA_C
    row = lax.broadcasted_iota(jnp.int32, (c_len, c_len), 0)
    col = lax.broadcasted_iota(jnp.int32, (c_len, c_len), 1)
    tri_f = (col <= row).astype(BF16)
    tri_b = (col >= row).astype(BF16)
    arow = lax.broadcasted_iota(jnp.int32, (c_len, 4 * c_len), 0)
    acol = lax.broadcasted_iota(jnp.int32, (c_len, 4 * c_len), 1) % c_len
    amask_f = acol <= arow
    amask_b = acol >= arow
    head_k = lax.broadcasted_iota(jnp.int32, (1, 128), 1) // GLA_DK
    head_v = lax.broadcasted_iota(jnp.int32, (1, 256), 1) // GLA_DV
    bd_mask = (lax.broadcasted_iota(jnp.int32, (256, 128), 0) // GLA_DV
               == lax.broadcasted_iota(jnp.int32, (256, 128), 1) // GLA_DK)
    ones_bd = jnp.where(lax.broadcasted_iota(jnp.int32, (256, 256), 0) // GLA_DV
                        == lax.broadcasted_iota(jnp.int32, (256, 256), 1) // GLA_DV,
                        1.0 / GLA_DV, 0.0).astype(BF16)
    mid = c_len // 2

    def scan_body(i, carry):
        def bwd_chunk(pos):
            return jnp.where(pos < n_ctx_chunks, n_ctx_chunks - 1 - pos, n_chunks - 1 - (pos - n_ctx_chunks))

        dirs = [d for d in (0, 1) for _ in range(GLA_STEP)]
        chunks = [i * GLA_STEP + u for u in range(GLA_STEP)] + [bwd_chunk(i * GLA_STEP + u) for u in range(GLA_STEP)]
        jobs = range(len(dirs))
        r0 = [pl.multiple_of(c * c_len, c_len) for c in chunks]
        st_refs = (stf_ref, stb_ref)
        tri = (tri_f, tri_b)
        amask = (amask_f, amask_b)
        end = (c_len - 1, 0)
        q = [qk_ref[0, pl.ds(r0[n], c_len), 0:128] for n in jobs]
        k = [qk_ref[0, pl.ds(r0[n], c_len), 128:256] for n in jobs]
        v = [v_ref[0, pl.ds(r0[n], c_len), :] for n in jobs]
        lg = [_split_bf16(lg_ref[0, pl.ds(r0[n], c_len), 128 * dirs[n]:128 * (dirs[n] + 1)]) for n in jobs]
        cum = [_dot(tri[dirs[n]], lg[n][0]) + _dot(tri[dirs[n]], lg[n][1]) for n in jobs]
        c_ref = [cum[n][mid:mid + 1, :] for n in jobs]
        c_end = [cum[n][end[dirs[n]]:end[dirs[n]] + 1, :] for n in jobs]
        qt = [q[n] * jnp.exp(cum[n] - c_ref[n]) for n in jobs]
        kt = [k[n] * jnp.exp(c_ref[n] - cum[n]) for n in jobs]
        ks = [jnp.concatenate([jnp.where(head_k == hh, kt[n], 0.0) for hh in range(GLA_HEADS)],
                              axis=0).astype(BF16) for n in jobs]
        att = [_dot_nt(qt[n].astype(BF16), ks[n]) for n in jobs]
        qh = [(qt[n] * jnp.exp(c_ref[n])).astype(BF16) for n in jobs]
        kh = [(kt[n] * jnp.exp(c_end[n] - c_ref[n])).astype(BF16) for n in jobs]
        upd = [_dot(v[n].T.astype(BF16), kh[n]) for n in jobs]
        att = [jnp.where(amask[dirs[n]], att[n], 0.0).astype(BF16) for n in jobs]
        vb = [v[n].astype(BF16) for n in jobs]
        vs = [jnp.concatenate([jnp.where(head_v == hh, vb[n], jnp.zeros_like(vb[n])) for hh in range(GLA_HEADS)],
                              axis=0) for n in jobs]
        o_intra = [_dot(att[n], vs[n]) for n in jobs]
        outs = (of_ref, ob_ref)
        st = [st_refs[d][...] for d in (0, 1)]
        for n in jobs:
            d = dirs[n]
            outs[d][pl.ds(r0[n], c_len), :] = o_intra[n] + _dot_nt(qh[n], st[d].astype(BF16))
            st[d] = st[d] * jnp.exp(c_end[n]) + jnp.where(bd_mask, upd[n], 0.0)
        for d in (0, 1):
            st_refs[d][...] = st[d]
        return carry

    def norm_body(c, carry):
        r0 = pl.multiple_of(c * TILE, TILE)
        o = of_ref[pl.ds(r0, TILE), :] + ob_ref[pl.ds(r0, TILE), :]
        ms = _dot((o * o).astype(BF16), ones_bd)
        y_ref[0, pl.ds(r0, TILE), :] = (o * lax.rsqrt(ms + NORM_EPS) * gn_ref[...]).astype(y_ref.dtype)
        return carry

    stf_ref[...] = jnp.zeros_like(stf_ref)
    stb_ref[...] = jnp.zeros_like(stb_ref)
    lax.fori_loop(0, n_chunks // GLA_STEP, scan_body, 0, unroll=4)
    lax.fori_loop(0, n_chunks * c_len // TILE, norm_body, 0, unroll=True)


def _gla(qk, v, lg, layer, g_norm):
    n_b, t_all, _ = qk.shape
    blk = lambda w: pl.BlockSpec((1, t_all, w), lambda b: (b, 0, 0))
    return pl.pallas_call(
        functools.partial(_gla_kernel, n_chunks=t_all // GLA_C, n_ctx_chunks=CTX_LEN // GLA_C),
        out_shape=jax.ShapeDtypeStruct((n_b, t_all, 256), BF16),
        grid=(n_b,),
        in_specs=[blk(256), blk(256), blk(256), _layer_spec((1, 256), layer)],
        out_specs=blk(256),
        scratch_shapes=[pltpu.VMEM((256, 128), F32), pltpu.VMEM((256, 128), F32),
                        pltpu.VMEM((t_all, 256), F32), pltpu.VMEM((t_all, 256), F32)],
        compiler_params=_cp(("parallel",)),
        name="gla_mixer",
    )(qk, v, lg, g_norm)


def _na_kernel(q_ref, kt0_ref, kt1_ref, kt2_ref, v0_ref, v1_ref, v2_ref, ktc_ref, vc_ref, rel_ref, o_ref, *, rows):
    j = pl.program_id(1)
    left = lax.broadcasted_iota(jnp.int32, (1, 128), 1) < NA_DH
    ones_blk = jnp.ones((TILE, 128), BF16)
    ro0 = NA_KR - 1 - NA_ROWS

    def attend(with_window):
        key_tiles = [(ktc_ref, vc_ref, None)]
        if with_window:
            g = j - 1
            q_row = lax.broadcasted_iota(jnp.int32, (TILE, TILE), 0) // GRID_W
            k_row = lax.broadcasted_iota(jnp.int32, (TILE, TILE), 1) // GRID_W
            first = jnp.clip(NA_ROWS * g + q_row - NA_KR // 2, 0, rows - NA_KR) - NA_ROWS * (g - 1)
            for d, (kr, vr) in enumerate(((kt0_ref, v0_ref), (kt1_ref, v1_ref), (kt2_ref, v2_ref))):
                window = None if d == 1 else jnp.where(
                    (k_row + NA_ROWS * d >= first) & (k_row + NA_ROWS * d < first + NA_KR), 0.0, NEG_BIG)
                key_tiles.append((kr, vr, (d, window)))
        def lanes(hh):
            return slice(128 * (hh // 2), 128 * (hh // 2 + 1))

        def score(hh):
            qp = q_ref[0, :, lanes(hh)]
            zero = jnp.zeros_like(qp)
            qh = jnp.where(left, qp, zero) if hh % 2 == 0 else jnp.where(left, zero, qp)
            kt_all = jnp.concatenate([kr[0, lanes(hh), :] for kr, _, _ in key_tiles], axis=1)
            return _dot(qh, kt_all)

        def probs(hh, s):
            if with_window:
                parts = [s[:, 0:TILE]]
                for _, _, (d, window) in key_tiles[1:]:
                    rel = jnp.concatenate(
                        [rel_ref[hh, (ro0 - i) % 2, :, GRID_W * ((ro0 - i) // 2 * 2) + d * TILE:
                                 GRID_W * ((ro0 - i) // 2 * 2) + (d + 1) * TILE] for i in range(NA_ROWS)], axis=0)
                    part = s[:, (d + 1) * TILE:(d + 2) * TILE] + rel
                    parts.append(part if window is None else part + window)
                s = jnp.concatenate(parts, axis=1)
            return jnp.exp(s - jnp.max(s, axis=-1, keepdims=True)).astype(BF16)

        def attend_values(pair, p_even, p_odd):
            vaug = jnp.concatenate([jnp.concatenate([vr[0, :, lanes(2 * pair)], ones_blk], axis=1)
                                    for _, vr, _ in key_tiles], axis=0)
            acc = _dot(jnp.concatenate([p_even, p_odd], axis=0), vaug)
            o = acc[:, 0:128] / acc[:, 128:256]
            o_ref[0, :, lanes(2 * pair)] = jnp.where(left, o[0:TILE], o[TILE:2 * TILE]).astype(o_ref.dtype)

        s_next = score(0)
        p = []
        for hh in range(NA_HEADS):
            s_cur = s_next
            if hh + 1 < NA_HEADS:
                s_next = score(hh + 1)
            p.append(probs(hh, s_cur))
            if hh % 2 == 1:
                attend_values(hh // 2, p[hh - 1], p[hh])

    @pl.when(j == 0)
    def _():
        attend(False)

    @pl.when(j > 0)
    def _():
        attend(True)


def _na(nq, nkt, nv, layer, rel_tab):
    n_b, t_all, _ = nq.shape
    n_tiles = t_all // TILE
    rows = (t_all - CTX_LEN) // GRID_W

    def nbr(delta):
        return lambda b, j: (b, jnp.clip(j + delta, 1, n_tiles - 1), 0)

    def nbr_t(delta):
        return lambda b, j: (b, 0, jnp.clip(j + delta, 1, n_tiles - 1))

    tile = lambda f: pl.BlockSpec((1, TILE, 256), f)
    tile_t = lambda f: pl.BlockSpec((1, 256, TILE), f)
    return pl.pallas_call(
        functools.partial(_na_kernel, rows=rows),
        out_shape=jax.ShapeDtypeStruct((n_b, t_all, 256), BF16),
        grid=(n_b, n_tiles),
        in_specs=[tile(lambda b, j: (b, j, 0)),
                  tile_t(nbr_t(-1)), tile_t(nbr_t(0)), tile_t(nbr_t(1)),
                  tile(nbr(-1)), tile(nbr(0)), tile(nbr(1)),
                  tile_t(lambda b, j: (b, 0, 0)), tile(lambda b, j: (b, 0, 0)),
                  _layer_spec((NA_HEADS, 2, GRID_W, NA_REL_W), layer)],
        out_specs=tile(lambda b, j: (b, j, 0)),
        compiler_params=_cp(("parallel", "arbitrary")),
        name="na_mixer",
    )(nq, nkt, nkt, nkt, nv, nv, nv, nkt, nv, rel_tab)


def _na_rel_table(rpb):
    col = np.arange(GRID_W)
    cs = np.clip(col - NA_KC // 2, 0, GRID_W - NA_KC)
    col_mask = (col[None, :] >= cs[:, None]) & (col[None, :] < cs[:, None] + NA_KC)
    co = np.clip(col[None, :] - col[:, None], -(NA_KC - 1), NA_KC - 1) + (NA_KC - 1)
    onehot = (co[None] == np.arange(2 * NA_KC - 1)[:, None, None]).astype(np.float32)
    toe = jnp.einsum("hrc,cqk->hqrk", rpb.astype(F32), jnp.asarray(onehot),
                     precision=lax.Precision.HIGHEST)
    toe = jnp.where(jnp.asarray(col_mask)[None, :, None, :], toe, NEG_BIG)
    flat = toe.reshape(NA_HEADS, GRID_W, (2 * NA_KR - 1) * GRID_W)
    flat = jnp.pad(flat, ((0, 0), (0, 0), (0, NA_REL_W + GRID_W - flat.shape[-1])), constant_values=NEG_BIG)
    return jnp.stack([flat[:, :, :NA_REL_W], flat[:, :, GRID_W:]], axis=1)


def _s5_operators(b32_ref, c32_ref, pw_ref, skip_ref, ktoe_ref, bexp_ref, cexp_ref):
    tok_s = lax.broadcasted_iota(jnp.int32, (S5_T * S5_HG, S5_T * S5_HG), 0) // S5_HG
    tok_t = lax.broadcasted_iota(jnp.int32, (S5_T * S5_HG, S5_T * S5_HG), 1) // S5_HG
    diag = (lax.broadcasted_iota(jnp.int32, (S5_T * S5_HG, S5_T * S5_HG), 0)
            == lax.broadcasted_iota(jnp.int32, (S5_T * S5_HG, S5_T * S5_HG), 1))

    def dot3(a, b):
        a_hi, a_lo = _split_bf16(a)
        b_hi, b_lo = _split_bf16(b)
        return _dot(a_hi, b_hi) + _dot(a_lo, b_hi) + _dot(a_hi, b_lo)

    for g in range(S5_G):
        b32 = b32_ref[g]
        ktoe = jnp.where(diag, skip_ref[g], 0.0)
        for d, keep in ((0, tok_t >= tok_s), (1, tok_s >= tok_t)):
            p_re = pw_ref[2 * d:2 * d + 1, g * S5_P:(g + 1) * S5_P]
            p_im = pw_ref[2 * d + 1:2 * d + 2, g * S5_P:(g + 1) * S5_P]
            inv = 1.0 / (p_re * p_re + p_im * p_im)
            q_re, q_im = p_re * inv, -p_im * inv
            b_re = b32[:, (2 * d) * S5_P:(2 * d + 1) * S5_P]
            b_im = b32[:, (2 * d + 1) * S5_P:(2 * d + 2) * S5_P]
            lag = (dot3(b_re * q_re - b_im * q_im, c32_ref[g, 2 * d])
                   + dot3(b_re * q_im + b_im * q_re, c32_ref[g, 2 * d + 1]))
            ktoe = ktoe + jnp.where(keep, lag, 0.0)
        ktoe_ref[g] = ktoe.astype(BF16)
        bexp_ref[g] = b32.astype(BF16)
        cexp_ref[g] = c32_ref[g].astype(BF16)


def _s5_kernel(su0_ref, su1_ref, b32_ref, c32_ref, pw_ref, skip_ref, y0_ref, y1_ref, ktoe_ref, bexp_ref, cexp_ref,
               a_ref, yg_ref, e_ref, x_ref, *, n_chunks, n_ctx_chunks):
    blk = 8
    gp = S5_G * S5_P

    @pl.when(pl.program_id(0) == 0)
    def _():
        _s5_operators(b32_ref, c32_ref, pw_ref, skip_ref, ktoe_ref, bexp_ref, cexp_ref)

    def block_transpose(src):
        lane_blk = lax.broadcasted_iota(jnp.int32, src[0].shape, 1) // S5_HG

        def pick(vals):
            acc = vals[0]
            for k in range(1, 8):
                acc = jnp.where(lane_blk == k, vals[k], acc)
            return acc

        rolled = []
        for d in range(8):
            diag = pick([src[(k + d) % 8] for k in range(8)])
            rolled.append(pltpu.roll(diag, S5_HG * d, 1) if d else diag)
        return [pick([rolled[(m - a) % 8] for m in range(8)]) for a in range(8)]

    def regroup_in(rb, carry):
        c0 = pl.multiple_of(rb * 2 * blk, 2 * blk)
        for hg, ref in enumerate((su0_ref, su1_ref)):
            for o in range(2):
                src = [jnp.concatenate(
                    [ref[0, pl.ds((rb * 2 + u) * (blk * S5_T) + o * 8 + m, blk, stride=S5_T), :] for u in range(2)],
                    axis=0).astype(BF16) for m in range(8)]
                for a, val in enumerate(block_transpose(src)):
                    a_ref[hg * 8 + a, pl.ds(c0, 2 * blk), o * 128:(o + 1) * 128] = val
        return carry

    lax.fori_loop(0, n_chunks // (2 * blk), regroup_in, 0, unroll=True)

    for g in range(S5_G):
        e = _dot(a_ref[g], bexp_ref[g])
        for kind in range(4):
            e_ref[kind, :, g * S5_P:(g + 1) * S5_P] = e[:, kind * S5_P:(kind + 1) * S5_P]

    afr, afi, abr, abi = pw_ref[0:1, :], pw_ref[1:2, :], pw_ref[2:3, :], pw_ref[3:4, :]

    def scan_body(i, carry):
        fr, fi, br, bi = carry
        cf = i
        cb = jnp.where(i < n_ctx_chunks, n_ctx_chunks - 1 - i, n_chunks - 1 - (i - n_ctx_chunks))
        x_ref[0, pl.ds(cf, 1), :] = fr
        x_ref[1, pl.ds(cf, 1), :] = fi
        x_ref[2, pl.ds(cb, 1), :] = br
        x_ref[3, pl.ds(cb, 1), :] = bi
        er = e_ref[0, pl.ds(cf, 1), :]
        ei = e_ref[1, pl.ds(cf, 1), :]
        gr = e_ref[2, pl.ds(cb, 1), :]
        gi = e_ref[3, pl.ds(cb, 1), :]
        return (afr * fr - afi * fi + er, afr * fi + afi * fr + ei,
                abr * br - abi * bi + gr, abr * bi + abi * br + gi)

    z = jnp.zeros((1, gp), F32)
    lax.fori_loop(0, n_chunks, scan_body, (z, z, z, z), unroll=8)

    for g in range(S5_G):
        y = _dot(a_ref[g], ktoe_ref[g])
        for kind in range(4):
            y = y + _dot(x_ref[kind, :, g * S5_P:(g + 1) * S5_P].astype(BF16), cexp_ref[g, kind])
        yg_ref[g] = y

    def regroup_out(rb, carry):
        c0 = pl.multiple_of(rb * 2 * blk, 2 * blk)
        for og, ref in enumerate((y0_ref, y1_ref)):
            for ht in range(2):
                src = [yg_ref[og * 8 + m, pl.ds(c0, 2 * blk), ht * 128:(ht + 1) * 128].astype(BF16) for m in range(8)]
                for a, val in enumerate(block_transpose(src)):
                    val = val.astype(F32)
                    for u in range(2):
                        ref[0, pl.ds((rb * 2 + u) * (blk * S5_T) + ht * 8 + a, blk, stride=S5_T), :] = \
                            val[u * blk:(u + 1) * blk]
        return carry

    lax.fori_loop(0, n_chunks // (2 * blk), regroup_out, 0, unroll=True)


def _s5(su0, su1, layer, bexp, cexp, pw16, skip):
    n_b, t_all, _ = su0.shape
    n_chunks = t_all // S5_T
    gp = S5_G * S5_P
    width = S5_T * S5_HG
    blk = pl.BlockSpec((1, t_all, 128), lambda b: (b, 0, 0))
    return pl.pallas_call(
        functools.partial(_s5_kernel, n_chunks=n_chunks, n_ctx_chunks=CTX_LEN // S5_T),
        out_shape=[jax.ShapeDtypeStruct((n_b, t_all, 128), F32)] * 2,
        grid=(n_b,),
        in_specs=[blk, blk, _layer_spec((S5_G, width, 4 * S5_P), layer), _layer_spec((S5_G, 4, S5_P, width), layer),
                  _layer_spec((4, gp), layer), _layer_spec((S5_G, 1, width), layer)],
        out_specs=[blk, blk],
        scratch_shapes=[pltpu.VMEM((S5_G, width, width), BF16),
                        pltpu.VMEM((S5_G, width, 4 * S5_P), BF16),
                        pltpu.VMEM((S5_G, 4, S5_P, width), BF16),
                        pltpu.VMEM((S5_G, n_chunks, width), BF16),
                        pltpu.VMEM((S5_G, n_chunks, width), F32),
                        pltpu.VMEM((4, n_chunks, gp), F32),
                        pltpu.VMEM((4, n_chunks, gp), F32)],
        compiler_params=_cp(("arbitrary",)),
        name="s5_mixer",
    )(su0, su1, bexp, cexp, pw16, skip)


def _s5_tables(lam_re, lam_im, log_dt, b_re, b_im, c_re, c_im, d_skip):
    lam_re, lam_im, log_dt = lam_re.astype(F32), lam_im.astype(F32), log_dt.astype(F32)
    b_re, b_im, c_re, c_im = b_re.astype(F32), b_im.astype(F32), c_re.astype(F32), c_im.astype(F32)
    dt = jnp.exp(log_dt)[..., None]
    n = jnp.arange(S5_T + 1, dtype=F32)[:, None, None, None]
    pw_re = jnp.exp(n * lam_re * dt) * jnp.cos(n * lam_im * dt)
    pw_im = jnp.exp(n * lam_re * dt) * jnp.sin(n * lam_im * dt)
    m = jnp.arange(S5_T, -1, -1, dtype=F32)[:, None, None, None]
    dsc_re = jnp.exp(m * lam_re * dt) * jnp.cos(m * lam_im * dt)
    dsc_im = jnp.exp(m * lam_re * dt) * jnp.sin(m * lam_im * dt)
    lb_re, lb_im = pw_re[1], pw_im[1]
    num_re, num_im = lb_re - 1.0, lb_im
    den = lam_re * lam_re + lam_im * lam_im
    coef_re = ((num_re * lam_re + num_im * lam_im) / den)[..., None]
    coef_im = ((num_im * lam_re - num_re * lam_im) / den)[..., None]
    bb_re = coef_re * b_re - coef_im * b_im
    bb_im = coef_re * b_im + coef_im * b_re

    def cmul_b(p_re, p_im, d):
        re = p_re[..., None] * bb_re[d][None] - p_im[..., None] * bb_im[d][None]
        im = p_re[..., None] * bb_im[d][None] + p_im[..., None] * bb_re[d][None]
        to = lambda z: z.transpose(1, 0, 3, 2).reshape(S5_G, S5_T * S5_HG, S5_P)
        return to(re), to(im)

    def cmul_c(p_re, p_im, d):
        re = c_re[d][None] * p_re[:, :, None, :] - c_im[d][None] * p_im[:, :, None, :]
        im = c_re[d][None] * p_im[:, :, None, :] + c_im[d][None] * p_re[:, :, None, :]
        to = lambda z: z.transpose(1, 3, 0, 2).reshape(S5_G, S5_P, S5_T * S5_HG)
        return to(re), to(-im)

    first = slice(0, S5_T)
    last = slice(1, S5_T + 1)
    bexp = jnp.concatenate(cmul_b(dsc_re[last, 0], dsc_im[last, 0], 0)
                           + cmul_b(pw_re[first, 1], pw_im[first, 1], 1), axis=-1)
    cexp = jnp.stack(cmul_c(pw_re[last, 0], pw_im[last, 0], 0)
                     + cmul_c(dsc_re[first, 1], dsc_im[first, 1], 1), axis=1)
    pw16 = jnp.stack([pw_re[S5_T, 0].reshape(-1), pw_im[S5_T, 0].reshape(-1),
                      pw_re[S5_T, 1].reshape(-1), pw_im[S5_T, 1].reshape(-1)], axis=0)
    skip = jnp.tile(d_skip.astype(F32).reshape(S5_G, 1, S5_HG), (1, 1, S5_T))
    return bexp, cexp, pw16, skip


def _pool_kernel(u_ref, w_ref, scale_ref, o_ref, pad_ref, wblk_ref, *, n_lat):
    assert POOL_WINDOWS == (2, 4, 8, 16) and POOL_GW * len(POOL_WINDOWS) == GROUP_W
    pad = max(POOL_WINDOWS) // 2
    seg = ((pad, 0, CTX_LEN), (pad + CTX_LEN + 2 * pad, CTX_LEN, n_lat))
    pad_ref[...] = jnp.zeros_like(pad_ref)
    for p0, u0, n in seg:
        pad_ref[p0:p0 + n, :] = u_ref[0, u0:u0 + n, :]
    wblk_ref[...] = jnp.zeros_like(wblk_ref)
    for i in range(len(POOL_WINDOWS)):
        wblk_ref[i * POOL_GW:(i + 1) * POOL_GW, i * POOL_GW:(i + 1) * POOL_GW] = w_ref[i].astype(BF16)
    grp = lax.broadcasted_iota(jnp.int32, (1, GROUP_W), 1) // POOL_GW
    width = jnp.left_shift(2, grp)
    rows = TILE + 2 * pad

    def up(a, k):
        return pltpu.roll(a, rows - k, 0)

    for p0, u0, n in seg:
        for t0 in range(0, n, TILE):
            w0 = pad_ref[p0 + t0 - pad:p0 + t0 - pad + rows, :]
            a1 = w0 + up(w0, 1)
            a2 = a1 + up(a1, 2)
            a3 = a2 + up(a2, 4)
            a4 = a3 + up(a3, 8)
            total = jnp.where(grp == 0, up(a1, pad - 1), jnp.where(grp == 1, up(a2, pad - 2),
                              jnp.where(grp == 2, up(a3, pad - 4), a4)))[0:TILE]
            if t0 == 0 or t0 + TILE == n:
                t = t0 + lax.broadcasted_iota(jnp.int32, (TILE, 1), 0)
                first = t - width // 2
                cnt = (jnp.clip(first + width, 0, n) - jnp.clip(first, 0, n)).astype(F32)
            else:
                cnt = width.astype(F32)
            diff = total / cnt - w0[pad:pad + TILE]
            o_ref[0, u0 + t0:u0 + t0 + TILE, :] = (_dot(diff.astype(BF16), wblk_ref[...])
                                                   * scale_ref[...]).astype(o_ref.dtype)


def _pool(pu, layer, w_pool, scale):
    n_b, t_all, _ = pu.shape
    blk = pl.BlockSpec((1, t_all, 256), lambda b: (b, 0, 0))
    return pl.pallas_call(
        functools.partial(_pool_kernel, n_lat=t_all - CTX_LEN),
        out_shape=jax.ShapeDtypeStruct((n_b, t_all, 256), BF16),
        grid=(n_b,),
        in_specs=[blk, _layer_spec((len(POOL_WINDOWS), POOL_GW, POOL_GW), layer), _layer_spec((1, 256), layer)],
        out_specs=blk,
        scratch_shapes=[pltpu.VMEM((t_all + 32, 256), F32), pltpu.VMEM((GROUP_W, GROUP_W), BF16)],
        compiler_params=_cp(("parallel",)),
        name="pool_mixer",
    )(pu, w_pool, scale)


def _gelu_tanh(x):
    return 0.5 * x * (1.0 + jnp.tanh(math.sqrt(2.0 / math.pi) * (x + 0.044715 * (x * x * x))))


def _outproj_res(res, mod_row, ygla_ref, yna_ref, ys50_ref, ys51_ref, ypool_ref, gate_ref, mod_ref, wout_ref,
                 gpost_ref, wglu_ref, bglu_ref):
    def gated_proj(part, i):
        lo, hi = i * GROUP_W, (i + 1) * GROUP_W
        gc = gate_ref[0, :, lo:hi].astype(F32)
        return _dot((part * (gc * jax.nn.sigmoid(gc))).astype(BF16), wout_ref[lo:hi, :])

    gate_m = mod_ref[pl.ds(mod_row, 1), :][:, 2 * D_MODEL:3 * D_MODEL]
    g5 = _gelu_tanh(jnp.concatenate([ys50_ref[0], ys51_ref[0]], axis=-1))
    z5 = _dot(g5.astype(BF16), wglu_ref[...])
    acc = gated_proj(ygla_ref[0], 0) + gated_proj(yna_ref[0], 1) + gated_proj(ypool_ref[0], 3)
    acc = acc + gated_proj(g5 * jax.nn.sigmoid(z5 + bglu_ref[...]), 2)
    ms = jnp.mean(acc * acc, axis=-1, keepdims=True)
    return res + gate_m * (acc * lax.rsqrt(ms + NORM_EPS) * gpost_ref[...])


N_MIX_REFS = 6
N_OUT_PARAM_REFS = 5
N_IN_PARAM_REFS = 9


def _outproj_kernel(*refs):
    x_ref, params, out_ref = refs[2 * N_MIX_REFS], refs[2 * N_MIX_REFS + 1:-1], refs[-1]
    for half in range(2):
        mix = refs[half * N_MIX_REFS:(half + 1) * N_MIX_REFS]
        rows = slice(half * TILE, (half + 1) * TILE)
        out_ref[0, rows, :] = _outproj_res(x_ref[0, rows, :], pl.program_id(0), *mix, *params)


def _outproj_inproj_kernel(*refs, n_batch):
    mix = refs[:N_MIX_REFS]
    x_ref, xc_ref = refs[N_MIX_REFS:N_MIX_REFS + 2]
    out_params = refs[N_MIX_REFS + 2:N_MIX_REFS + 2 + N_OUT_PARAM_REFS]
    rest = refs[N_MIX_REFS + 2 + N_OUT_PARAM_REFS:]
    in_params, (xo_ref, xco_ref), in_outs = rest[:N_IN_PARAM_REFS], rest[N_IN_PARAM_REFS:N_IN_PARAM_REFS + 2], \
        rest[N_IN_PARAM_REFS + 2:]
    b = pl.program_id(0)
    j = pl.program_id(1)

    def body(res_ref, out_ref, mod_row, with_rope):
        res = _outproj_res(res_ref[0], mod_row, *mix, *out_params)
        out_ref[0] = res
        _inproj_body(res, mod_row, with_rope, *in_params, *in_outs)

    pl.when(j == 0)(lambda: body(xc_ref, xco_ref, n_batch, False))
    pl.when(j > 0)(lambda: body(x_ref, xo_ref, b, True))


def _outproj_param_specs(d, layer):
    return [_layer_spec((8, 3 * d), layer), _layer_spec((MIX_W, d), layer), _layer_spec((1, d), layer),
            _layer_spec((GROUP_W, GROUP_W), layer), _layer_spec((1, GROUP_W), layer)]


def _mix_specs(index_map):
    return [pl.BlockSpec((1, TILE, w), index_map) for w in (256, 256, 128, 128, 256, MIX_W)]


def _outproj(mix, x, layer, mod_all, w_out, g_post, w_glu, b_glu):
    n_b, n_lat, d = x.shape
    assert n_lat % (2 * TILE) == 0
    pair = pl.BlockSpec((1, 2 * TILE, d), lambda b, j: (b, j, 0))
    return pl.pallas_call(
        _outproj_kernel,
        out_shape=jax.ShapeDtypeStruct(x.shape, F32),
        grid=(n_b, n_lat // (2 * TILE)),
        in_specs=_mix_specs(lambda b, j: (b, 2 * j + 1, 0)) + _mix_specs(lambda b, j: (b, 2 * j + 2, 0)) + [pair]
                 + _outproj_param_specs(d, layer),
        out_specs=pair,
        compiler_params=_cp(("parallel", "arbitrary")),
        name="gate_outproj_residual",
    )(*mix, *mix, x, mod_all, w_out, g_post, w_glu, b_glu)


def _outproj_inproj(mix, x, xc, layer, mod_all, w_out, g_post, w_glu, b_glu, g_pre, w_pad, wg, bg, rope_tabs):
    n_b, n_lat, d = x.shape
    t_all = n_lat + CTX_LEN
    in_specs, in_shape, in_out_specs = _inproj_specs(n_b, t_all, d, layer + 1)
    outs = pl.pallas_call(
        functools.partial(_outproj_inproj_kernel, n_batch=n_b),
        out_shape=[jax.ShapeDtypeStruct(x.shape, F32), jax.ShapeDtypeStruct(xc.shape, F32)] + in_shape,
        grid=(n_b, t_all // TILE),
        in_specs=_mix_specs(lambda b, j: (b, j, 0))
                 + [pl.BlockSpec((1, TILE, d), _lat_map), pl.BlockSpec((1, CTX_LEN, d), _ctx_map)]
                 + _outproj_param_specs(d, layer) + in_specs,
        out_specs=[pl.BlockSpec((1, TILE, d), _lat_map), pl.BlockSpec((1, CTX_LEN, d), _ctx_map)] + in_out_specs,
        compiler_params=_cp(("parallel", "arbitrary")),
        name="outproj_then_inproj",
    )(*mix, x, xc, mod_all, w_out, g_post, w_glu, b_glu, mod_all, g_pre, *w_pad, wg, bg, *rope_tabs)
    return outs[0], outs[1], outs[2:]


def _rope_tables(n_tok):
    t = np.arange(n_tok)
    half = GLA_DK // 2
    freqs = ROPE_BASE ** (-np.arange(0, half, 2, dtype=np.float32) / half)
    d = np.arange(GLA_DK)
    pos = np.where((d // half)[None, :] == 0, (t // GRID_W)[:, None], (t % GRID_W)[:, None]).astype(np.float32)
    ang = jnp.asarray(pos) * jnp.asarray(freqs[(d % half) % (half // 2)])[None, :]
    first = jnp.asarray(((d % half) < half // 2)[None, :])
    cos, sin = jnp.cos(ang), jnp.sin(ang)
    tabs = (cos, jnp.where(first, -sin, 0.0), jnp.where(first, 0.0, sin))
    return tuple(jnp.tile(z, (1, GLA_HEADS)).astype(F32) for z in tabs)


def _split_w_in(w_in):
    pad = jnp.zeros(w_in.shape[:2] + (N_HEAD_COLS - N_GATE_END - 128,), w_in.dtype)
    head = jnp.concatenate([w_in[:, :, GQ_COL:GQ_COL + 128], w_in[:, :, :N_GATE_END], pad], axis=2)
    return head.astype(BF16), w_in[:, :, N_GATE_END:].astype(BF16)


def _gate_weights(w_gate, b_gate):
    n_l = w_gate.shape[0]
    wg = jnp.zeros((n_l, 128, 256), F32)
    wg = wg.at[:, 0:GLA_RANK, 0:128].set(w_gate[:, 0].astype(F32))
    wg = wg.at[:, GLA_RANK:2 * GLA_RANK, 128:256].set(w_gate[:, 1].astype(F32))
    return wg.astype(BF16), b_gate.astype(F32).reshape(n_l, 1, 256)


def kernel(x, c, ctx, c_ctx, w_mod, b_mod, g_pre, g_post, w_in, w_out, gla_w_gate, gla_b_gate, gla_g_norm, na_rpb,
           s5_lam_re, s5_lam_im, s5_log_dt, s5_b_re, s5_b_im, s5_c_re, s5_c_im, s5_d, s5_w_glu, s5_b_glu,
           pool_w, pool_scale):
    n_b, n_lat, d = x.shape
    depth = w_mod.shape[0]
    rows = n_lat // GRID_W
    assert d == D_MODEL and ctx.shape[1] == CTX_LEN and n_lat % TILE == 0 and n_b < 8 and rows >= NA_KR
    c_rows = jnp.zeros((8, d), F32).at[0:n_b].set(c).at[n_b].set(c_ctx)
    mod_all = _modulation(c_rows, w_mod, b_mod)
    rope_tabs = _rope_tables(n_lat)
    w_pad = _split_w_in(w_in)
    wg, bg = _gate_weights(gla_w_gate, gla_b_gate)
    g_pre3 = g_pre.astype(F32).reshape(depth, 1, d)
    g_post3 = g_post.astype(F32).reshape(depth, 1, d)
    gn = jnp.tile(gla_g_norm.astype(F32), (1, GLA_HEADS)).reshape(depth, 1, GLA_HEADS * GLA_DV)
    rel_tab = jax.vmap(_na_rel_table)(na_rpb)
    bexp, cexp, pw16, skip = jax.vmap(_s5_tables)(s5_lam_re, s5_lam_im, s5_log_dt, s5_b_re, s5_b_im,
                                                  s5_c_re, s5_c_im, s5_d)
    w_out_b = w_out.astype(BF16)
    w_glu_b = s5_w_glu.astype(BF16)
    b_glu3 = s5_b_glu.astype(F32).reshape(depth, 1, GROUP_W)
    pool_scale3 = pool_scale.astype(F32).reshape(depth, 1, GROUP_W)
    xc = ctx
    proj = _inproj(x, xc, 0, mod_all, g_pre3, w_pad, wg, bg, rope_tabs)
    for l in range(depth):
        qk, gv, lg, nq, nkt, nv, su0, su1, pu, gate_cols = proj
        y_gla = _gla(qk, gv, lg, l, gn)
        y_na = _na(nq, nkt, nv, l, rel_tab)
        y_s50, y_s51 = _s5(su0, su1, l, bexp, cexp, pw16, skip)
        y_pool = _pool(pu, l, pool_w, pool_scale3)
        mix = (y_gla, y_na, y_s50, y_s51, y_pool, gate_cols)
        if l == depth - 1:
            x = _outproj(mix, x, l, mod_all, w_out_b, g_post3, w_glu_b, b_glu3)
        else:
            x, xc, proj = _outproj_inproj(mix, x, xc, l, mod_all, w_out_b, g_post3, w_glu_b, b_glu3,
                                          g_pre3, w_pad, wg, bg, rope_tabs)
    return x
```

```python
import functools
import math

import numpy as np
import jax
import jax.numpy as jnp
from jax import lax
from jax.experimental import pallas as pl
from jax.experimental.pallas import tpu as pltpu

F32 = jnp.float32
BF16 = jnp.bfloat16

D_MODEL = 1024
GRID_W = 64
CTX_LEN = 256
GROUP_W = 256
MIX_W = 1024
NORM_EPS = 1e-6
GLA_HEADS = 4
GLA_DV = 64
GLA_DK = 32
GLA_RANK = 16
GLA_TAU = 16.0
ROPE_BASE = 10000.0
NA_HEADS = 4
NA_DH = 64
NA_KR = 8
NA_KC = 16
S5_HG = 16
S5_G = 16
S5_P = 64
POOL_WINDOWS = (2, 4, 8, 16)
POOL_GW = 64

TILE = 256
GLA_C = 128
GLA_STEP = 2
S5_T = 16
NA_ROWS = TILE // GRID_W
NA_REL_W = 1024
NEG_BIG = -1e30

N_GATE_END = 128 + 256 + 2 * GLA_RANK
_HEAD_COL = {"gqk": (0, 256), "gv": (256, 512), "gg": (512, 640)}
_TAIL_COL = {}
_off = 0
for _name, _w in (("nk", 256), ("nv", 256), ("su", 256), ("gq_unused", 128), ("nq", 256), ("pu", 256),
                  ("gate", 1024)):
    _TAIL_COL[_name] = (_off, _off + _w)
    _off += _w
N_HEAD_COLS = _HEAD_COL["gg"][1]
N_TAIL_COLS = _off
GQ_COL = N_GATE_END + _TAIL_COL["gq_unused"][0]

_VMEM_LIMIT = 56 * 1024 * 1024


def _cp(sem, vmem=_VMEM_LIMIT):
    return pltpu.CompilerParams(dimension_semantics=sem, vmem_limit_bytes=vmem)


def _dot(a, b):
    return jnp.dot(a, b, preferred_element_type=F32)


def _dot_nt(a, b):
    return lax.dot_general(a, b, (((1,), (1,)), ((), ())), preferred_element_type=F32)


def _split_bf16(a):
    hi = a.astype(BF16)
    lo = (a - hi.astype(F32)).astype(BF16)
    return hi, lo


def _mod_kernel(s_ref, w_ref, b_ref, o_ref):
    s = s_ref[...]
    a = (s * jax.nn.sigmoid(s)).astype(BF16)
    o_ref[0] = _dot(a, w_ref[0].astype(BF16)) + b_ref[0]


def _modulation(c_rows, w_mod, b_mod):
    n_l, d, d3 = w_mod.shape
    return pl.pallas_call(
        _mod_kernel,
        out_shape=jax.ShapeDtypeStruct((n_l, 8, d3), F32),
        grid=(n_l, d3 // d),
        in_specs=[pl.BlockSpec((8, d), lambda l, j: (0, 0)),
                  pl.BlockSpec((1, d, d), lambda l, j: (l, 0, j)),
                  pl.BlockSpec((1, 1, d), lambda l, j: (l, 0, j))],
        out_specs=pl.BlockSpec((1, 8, d), lambda l, j: (l, 0, j)),
        compiler_params=_cp(("parallel", "parallel")),
        name="adaln_modulation",
    )(c_rows, w_mod, b_mod.reshape(n_l, 1, d3))


def _inproj_body(xt, mod_row, with_rope, mod_ref, gpre_ref, wa_ref, wb_ref, wg_ref, bg_ref, cos_ref, sa_ref, sb_ref,
                 qk_ref, v_ref, lg_ref, nq_ref, nk_ref, nv_ref, su0_ref, su1_ref, pu_ref, gate_ref):
    m = mod_ref[pl.ds(mod_row, 1), :]
    shift = m[:, 0:D_MODEL]
    scale = m[:, D_MODEL:2 * D_MODEL]
    ms = jnp.mean(xt * xt, axis=-1, keepdims=True)
    h = xt * lax.rsqrt(ms + NORM_EPS) * gpre_ref[...] * (1.0 + scale) + shift
    hb = h.astype(BF16)

    def proj(name):
        if name in _HEAD_COL:
            lo, hi = _HEAD_COL[name]
            return _dot(hb, wa_ref[:, lo:hi])
        lo, hi = _TAIL_COL[name]
        return _dot(hb, wb_ref[:, lo:hi])

    def rope(t):
        if not with_rope:
            return t
        return t * cos_ref[...] + pltpu.roll(t, 128 - 8, 1) * sa_ref[...] + pltpu.roll(t, 8, 1) * sb_ref[...]

    qk = proj("gqk")
    qk_ref[0, :, 0:128] = rope(qk[:, 0:128]) * (GLA_DK ** -0.5)
    qk_ref[0, :, 128:256] = rope(qk[:, 128:256])
    v_ref[0] = proj("gv")
    z = _dot(proj("gg").astype(BF16), wg_ref[...]) + bg_ref[...]
    lg_ref[0] = (jnp.minimum(z, 0.0) - jnp.log(1.0 + jnp.exp(-jnp.abs(z)))) * (1.0 / GLA_TAU)
    nq_ref[0] = (proj("nq") * (NA_DH ** -0.5)).astype(BF16)
    nk_ref[0] = proj("nk").T.astype(BF16)
    nv_ref[0] = proj("nv").astype(BF16)
    su = proj("su")
    su0_ref[0] = su[:, 0:128]
    su1_ref[0] = su[:, 128:256]
    pu_ref[0] = proj("pu")
    gate_ref[0] = proj("gate").astype(BF16)


def _inproj_kernel(x_ref, xc_ref, *refs, n_batch):
    b = pl.program_id(0)
    j = pl.program_id(1)
    pl.when(j == 0)(lambda: _inproj_body(xc_ref[0], n_batch, False, *refs))
    pl.when(j > 0)(lambda: _inproj_body(x_ref[0], b, True, *refs))


def _layer_spec(shape, layer):
    return pl.BlockSpec((None,) + tuple(shape), lambda *_: (layer,) + (0,) * len(shape))


def _lat_map(b, j):
    return (b, jnp.maximum(j - 1, 0), 0)


def _ctx_map(b, j):
    return (b, 0, 0)


def _inproj_specs(n_b, t_all, d, layer):
    tab_map = lambda b, j: (jnp.maximum(j - 1, 0), 0)
    outs = (("qk", 256, F32), ("gv", 256, F32), ("lg", 256, F32), ("nq", 256, BF16), ("nkT", 256, BF16),
            ("nv", 256, BF16), ("su0", 128, F32), ("su1", 128, F32), ("pu", 256, F32), ("gate", 1024, BF16))
    out_shape = [jax.ShapeDtypeStruct((n_b, w, t_all) if n == "nkT" else (n_b, t_all, w), dt) for n, w, dt in outs]
    out_specs = [pl.BlockSpec((1, w, TILE), lambda b, j: (b, 0, j)) if n == "nkT"
                 else pl.BlockSpec((1, TILE, w), lambda b, j: (b, j, 0)) for n, w, dt in outs]
    in_specs = [_layer_spec((8, 3 * d), layer),
                _layer_spec((1, d), layer),
                _layer_spec((d, N_HEAD_COLS), layer),
                _layer_spec((d, N_TAIL_COLS), layer),
                _layer_spec((128, 256), layer),
                _layer_spec((1, 256), layer),
                pl.BlockSpec((TILE, 128), tab_map),
                pl.BlockSpec((TILE, 128), tab_map),
                pl.BlockSpec((TILE, 128), tab_map)]
    return in_specs, out_shape, out_specs


def _inproj(x, xc, layer, mod_all, g_pre, w_pad, wg, bg, rope_tabs):
    n_b, n_lat, d = x.shape
    t_all = n_lat + CTX_LEN
    in_specs, out_shape, out_specs = _inproj_specs(n_b, t_all, d, layer)
    return pl.pallas_call(
        functools.partial(_inproj_kernel, n_batch=n_b),
        out_shape=out_shape,
        grid=(n_b, t_all // TILE),
        in_specs=[pl.BlockSpec((1, TILE, d), _lat_map), pl.BlockSpec((1, CTX_LEN, d), _ctx_map)] + in_specs,
        out_specs=out_specs,
        compiler_params=_cp(("parallel", "arbitrary")),
        name="modnorm_inproj",
    )(x, xc, mod_all, g_pre, *w_pad, wg, bg, *rope_tabs)


def _gla_kernel(qk_ref, v_ref, lg_ref, gn_ref, y_ref, stf_ref, stb_ref, of_ref, ob_ref, *, n_chunks, n_ctx_chunks):
    c_len = GLA_C
    row = lax.broadcasted_iota(jnp.int32, (c_len, c_len), 0)
    col = lax.broadcasted_iota(jnp.int32, (c_len, c_len), 1)
    tri_f = (col <= row).astype(BF16)
    tri_b = (col >= row).astype(BF16)
    arow = lax.broadcasted_iota(jnp.int32, (c_len, 4 * c_len), 0)
    acol = lax.broadcasted_iota(jnp.int32, (c_len, 4 * c_len), 1) % c_len
    amask_f = acol <= arow
    amask_b = acol >= arow
    head_k = lax.broadcasted_iota(jnp.int32, (1, 128), 1) // GLA_DK
    head_v = lax.broadcasted_iota(jnp.int32, (1, 256), 1) // GLA_DV
    bd_mask = (lax.broadcasted_iota(jnp.int32, (256, 128), 0) // GLA_DV
               == lax.broadcasted_iota(jnp.int32, (256, 128), 1) // GLA_DK)
    ones_bd = jnp.where(lax.broadcasted_iota(jnp.int32, (256, 256), 0) // GLA_DV
                        == lax.broadcasted_iota(jnp.int32, (256, 256), 1) // GLA_DV,
                        1.0 / GLA_DV, 0.0).astype(BF16)
    mid = c_len // 2

    def scan_body(i, carry):
        def bwd_chunk(pos):
            return jnp.where(pos < n_ctx_chunks, n_ctx_chunks - 1 - pos, n_chunks - 1 - (pos - n_ctx_chunks))

        dirs = [d for d in (0, 1) for _ in range(GLA_STEP)]
        chunks = [i * GLA_STEP + u for u in range(GLA_STEP)] + [bwd_chunk(i * GLA_STEP + u) for u in range(GLA_STEP)]
        jobs = range(len(dirs))
        r0 = [pl.multiple_of(c * c_len, c_len) for c in chunks]
        st_refs = (stf_ref, stb_ref)
        tri = (tri_f, tri_b)
        amask = (amask_f, amask_b)
        end = (c_len - 1, 0)
        q = [qk_ref[0, pl.ds(r0[n], c_len), 0:128] for n in jobs]
        k = [qk_ref[0, pl.ds(r0[n], c_len), 128:256] for n in jobs]
        v = [v_ref[0, pl.ds(r0[n], c_len), :] for n in jobs]
        lg = [_split_bf16(lg_ref[0, pl.ds(r0[n], c_len), 128 * dirs[n]:128 * (dirs[n] + 1)]) for n in jobs]
        cum = [_dot(tri[dirs[n]], lg[n][0]) + _dot(tri[dirs[n]], lg[n][1]) for n in jobs]
        c_ref = [cum[n][mid:mid + 1, :] for n in jobs]
        c_end = [cum[n][end[dirs[n]]:end[dirs[n]] + 1, :] for n in jobs]
        qt = [q[n] * jnp.exp(cum[n] - c_ref[n]) for n in jobs]
        kt = [k[n] * jnp.exp(c_ref[n] - cum[n]) for n in jobs]
        ks = [jnp.concatenate([jnp.where(head_k == hh, kt[n], 0.0) for hh in range(GLA_HEADS)],
                              axis=0).astype(BF16) for n in jobs]
        att = [_dot_nt(qt[n].astype(BF16), ks[n]) for n in jobs]
        qh = [(qt[n] * jnp.exp(c_ref[n])).astype(BF16) for n in jobs]
        kh = [(kt[n] * jnp.exp(c_end[n] - c_ref[n])).astype(BF16) for n in jobs]
        upd = [_dot(v[n].T.astype(BF16), kh[n]) for n in jobs]
        att = [jnp.where(amask[dirs[n]], att[n], 0.0).astype(BF16) for n in jobs]
        vb = [v[n].astype(BF16) for n in jobs]
        vs = [jnp.concatenate([jnp.where(head_v == hh, vb[n], jnp.zeros_like(vb[n])) for hh in range(GLA_HEADS)],
                              axis=0) for n in jobs]
        o_intra = [_dot(att[n], vs[n]) for n in jobs]
        outs = (of_ref, ob_ref)
        st = [st_refs[d][...] for d in (0, 1)]
        for n in jobs:
            d = dirs[n]
            outs[d][pl.ds(r0[n], c_len), :] = o_intra[n] + _dot_nt(qh[n], st[d].astype(BF16))
            st[d] = st[d] * jnp.exp(c_end[n]) + jnp.where(bd_mask, upd[n], 0.0)
        for d in (0, 1):
            st_refs[d][...] = st[d]
        return carry

    def norm_body(c, carry):
        r0 = pl.multiple_of(c * TILE, TILE)
        o = of_ref[pl.ds(r0, TILE), :] + ob_ref[pl.ds(r0, TILE), :]
        ms = _dot((o * o).astype(BF16), ones_bd)
        y_ref[0, pl.ds(r0, TILE), :] = (o * lax.rsqrt(ms + NORM_EPS) * gn_ref[...]).astype(y_ref.dtype)
        return carry

    stf_ref[...] = jnp.zeros_like(stf_ref)
    stb_ref[...] = jnp.zeros_like(stb_ref)
    lax.fori_loop(0, n_chunks // GLA_STEP, scan_body, 0, unroll=4)
    lax.fori_loop(0, n_chunks * c_len // TILE, norm_body, 0, unroll=True)


def _gla(qk, v, lg, layer, g_norm):
    n_b, t_all, _ = qk.shape
    blk = lambda w: pl.BlockSpec((1, t_all, w), lambda b: (b, 0, 0))
    return pl.pallas_call(
        functools.partial(_gla_kernel, n_chunks=t_all // GLA_C, n_ctx_chunks=CTX_LEN // GLA_C),
        out_shape=jax.ShapeDtypeStruct((n_b, t_all, 256), BF16),
        grid=(n_b,),
        in_specs=[blk(256), blk(256), blk(256), _layer_spec((1, 256), layer)],
        out_specs=blk(256),
        scratch_shapes=[pltpu.VMEM((256, 128), F32), pltpu.VMEM((256, 128), F32),
                        pltpu.VMEM((t_all, 256), F32), pltpu.VMEM((t_all, 256), F32)],
        compiler_params=_cp(("parallel",)),
        name="gla_mixer",
    )(qk, v, lg, g_norm)


def _na_attend(q_ref, ktc_ref, vc_ref, window_tiles, rel_ref, g, rows):
    left = lax.broadcasted_iota(jnp.int32, (1, 128), 1) < NA_DH
    ones_blk = jnp.ones((TILE, 128), BF16)
    ro0 = NA_KR - 1 - NA_ROWS
    key_tiles = [(ktc_ref, vc_ref, None)]
    if window_tiles:
        q_row = lax.broadcasted_iota(jnp.int32, (TILE, TILE), 0) // GRID_W
        k_row = lax.broadcasted_iota(jnp.int32, (TILE, TILE), 1) // GRID_W
        first = jnp.clip(NA_ROWS * g + q_row - NA_KR // 2, 0, rows - NA_KR) - NA_ROWS * (g - 1)
        for d, (kr, vr) in enumerate(window_tiles):
            window = None if d == 1 else jnp.where(
                (k_row + NA_ROWS * d >= first) & (k_row + NA_ROWS * d < first + NA_KR), 0.0, NEG_BIG)
            key_tiles.append((kr, vr, (d, window)))

    def lanes(hh):
        return slice(128 * (hh // 2), 128 * (hh // 2 + 1))

    def score(hh):
        qp = q_ref[0, :, lanes(hh)]
        zero = jnp.zeros_like(qp)
        qh = jnp.where(left, qp, zero) if hh % 2 == 0 else jnp.where(left, zero, qp)
        kt_all = jnp.concatenate([kr[0, lanes(hh), :] for kr, _, _ in key_tiles], axis=1)
        return _dot(qh, kt_all)

    def probs(hh, s):
        if window_tiles:
            parts = [s[:, 0:TILE]]
            for _, _, (d, window) in key_tiles[1:]:
                rel = jnp.concatenate(
                    [rel_ref[hh, (ro0 - i) % 2, :, GRID_W * ((ro0 - i) // 2 * 2) + d * TILE:
                             GRID_W * ((ro0 - i) // 2 * 2) + (d + 1) * TILE] for i in range(NA_ROWS)], axis=0)
                part = s[:, (d + 1) * TILE:(d + 2) * TILE] + rel
                parts.append(part if window is None else part + window)
            s = jnp.concatenate(parts, axis=1)
        return jnp.exp(s - jnp.max(s, axis=-1, keepdims=True)).astype(BF16)

    def attend_values(pair, p_even, p_odd):
        vaug = jnp.concatenate([jnp.concatenate([vr[0, :, lanes(2 * pair)], ones_blk], axis=1)
                                for _, vr, _ in key_tiles], axis=0)
        acc = _dot(jnp.concatenate([p_even, p_odd], axis=0), vaug)
        o = acc[:, 0:128] / acc[:, 128:256]
        return jnp.where(left, o[0:TILE], o[TILE:2 * TILE])

    s_next = score(0)
    p, outs = [], []
    for hh in range(NA_HEADS):
        s_cur = s_next
        if hh + 1 < NA_HEADS:
            s_next = score(hh + 1)
        p.append(probs(hh, s_cur))
        if hh % 2 == 1:
            outs.append(attend_values(hh // 2, p[hh - 1], p[hh]))
    return jnp.concatenate(outs, axis=1)


def _na_kernel(q_ref, kt0_ref, kt1_ref, kt2_ref, v0_ref, v1_ref, v2_ref, ktc_ref, vc_ref, rel_ref, o_ref, *, rows):
    j = pl.program_id(1)
    window = ((kt0_ref, v0_ref), (kt1_ref, v1_ref), (kt2_ref, v2_ref))

    @pl.when(j == 0)
    def _():
        o_ref[0] = _na_attend(q_ref, ktc_ref, vc_ref, (), rel_ref, None, rows).astype(o_ref.dtype)

    @pl.when(j > 0)
    def _():
        o_ref[0] = _na_attend(q_ref, ktc_ref, vc_ref, window, rel_ref, j - 1, rows).astype(o_ref.dtype)


def _na(nq, nkt, nv, layer, rel_tab):
    n_b, t_all, _ = nq.shape
    n_tiles = t_all // TILE
    rows = (t_all - CTX_LEN) // GRID_W

    def nbr(delta):
        return lambda b, j: (b, jnp.clip(j + delta, 1, n_tiles - 1), 0)

    def nbr_t(delta):
        return lambda b, j: (b, 0, jnp.clip(j + delta, 1, n_tiles - 1))

    tile = lambda f: pl.BlockSpec((1, TILE, 256), f)
    tile_t = lambda f: pl.BlockSpec((1, 256, TILE), f)
    return pl.pallas_call(
        functools.partial(_na_kernel, rows=rows),
        out_shape=jax.ShapeDtypeStruct((n_b, t_all, 256), BF16),
        grid=(n_b, n_tiles),
        in_specs=[tile(lambda b, j: (b, j, 0)),
                  tile_t(nbr_t(-1)), tile_t(nbr_t(0)), tile_t(nbr_t(1)),
                  tile(nbr(-1)), tile(nbr(0)), tile(nbr(1)),
                  tile_t(lambda b, j: (b, 0, 0)), tile(lambda b, j: (b, 0, 0)),
                  _layer_spec((NA_HEADS, 2, GRID_W, NA_REL_W), layer)],
        out_specs=tile(lambda b, j: (b, j, 0)),
        compiler_params=_cp(("parallel", "arbitrary")),
        name="na_mixer",
    )(nq, nkt, nkt, nkt, nv, nv, nv, nkt, nv, rel_tab)


def _na_rel_table(rpb):
    col = np.arange(GRID_W)
    cs = np.clip(col - NA_KC // 2, 0, GRID_W - NA_KC)
    col_mask = (col[None, :] >= cs[:, None]) & (col[None, :] < cs[:, None] + NA_KC)
    co = np.clip(col[None, :] - col[:, None], -(NA_KC - 1), NA_KC - 1) + (NA_KC - 1)
    onehot = (co[None] == np.arange(2 * NA_KC - 1)[:, None, None]).astype(np.float32)
    toe = jnp.einsum("hrc,cqk->hqrk", rpb.astype(F32), jnp.asarray(onehot),
                     precision=lax.Precision.HIGHEST)
    toe = jnp.where(jnp.asarray(col_mask)[None, :, None, :], toe, NEG_BIG)
    flat = toe.reshape(NA_HEADS, GRID_W, (2 * NA_KR - 1) * GRID_W)
    flat = jnp.pad(flat, ((0, 0), (0, 0), (0, NA_REL_W + GRID_W - flat.shape[-1])), constant_values=NEG_BIG)
    return jnp.stack([flat[:, :, :NA_REL_W], flat[:, :, GRID_W:]], axis=1)


def _s5_operators(b32_ref, c32_ref, pw_ref, skip_ref, ktoe_ref, bexp_ref, cexp_ref):
    tok_s = lax.broadcasted_iota(jnp.int32, (S5_T * S5_HG, S5_T * S5_HG), 0) // S5_HG
    tok_t = lax.broadcasted_iota(jnp.int32, (S5_T * S5_HG, S5_T * S5_HG), 1) // S5_HG
    diag = (lax.broadcasted_iota(jnp.int32, (S5_T * S5_HG, S5_T * S5_HG), 0)
            == lax.broadcasted_iota(jnp.int32, (S5_T * S5_HG, S5_T * S5_HG), 1))

    def dot3(a, b):
        a_hi, a_lo = _split_bf16(a)
        b_hi, b_lo = _split_bf16(b)
        return _dot(a_hi, b_hi) + _dot(a_lo, b_hi) + _dot(a_hi, b_lo)

    for g in range(S5_G):
        b32 = b32_ref[g]
        ktoe = jnp.where(diag, skip_ref[g], 0.0)
        for d, keep in ((0, tok_t >= tok_s), (1, tok_s >= tok_t)):
            p_re = pw_ref[2 * d:2 * d + 1, g * S5_P:(g + 1) * S5_P]
            p_im = pw_ref[2 * d + 1:2 * d + 2, g * S5_P:(g + 1) * S5_P]
            inv = 1.0 / (p_re * p_re + p_im * p_im)
            q_re, q_im = p_re * inv, -p_im * inv
            b_re = b32[:, (2 * d) * S5_P:(2 * d + 1) * S5_P]
            b_im = b32[:, (2 * d + 1) * S5_P:(2 * d + 2) * S5_P]
            lag = (dot3(b_re * q_re - b_im * q_im, c32_ref[g, 2 * d])
                   + dot3(b_re * q_im + b_im * q_re, c32_ref[g, 2 * d + 1]))
            ktoe = ktoe + jnp.where(keep, lag, 0.0)
        ktoe_ref[g] = ktoe.astype(BF16)
        bexp_ref[g] = b32.astype(BF16)
        cexp_ref[g] = c32_ref[g].astype(BF16)


def _s5_kernel(su0_ref, su1_ref, b32_ref, c32_ref, pw_ref, skip_ref, y0_ref, y1_ref, ktoe_ref, bexp_ref, cexp_ref,
               a_ref, yg_ref, e_ref, x_ref, *, n_chunks, n_ctx_chunks):
    blk = 8
    gp = S5_G * S5_P

    @pl.when(pl.program_id(0) == 0)
    def _():
        _s5_operators(b32_ref, c32_ref, pw_ref, skip_ref, ktoe_ref, bexp_ref, cexp_ref)

    def block_transpose(src):
        lane_blk = lax.broadcasted_iota(jnp.int32, src[0].shape, 1) // S5_HG

        def pick(vals):
            acc = vals[0]
            for k in range(1, 8):
                acc = jnp.where(lane_blk == k, vals[k], acc)
            return acc

        rolled = []
        for d in range(8):
            diag = pick([src[(k + d) % 8] for k in range(8)])
            rolled.append(pltpu.roll(diag, S5_HG * d, 1) if d else diag)
        return [pick([rolled[(m - a) % 8] for m in range(8)]) for a in range(8)]

    def regroup_in(rb, carry):
        c0 = pl.multiple_of(rb * 2 * blk, 2 * blk)
        for hg, ref in enumerate((su0_ref, su1_ref)):
            for o in range(2):
                src = [jnp.concatenate(
                    [ref[0, pl.ds((rb * 2 + u) * (blk * S5_T) + o * 8 + m, blk, stride=S5_T), :] for u in range(2)],
                    axis=0).astype(BF16) for m in range(8)]
                for a, val in enumerate(block_transpose(src)):
                    a_ref[hg * 8 + a, pl.ds(c0, 2 * blk), o * 128:(o + 1) * 128] = val
        return carry

    lax.fori_loop(0, n_chunks // (2 * blk), regroup_in, 0, unroll=True)

    for g in range(S5_G):
        e = _dot(a_ref[g], bexp_ref[g])
        for kind in range(4):
            e_ref[kind, :, g * S5_P:(g + 1) * S5_P] = e[:, kind * S5_P:(kind + 1) * S5_P]

    afr, afi, abr, abi = pw_ref[0:1, :], pw_ref[1:2, :], pw_ref[2:3, :], pw_ref[3:4, :]

    def scan_body(i, carry):
        fr, fi, br, bi = carry
        cf = i
        cb = jnp.where(i < n_ctx_chunks, n_ctx_chunks - 1 - i, n_chunks - 1 - (i - n_ctx_chunks))
        x_ref[0, pl.ds(cf, 1), :] = fr
        x_ref[1, pl.ds(cf, 1), :] = fi
        x_ref[2, pl.ds(cb, 1), :] = br
        x_ref[3, pl.ds(cb, 1), :] = bi
        er = e_ref[0, pl.ds(cf, 1), :]
        ei = e_ref[1, pl.ds(cf, 1), :]
        gr = e_ref[2, pl.ds(cb, 1), :]
        gi = e_ref[3, pl.ds(cb, 1), :]
        return (afr * fr - afi * fi + er, afr * fi + afi * fr + ei,
                abr * br - abi * bi + gr, abr * bi + abi * br + gi)

    z = jnp.zeros((1, gp), F32)
    lax.fori_loop(0, n_chunks, scan_body, (z, z, z, z), unroll=8)

    for g in range(S5_G):
        y = _dot(a_ref[g], ktoe_ref[g])
        for kind in range(4):
            y = y + _dot(x_ref[kind, :, g * S5_P:(g + 1) * S5_P].astype(BF16), cexp_ref[g, kind])
        yg_ref[g] = y

    def regroup_out(rb, carry):
        c0 = pl.multiple_of(rb * 2 * blk, 2 * blk)
        for og, ref in enumerate((y0_ref, y1_ref)):
            for ht in range(2):
                src = [yg_ref[og * 8 + m, pl.ds(c0, 2 * blk), ht * 128:(ht + 1) * 128].astype(BF16) for m in range(8)]
                for a, val in enumerate(block_transpose(src)):
                    val = val.astype(F32)
                    for u in range(2):
                        ref[0, pl.ds((rb * 2 + u) * (blk * S5_T) + ht * 8 + a, blk, stride=S5_T), :] = \
                            val[u * blk:(u + 1) * blk]
        return carry

    lax.fori_loop(0, n_chunks // (2 * blk), regroup_out, 0, unroll=True)


def _s5(su0, su1, layer, bexp, cexp, pw16, skip):
    n_b, t_all, _ = su0.shape
    n_chunks = t_all // S5_T
    gp = S5_G * S5_P
    width = S5_T * S5_HG
    blk = pl.BlockSpec((1, t_all, 128), lambda b: (b, 0, 0))
    return pl.pallas_call(
        functools.partial(_s5_kernel, n_chunks=n_chunks, n_ctx_chunks=CTX_LEN // S5_T),
        out_shape=[jax.ShapeDtypeStruct((n_b, t_all, 128), F32)] * 2,
        grid=(n_b,),
        in_specs=[blk, blk, _layer_spec((S5_G, width, 4 * S5_P), layer), _layer_spec((S5_G, 4, S5_P, width), layer),
                  _layer_spec((4, gp), layer), _layer_spec((S5_G, 1, width), layer)],
        out_specs=[blk, blk],
        scratch_shapes=[pltpu.VMEM((S5_G, width, width), BF16),
                        pltpu.VMEM((S5_G, width, 4 * S5_P), BF16),
                        pltpu.VMEM((S5_G, 4, S5_P, width), BF16),
                        pltpu.VMEM((S5_G, n_chunks, width), BF16),
                        pltpu.VMEM((S5_G, n_chunks, width), F32),
                        pltpu.VMEM((4, n_chunks, gp), F32),
                        pltpu.VMEM((4, n_chunks, gp), F32)],
        compiler_params=_cp(("arbitrary",)),
        name="s5_mixer",
    )(su0, su1, bexp, cexp, pw16, skip)


def _s5_tables(lam_re, lam_im, log_dt, b_re, b_im, c_re, c_im, d_skip):
    lam_re, lam_im, log_dt = lam_re.astype(F32), lam_im.astype(F32), log_dt.astype(F32)
    b_re, b_im, c_re, c_im = b_re.astype(F32), b_im.astype(F32), c_re.astype(F32), c_im.astype(F32)
    dt = jnp.exp(log_dt)[..., None]
    n = jnp.arange(S5_T + 1, dtype=F32)[:, None, None, None]
    pw_re = jnp.exp(n * lam_re * dt) * jnp.cos(n * lam_im * dt)
    pw_im = jnp.exp(n * lam_re * dt) * jnp.sin(n * lam_im * dt)
    m = jnp.arange(S5_T, -1, -1, dtype=F32)[:, None, None, None]
    dsc_re = jnp.exp(m * lam_re * dt) * jnp.cos(m * lam_im * dt)
    dsc_im = jnp.exp(m * lam_re * dt) * jnp.sin(m * lam_im * dt)
    lb_re, lb_im = pw_re[1], pw_im[1]
    num_re, num_im = lb_re - 1.0, lb_im
    den = lam_re * lam_re + lam_im * lam_im
    coef_re = ((num_re * lam_re + num_im * lam_im) / den)[..., None]
    coef_im = ((num_im * lam_re - num_re * lam_im) / den)[..., None]
    bb_re = coef_re * b_re - coef_im * b_im
    bb_im = coef_re * b_im + coef_im * b_re

    def cmul_b(p_re, p_im, d):
        re = p_re[..., None] * bb_re[d][None] - p_im[..., None] * bb_im[d][None]
        im = p_re[..., None] * bb_im[d][None] + p_im[..., None] * bb_re[d][None]
        to = lambda z: z.transpose(1, 0, 3, 2).reshape(S5_G, S5_T * S5_HG, S5_P)
        return to(re), to(im)

    def cmul_c(p_re, p_im, d):
        re = c_re[d][None] * p_re[:, :, None, :] - c_im[d][None] * p_im[:, :, None, :]
        im = c_re[d][None] * p_im[:, :, None, :] + c_im[d][None] * p_re[:, :, None, :]
        to = lambda z: z.transpose(1, 3, 0, 2).reshape(S5_G, S5_P, S5_T * S5_HG)
        return to(re), to(-im)

    first = slice(0, S5_T)
    last = slice(1, S5_T + 1)
    bexp = jnp.concatenate(cmul_b(dsc_re[last, 0], dsc_im[last, 0], 0)
                           + cmul_b(pw_re[first, 1], pw_im[first, 1], 1), axis=-1)
    cexp = jnp.stack(cmul_c(pw_re[last, 0], pw_im[last, 0], 0)
                     + cmul_c(dsc_re[first, 1], dsc_im[first, 1], 1), axis=1)
    pw16 = jnp.stack([pw_re[S5_T, 0].reshape(-1), pw_im[S5_T, 0].reshape(-1),
                      pw_re[S5_T, 1].reshape(-1), pw_im[S5_T, 1].reshape(-1)], axis=0)
    skip = jnp.tile(d_skip.astype(F32).reshape(S5_G, 1, S5_HG), (1, 1, S5_T))
    return bexp, cexp, pw16, skip


def _pool_kernel(u_ref, w_ref, scale_ref, o_ref, pad_ref, wblk_ref, *, n_lat):
    assert POOL_WINDOWS == (2, 4, 8, 16) and POOL_GW * len(POOL_WINDOWS) == GROUP_W
    pad = max(POOL_WINDOWS) // 2
    seg = ((pad, 0, CTX_LEN), (pad + CTX_LEN + 2 * pad, CTX_LEN, n_lat))
    pad_ref[...] = jnp.zeros_like(pad_ref)
    for p0, u0, n in seg:
        pad_ref[p0:p0 + n, :] = u_ref[0, u0:u0 + n, :]
    wblk_ref[...] = jnp.zeros_like(wblk_ref)
    for i in range(len(POOL_WINDOWS)):
        wblk_ref[i * POOL_GW:(i + 1) * POOL_GW, i * POOL_GW:(i + 1) * POOL_GW] = w_ref[i].astype(BF16)
    grp = lax.broadcasted_iota(jnp.int32, (1, GROUP_W), 1) // POOL_GW
    width = jnp.left_shift(2, grp)
    rows = TILE + 2 * pad

    def up(a, k):
        return pltpu.roll(a, rows - k, 0)

    for p0, u0, n in seg:
        for t0 in range(0, n, TILE):
            w0 = pad_ref[p0 + t0 - pad:p0 + t0 - pad + rows, :]
            a1 = w0 + up(w0, 1)
            a2 = a1 + up(a1, 2)
            a3 = a2 + up(a2, 4)
            a4 = a3 + up(a3, 8)
            total = jnp.where(grp == 0, up(a1, pad - 1), jnp.where(grp == 1, up(a2, pad - 2),
                              jnp.where(grp == 2, up(a3, pad - 4), a4)))[0:TILE]
            if t0 == 0 or t0 + TILE == n:
                t = t0 + lax.broadcasted_iota(jnp.int32, (TILE, 1), 0)
                first = t - width // 2
                cnt = (jnp.clip(first + width, 0, n) - jnp.clip(first, 0, n)).astype(F32)
            else:
                cnt = width.astype(F32)
            diff = total / cnt - w0[pad:pad + TILE]
            o_ref[0, u0 + t0:u0 + t0 + TILE, :] = (_dot(diff.astype(BF16), wblk_ref[...])
                                                   * scale_ref[...]).astype(o_ref.dtype)


def _pool(pu, layer, w_pool, scale):
    n_b, t_all, _ = pu.shape
    blk = pl.BlockSpec((1, t_all, 256), lambda b: (b, 0, 0))
    return pl.pallas_call(
        functools.partial(_pool_kernel, n_lat=t_all - CTX_LEN),
        out_shape=jax.ShapeDtypeStruct((n_b, t_all, 256), BF16),
        grid=(n_b,),
        in_specs=[blk, _layer_spec((len(POOL_WINDOWS), POOL_GW, POOL_GW), layer), _layer_spec((1, 256), layer)],
        out_specs=blk,
        scratch_shapes=[pltpu.VMEM((t_all + 32, 256), F32), pltpu.VMEM((GROUP_W, GROUP_W), BF16)],
        compiler_params=_cp(("parallel",)),
        name="pool_mixer",
    )(pu, w_pool, scale)


def _gelu_tanh(x):
    return 0.5 * x * (1.0 + jnp.tanh(math.sqrt(2.0 / math.pi) * (x + 0.044715 * (x * x * x))))


def _outproj_res(res, mod_row, ygla_ref, yna, ys50_ref, ys51_ref, ypool_ref, gate_ref, mod_ref, wout_ref,
                 gpost_ref, wglu_ref, bglu_ref):
    def gated_proj(part, i):
        lo, hi = i * GROUP_W, (i + 1) * GROUP_W
        gc = gate_ref[0, :, lo:hi].astype(F32)
        return _dot((part * (gc * jax.nn.sigmoid(gc))).astype(BF16), wout_ref[lo:hi, :])

    gate_m = mod_ref[pl.ds(mod_row, 1), :][:, 2 * D_MODEL:3 * D_MODEL]
    g5 = _gelu_tanh(jnp.concatenate([ys50_ref[0], ys51_ref[0]], axis=-1))
    z5 = _dot(g5.astype(BF16), wglu_ref[...])
    acc = gated_proj(ygla_ref[0], 0) + gated_proj(ypool_ref[0], 3)
    acc = acc + gated_proj(g5 * jax.nn.sigmoid(z5 + bglu_ref[...]), 2)
    acc = acc + gated_proj(yna() if callable(yna) else yna, 1)
    ms = jnp.mean(acc * acc, axis=-1, keepdims=True)
    return res + gate_m * (acc * lax.rsqrt(ms + NORM_EPS) * gpost_ref[...])


N_MIX_REFS = 6
N_OUT_PARAM_REFS = 5
N_IN_PARAM_REFS = 9


def _na_outproj_kernel(q_ref, kt0_ref, kt1_ref, kt2_ref, v0_ref, v1_ref, v2_ref, ktc_ref, vc_ref, rel_ref,
                       ygla_ref, ys50_ref, ys51_ref, ypool_ref, gate_ref, x_ref, *rest, rows):
    params, out_ref = rest[:-1], rest[-1]
    window = ((kt0_ref, v0_ref), (kt1_ref, v1_ref), (kt2_ref, v2_ref))
    yna = lambda: _na_attend(q_ref, ktc_ref, vc_ref, window, rel_ref, pl.program_id(1), rows)
    out_ref[0] = _outproj_res(x_ref[0], pl.program_id(0), ygla_ref, yna, ys50_ref, ys51_ref, ypool_ref, gate_ref,
                              *params)


def _outproj_inproj_kernel(*refs, n_batch):
    mix = refs[:N_MIX_REFS]
    x_ref, xc_ref = refs[N_MIX_REFS:N_MIX_REFS + 2]
    out_params = refs[N_MIX_REFS + 2:N_MIX_REFS + 2 + N_OUT_PARAM_REFS]
    rest = refs[N_MIX_REFS + 2 + N_OUT_PARAM_REFS:]
    in_params, (xo_ref, xco_ref), in_outs = rest[:N_IN_PARAM_REFS], rest[N_IN_PARAM_REFS:N_IN_PARAM_REFS + 2], \
        rest[N_IN_PARAM_REFS + 2:]
    b = pl.program_id(0)
    j = pl.program_id(1)

    def body(res_ref, out_ref, mod_row, with_rope):
        res = _outproj_res(res_ref[0], mod_row, mix[0], mix[1][0], *mix[2:], *out_params)
        out_ref[0] = res
        _inproj_body(res, mod_row, with_rope, *in_params, *in_outs)

    pl.when(j == 0)(lambda: body(xc_ref, xco_ref, n_batch, False))
    pl.when(j > 0)(lambda: body(x_ref, xo_ref, b, True))


def _outproj_param_specs(d, layer):
    return [_layer_spec((8, 3 * d), layer), _layer_spec((MIX_W, d), layer), _layer_spec((1, d), layer),
            _layer_spec((GROUP_W, GROUP_W), layer), _layer_spec((1, GROUP_W), layer)]


def _mix_specs(index_map):
    return [pl.BlockSpec((1, TILE, w), index_map) for w in (256, 256, 128, 128, 256, MIX_W)]


def _na_outproj(nq, nkt, nv, rel_tab, mix, x, layer, mod_all, w_out, g_post, w_glu, b_glu):
    n_b, n_lat, d = x.shape
    n_tiles = n_lat // TILE + 1
    rows = n_lat // GRID_W

    def nbr(delta):
        return lambda b, j: (b, jnp.clip(j + 1 + delta, 1, n_tiles - 1), 0)

    def nbr_t(delta):
        return lambda b, j: (b, 0, jnp.clip(j + 1 + delta, 1, n_tiles - 1))

    here = lambda b, j: (b, j + 1, 0)
    tile = lambda f: pl.BlockSpec((1, TILE, 256), f)
    tile_t = lambda f: pl.BlockSpec((1, 256, TILE), f)
    return pl.pallas_call(
        functools.partial(_na_outproj_kernel, rows=rows),
        out_shape=jax.ShapeDtypeStruct(x.shape, F32),
        grid=(n_b, n_lat // TILE),
        in_specs=[tile(here), tile_t(nbr_t(-1)), tile_t(nbr_t(0)), tile_t(nbr_t(1)),
                  tile(nbr(-1)), tile(nbr(0)), tile(nbr(1)),
                  tile_t(lambda b, j: (b, 0, 0)), tile(lambda b, j: (b, 0, 0)),
                  _layer_spec((NA_HEADS, 2, GRID_W, NA_REL_W), layer)]
                 + [pl.BlockSpec((1, TILE, w), here) for w in (256, 128, 128, 256, MIX_W)]
                 + [pl.BlockSpec((1, TILE, d), lambda b, j: (b, j, 0))] + _outproj_param_specs(d, layer),
        out_specs=pl.BlockSpec((1, TILE, d), lambda b, j: (b, j, 0)),
        compiler_params=_cp(("parallel", "arbitrary")),
        name="na_then_outproj",
    )(nq, nkt, nkt, nkt, nv, nv, nv, nkt, nv, rel_tab, *mix, x, mod_all, w_out, g_post, w_glu, b_glu)


def _outproj_inproj(mix, x, xc, layer, mod_all, w_out, g_post, w_glu, b_glu, g_pre, w_pad, wg, bg, rope_tabs):
    n_b, n_lat, d = x.shape
    t_all = n_lat + CTX_LEN
    in_specs, in_shape, in_out_specs = _inproj_specs(n_b, t_all, d, layer + 1)
    outs = pl.pallas_call(
        functools.partial(_outproj_inproj_kernel, n_batch=n_b),
        out_shape=[jax.ShapeDtypeStruct(x.shape, F32), jax.ShapeDtypeStruct(xc.shape, F32)] + in_shape,
        grid=(n_b, t_all // TILE),
        in_specs=_mix_specs(lambda b, j: (b, j, 0))
                 + [pl.BlockSpec((1, TILE, d), _lat_map), pl.BlockSpec((1, CTX_LEN, d), _ctx_map)]
                 + _outproj_param_specs(d, layer) + in_specs,
        out_specs=[pl.BlockSpec((1, TILE, d), _lat_map), pl.BlockSpec((1, CTX_LEN, d), _ctx_map)] + in_out_specs,
        compiler_params=_cp(("parallel", "arbitrary")),
        name="outproj_then_inproj",
    )(*mix, x, xc, mod_all, w_out, g_post, w_glu, b_glu, mod_all, g_pre, *w_pad, wg, bg, *rope_tabs)
    return outs[0], outs[1], outs[2:]


def _rope_tables(n_tok):
    t = np.arange(n_tok)
    half = GLA_DK // 2
    freqs = ROPE_BASE ** (-np.arange(0, half, 2, dtype=np.float32) / half)
    d = np.arange(GLA_DK)
    pos = np.where((d // half)[None, :] == 0, (t // GRID_W)[:, None], (t % GRID_W)[:, None]).astype(np.float32)
    ang = jnp.asarray(pos) * jnp.asarray(freqs[(d % half) % (half // 2)])[None, :]
    first = jnp.asarray(((d % half) < half // 2)[None, :])
    cos, sin = jnp.cos(ang), jnp.sin(ang)
    tabs = (cos, jnp.where(first, -sin, 0.0), jnp.where(first, 0.0, sin))
    return tuple(jnp.tile(z, (1, GLA_HEADS)).astype(F32) for z in tabs)


def _split_w_in(w_in):
    pad = jnp.zeros(w_in.shape[:2] + (N_HEAD_COLS - N_GATE_END - 128,), w_in.dtype)
    head = jnp.concatenate([w_in[:, :, GQ_COL:GQ_COL + 128], w_in[:, :, :N_GATE_END], pad], axis=2)
    return head.astype(BF16), w_in[:, :, N_GATE_END:].astype(BF16)


def _gate_weights(w_gate, b_gate):
    n_l = w_gate.shape[0]
    wg = jnp.zeros((n_l, 128, 256), F32)
    wg = wg.at[:, 0:GLA_RANK, 0:128].set(w_gate[:, 0].astype(F32))
    wg = wg.at[:, GLA_RANK:2 * GLA_RANK, 128:256].set(w_gate[:, 1].astype(F32))
    return wg.astype(BF16), b_gate.astype(F32).reshape(n_l, 1, 256)


def kernel(x, c, ctx, c_ctx, w_mod, b_mod, g_pre, g_post, w_in, w_out, gla_w_gate, gla_b_gate, gla_g_norm, na_rpb,
           s5_lam_re, s5_lam_im, s5_log_dt, s5_b_re, s5_b_im, s5_c_re, s5_c_im, s5_d, s5_w_glu, s5_b_glu,
           pool_w, pool_scale):
    n_b, n_lat, d = x.shape
    depth = w_mod.shape[0]
    rows = n_lat // GRID_W
    assert d == D_MODEL and ctx.shape[1] == CTX_LEN and n_lat % TILE == 0 and n_b < 8 and rows >= NA_KR
    c_rows = jnp.zeros((8, d), F32).at[0:n_b].set(c).at[n_b].set(c_ctx)
    mod_all = _modulation(c_rows, w_mod, b_mod)
    rope_tabs = _rope_tables(n_lat)
    w_pad = _split_w_in(w_in)
    wg, bg = _gate_weights(gla_w_gate, gla_b_gate)
    g_pre3 = g_pre.astype(F32).reshape(depth, 1, d)
    g_post3 = g_post.astype(F32).reshape(depth, 1, d)
    gn = jnp.tile(gla_g_norm.astype(F32), (1, GLA_HEADS)).reshape(depth, 1, GLA_HEADS * GLA_DV)
    rel_tab = jax.vmap(_na_rel_table)(na_rpb)
    bexp, cexp, pw16, skip = jax.vmap(_s5_tables)(s5_lam_re, s5_lam_im, s5_log_dt, s5_b_re, s5_b_im,
                                                  s5_c_re, s5_c_im, s5_d)
    w_out_b = w_out.astype(BF16)
    w_glu_b = s5_w_glu.astype(BF16)
    b_glu3 = s5_b_glu.astype(F32).reshape(depth, 1, GROUP_W)
    pool_scale3 = pool_scale.astype(F32).reshape(depth, 1, GROUP_W)
    xc = ctx
    proj = _inproj(x, xc, 0, mod_all, g_pre3, w_pad, wg, bg, rope_tabs)
    for l in range(depth):
        qk, gv, lg, nq, nkt, nv, su0, su1, pu, gate_cols = proj
        y_gla = _gla(qk, gv, lg, l, gn)
        y_s50, y_s51 = _s5(su0, su1, l, bexp, cexp, pw16, skip)
        y_pool = _pool(pu, l, pool_w, pool_scale3)
        if l == depth - 1:
            x = _na_outproj(nq, nkt, nv, rel_tab, (y_gla, y_s50, y_s51, y_pool, gate_cols), x, l, mod_all,
                            w_out_b, g_post3, w_glu_b, b_glu3)
        else:
            y_na = _na(nq, nkt, nv, l, rel_tab)
            x, xc, proj = _outproj_inproj((y_gla, y_na, y_s50, y_s51, y_pool, gate_cols), x, xc, l, mod_all,
                                          w_out_b, g_post3, w_glu_b, b_glu3, g_pre3, w_pad, wg, bg, rope_tabs)
    return x
```

```python
import functools
import math

import numpy as np
import jax
import jax.numpy as jnp
from jax import lax
from jax.experimental import pallas as pl
from jax.experimental.pallas import tpu as pltpu

F32 = jnp.float32
BF16 = jnp.bfloat16

D_MODEL = 1024
GRID_W = 64
CTX_LEN = 256
GROUP_W = 256
MIX_W = 1024
NORM_EPS = 1e-6
GLA_HEADS = 4
GLA_DV = 64
GLA_DK = 32
GLA_RANK = 16
GLA_TAU = 16.0
ROPE_BASE = 10000.0
NA_HEADS = 4
NA_DH = 64
NA_KR = 8
NA_KC = 16
S5_HG = 16
S5_G = 16
S5_P = 64
POOL_WINDOWS = (2, 4, 8, 16)
POOL_GW = 64

TILE = 256
GLA_C = 128
GLA_STEP = 2
S5_T = 16
NA_ROWS = TILE // GRID_W
NA_REL_W = 1024
NEG_BIG = -1e30

N_GATE_END = 128 + 256 + 2 * GLA_RANK
_HEAD_COL = {"gqk": (0, 256), "gv": (256, 512), "gg": (512, 640)}
_TAIL_COL = {}
_off = 0
for _name, _w in (("nk", 256), ("nv", 256), ("su", 256), ("gq_unused", 128), ("nq", 256), ("pu", 256),
                  ("gate", 1024)):
    _TAIL_COL[_name] = (_off, _off + _w)
    _off += _w
N_HEAD_COLS = _HEAD_COL["gg"][1]
N_TAIL_COLS = _off
GQ_COL = N_GATE_END + _TAIL_COL["gq_unused"][0]

_VMEM_LIMIT = 56 * 1024 * 1024


def _cp(sem, vmem=_VMEM_LIMIT):
    return pltpu.CompilerParams(dimension_semantics=sem, vmem_limit_bytes=vmem)


def _dot(a, b):
    return jnp.dot(a, b, preferred_element_type=F32)


def _dot_nt(a, b):
    return lax.dot_general(a, b, (((1,), (1,)), ((), ())), preferred_element_type=F32)


def _split_bf16(a):
    hi = a.astype(BF16)
    lo = (a - hi.astype(F32)).astype(BF16)
    return hi, lo


def _mod_kernel(s_ref, w_ref, b_ref, o_ref):
    s = s_ref[...]
    a = (s * jax.nn.sigmoid(s)).astype(BF16)
    o_ref[0] = _dot(a, w_ref[0].astype(BF16)) + b_ref[0]


def _modulation(c_rows, w_mod, b_mod):
    n_l, d, d3 = w_mod.shape
    return pl.pallas_call(
        _mod_kernel,
        out_shape=jax.ShapeDtypeStruct((n_l, 8, d3), F32),
        grid=(n_l, d3 // d),
        in_specs=[pl.BlockSpec((8, d), lambda l, j: (0, 0)),
                  pl.BlockSpec((1, d, d), lambda l, j: (l, 0, j)),
                  pl.BlockSpec((1, 1, d), lambda l, j: (l, 0, j))],
        out_specs=pl.BlockSpec((1, 8, d), lambda l, j: (l, 0, j)),
        compiler_params=_cp(("parallel", "parallel")),
        name="adaln_modulation",
    )(c_rows, w_mod, b_mod.reshape(n_l, 1, d3))


def _inproj_body(xt, mod_row, with_rope, mod_ref, gpre_ref, wa_ref, wb_ref, wg_ref, bg_ref, cos_ref, sa_ref, sb_ref,
                 qk_ref, v_ref, lg_ref, nq_ref, nk_ref, nv_ref, su0_ref, su1_ref, pu_ref, gate_ref):
    m = mod_ref[pl.ds(mod_row, 1), :]
    shift = m[:, 0:D_MODEL]
    scale = m[:, D_MODEL:2 * D_MODEL]
    ms = jnp.mean(xt * xt, axis=-1, keepdims=True)
    h = xt * lax.rsqrt(ms + NORM_EPS) * gpre_ref[...] * (1.0 + scale) + shift
    hb = h.astype(BF16)

    def proj(name):
        if name in _HEAD_COL:
            lo, hi = _HEAD_COL[name]
            return _dot(hb, wa_ref[:, lo:hi])
        lo, hi = _TAIL_COL[name]
        return _dot(hb, wb_ref[:, lo:hi])

    def rope(t):
        if not with_rope:
            return t
        return t * cos_ref[...] + pltpu.roll(t, 128 - 8, 1) * sa_ref[...] + pltpu.roll(t, 8, 1) * sb_ref[...]

    qk = proj("gqk")
    qk_ref[0, :, 0:128] = rope(qk[:, 0:128]) * (GLA_DK ** -0.5)
    qk_ref[0, :, 128:256] = rope(qk[:, 128:256])
    v_ref[0] = proj("gv")
    z = _dot(proj("gg").astype(BF16), wg_ref[...]) + bg_ref[...]
    lg_ref[0] = (jnp.minimum(z, 0.0) - jnp.log(1.0 + jnp.exp(-jnp.abs(z)))) * (1.0 / GLA_TAU)
    nq_ref[0] = (proj("nq") * (NA_DH ** -0.5)).astype(BF16)
    nk_ref[0] = proj("nk").T.astype(BF16)
    nv_ref[0] = proj("nv").astype(BF16)
    su = proj("su")
    su0_ref[0] = su[:, 0:128]
    su1_ref[0] = su[:, 128:256]
    pu_ref[0] = proj("pu")
    gate_ref[0] = proj("gate").astype(BF16)


def _inproj_kernel(x_ref, xc_ref, *refs, n_batch):
    b = pl.program_id(0)
    j = pl.program_id(1)
    pl.when(j == 0)(lambda: _inproj_body(xc_ref[0], n_batch, False, *refs))
    pl.when(j > 0)(lambda: _inproj_body(x_ref[0], b, True, *refs))


def _layer_spec(shape, layer):
    return pl.BlockSpec((None,) + tuple(shape), lambda *_: (layer,) + (0,) * len(shape))


def _lat_map(b, j):
    return (b, jnp.maximum(j - 1, 0), 0)


def _ctx_map(b, j):
    return (b, 0, 0)


def _inproj_specs(n_b, t_all, d, layer):
    tab_map = lambda b, j: (jnp.maximum(j - 1, 0), 0)
    outs = (("qk", 256, F32), ("gv", 256, F32), ("lg", 256, F32), ("nq", 256, BF16), ("nkT", 256, BF16),
            ("nv", 256, BF16), ("su0", 128, F32), ("su1", 128, F32), ("pu", 256, F32), ("gate", 1024, BF16))
    out_shape = [jax.ShapeDtypeStruct((n_b, w, t_all) if n == "nkT" else (n_b, t_all, w), dt) for n, w, dt in outs]
    out_specs = [pl.BlockSpec((1, w, TILE), lambda b, j: (b, 0, j)) if n == "nkT"
                 else pl.BlockSpec((1, TILE, w), lambda b, j: (b, j, 0)) for n, w, dt in outs]
    in_specs = [_layer_spec((8, 3 * d), layer),
                _layer_spec((1, d), layer),
                _layer_spec((d, N_HEAD_COLS), layer),
                _layer_spec((d, N_TAIL_COLS), layer),
                _layer_spec((128, 256), layer),
                _layer_spec((1, 256), layer),
                pl.BlockSpec((TILE, 128), tab_map),
                pl.BlockSpec((TILE, 128), tab_map),
                pl.BlockSpec((TILE, 128), tab_map)]
    return in_specs, out_shape, out_specs


def _inproj(x, xc, layer, mod_all, g_pre, w_pad, wg, bg, rope_tabs):
    n_b, n_lat, d = x.shape
    t_all = n_lat + CTX_LEN
    in_specs, out_shape, out_specs = _inproj_specs(n_b, t_all, d, layer)
    return pl.pallas_call(
        functools.partial(_inproj_kernel, n_batch=n_b),
        out_shape=out_shape,
        grid=(n_b, t_all // TILE),
        in_specs=[pl.BlockSpec((1, TILE, d), _lat_map), pl.BlockSpec((1, CTX_LEN, d), _ctx_map)] + in_specs,
        out_specs=out_specs,
        compiler_params=_cp(("parallel", "arbitrary")),
        name="modnorm_inproj",
    )(x, xc, mod_all, g_pre, *w_pad, wg, bg, *rope_tabs)


def _gla_kernel(qk_ref, v_ref, lg_ref, gn_ref, y_ref, stf_ref, stb_ref, of_ref, ob_ref, *, n_chunks, n_ctx_chunks):
    c_len = GLA_C
    row = lax.broadcasted_iota(jnp.int32, (c_len, c_len), 0)
    col = lax.broadcasted_iota(jnp.int32, (c_len, c_len), 1)
    tri_f = (col <= row).astype(BF16)
    tri_b = (col >= row).astype(BF16)
    arow = lax.broadcasted_iota(jnp.int32, (c_len, 4 * c_len), 0)
    acol = lax.broadcasted_iota(jnp.int32, (c_len, 4 * c_len), 1) % c_len
    amask_f = acol <= arow
    amask_b = acol >= arow
    head_k = lax.broadcasted_iota(jnp.int32, (1, 128), 1) // GLA_DK
    head_v = lax.broadcasted_iota(jnp.int32, (1, 256), 1) // GLA_DV
    bd_mask = (lax.broadcasted_iota(jnp.int32, (256, 128), 0) // GLA_DV
               == lax.broadcasted_iota(jnp.int32, (256, 128), 1) // GLA_DK)
    ones_bd = jnp.where(lax.broadcasted_iota(jnp.int32, (256, 256), 0) // GLA_DV
                        == lax.broadcasted_iota(jnp.int32, (256, 256), 1) // GLA_DV,
                        1.0 / GLA_DV, 0.0).astype(BF16)
    mid = c_len // 2

    def scan_body(i, carry):
        def bwd_chunk(pos):
            return jnp.where(pos < n_ctx_chunks, n_ctx_chunks - 1 - pos, n_chunks - 1 - (pos - n_ctx_chunks))

        dirs = [d for d in (0, 1) for _ in range(GLA_STEP)]
        chunks = [i * GLA_STEP + u for u in range(GLA_STEP)] + [bwd_chunk(i * GLA_STEP + u) for u in range(GLA_STEP)]
        jobs = range(len(dirs))
        r0 = [pl.multiple_of(c * c_len, c_len) for c in chunks]
        st_refs = (stf_ref, stb_ref)
        tri = (tri_f, tri_b)
        amask = (amask_f, amask_b)
        end = (c_len - 1, 0)
        q = [qk_ref[0, pl.ds(r0[n], c_len), 0:128] for n in jobs]
        k = [qk_ref[0, pl.ds(r0[n], c_len), 128:256] for n in jobs]
        v = [v_ref[0, pl.ds(r0[n], c_len), :] for n in jobs]
        lg = [_split_bf16(lg_ref[0, pl.ds(r0[n], c_len), 128 * dirs[n]:128 * (dirs[n] + 1)]) for n in jobs]
        cum = [_dot(tri[dirs[n]], lg[n][0]) + _dot(tri[dirs[n]], lg[n][1]) for n in jobs]
        c_ref = [cum[n][mid:mid + 1, :] for n in jobs]
        c_end = [cum[n][end[dirs[n]]:end[dirs[n]] + 1, :] for n in jobs]
        qt = [q[n] * jnp.exp(cum[n] - c_ref[n]) for n in jobs]
        kt = [k[n] * jnp.exp(c_ref[n] - cum[n]) for n in jobs]
        ks = [jnp.concatenate([jnp.where(head_k == hh, kt[n], 0.0) for hh in range(GLA_HEADS)],
                              axis=0).astype(BF16) for n in jobs]
        att = [_dot_nt(qt[n].astype(BF16), ks[n]) for n in jobs]
        qh = [(qt[n] * jnp.exp(c_ref[n])).astype(BF16) for n in jobs]
        kh = [(kt[n] * jnp.exp(c_end[n] - c_ref[n])).astype(BF16) for n in jobs]
        upd = [_dot(v[n].T.astype(BF16), kh[n]) for n in jobs]
        att = [jnp.where(amask[dirs[n]], att[n], 0.0).astype(BF16) for n in jobs]
        vb = [v[n].astype(BF16) for n in jobs]
        vs = [jnp.concatenate([jnp.where(head_v == hh, vb[n], jnp.zeros_like(vb[n])) for hh in range(GLA_HEADS)],
                              axis=0) for n in jobs]
        o_intra = [_dot(att[n], vs[n]) for n in jobs]
        outs = (of_ref, ob_ref)
        st = [st_refs[d][...] for d in (0, 1)]
        for n in jobs:
            d = dirs[n]
            outs[d][pl.ds(r0[n], c_len), :] = o_intra[n] + _dot_nt(qh[n], st[d].astype(BF16))
            st[d] = st[d] * jnp.exp(c_end[n]) + jnp.where(bd_mask, upd[n], 0.0)
        for d in (0, 1):
            st_refs[d][...] = st[d]
        return carry

    def norm_body(c, carry):
        r0 = pl.multiple_of(c * TILE, TILE)
        o = of_ref[pl.ds(r0, TILE), :] + ob_ref[pl.ds(r0, TILE), :]
        ms = _dot((o * o).astype(BF16), ones_bd)
        y_ref[0, pl.ds(r0, TILE), :] = (o * lax.rsqrt(ms + NORM_EPS) * gn_ref[...]).astype(y_ref.dtype)
        return carry

    stf_ref[...] = jnp.zeros_like(stf_ref)
    stb_ref[...] = jnp.zeros_like(stb_ref)
    lax.fori_loop(0, n_chunks // GLA_STEP, scan_body, 0, unroll=4)
    lax.fori_loop(0, n_chunks * c_len // TILE, norm_body, 0, unroll=True)


def _gla(qk, v, lg, layer, g_norm):
    n_b, t_all, _ = qk.shape
    blk = lambda w: pl.BlockSpec((1, t_all, w), lambda b: (b, 0, 0))
    return pl.pallas_call(
        functools.partial(_gla_kernel, n_chunks=t_all // GLA_C, n_ctx_chunks=CTX_LEN // GLA_C),
        out_shape=jax.ShapeDtypeStruct((n_b, t_all, 256), BF16),
        grid=(n_b,),
        in_specs=[blk(256), blk(256), blk(256), _layer_spec((1, 256), layer)],
        out_specs=blk(256),
        scratch_shapes=[pltpu.VMEM((256, 128), F32), pltpu.VMEM((256, 128), F32),
                        pltpu.VMEM((t_all, 256), F32), pltpu.VMEM((t_all, 256), F32)],
        compiler_params=_cp(("parallel",)),
        name="gla_mixer",
    )(qk, v, lg, g_norm)


def _na_attend(q_ref, ktc_ref, vc_ref, window_tiles, rel_ref, g, rows):
    left = lax.broadcasted_iota(jnp.int32, (1, 128), 1) < NA_DH
    ones_blk = jnp.ones((TILE, 128), BF16)
    ro0 = NA_KR - 1 - NA_ROWS
    key_tiles = [(ktc_ref, vc_ref, None)]
    if window_tiles:
        q_row = lax.broadcasted_iota(jnp.int32, (TILE, TILE), 0) // GRID_W
        k_row = lax.broadcasted_iota(jnp.int32, (TILE, TILE), 1) // GRID_W
        first = jnp.clip(NA_ROWS * g + q_row - NA_KR // 2, 0, rows - NA_KR) - NA_ROWS * (g - 1)
        for d, (kr, vr) in enumerate(window_tiles):
            window = None if d == 1 else jnp.where(
                (k_row + NA_ROWS * d >= first) & (k_row + NA_ROWS * d < first + NA_KR), 0.0, NEG_BIG)
            key_tiles.append((kr, vr, (d, window)))

    def lanes(hh):
        return slice(128 * (hh // 2), 128 * (hh // 2 + 1))

    def score(hh):
        qp = q_ref[0, :, lanes(hh)]
        zero = jnp.zeros_like(qp)
        qh = jnp.where(left, qp, zero) if hh % 2 == 0 else jnp.where(left, zero, qp)
        kt_all = jnp.concatenate([kr[0, lanes(hh), :] for kr, _, _ in key_tiles], axis=1)
        return _dot(qh, kt_all)

    def probs(hh, s):
        if window_tiles:
            parts = [s[:, 0:TILE]]
            for _, _, (d, window) in key_tiles[1:]:
                rel = jnp.concatenate(
                    [rel_ref[hh, (ro0 - i) % 2, :, GRID_W * ((ro0 - i) // 2 * 2) + d * TILE:
                             GRID_W * ((ro0 - i) // 2 * 2) + (d + 1) * TILE] for i in range(NA_ROWS)], axis=0)
                part = s[:, (d + 1) * TILE:(d + 2) * TILE] + rel
                parts.append(part if window is None else part + window)
            s = jnp.concatenate(parts, axis=1)
        return jnp.exp(s - jnp.max(s, axis=-1, keepdims=True)).astype(BF16)

    def attend_values(pair, p_even, p_odd):
        vaug = jnp.concatenate([jnp.concatenate([vr[0, :, lanes(2 * pair)], ones_blk], axis=1)
                                for _, vr, _ in key_tiles], axis=0)
        acc = _dot(jnp.concatenate([p_even, p_odd], axis=0), vaug)
        o = acc[:, 0:128] / acc[:, 128:256]
        return jnp.where(left, o[0:TILE], o[TILE:2 * TILE])

    s_next = score(0)
    p, outs = [], []
    for hh in range(NA_HEADS):
        s_cur = s_next
        if hh + 1 < NA_HEADS:
            s_next = score(hh + 1)
        p.append(probs(hh, s_cur))
        if hh % 2 == 1:
            outs.append(attend_values(hh // 2, p[hh - 1], p[hh]))
    return jnp.concatenate(outs, axis=1)


def _na_rel_table(rpb):
    col = np.arange(GRID_W)
    cs = np.clip(col - NA_KC // 2, 0, GRID_W - NA_KC)
    col_mask = (col[None, :] >= cs[:, None]) & (col[None, :] < cs[:, None] + NA_KC)
    co = np.clip(col[None, :] - col[:, None], -(NA_KC - 1), NA_KC - 1) + (NA_KC - 1)
    onehot = (co[None] == np.arange(2 * NA_KC - 1)[:, None, None]).astype(np.float32)
    toe = jnp.einsum("hrc,cqk->hqrk", rpb.astype(F32), jnp.asarray(onehot),
                     precision=lax.Precision.HIGHEST)
    toe = jnp.where(jnp.asarray(col_mask)[None, :, None, :], toe, NEG_BIG)
    flat = toe.reshape(NA_HEADS, GRID_W, (2 * NA_KR - 1) * GRID_W)
    flat = jnp.pad(flat, ((0, 0), (0, 0), (0, NA_REL_W + GRID_W - flat.shape[-1])), constant_values=NEG_BIG)
    return jnp.stack([flat[:, :, :NA_REL_W], flat[:, :, GRID_W:]], axis=1)


def _s5_operators(b32_ref, c32_ref, pw_ref, skip_ref, ktoe_ref, bexp_ref, cexp_ref):
    tok_s = lax.broadcasted_iota(jnp.int32, (S5_T * S5_HG, S5_T * S5_HG), 0) // S5_HG
    tok_t = lax.broadcasted_iota(jnp.int32, (S5_T * S5_HG, S5_T * S5_HG), 1) // S5_HG
    diag = (lax.broadcasted_iota(jnp.int32, (S5_T * S5_HG, S5_T * S5_HG), 0)
            == lax.broadcasted_iota(jnp.int32, (S5_T * S5_HG, S5_T * S5_HG), 1))

    def dot3(a, b):
        a_hi, a_lo = _split_bf16(a)
        b_hi, b_lo = _split_bf16(b)
        return _dot(a_hi, b_hi) + _dot(a_lo, b_hi) + _dot(a_hi, b_lo)

    for g in range(S5_G):
        b32 = b32_ref[g]
        ktoe = jnp.where(diag, skip_ref[g], 0.0)
        for d, keep in ((0, tok_t >= tok_s), (1, tok_s >= tok_t)):
            p_re = pw_ref[2 * d:2 * d + 1, g * S5_P:(g + 1) * S5_P]
            p_im = pw_ref[2 * d + 1:2 * d + 2, g * S5_P:(g + 1) * S5_P]
            inv = 1.0 / (p_re * p_re + p_im * p_im)
            q_re, q_im = p_re * inv, -p_im * inv
            b_re = b32[:, (2 * d) * S5_P:(2 * d + 1) * S5_P]
            b_im = b32[:, (2 * d + 1) * S5_P:(2 * d + 2) * S5_P]
            lag = (dot3(b_re * q_re - b_im * q_im, c32_ref[g, 2 * d])
                   + dot3(b_re * q_im + b_im * q_re, c32_ref[g, 2 * d + 1]))
            ktoe = ktoe + jnp.where(keep, lag, 0.0)
        ktoe_ref[g] = ktoe.astype(BF16)
        bexp_ref[g] = b32.astype(BF16)
        cexp_ref[g] = c32_ref[g].astype(BF16)


def _s5_kernel(su0_ref, su1_ref, b32_ref, c32_ref, pw_ref, skip_ref, y0_ref, y1_ref, ktoe_ref, bexp_ref, cexp_ref,
               a_ref, yg_ref, e_ref, x_ref, *, n_chunks, n_ctx_chunks):
    blk = 8
    gp = S5_G * S5_P

    @pl.when(pl.program_id(0) == 0)
    def _():
        _s5_operators(b32_ref, c32_ref, pw_ref, skip_ref, ktoe_ref, bexp_ref, cexp_ref)

    def block_transpose(src):
        lane_blk = lax.broadcasted_iota(jnp.int32, src[0].shape, 1) // S5_HG

        def pick(vals):
            acc = vals[0]
            for k in range(1, 8):
                acc = jnp.where(lane_blk == k, vals[k], acc)
            return acc

        rolled = []
        for d in range(8):
            diag = pick([src[(k + d) % 8] for k in range(8)])
            rolled.append(pltpu.roll(diag, S5_HG * d, 1) if d else diag)
        return [pick([rolled[(m - a) % 8] for m in range(8)]) for a in range(8)]

    def regroup_in(rb, carry):
        c0 = pl.multiple_of(rb * 2 * blk, 2 * blk)
        for hg, ref in enumerate((su0_ref, su1_ref)):
            for o in range(2):
                src = [jnp.concatenate(
                    [ref[0, pl.ds((rb * 2 + u) * (blk * S5_T) + o * 8 + m, blk, stride=S5_T), :] for u in range(2)],
                    axis=0).astype(BF16) for m in range(8)]
                for a, val in enumerate(block_transpose(src)):
                    a_ref[hg * 8 + a, pl.ds(c0, 2 * blk), o * 128:(o + 1) * 128] = val
        return carry

    lax.fori_loop(0, n_chunks // (2 * blk), regroup_in, 0, unroll=True)

    for g in range(S5_G):
        e = _dot(a_ref[g], bexp_ref[g])
        for kind in range(4):
            e_ref[kind, :, g * S5_P:(g + 1) * S5_P] = e[:, kind * S5_P:(kind + 1) * S5_P]

    afr, afi, abr, abi = pw_ref[0:1, :], pw_ref[1:2, :], pw_ref[2:3, :], pw_ref[3:4, :]

    def scan_body(i, carry):
        fr, fi, br, bi = carry
        cf = i
        cb = jnp.where(i < n_ctx_chunks, n_ctx_chunks - 1 - i, n_chunks - 1 - (i - n_ctx_chunks))
        x_ref[0, pl.ds(cf, 1), :] = fr
        x_ref[1, pl.ds(cf, 1), :] = fi
        x_ref[2, pl.ds(cb, 1), :] = br
        x_ref[3, pl.ds(cb, 1), :] = bi
        er = e_ref[0, pl.ds(cf, 1), :]
        ei = e_ref[1, pl.ds(cf, 1), :]
        gr = e_ref[2, pl.ds(cb, 1), :]
        gi = e_ref[3, pl.ds(cb, 1), :]
        return (afr * fr - afi * fi + er, afr * fi + afi * fr + ei,
                abr * br - abi * bi + gr, abr * bi + abi * br + gi)

    z = jnp.zeros((1, gp), F32)
    lax.fori_loop(0, n_chunks, scan_body, (z, z, z, z), unroll=8)

    for g in range(S5_G):
        y = _dot(a_ref[g], ktoe_ref[g])
        for kind in range(4):
            y = y + _dot(x_ref[kind, :, g * S5_P:(g + 1) * S5_P].astype(BF16), cexp_ref[g, kind])
        yg_ref[g] = y

    def regroup_out(rb, carry):
        c0 = pl.multiple_of(rb * 2 * blk, 2 * blk)
        for og, ref in enumerate((y0_ref, y1_ref)):
            for ht in range(2):
                src = [yg_ref[og * 8 + m, pl.ds(c0, 2 * blk), ht * 128:(ht + 1) * 128].astype(BF16) for m in range(8)]
                for a, val in enumerate(block_transpose(src)):
                    val = val.astype(F32)
                    for u in range(2):
                        ref[0, pl.ds((rb * 2 + u) * (blk * S5_T) + ht * 8 + a, blk, stride=S5_T), :] = \
                            val[u * blk:(u + 1) * blk]
        return carry

    lax.fori_loop(0, n_chunks // (2 * blk), regroup_out, 0, unroll=True)


def _s5(su0, su1, layer, bexp, cexp, pw16, skip):
    n_b, t_all, _ = su0.shape
    n_chunks = t_all // S5_T
    gp = S5_G * S5_P
    width = S5_T * S5_HG
    blk = pl.BlockSpec((1, t_all, 128), lambda b: (b, 0, 0))
    return pl.pallas_call(
        functools.partial(_s5_kernel, n_chunks=n_chunks, n_ctx_chunks=CTX_LEN // S5_T),
        out_shape=[jax.ShapeDtypeStruct((n_b, t_all, 128), F32)] * 2,
        grid=(n_b,),
        in_specs=[blk, blk, _layer_spec((S5_G, width, 4 * S5_P), layer), _layer_spec((S5_G, 4, S5_P, width), layer),
                  _layer_spec((4, gp), layer), _layer_spec((S5_G, 1, width), layer)],
        out_specs=[blk, blk],
        scratch_shapes=[pltpu.VMEM((S5_G, width, width), BF16),
                        pltpu.VMEM((S5_G, width, 4 * S5_P), BF16),
                        pltpu.VMEM((S5_G, 4, S5_P, width), BF16),
                        pltpu.VMEM((S5_G, n_chunks, width), BF16),
                        pltpu.VMEM((S5_G, n_chunks, width), F32),
                        pltpu.VMEM((4, n_chunks, gp), F32),
                        pltpu.VMEM((4, n_chunks, gp), F32)],
        compiler_params=_cp(("arbitrary",)),
        name="s5_mixer",
    )(su0, su1, bexp, cexp, pw16, skip)


def _s5_tables(lam_re, lam_im, log_dt, b_re, b_im, c_re, c_im, d_skip):
    lam_re, lam_im, log_dt = lam_re.astype(F32), lam_im.astype(F32), log_dt.astype(F32)
    b_re, b_im, c_re, c_im = b_re.astype(F32), b_im.astype(F32), c_re.astype(F32), c_im.astype(F32)
    dt = jnp.exp(log_dt)[..., None]
    n = jnp.arange(S5_T + 1, dtype=F32)[:, None, None, None]
    pw_re = jnp.exp(n * lam_re * dt) * jnp.cos(n * lam_im * dt)
    pw_im = jnp.exp(n * lam_re * dt) * jnp.sin(n * lam_im * dt)
    m = jnp.arange(S5_T, -1, -1, dtype=F32)[:, None, None, None]
    dsc_re = jnp.exp(m * lam_re * dt) * jnp.cos(m * lam_im * dt)
    dsc_im = jnp.exp(m * lam_re * dt) * jnp.sin(m * lam_im * dt)
    lb_re, lb_im = pw_re[1], pw_im[1]
    num_re, num_im = lb_re - 1.0, lb_im
    den = lam_re * lam_re + lam_im * lam_im
    coef_re = ((num_re * lam_re + num_im * lam_im) / den)[..., None]
    coef_im = ((num_im * lam_re - num_re * lam_im) / den)[..., None]
    bb_re = coef_re * b_re - coef_im * b_im
    bb_im = coef_re * b_im + coef_im * b_re

    def cmul_b(p_re, p_im, d):
        re = p_re[..., None] * bb_re[d][None] - p_im[..., None] * bb_im[d][None]
        im = p_re[..., None] * bb_im[d][None] + p_im[..., None] * bb_re[d][None]
        to = lambda z: z.transpose(1, 0, 3, 2).reshape(S5_G, S5_T * S5_HG, S5_P)
        return to(re), to(im)

    def cmul_c(p_re, p_im, d):
        re = c_re[d][None] * p_re[:, :, None, :] - c_im[d][None] * p_im[:, :, None, :]
        im = c_re[d][None] * p_im[:, :, None, :] + c_im[d][None] * p_re[:, :, None, :]
        to = lambda z: z.transpose(1, 3, 0, 2).reshape(S5_G, S5_P, S5_T * S5_HG)
        return to(re), to(-im)

    first = slice(0, S5_T)
    last = slice(1, S5_T + 1)
    bexp = jnp.concatenate(cmul_b(dsc_re[last, 0], dsc_im[last, 0], 0)
                           + cmul_b(pw_re[first, 1], pw_im[first, 1], 1), axis=-1)
    cexp = jnp.stack(cmul_c(pw_re[last, 0], pw_im[last, 0], 0)
                     + cmul_c(dsc_re[first, 1], dsc_im[first, 1], 1), axis=1)
    pw16 = jnp.stack([pw_re[S5_T, 0].reshape(-1), pw_im[S5_T, 0].reshape(-1),
                      pw_re[S5_T, 1].reshape(-1), pw_im[S5_T, 1].reshape(-1)], axis=0)
    skip = jnp.tile(d_skip.astype(F32).reshape(S5_G, 1, S5_HG), (1, 1, S5_T))
    return bexp, cexp, pw16, skip


def _pool_kernel(u_ref, w_ref, scale_ref, o_ref, pad_ref, wblk_ref, *, n_lat):
    assert POOL_WINDOWS == (2, 4, 8, 16) and POOL_GW * len(POOL_WINDOWS) == GROUP_W
    pad = max(POOL_WINDOWS) // 2
    seg = ((pad, 0, CTX_LEN), (pad + CTX_LEN + 2 * pad, CTX_LEN, n_lat))
    pad_ref[...] = jnp.zeros_like(pad_ref)
    for p0, u0, n in seg:
        pad_ref[p0:p0 + n, :] = u_ref[0, u0:u0 + n, :]
    wblk_ref[...] = jnp.zeros_like(wblk_ref)
    for i in range(len(POOL_WINDOWS)):
        wblk_ref[i * POOL_GW:(i + 1) * POOL_GW, i * POOL_GW:(i + 1) * POOL_GW] = w_ref[i].astype(BF16)
    grp = lax.broadcasted_iota(jnp.int32, (1, GROUP_W), 1) // POOL_GW
    width = jnp.left_shift(2, grp)
    rows = TILE + 2 * pad

    def up(a, k):
        return pltpu.roll(a, rows - k, 0)

    for p0, u0, n in seg:
        for t0 in range(0, n, TILE):
            w0 = pad_ref[p0 + t0 - pad:p0 + t0 - pad + rows, :]
            a1 = w0 + up(w0, 1)
            a2 = a1 + up(a1, 2)
            a3 = a2 + up(a2, 4)
            a4 = a3 + up(a3, 8)
            total = jnp.where(grp == 0, up(a1, pad - 1), jnp.where(grp == 1, up(a2, pad - 2),
                              jnp.where(grp == 2, up(a3, pad - 4), a4)))[0:TILE]
            if t0 == 0 or t0 + TILE == n:
                t = t0 + lax.broadcasted_iota(jnp.int32, (TILE, 1), 0)
                first = t - width // 2
                cnt = (jnp.clip(first + width, 0, n) - jnp.clip(first, 0, n)).astype(F32)
            else:
                cnt = width.astype(F32)
            diff = total / cnt - w0[pad:pad + TILE]
            o_ref[0, u0 + t0:u0 + t0 + TILE, :] = (_dot(diff.astype(BF16), wblk_ref[...])
                                                   * scale_ref[...]).astype(o_ref.dtype)


def _pool(pu, layer, w_pool, scale):
    n_b, t_all, _ = pu.shape
    blk = pl.BlockSpec((1, t_all, 256), lambda b: (b, 0, 0))
    return pl.pallas_call(
        functools.partial(_pool_kernel, n_lat=t_all - CTX_LEN),
        out_shape=jax.ShapeDtypeStruct((n_b, t_all, 256), BF16),
        grid=(n_b,),
        in_specs=[blk, _layer_spec((len(POOL_WINDOWS), POOL_GW, POOL_GW), layer), _layer_spec((1, 256), layer)],
        out_specs=blk,
        scratch_shapes=[pltpu.VMEM((t_all + 32, 256), F32), pltpu.VMEM((GROUP_W, GROUP_W), BF16)],
        compiler_params=_cp(("parallel",)),
        name="pool_mixer",
    )(pu, w_pool, scale)


def _gelu_tanh(x):
    return 0.5 * x * (1.0 + jnp.tanh(math.sqrt(2.0 / math.pi) * (x + 0.044715 * (x * x * x))))


def _outproj_res(res, mod_row, ygla_ref, yna, ys50_ref, ys51_ref, ypool_ref, gate_ref, mod_ref, wout_ref,
                 gpost_ref, wglu_ref, bglu_ref):
    def gated_proj(part, i):
        lo, hi = i * GROUP_W, (i + 1) * GROUP_W
        gc = gate_ref[0, :, lo:hi].astype(F32)
        return _dot((part * (gc * jax.nn.sigmoid(gc))).astype(BF16), wout_ref[lo:hi, :])

    gate_m = mod_ref[pl.ds(mod_row, 1), :][:, 2 * D_MODEL:3 * D_MODEL]
    g5 = _gelu_tanh(jnp.concatenate([ys50_ref[0], ys51_ref[0]], axis=-1))
    z5 = _dot(g5.astype(BF16), wglu_ref[...])
    acc = gated_proj(ygla_ref[0], 0) + gated_proj(ypool_ref[0], 3)
    acc = acc + gated_proj(g5 * jax.nn.sigmoid(z5 + bglu_ref[...]), 2)
    acc = acc + gated_proj(yna() if callable(yna) else yna, 1)
    ms = jnp.mean(acc * acc, axis=-1, keepdims=True)
    return res + gate_m * (acc * lax.rsqrt(ms + NORM_EPS) * gpost_ref[...])


N_OUT_PARAM_REFS = 5
N_IN_PARAM_REFS = 9


def _na_outproj_kernel(q_ref, kt0_ref, kt1_ref, kt2_ref, v0_ref, v1_ref, v2_ref, ktc_ref, vc_ref, rel_ref,
                       ygla_ref, ys50_ref, ys51_ref, ypool_ref, gate_ref, x_ref, *rest, rows):
    params, out_ref = rest[:-1], rest[-1]
    window = ((kt0_ref, v0_ref), (kt1_ref, v1_ref), (kt2_ref, v2_ref))
    yna = lambda: _na_attend(q_ref, ktc_ref, vc_ref, window, rel_ref, pl.program_id(1), rows)
    out_ref[0] = _outproj_res(x_ref[0], pl.program_id(0), ygla_ref, yna, ys50_ref, ys51_ref, ypool_ref, gate_ref,
                              *params)


def _na_outproj_inproj_kernel(q_ref, kt0_ref, kt1_ref, kt2_ref, v0_ref, v1_ref, v2_ref, ktc_ref, vc_ref, rel_ref,
                              ygla_ref, ys50_ref, ys51_ref, ypool_ref, gate_ref, x_ref, xc_ref, *rest, n_batch, rows):
    out_params = rest[:N_OUT_PARAM_REFS]
    rest = rest[N_OUT_PARAM_REFS:]
    in_params, (xo_ref, xco_ref), in_outs = rest[:N_IN_PARAM_REFS], rest[N_IN_PARAM_REFS:N_IN_PARAM_REFS + 2], \
        rest[N_IN_PARAM_REFS + 2:]
    b = pl.program_id(0)
    j = pl.program_id(1)
    window = ((kt0_ref, v0_ref), (kt1_ref, v1_ref), (kt2_ref, v2_ref))

    def body(res_ref, out_ref, tiles, mod_row, with_rope):
        yna = lambda: _na_attend(q_ref, ktc_ref, vc_ref, tiles, rel_ref, j - 1, rows)
        res = _outproj_res(res_ref[0], mod_row, ygla_ref, yna, ys50_ref, ys51_ref, ypool_ref, gate_ref, *out_params)
        out_ref[0] = res
        _inproj_body(res, mod_row, with_rope, *in_params, *in_outs)

    pl.when(j == 0)(lambda: body(xc_ref, xco_ref, (), n_batch, False))
    pl.when(j > 0)(lambda: body(x_ref, xo_ref, window, b, True))


def _outproj_param_specs(d, layer):
    return [_layer_spec((8, 3 * d), layer), _layer_spec((MIX_W, d), layer), _layer_spec((1, d), layer),
            _layer_spec((GROUP_W, GROUP_W), layer), _layer_spec((1, GROUP_W), layer)]


def _na_outproj(nq, nkt, nv, rel_tab, mix, x, layer, mod_all, w_out, g_post, w_glu, b_glu):
    n_b, n_lat, d = x.shape
    n_tiles = n_lat // TILE + 1
    rows = n_lat // GRID_W

    def nbr(delta):
        return lambda b, j: (b, jnp.clip(j + 1 + delta, 1, n_tiles - 1), 0)

    def nbr_t(delta):
        return lambda b, j: (b, 0, jnp.clip(j + 1 + delta, 1, n_tiles - 1))

    here = lambda b, j: (b, j + 1, 0)
    tile = lambda f: pl.BlockSpec((1, TILE, 256), f)
    tile_t = lambda f: pl.BlockSpec((1, 256, TILE), f)
    return pl.pallas_call(
        functools.partial(_na_outproj_kernel, rows=rows),
        out_shape=jax.ShapeDtypeStruct(x.shape, F32),
        grid=(n_b, n_lat // TILE),
        in_specs=[tile(here), tile_t(nbr_t(-1)), tile_t(nbr_t(0)), tile_t(nbr_t(1)),
                  tile(nbr(-1)), tile(nbr(0)), tile(nbr(1)),
                  tile_t(lambda b, j: (b, 0, 0)), tile(lambda b, j: (b, 0, 0)),
                  _layer_spec((NA_HEADS, 2, GRID_W, NA_REL_W), layer)]
                 + [pl.BlockSpec((1, TILE, w), here) for w in (256, 128, 128, 256, MIX_W)]
                 + [pl.BlockSpec((1, TILE, d), lambda b, j: (b, j, 0))] + _outproj_param_specs(d, layer),
        out_specs=pl.BlockSpec((1, TILE, d), lambda b, j: (b, j, 0)),
        compiler_params=_cp(("parallel", "arbitrary")),
        name="na_then_outproj",
    )(nq, nkt, nkt, nkt, nv, nv, nv, nkt, nv, rel_tab, *mix, x, mod_all, w_out, g_post, w_glu, b_glu)


def _na_outproj_inproj(nq, nkt, nv, rel_tab, mix, x, xc, layer, mod_all, w_out, g_post, w_glu, b_glu, g_pre, w_pad,
                       wg, bg, rope_tabs):
    n_b, n_lat, d = x.shape
    t_all = n_lat + CTX_LEN
    n_tiles = t_all // TILE
    rows = n_lat // GRID_W

    def nbr(delta):
        return lambda b, j: (b, jnp.clip(j + delta, 1, n_tiles - 1), 0)

    def nbr_t(delta):
        return lambda b, j: (b, 0, jnp.clip(j + delta, 1, n_tiles - 1))

    here = lambda b, j: (b, j, 0)
    tile = lambda f: pl.BlockSpec((1, TILE, 256), f)
    tile_t = lambda f: pl.BlockSpec((1, 256, TILE), f)
    in_specs, in_shape, in_out_specs = _inproj_specs(n_b, t_all, d, layer + 1)
    outs = pl.pallas_call(
        functools.partial(_na_outproj_inproj_kernel, n_batch=n_b, rows=rows),
        out_shape=[jax.ShapeDtypeStruct(x.shape, F32), jax.ShapeDtypeStruct(xc.shape, F32)] + in_shape,
        grid=(n_b, n_tiles),
        in_specs=[tile(here), tile_t(nbr_t(-1)), tile_t(nbr_t(0)), tile_t(nbr_t(1)),
                  tile(nbr(-1)), tile(nbr(0)), tile(nbr(1)),
                  tile_t(lambda b, j: (b, 0, 0)), tile(lambda b, j: (b, 0, 0)),
                  _layer_spec((NA_HEADS, 2, GRID_W, NA_REL_W), layer)]
                 + [pl.BlockSpec((1, TILE, w), here) for w in (256, 128, 128, 256, MIX_W)]
                 + [pl.BlockSpec((1, TILE, d), _lat_map), pl.BlockSpec((1, CTX_LEN, d), _ctx_map)]
                 + _outproj_param_specs(d, layer) + in_specs,
        out_specs=[pl.BlockSpec((1, TILE, d), _lat_map), pl.BlockSpec((1, CTX_LEN, d), _ctx_map)] + in_out_specs,
        compiler_params=_cp(("parallel", "arbitrary")),
        name="na_outproj_then_inproj",
    )(nq, nkt, nkt, nkt, nv, nv, nv, nkt, nv, rel_tab, *mix, x, xc, mod_all, w_out, g_post, w_glu, b_glu,
      mod_all, g_pre, *w_pad, wg, bg, *rope_tabs)
    return outs[0], outs[1], outs[2:]


def _rope_tables(n_tok):
    t = np.arange(n_tok)
    half = GLA_DK // 2
    freqs = ROPE_BASE ** (-np.arange(0, half, 2, dtype=np.float32) / half)
    d = np.arange(GLA_DK)
    pos = np.where((d // half)[None, :] == 0, (t // GRID_W)[:, None], (t % GRID_W)[:, None]).astype(np.float32)
    ang = jnp.asarray(pos) * jnp.asarray(freqs[(d % half) % (half // 2)])[None, :]
    first = jnp.asarray(((d % half) < half // 2)[None, :])
    cos, sin = jnp.cos(ang), jnp.sin(ang)
    tabs = (cos, jnp.where(first, -sin, 0.0), jnp.where(first, 0.0, sin))
    return tuple(jnp.tile(z, (1, GLA_HEADS)).astype(F32) for z in tabs)


def _split_w_in(w_in):
    pad = jnp.zeros(w_in.shape[:2] + (N_HEAD_COLS - N_GATE_END - 128,), w_in.dtype)
    head = jnp.concatenate([w_in[:, :, GQ_COL:GQ_COL + 128], w_in[:, :, :N_GATE_END], pad], axis=2)
    return head.astype(BF16), w_in[:, :, N_GATE_END:].astype(BF16)


def _gate_weights(w_gate, b_gate):
    n_l = w_gate.shape[0]
    wg = jnp.zeros((n_l, 128, 256), F32)
    wg = wg.at[:, 0:GLA_RANK, 0:128].set(w_gate[:, 0].astype(F32))
    wg = wg.at[:, GLA_RANK:2 * GLA_RANK, 128:256].set(w_gate[:, 1].astype(F32))
    return wg.astype(BF16), b_gate.astype(F32).reshape(n_l, 1, 256)


def kernel(x, c, ctx, c_ctx, w_mod, b_mod, g_pre, g_post, w_in, w_out, gla_w_gate, gla_b_gate, gla_g_norm, na_rpb,
           s5_lam_re, s5_lam_im, s5_log_dt, s5_b_re, s5_b_im, s5_c_re, s5_c_im, s5_d, s5_w_glu, s5_b_glu,
           pool_w, pool_scale):
    n_b, n_lat, d = x.shape
    depth = w_mod.shape[0]
    rows = n_lat // GRID_W
    assert d == D_MODEL and ctx.shape[1] == CTX_LEN and n_lat % TILE == 0 and n_b < 8 and rows >= NA_KR
    c_rows = jnp.zeros((8, d), F32).at[0:n_b].set(c).at[n_b].set(c_ctx)
    mod_all = _modulation(c_rows, w_mod, b_mod)
    rope_tabs = _rope_tables(n_lat)
    w_pad = _split_w_in(w_in)
    wg, bg = _gate_weights(gla_w_gate, gla_b_gate)
    g_pre3 = g_pre.astype(F32).reshape(depth, 1, d)
    g_post3 = g_post.astype(F32).reshape(depth, 1, d)
    gn = jnp.tile(gla_g_norm.astype(F32), (1, GLA_HEADS)).reshape(depth, 1, GLA_HEADS * GLA_DV)
    rel_tab = jax.vmap(_na_rel_table)(na_rpb)
    bexp, cexp, pw16, skip = jax.vmap(_s5_tables)(s5_lam_re, s5_lam_im, s5_log_dt, s5_b_re, s5_b_im,
                                                  s5_c_re, s5_c_im, s5_d)
    w_out_b = w_out.astype(BF16)
    w_glu_b = s5_w_glu.astype(BF16)
    b_glu3 = s5_b_glu.astype(F32).reshape(depth, 1, GROUP_W)
    pool_scale3 = pool_scale.astype(F32).reshape(depth, 1, GROUP_W)
    xc = ctx
    proj = _inproj(x, xc, 0, mod_all, g_pre3, w_pad, wg, bg, rope_tabs)
    for l in range(depth):
        qk, gv, lg, nq, nkt, nv, su0, su1, pu, gate_cols = proj
        y_gla = _gla(qk, gv, lg, l, gn)
        y_s50, y_s51 = _s5(su0, su1, l, bexp, cexp, pw16, skip)
        y_pool = _pool(pu, l, pool_w, pool_scale3)
        if l == depth - 1:
            x = _na_outproj(nq, nkt, nv, rel_tab, (y_gla, y_s50, y_s51, y_pool, gate_cols), x, l, mod_all,
                            w_out_b, g_post3, w_glu_b, b_glu3)
        else:
            x, xc, proj = _na_outproj_inproj(nq, nkt, nv, rel_tab, (y_gla, y_s50, y_s51, y_pool, gate_cols), x, xc, l,
                                             mod_all, w_out_b, g_post3, w_glu_b, b_glu3, g_pre3, w_pad, wg, bg,
                                             rope_tabs)
    return x
```

```python
import functools
import math

import numpy as np
import jax
import jax.numpy as jnp
from jax import lax
from jax.experimental import pallas as pl
from jax.experimental.pallas import tpu as pltpu

F32 = jnp.float32
BF16 = jnp.bfloat16

D_MODEL = 1024
GRID_W = 64
CTX_LEN = 256
GROUP_W = 256
MIX_W = 1024
NORM_EPS = 1e-6
GLA_HEADS = 4
GLA_DV = 64
GLA_DK = 32
GLA_RANK = 16
GLA_TAU = 16.0
ROPE_BASE = 10000.0
NA_HEADS = 4
NA_DH = 64
NA_KR = 8
NA_KC = 16
S5_HG = 16
S5_G = 16
S5_P = 64
POOL_WINDOWS = (2, 4, 8, 16)
POOL_GW = 64

TILE = 256
GLA_C = 128
GLA_STEP = 2
S5_T = 16
NA_ROWS = TILE // GRID_W
NA_REL_W = 1024
NEG_BIG = -1e30

N_GATE_END = 128 + 256 + 2 * GLA_RANK
_HEAD_COL = {"gqk": (0, 256), "gv": (256, 512), "gg": (512, 640)}
_TAIL_COL = {}
_off = 0
for _name, _w in (("nk", 256), ("nv", 256), ("su", 256), ("gq_unused", 128), ("nq", 256), ("pu", 256),
                  ("gate", 1024)):
    _TAIL_COL[_name] = (_off, _off + _w)
    _off += _w
N_HEAD_COLS = _HEAD_COL["gg"][1]
N_TAIL_COLS = _off
GQ_COL = N_GATE_END + _TAIL_COL["gq_unused"][0]

_VMEM_LIMIT = 56 * 1024 * 1024


def _cp(sem, vmem=_VMEM_LIMIT):
    return pltpu.CompilerParams(dimension_semantics=sem, vmem_limit_bytes=vmem)


def _dot(a, b):
    return jnp.dot(a, b, preferred_element_type=F32)


def _dot_nt(a, b):
    return lax.dot_general(a, b, (((1,), (1,)), ((), ())), preferred_element_type=F32)


def _split_bf16(a):
    hi = a.astype(BF16)
    lo = (a - hi.astype(F32)).astype(BF16)
    return hi, lo


def _mod_kernel(s_ref, w_ref, b_ref, o_ref):
    s = s_ref[...]
    a = (s * jax.nn.sigmoid(s)).astype(BF16)
    o_ref[0] = _dot(a, w_ref[0].astype(BF16)) + b_ref[0]


def _modulation(c_rows, w_mod, b_mod):
    n_l, d, d3 = w_mod.shape
    return pl.pallas_call(
        _mod_kernel,
        out_shape=jax.ShapeDtypeStruct((n_l, 8, d3), F32),
        grid=(n_l, d3 // d),
        in_specs=[pl.BlockSpec((8, d), lambda l, j: (0, 0)),
                  pl.BlockSpec((1, d, d), lambda l, j: (l, 0, j)),
                  pl.BlockSpec((1, 1, d), lambda l, j: (l, 0, j))],
        out_specs=pl.BlockSpec((1, 8, d), lambda l, j: (l, 0, j)),
        compiler_params=_cp(("parallel", "parallel")),
        name="adaln_modulation",
    )(c_rows, w_mod, b_mod.reshape(n_l, 1, d3))


def _inproj_body(xt, mod_row, with_rope, mod_ref, gpre_ref, wa_ref, wb_ref, wg_ref, bg_ref, cos_ref, sa_ref, sb_ref,
                 qk_ref, v_ref, lg_ref, nq_ref, nk_ref, nv_ref, su0_ref, su1_ref, pu_ref, gate_ref):
    m = mod_ref[pl.ds(mod_row, 1), :]
    shift = m[:, 0:D_MODEL]
    scale = m[:, D_MODEL:2 * D_MODEL]
    ms = jnp.mean(xt * xt, axis=-1, keepdims=True)
    h = xt * lax.rsqrt(ms + NORM_EPS) * gpre_ref[...] * (1.0 + scale) + shift
    hb = h.astype(BF16)

    def proj(name):
        if name in _HEAD_COL:
            lo, hi = _HEAD_COL[name]
            return _dot(hb, wa_ref[:, lo:hi])
        lo, hi = _TAIL_COL[name]
        return _dot(hb, wb_ref[:, lo:hi])

    def rope(t):
        if not with_rope:
            return t
        return t * cos_ref[...] + pltpu.roll(t, 128 - 8, 1) * sa_ref[...] + pltpu.roll(t, 8, 1) * sb_ref[...]

    qk = proj("gqk")
    qk_ref[0, :, 0:128] = rope(qk[:, 0:128]) * (GLA_DK ** -0.5)
    qk_ref[0, :, 128:256] = rope(qk[:, 128:256])
    v_ref[0] = proj("gv")
    z = _dot(proj("gg").astype(BF16), wg_ref[...]) + bg_ref[...]
    lg_ref[0] = (jnp.minimum(z, 0.0) - jnp.log(1.0 + jnp.exp(-jnp.abs(z)))) * (1.0 / GLA_TAU)
    nq_ref[0] = (proj("nq") * (NA_DH ** -0.5)).astype(BF16)
    nk_ref[0] = proj("nk").T.astype(BF16)
    nv_ref[0] = proj("nv").astype(BF16)
    su = proj("su")
    su0_ref[0] = su[:, 0:128]
    su1_ref[0] = su[:, 128:256]
    pu_ref[0] = proj("pu")
    gate_ref[0] = proj("gate").astype(BF16)


def _inproj_kernel(x_ref, xc_ref, *refs, n_batch):
    b = pl.program_id(0)
    j = pl.program_id(1)
    pl.when(j == 0)(lambda: _inproj_body(xc_ref[0], n_batch, False, *refs))
    pl.when(j > 0)(lambda: _inproj_body(x_ref[0], b, True, *refs))


def _layer_spec(shape, layer):
    return pl.BlockSpec((None,) + tuple(shape), lambda *_: (layer,) + (0,) * len(shape))


def _lat_map(b, j):
    return (b, jnp.maximum(j - 1, 0), 0)


def _ctx_map(b, j):
    return (b, 0, 0)


def _inproj_specs(n_b, t_all, d, layer):
    tab_map = lambda b, j: (jnp.maximum(j - 1, 0), 0)
    outs = (("qk", 256, F32), ("gv", 256, F32), ("lg", 256, F32), ("nq", 256, BF16), ("nkT", 256, BF16),
            ("nv", 256, BF16), ("su0", 128, F32), ("su1", 128, F32), ("pu", 256, F32), ("gate", 1024, BF16))
    out_shape = [jax.ShapeDtypeStruct((n_b, w, t_all) if n == "nkT" else (n_b, t_all, w), dt) for n, w, dt in outs]
    out_specs = [pl.BlockSpec((1, w, TILE), lambda b, j: (b, 0, j)) if n == "nkT"
                 else pl.BlockSpec((1, TILE, w), lambda b, j: (b, j, 0)) for n, w, dt in outs]
    in_specs = [_layer_spec((8, 3 * d), layer),
                _layer_spec((1, d), layer),
                _layer_spec((d, N_HEAD_COLS), layer),
                _layer_spec((d, N_TAIL_COLS), layer),
                _layer_spec((128, 256), layer),
                _layer_spec((1, 256), layer),
                pl.BlockSpec((TILE, 128), tab_map),
                pl.BlockSpec((TILE, 128), tab_map),
                pl.BlockSpec((TILE, 128), tab_map)]
    return in_specs, out_shape, out_specs


def _inproj(x, xc, layer, mod_all, g_pre, w_pad, wg, bg, rope_tabs):
    n_b, n_lat, d = x.shape
    t_all = n_lat + CTX_LEN
    in_specs, out_shape, out_specs = _inproj_specs(n_b, t_all, d, layer)
    return pl.pallas_call(
        functools.partial(_inproj_kernel, n_batch=n_b),
        out_shape=out_shape,
        grid=(n_b, t_all // TILE),
        in_specs=[pl.BlockSpec((1, TILE, d), _lat_map), pl.BlockSpec((1, CTX_LEN, d), _ctx_map)] + in_specs,
        out_specs=out_specs,
        compiler_params=_cp(("parallel", "arbitrary")),
        name="modnorm_inproj",
    )(x, xc, mod_all, g_pre, *w_pad, wg, bg, *rope_tabs)


def _gla_kernel(qk_ref, v_ref, lg_ref, gn_ref, y_ref, stf_ref, stb_ref, of_ref, ob_ref, *, n_chunks, n_ctx_chunks):
    c_len = GLA_C
    row = lax.broadcasted_iota(jnp.int32, (c_len, c_len), 0)
    col = lax.broadcasted_iota(jnp.int32, (c_len, c_len), 1)
    tri_f = (col <= row).astype(BF16)
    tri_b = (col >= row).astype(BF16)
    arow = lax.broadcasted_iota(jnp.int32, (c_len, 4 * c_len), 0)
    acol = lax.broadcasted_iota(jnp.int32, (c_len, 4 * c_len), 1) % c_len
    amask_f = acol <= arow
    amask_b = acol >= arow
    head_k = lax.broadcasted_iota(jnp.int32, (1, 128), 1) // GLA_DK
    head_v = lax.broadcasted_iota(jnp.int32, (1, 256), 1) // GLA_DV
    bd_mask = (lax.broadcasted_iota(jnp.int32, (256, 128), 0) // GLA_DV
               == lax.broadcasted_iota(jnp.int32, (256, 128), 1) // GLA_DK)
    ones_bd = jnp.where(lax.broadcasted_iota(jnp.int32, (256, 256), 0) // GLA_DV
                        == lax.broadcasted_iota(jnp.int32, (256, 256), 1) // GLA_DV,
                        1.0 / GLA_DV, 0.0).astype(BF16)
    mid = c_len // 2

    def scan_body(i, carry):
        def bwd_chunk(pos):
            return jnp.where(pos < n_ctx_chunks, n_ctx_chunks - 1 - pos, n_chunks - 1 - (pos - n_ctx_chunks))

        dirs = [d for d in (0, 1) for _ in range(GLA_STEP)]
        chunks = [i * GLA_STEP + u for u in range(GLA_STEP)] + [bwd_chunk(i * GLA_STEP + u) for u in range(GLA_STEP)]
        jobs = range(len(dirs))
        r0 = [pl.multiple_of(c * c_len, c_len) for c in chunks]
        st_refs = (stf_ref, stb_ref)
        tri = (tri_f, tri_b)
        amask = (amask_f, amask_b)
        end = (c_len - 1, 0)
        q = [qk_ref[0, pl.ds(r0[n], c_len), 0:128] for n in jobs]
        k = [qk_ref[0, pl.ds(r0[n], c_len), 128:256] for n in jobs]
        v = [v_ref[0, pl.ds(r0[n], c_len), :] for n in jobs]
        lg = [_split_bf16(lg_ref[0, pl.ds(r0[n], c_len), 128 * dirs[n]:128 * (dirs[n] + 1)]) for n in jobs]
        cum = [_dot(tri[dirs[n]], lg[n][0]) + _dot(tri[dirs[n]], lg[n][1]) for n in jobs]
        c_ref = [cum[n][mid:mid + 1, :] for n in jobs]
        c_end = [cum[n][end[dirs[n]]:end[dirs[n]] + 1, :] for n in jobs]
        qt = [q[n] * jnp.exp(cum[n] - c_ref[n]) for n in jobs]
        kt = [k[n] * jnp.exp(c_ref[n] - cum[n]) for n in jobs]
        ks = [jnp.concatenate([jnp.where(head_k == hh, kt[n], 0.0) for hh in range(GLA_HEADS)],
                              axis=0).astype(BF16) for n in jobs]
        att = [_dot_nt(qt[n].astype(BF16), ks[n]) for n in jobs]
        qh = [(qt[n] * jnp.exp(c_ref[n])).astype(BF16) for n in jobs]
        kh = [(kt[n] * jnp.exp(c_end[n] - c_ref[n])).astype(BF16) for n in jobs]
        upd = [_dot(v[n].T.astype(BF16), kh[n]) for n in jobs]
        att = [jnp.where(amask[dirs[n]], att[n], 0.0).astype(BF16) for n in jobs]
        vb = [v[n].astype(BF16) for n in jobs]
        vs = [jnp.concatenate([jnp.where(head_v == hh, vb[n], jnp.zeros_like(vb[n])) for hh in range(GLA_HEADS)],
                              axis=0) for n in jobs]
        o_intra = [_dot(att[n], vs[n]) for n in jobs]
        outs = (of_ref, ob_ref)
        st = [st_refs[d][...] for d in (0, 1)]
        for n in jobs:
            d = dirs[n]
            outs[d][pl.ds(r0[n], c_len), :] = o_intra[n] + _dot_nt(qh[n], st[d].astype(BF16))
            st[d] = st[d] * jnp.exp(c_end[n]) + jnp.where(bd_mask, upd[n], 0.0)
        for d in (0, 1):
            st_refs[d][...] = st[d]
        return carry

    def norm_body(c, carry):
        r0 = pl.multiple_of(c * TILE, TILE)
        o = of_ref[pl.ds(r0, TILE), :] + ob_ref[pl.ds(r0, TILE), :]
        ms = _dot((o * o).astype(BF16), ones_bd)
        y_ref[0, pl.ds(r0, TILE), :] = (o * lax.rsqrt(ms + NORM_EPS) * gn_ref[...]).astype(y_ref.dtype)
        return carry

    stf_ref[...] = jnp.zeros_like(stf_ref)
    stb_ref[...] = jnp.zeros_like(stb_ref)
    lax.fori_loop(0, n_chunks // GLA_STEP, scan_body, 0, unroll=True)
    lax.fori_loop(0, n_chunks * c_len // TILE, norm_body, 0, unroll=True)


def _gla(qk, v, lg, layer, g_norm):
    n_b, t_all, _ = qk.shape
    blk = lambda w: pl.BlockSpec((1, t_all, w), lambda b: (b, 0, 0))
    return pl.pallas_call(
        functools.partial(_gla_kernel, n_chunks=t_all // GLA_C, n_ctx_chunks=CTX_LEN // GLA_C),
        out_shape=jax.ShapeDtypeStruct((n_b, t_all, 256), BF16),
        grid=(n_b,),
        in_specs=[blk(256), blk(256), blk(256), _layer_spec((1, 256), layer)],
        out_specs=blk(256),
        scratch_shapes=[pltpu.VMEM((256, 128), F32), pltpu.VMEM((256, 128), F32),
                        pltpu.VMEM((t_all, 256), F32), pltpu.VMEM((t_all, 256), F32)],
        compiler_params=_cp(("parallel",)),
        name="gla_mixer",
    )(qk, v, lg, g_norm)


def _na_attend(q_ref, ktc_ref, vc_ref, window_tiles, rel_ref, g, rows):
    left = lax.broadcasted_iota(jnp.int32, (1, 128), 1) < NA_DH
    ones_blk = jnp.ones((TILE, 128), BF16)
    ro0 = NA_KR - 1 - NA_ROWS
    key_tiles = [(ktc_ref, vc_ref, None)]
    if window_tiles:
        q_row = lax.broadcasted_iota(jnp.int32, (TILE, TILE), 0) // GRID_W
        k_row = lax.broadcasted_iota(jnp.int32, (TILE, TILE), 1) // GRID_W
        first = jnp.clip(NA_ROWS * g + q_row - NA_KR // 2, 0, rows - NA_KR) - NA_ROWS * (g - 1)
        for d, (kr, vr) in enumerate(window_tiles):
            window = None if d == 1 else jnp.where(
                (k_row + NA_ROWS * d >= first) & (k_row + NA_ROWS * d < first + NA_KR), 0.0, NEG_BIG)
            key_tiles.append((kr, vr, (d, window)))

    def lanes(hh):
        return slice(128 * (hh // 2), 128 * (hh // 2 + 1))

    def score(hh):
        qp = q_ref[0, :, lanes(hh)]
        zero = jnp.zeros_like(qp)
        qh = jnp.where(left, qp, zero) if hh % 2 == 0 else jnp.where(left, zero, qp)
        kt_all = jnp.concatenate([kr[0, lanes(hh), :] for kr, _, _ in key_tiles], axis=1)
        return _dot(qh, kt_all)

    def probs(hh, s):
        if window_tiles:
            parts = [s[:, 0:TILE]]
            for _, _, (d, window) in key_tiles[1:]:
                rel = jnp.concatenate(
                    [rel_ref[hh, (ro0 - i) % 2, :, GRID_W * ((ro0 - i) // 2 * 2) + d * TILE:
                             GRID_W * ((ro0 - i) // 2 * 2) + (d + 1) * TILE] for i in range(NA_ROWS)], axis=0)
                part = s[:, (d + 1) * TILE:(d + 2) * TILE] + rel
                parts.append(part if window is None else part + window)
            s = jnp.concatenate(parts, axis=1)
        return jnp.exp(s - jnp.max(s, axis=-1, keepdims=True)).astype(BF16)

    def attend_values(pair, p_even, p_odd):
        vaug = jnp.concatenate([jnp.concatenate([vr[0, :, lanes(2 * pair)], ones_blk], axis=1)
                                for _, vr, _ in key_tiles], axis=0)
        acc = _dot(jnp.concatenate([p_even, p_odd], axis=0), vaug)
        o = acc[:, 0:128] / acc[:, 128:256]
        return jnp.where(left, o[0:TILE], o[TILE:2 * TILE])

    s_next = score(0)
    p, outs = [], []
    for hh in range(NA_HEADS):
        s_cur = s_next
        if hh + 1 < NA_HEADS:
            s_next = score(hh + 1)
        p.append(probs(hh, s_cur))
        if hh % 2 == 1:
            outs.append(attend_values(hh // 2, p[hh - 1], p[hh]))
    return jnp.concatenate(outs, axis=1)


def _na_rel_table(rpb):
    col = np.arange(GRID_W)
    cs = np.clip(col - NA_KC // 2, 0, GRID_W - NA_KC)
    col_mask = (col[None, :] >= cs[:, None]) & (col[None, :] < cs[:, None] + NA_KC)
    co = np.clip(col[None, :] - col[:, None], -(NA_KC - 1), NA_KC - 1) + (NA_KC - 1)
    onehot = (co[None] == np.arange(2 * NA_KC - 1)[:, None, None]).astype(np.float32)
    toe = jnp.einsum("hrc,cqk->hqrk", rpb.astype(F32), jnp.asarray(onehot),
                     precision=lax.Precision.HIGHEST)
    toe = jnp.where(jnp.asarray(col_mask)[None, :, None, :], toe, NEG_BIG)
    flat = toe.reshape(NA_HEADS, GRID_W, (2 * NA_KR - 1) * GRID_W)
    flat = jnp.pad(flat, ((0, 0), (0, 0), (0, NA_REL_W + GRID_W - flat.shape[-1])), constant_values=NEG_BIG)
    return jnp.stack([flat[:, :, :NA_REL_W], flat[:, :, GRID_W:]], axis=1)


def _s5_operators(b32_ref, c32_ref, pw_ref, skip_ref, ktoe_ref, bexp_ref, cexp_ref):
    tok_s = lax.broadcasted_iota(jnp.int32, (S5_T * S5_HG, S5_T * S5_HG), 0) // S5_HG
    tok_t = lax.broadcasted_iota(jnp.int32, (S5_T * S5_HG, S5_T * S5_HG), 1) // S5_HG
    diag = (lax.broadcasted_iota(jnp.int32, (S5_T * S5_HG, S5_T * S5_HG), 0)
            == lax.broadcasted_iota(jnp.int32, (S5_T * S5_HG, S5_T * S5_HG), 1))

    def dot3(a, b):
        a_hi, a_lo = _split_bf16(a)
        b_hi, b_lo = _split_bf16(b)
        return _dot(a_hi, b_hi) + _dot(a_lo, b_hi) + _dot(a_hi, b_lo)

    for g in range(S5_G):
        b32 = b32_ref[g]
        ktoe = jnp.where(diag, skip_ref[g], 0.0)
        for d, keep in ((0, tok_t >= tok_s), (1, tok_s >= tok_t)):
            p_re = pw_ref[2 * d:2 * d + 1, g * S5_P:(g + 1) * S5_P]
            p_im = pw_ref[2 * d + 1:2 * d + 2, g * S5_P:(g + 1) * S5_P]
            inv = 1.0 / (p_re * p_re + p_im * p_im)
            q_re, q_im = p_re * inv, -p_im * inv
            b_re = b32[:, (2 * d) * S5_P:(2 * d + 1) * S5_P]
            b_im = b32[:, (2 * d + 1) * S5_P:(2 * d + 2) * S5_P]
            lag = (dot3(b_re * q_re - b_im * q_im, c32_ref[g, 2 * d])
                   + dot3(b_re * q_im + b_im * q_re, c32_ref[g, 2 * d + 1]))
            ktoe = ktoe + jnp.where(keep, lag, 0.0)
        ktoe_ref[g] = ktoe.astype(BF16)
        bexp_ref[g] = b32.astype(BF16)
        cexp_ref[g] = c32_ref[g].astype(BF16)


def _s5_kernel(su0_ref, su1_ref, b32_ref, c32_ref, pw_ref, skip_ref, y0_ref, y1_ref, ktoe_ref, bexp_ref, cexp_ref,
               a_ref, yg_ref, e_ref, x_ref, *, n_chunks, n_ctx_chunks):
    blk = 8
    gp = S5_G * S5_P

    @pl.when(pl.program_id(0) == 0)
    def _():
        _s5_operators(b32_ref, c32_ref, pw_ref, skip_ref, ktoe_ref, bexp_ref, cexp_ref)

    def block_transpose(src):
        lane_blk = lax.broadcasted_iota(jnp.int32, src[0].shape, 1) // S5_HG

        def pick(vals):
            acc = vals[0]
            for k in range(1, 8):
                acc = jnp.where(lane_blk == k, vals[k], acc)
            return acc

        rolled = []
        for d in range(8):
            diag = pick([src[(k + d) % 8] for k in range(8)])
            rolled.append(pltpu.roll(diag, S5_HG * d, 1) if d else diag)
        return [pick([rolled[(m - a) % 8] for m in range(8)]) for a in range(8)]

    def regroup_in(rb, carry):
        c0 = pl.multiple_of(rb * 2 * blk, 2 * blk)
        for hg, ref in enumerate((su0_ref, su1_ref)):
            for o in range(2):
                src = [jnp.concatenate(
                    [ref[0, pl.ds((rb * 2 + u) * (blk * S5_T) + o * 8 + m, blk, stride=S5_T), :] for u in range(2)],
                    axis=0).astype(BF16) for m in range(8)]
                for a, val in enumerate(block_transpose(src)):
                    a_ref[hg * 8 + a, pl.ds(c0, 2 * blk), o * 128:(o + 1) * 128] = val
        return carry

    lax.fori_loop(0, n_chunks // (2 * blk), regroup_in, 0, unroll=True)

    for g in range(S5_G):
        e = _dot(a_ref[g], bexp_ref[g])
        for kind in range(4):
            e_ref[kind, :, g * S5_P:(g + 1) * S5_P] = e[:, kind * S5_P:(kind + 1) * S5_P]

    afr, afi, abr, abi = pw_ref[0:1, :], pw_ref[1:2, :], pw_ref[2:3, :], pw_ref[3:4, :]

    def scan_body(i, carry):
        fr, fi, br, bi = carry
        cf = i
        cb = jnp.where(i < n_ctx_chunks, n_ctx_chunks - 1 - i, n_chunks - 1 - (i - n_ctx_chunks))
        x_ref[0, pl.ds(cf, 1), :] = fr
        x_ref[1, pl.ds(cf, 1), :] = fi
        x_ref[2, pl.ds(cb, 1), :] = br
        x_ref[3, pl.ds(cb, 1), :] = bi
        er = e_ref[0, pl.ds(cf, 1), :]
        ei = e_ref[1, pl.ds(cf, 1), :]
        gr = e_ref[2, pl.ds(cb, 1), :]
        gi = e_ref[3, pl.ds(cb, 1), :]
        return (afr * fr - afi * fi + er, afr * fi + afi * fr + ei,
                abr * br - abi * bi + gr, abr * bi + abi * br + gi)

    z = jnp.zeros((1, gp), F32)
    lax.fori_loop(0, n_chunks, scan_body, (z, z, z, z), unroll=8)

    for g in range(S5_G):
        x_cat = jnp.concatenate([x_ref[kind, :, g * S5_P:(g + 1) * S5_P] for kind in range(4)], axis=1).astype(BF16)
        rhs = jnp.concatenate([ktoe_ref[g]] + [cexp_ref[g, kind] for kind in range(4)], axis=0)
        yg_ref[g] = _dot(jnp.concatenate([a_ref[g], x_cat], axis=1), rhs)

    def regroup_out(rb, carry):
        c0 = pl.multiple_of(rb * 2 * blk, 2 * blk)
        for og, ref in enumerate((y0_ref, y1_ref)):
            for ht in range(2):
                src = [yg_ref[og * 8 + m, pl.ds(c0, 2 * blk), ht * 128:(ht + 1) * 128].astype(BF16) for m in range(8)]
                for a, val in enumerate(block_transpose(src)):
                    val = val.astype(F32)
                    for u in range(2):
                        ref[0, pl.ds((rb * 2 + u) * (blk * S5_T) + ht * 8 + a, blk, stride=S5_T), :] = \
                            val[u * blk:(u + 1) * blk]
        return carry

    lax.fori_loop(0, n_chunks // (2 * blk), regroup_out, 0, unroll=True)


def _s5(su0, su1, layer, bexp, cexp, pw16, skip):
    n_b, t_all, _ = su0.shape
    n_chunks = t_all // S5_T
    gp = S5_G * S5_P
    width = S5_T * S5_HG
    blk = pl.BlockSpec((1, t_all, 128), lambda b: (b, 0, 0))
    return pl.pallas_call(
        functools.partial(_s5_kernel, n_chunks=n_chunks, n_ctx_chunks=CTX_LEN // S5_T),
        out_shape=[jax.ShapeDtypeStruct((n_b, t_all, 128), F32)] * 2,
        grid=(n_b,),
        in_specs=[blk, blk, _layer_spec((S5_G, width, 4 * S5_P), layer), _layer_spec((S5_G, 4, S5_P, width), layer),
                  _layer_spec((4, gp), layer), _layer_spec((S5_G, 1, width), layer)],
        out_specs=[blk, blk],
        scratch_shapes=[pltpu.VMEM((S5_G, width, width), BF16),
                        pltpu.VMEM((S5_G, width, 4 * S5_P), BF16),
                        pltpu.VMEM((S5_G, 4, S5_P, width), BF16),
                        pltpu.VMEM((S5_G, n_chunks, width), BF16),
                        pltpu.VMEM((S5_G, n_chunks, width), F32),
                        pltpu.VMEM((4, n_chunks, gp), F32),
                        pltpu.VMEM((4, n_chunks, gp), F32)],
        compiler_params=_cp(("arbitrary",)),
        name="s5_mixer",
    )(su0, su1, bexp, cexp, pw16, skip)


def _s5_tables(lam_re, lam_im, log_dt, b_re, b_im, c_re, c_im, d_skip):
    lam_re, lam_im, log_dt = lam_re.astype(F32), lam_im.astype(F32), log_dt.astype(F32)
    b_re, b_im, c_re, c_im = b_re.astype(F32), b_im.astype(F32), c_re.astype(F32), c_im.astype(F32)
    dt = jnp.exp(log_dt)[..., None]
    n = jnp.arange(S5_T + 1, dtype=F32)[:, None, None, None]
    pw_re = jnp.exp(n * lam_re * dt) * jnp.cos(n * lam_im * dt)
    pw_im = jnp.exp(n * lam_re * dt) * jnp.sin(n * lam_im * dt)
    m = jnp.arange(S5_T, -1, -1, dtype=F32)[:, None, None, None]
    dsc_re = jnp.exp(m * lam_re * dt) * jnp.cos(m * lam_im * dt)
    dsc_im = jnp.exp(m * lam_re * dt) * jnp.sin(m * lam_im * dt)
    lb_re, lb_im = pw_re[1], pw_im[1]
    num_re, num_im = lb_re - 1.0, lb_im
    den = lam_re * lam_re + lam_im * lam_im
    coef_re = ((num_re * lam_re + num_im * lam_im) / den)[..., None]
    coef_im = ((num_im * lam_re - num_re * lam_im) / den)[..., None]
    bb_re = coef_re * b_re - coef_im * b_im
    bb_im = coef_re * b_im + coef_im * b_re

    def cmul_b(p_re, p_im, d):
        re = p_re[..., None] * bb_re[d][None] - p_im[..., None] * bb_im[d][None]
        im = p_re[..., None] * bb_im[d][None] + p_im[..., None] * bb_re[d][None]
        to = lambda z: z.transpose(1, 0, 3, 2).reshape(S5_G, S5_T * S5_HG, S5_P)
        return to(re), to(im)

    def cmul_c(p_re, p_im, d):
        re = c_re[d][None] * p_re[:, :, None, :] - c_im[d][None] * p_im[:, :, None, :]
        im = c_re[d][None] * p_im[:, :, None, :] + c_im[d][None] * p_re[:, :, None, :]
        to = lambda z: z.transpose(1, 3, 0, 2).reshape(S5_G, S5_P, S5_T * S5_HG)
        return to(re), to(-im)

    first = slice(0, S5_T)
    last = slice(1, S5_T + 1)
    bexp = jnp.concatenate(cmul_b(dsc_re[last, 0], dsc_im[last, 0], 0)
                           + cmul_b(pw_re[first, 1], pw_im[first, 1], 1), axis=-1)
    cexp = jnp.stack(cmul_c(pw_re[last, 0], pw_im[last, 0], 0)
                     + cmul_c(dsc_re[first, 1], dsc_im[first, 1], 1), axis=1)
    pw16 = jnp.stack([pw_re[S5_T, 0].reshape(-1), pw_im[S5_T, 0].reshape(-1),
                      pw_re[S5_T, 1].reshape(-1), pw_im[S5_T, 1].reshape(-1)], axis=0)
    skip = jnp.tile(d_skip.astype(F32).reshape(S5_G, 1, S5_HG), (1, 1, S5_T))
    return bexp, cexp, pw16, skip


def _pool_kernel(u_ref, w_ref, scale_ref, o_ref, pad_ref, wblk_ref, *, n_lat):
    assert POOL_WINDOWS == (2, 4, 8, 16) and POOL_GW * len(POOL_WINDOWS) == GROUP_W
    pad = max(POOL_WINDOWS) // 2
    seg = ((pad, 0, CTX_LEN), (pad + CTX_LEN + 2 * pad, CTX_LEN, n_lat))
    pad_ref[...] = jnp.zeros_like(pad_ref)
    for p0, u0, n in seg:
        pad_ref[p0:p0 + n, :] = u_ref[0, u0:u0 + n, :]
    wblk_ref[...] = jnp.zeros_like(wblk_ref)
    for i in range(len(POOL_WINDOWS)):
        wblk_ref[i * POOL_GW:(i + 1) * POOL_GW, i * POOL_GW:(i + 1) * POOL_GW] = w_ref[i].astype(BF16)
    grp = lax.broadcasted_iota(jnp.int32, (1, GROUP_W), 1) // POOL_GW
    width = jnp.left_shift(2, grp)
    rows = TILE + 2 * pad

    def up(a, k):
        return pltpu.roll(a, rows - k, 0)

    for p0, u0, n in seg:
        for t0 in range(0, n, TILE):
            w0 = pad_ref[p0 + t0 - pad:p0 + t0 - pad + rows, :]
            a1 = w0 + up(w0, 1)
            a2 = a1 + up(a1, 2)
            a3 = a2 + up(a2, 4)
            a4 = a3 + up(a3, 8)
            total = jnp.where(grp == 0, up(a1, pad - 1), jnp.where(grp == 1, up(a2, pad - 2),
                              jnp.where(grp == 2, up(a3, pad - 4), a4)))[0:TILE]
            if t0 == 0 or t0 + TILE == n:
                t = t0 + lax.broadcasted_iota(jnp.int32, (TILE, 1), 0)
                first = t - width // 2
                cnt = (jnp.clip(first + width, 0, n) - jnp.clip(first, 0, n)).astype(F32)
            else:
                cnt = width.astype(F32)
            diff = total / cnt - w0[pad:pad + TILE]
            o_ref[0, u0 + t0:u0 + t0 + TILE, :] = (_dot(diff.astype(BF16), wblk_ref[...])
                                                   * scale_ref[...]).astype(o_ref.dtype)


def _pool(pu, layer, w_pool, scale):
    n_b, t_all, _ = pu.shape
    blk = pl.BlockSpec((1, t_all, 256), lambda b: (b, 0, 0))
    return pl.pallas_call(
        functools.partial(_pool_kernel, n_lat=t_all - CTX_LEN),
        out_shape=jax.ShapeDtypeStruct((n_b, t_all, 256), BF16),
        grid=(n_b,),
        in_specs=[blk, _layer_spec((len(POOL_WINDOWS), POOL_GW, POOL_GW), layer), _layer_spec((1, 256), layer)],
        out_specs=blk,
        scratch_shapes=[pltpu.VMEM((t_all + 32, 256), F32), pltpu.VMEM((GROUP_W, GROUP_W), BF16)],
        compiler_params=_cp(("parallel",)),
        name="pool_mixer",
    )(pu, w_pool, scale)


def _gelu_tanh(x):
    return 0.5 * x * (1.0 + jnp.tanh(math.sqrt(2.0 / math.pi) * (x + 0.044715 * (x * x * x))))


def _outproj_res(res, mod_row, ygla_ref, yna, ys50_ref, ys51_ref, ypool_ref, gate_ref, mod_ref, wout_ref,
                 gpost_ref, wglu_ref, bglu_ref):
    def gated_proj(part, i):
        lo, hi = i * GROUP_W, (i + 1) * GROUP_W
        gc = gate_ref[0, :, lo:hi].astype(F32)
        return _dot((part * (gc * jax.nn.sigmoid(gc))).astype(BF16), wout_ref[lo:hi, :])

    gate_m = mod_ref[pl.ds(mod_row, 1), :][:, 2 * D_MODEL:3 * D_MODEL]
    g5 = _gelu_tanh(jnp.concatenate([ys50_ref[0], ys51_ref[0]], axis=-1))
    z5 = _dot(g5.astype(BF16), wglu_ref[...])
    acc = gated_proj(ygla_ref[0], 0) + gated_proj(ypool_ref[0], 3)
    acc = acc + gated_proj(g5 * jax.nn.sigmoid(z5 + bglu_ref[...]), 2)
    acc = acc + gated_proj(yna() if callable(yna) else yna, 1)
    ms = jnp.mean(acc * acc, axis=-1, keepdims=True)
    return res + gate_m * (acc * lax.rsqrt(ms + NORM_EPS) * gpost_ref[...])


N_OUT_PARAM_REFS = 5
N_IN_PARAM_REFS = 9


def _na_outproj_kernel(q_ref, kt0_ref, kt1_ref, kt2_ref, v0_ref, v1_ref, v2_ref, ktc_ref, vc_ref, rel_ref,
                       ygla_ref, ys50_ref, ys51_ref, ypool_ref, gate_ref, x_ref, *rest, rows):
    params, out_ref = rest[:-1], rest[-1]
    window = ((kt0_ref, v0_ref), (kt1_ref, v1_ref), (kt2_ref, v2_ref))
    yna = lambda: _na_attend(q_ref, ktc_ref, vc_ref, window, rel_ref, pl.program_id(1), rows)
    out_ref[0] = _outproj_res(x_ref[0], pl.program_id(0), ygla_ref, yna, ys50_ref, ys51_ref, ypool_ref, gate_ref,
                              *params)


def _na_outproj_inproj_kernel(q_ref, kt0_ref, kt1_ref, kt2_ref, v0_ref, v1_ref, v2_ref, ktc_ref, vc_ref, rel_ref,
                              ygla_ref, ys50_ref, ys51_ref, ypool_ref, gate_ref, x_ref, xc_ref, *rest, n_batch, rows):
    out_params = rest[:N_OUT_PARAM_REFS]
    rest = rest[N_OUT_PARAM_REFS:]
    in_params, (xo_ref, xco_ref), in_outs = rest[:N_IN_PARAM_REFS], rest[N_IN_PARAM_REFS:N_IN_PARAM_REFS + 2], \
        rest[N_IN_PARAM_REFS + 2:]
    b = pl.program_id(0)
    j = pl.program_id(1)
    window = ((kt0_ref, v0_ref), (kt1_ref, v1_ref), (kt2_ref, v2_ref))

    def body(res_ref, out_ref, tiles, mod_row, with_rope):
        yna = lambda: _na_attend(q_ref, ktc_ref, vc_ref, tiles, rel_ref, j - 1, rows)
        res = _outproj_res(res_ref[0], mod_row, ygla_ref, yna, ys50_ref, ys51_ref, ypool_ref, gate_ref, *out_params)
        out_ref[0] = res
        _inproj_body(res, mod_row, with_rope, *in_params, *in_outs)

    pl.when(j == 0)(lambda: body(xc_ref, xco_ref, (), n_batch, False))
    pl.when(j > 0)(lambda: body(x_ref, xo_ref, window, b, True))


def _outproj_param_specs(d, layer):
    return [_layer_spec((8, 3 * d), layer), _layer_spec((MIX_W, d), layer), _layer_spec((1, d), layer),
            _layer_spec((GROUP_W, GROUP_W), layer), _layer_spec((1, GROUP_W), layer)]


def _na_outproj(nq, nkt, nv, rel_tab, mix, x, layer, mod_all, w_out, g_post, w_glu, b_glu):
    n_b, n_lat, d = x.shape
    n_tiles = n_lat // TILE + 1
    rows = n_lat // GRID_W

    def nbr(delta):
        return lambda b, j: (b, jnp.clip(j + 1 + delta, 1, n_tiles - 1), 0)

    def nbr_t(delta):
        return lambda b, j: (b, 0, jnp.clip(j + 1 + delta, 1, n_tiles - 1))

    here = lambda b, j: (b, j + 1, 0)
    tile = lambda f: pl.BlockSpec((1, TILE, 256), f)
    tile_t = lambda f: pl.BlockSpec((1, 256, TILE), f)
    return pl.pallas_call(
        functools.partial(_na_outproj_kernel, rows=rows),
        out_shape=jax.ShapeDtypeStruct(x.shape, F32),
        grid=(n_b, n_lat // TILE),
        in_specs=[tile(here), tile_t(nbr_t(-1)), tile_t(nbr_t(0)), tile_t(nbr_t(1)),
                  tile(nbr(-1)), tile(nbr(0)), tile(nbr(1)),
                  tile_t(lambda b, j: (b, 0, 0)), tile(lambda b, j: (b, 0, 0)),
                  _layer_spec((NA_HEADS, 2, GRID_W, NA_REL_W), layer)]
                 + [pl.BlockSpec((1, TILE, w), here) for w in (256, 128, 128, 256, MIX_W)]
                 + [pl.BlockSpec((1, TILE, d), lambda b, j: (b, j, 0))] + _outproj_param_specs(d, layer),
        out_specs=pl.BlockSpec((1, TILE, d), lambda b, j: (b, j, 0)),
        compiler_params=_cp(("parallel", "arbitrary")),
        name="na_then_outproj",
    )(nq, nkt, nkt, nkt, nv, nv, nv, nkt, nv, rel_tab, *mix, x, mod_all, w_out, g_post, w_glu, b_glu)


def _na_outproj_inproj(nq, nkt, nv, rel_tab, mix, x, xc, layer, mod_all, w_out, g_post, w_glu, b_glu, g_pre, w_pad,
                       wg, bg, rope_tabs):
    n_b, n_lat, d = x.shape
    t_all = n_lat + CTX_LEN
    n_tiles = t_all // TILE
    rows = n_lat // GRID_W

    def nbr(delta):
        return lambda b, j: (b, jnp.clip(j + delta, 1, n_tiles - 1), 0)

    def nbr_t(delta):
        return lambda b, j: (b, 0, jnp.clip(j + delta, 1, n_tiles - 1))

    here = lambda b, j: (b, j, 0)
    tile = lambda f: pl.BlockSpec((1, TILE, 256), f)
    tile_t = lambda f: pl.BlockSpec((1, 256, TILE), f)
    in_specs, in_shape, in_out_specs = _inproj_specs(n_b, t_all, d, layer + 1)
    outs = pl.pallas_call(
        functools.partial(_na_outproj_inproj_kernel, n_batch=n_b, rows=rows),
        out_shape=[jax.ShapeDtypeStruct(x.shape, F32), jax.ShapeDtypeStruct(xc.shape, F32)] + in_shape,
        grid=(n_b, n_tiles),
        in_specs=[tile(here), tile_t(nbr_t(-1)), tile_t(nbr_t(0)), tile_t(nbr_t(1)),
                  tile(nbr(-1)), tile(nbr(0)), tile(nbr(1)),
                  tile_t(lambda b, j: (b, 0, 0)), tile(lambda b, j: (b, 0, 0)),
                  _layer_spec((NA_HEADS, 2, GRID_W, NA_REL_W), layer)]
                 + [pl.BlockSpec((1, TILE, w), here) for w in (256, 128, 128, 256, MIX_W)]
                 + [pl.BlockSpec((1, TILE, d), _lat_map), pl.BlockSpec((1, CTX_LEN, d), _ctx_map)]
                 + _outproj_param_specs(d, layer) + in_specs,
        out_specs=[pl.BlockSpec((1, TILE, d), _lat_map), pl.BlockSpec((1, CTX_LEN, d), _ctx_map)] + in_out_specs,
        compiler_params=_cp(("parallel", "arbitrary")),
        name="na_outproj_then_inproj",
    )(nq, nkt, nkt, nkt, nv, nv, nv, nkt, nv, rel_tab, *mix, x, xc, mod_all, w_out, g_post, w_glu, b_glu,
      mod_all, g_pre, *w_pad, wg, bg, *rope_tabs)
    return outs[0], outs[1], outs[2:]


def _rope_tables(n_tok):
    t = np.arange(n_tok)
    half = GLA_DK // 2
    freqs = ROPE_BASE ** (-np.arange(0, half, 2, dtype=np.float32) / half)
    d = np.arange(GLA_DK)
    pos = np.where((d // half)[None, :] == 0, (t // GRID_W)[:, None], (t % GRID_W)[:, None]).astype(np.float32)
    ang = jnp.asarray(pos) * jnp.asarray(freqs[(d % half) % (half // 2)])[None, :]
    first = jnp.asarray(((d % half) < half // 2)[None, :])
    cos, sin = jnp.cos(ang), jnp.sin(ang)
    tabs = (cos, jnp.where(first, -sin, 0.0), jnp.where(first, 0.0, sin))
    return tuple(jnp.tile(z, (1, GLA_HEADS)).astype(F32) for z in tabs)


def _split_w_in(w_in):
    pad = jnp.zeros(w_in.shape[:2] + (N_HEAD_COLS - N_GATE_END - 128,), w_in.dtype)
    head = jnp.concatenate([w_in[:, :, GQ_COL:GQ_COL + 128], w_in[:, :, :N_GATE_END], pad], axis=2)
    return head.astype(BF16), w_in[:, :, N_GATE_END:].astype(BF16)


def _gate_weights(w_gate, b_gate):
    n_l = w_gate.shape[0]
    wg = jnp.zeros((n_l, 128, 256), F32)
    wg = wg.at[:, 0:GLA_RANK, 0:128].set(w_gate[:, 0].astype(F32))
    wg = wg.at[:, GLA_RANK:2 * GLA_RANK, 128:256].set(w_gate[:, 1].astype(F32))
    return wg.astype(BF16), b_gate.astype(F32).reshape(n_l, 1, 256)


def kernel(x, c, ctx, c_ctx, w_mod, b_mod, g_pre, g_post, w_in, w_out, gla_w_gate, gla_b_gate, gla_g_norm, na_rpb,
           s5_lam_re, s5_lam_im, s5_log_dt, s5_b_re, s5_b_im, s5_c_re, s5_c_im, s5_d, s5_w_glu, s5_b_glu,
           pool_w, pool_scale):
    n_b, n_lat, d = x.shape
    depth = w_mod.shape[0]
    rows = n_lat // GRID_W
    assert d == D_MODEL and ctx.shape[1] == CTX_LEN and n_lat % TILE == 0 and n_b < 8 and rows >= NA_KR
    c_rows = jnp.zeros((8, d), F32).at[0:n_b].set(c).at[n_b].set(c_ctx)
    mod_all = _modulation(c_rows, w_mod, b_mod)
    rope_tabs = _rope_tables(n_lat)
    w_pad = _split_w_in(w_in)
    wg, bg = _gate_weights(gla_w_gate, gla_b_gate)
    g_pre3 = g_pre.astype(F32).reshape(depth, 1, d)
    g_post3 = g_post.astype(F32).reshape(depth, 1, d)
    gn = jnp.tile(gla_g_norm.astype(F32), (1, GLA_HEADS)).reshape(depth, 1, GLA_HEADS * GLA_DV)
    rel_tab = jax.vmap(_na_rel_table)(na_rpb)
    bexp, cexp, pw16, skip = jax.vmap(_s5_tables)(s5_lam_re, s5_lam_im, s5_log_dt, s5_b_re, s5_b_im,
                                                  s5_c_re, s5_c_im, s5_d)
    w_out_b = w_out.astype(BF16)
    w_glu_b = s5_w_glu.astype(BF16)
    b_glu3 = s5_b_glu.astype(F32).reshape(depth, 1, GROUP_W)
    pool_scale3 = pool_scale.astype(F32).reshape(depth, 1, GROUP_W)
    xc = ctx
    proj = _inproj(x, xc, 0, mod_all, g_pre3, w_pad, wg, bg, rope_tabs)
    for l in range(depth):
        qk, gv, lg, nq, nkt, nv, su0, su1, pu, gate_cols = proj
        y_gla = _gla(qk, gv, lg, l, gn)
        y_s50, y_s51 = _s5(su0, su1, l, bexp, cexp, pw16, skip)
        y_pool = _pool(pu, l, pool_w, pool_scale3)
        if l == depth - 1:
            x = _na_outproj(nq, nkt, nv, rel_tab, (y_gla, y_s50, y_s51, y_pool, gate_cols), x, l, mod_all,
                            w_out_b, g_post3, w_glu_b, b_glu3)
        else:
            x, xc, proj = _na_outproj_inproj(nq, nkt, nv, rel_tab, (y_gla, y_s50, y_s51, y_pool, gate_cols), x, xc, l,
                                             mod_all, w_out_b, g_post3, w_glu_b, b_glu3, g_pre3, w_pad, wg, bg,
                                             rope_tabs)
    return x
```
